```python
import math
import jax, jax.numpy as jnp
from jax import lax
import numpy as np

D_MODEL = 2048
BATCH = 8
SEQ = 4096
DEPTH = 1
DEC_BATCH = 8
DEC_SEQ = 16
PAST_LEN = 1024

CHUNK = 64
Q_BLOCK = CHUNK
N_HEADS_A = 8
HEAD_DIM_A = 128
N_KV_A = 2
D_ATTN = N_HEADS_A * HEAD_DIM_A
D_KV = N_KV_A * HEAD_DIM_A
N_IDX_HEADS = 8
IDX_DIM = 64
TOPK_MAX = 256
POOL_WINDOWS = (2, 4, 8, 16)
POOL_GROUP_DIM = 256
D_POOL = len(POOL_WINDOWS) * POOL_GROUP_DIM
POOL_HIST = max(POOL_WINDOWS) - 1
D_MIX = D_ATTN + D_POOL
IN_SPLITS = (D_ATTN, D_KV, D_KV, N_IDX_HEADS * IDX_DIM, IDX_DIM, N_IDX_HEADS, D_POOL)
D_IN = sum(IN_SPLITS)
N_BUCKETS = 32
MAX_DISTANCE = 128
N_MEM = 256
N_XHEADS = 4
XHEAD_DIM = 128
D_X = N_XHEADS * XHEAD_DIM
N_GROUPS = 4
EXPERTS_PER_GROUP = 8
N_EXPERTS = N_GROUPS * EXPERTS_PER_GROUP
TOP_K_EXPERTS = 2
D_EXPERT = 256
EPS = 1e-6

kernel_name = 'hybrid_dsa_pool_hmoe_stream_step'


def rms_norm(x, g):
    xf = x.astype(jnp.float32)
    y = xf * lax.rsqrt(jnp.mean(xf * xf, axis=-1, keepdims=True) + EPS)
    return (y * g.astype(jnp.float32)).astype(x.dtype)


def t5_bucket(rel):
    nb = N_BUCKETS // 2
    max_exact = nb // 2
    bucket = (rel > 0).astype(jnp.int32) * nb
    n = jnp.abs(rel)
    nf = jnp.maximum(n, 1).astype(jnp.float32)
    large = max_exact + (jnp.log(nf / max_exact) / math.log(MAX_DISTANCE / max_exact)
                         * (nb - max_exact)).astype(jnp.int32)
    large = jnp.minimum(large, nb - 1)
    return bucket + jnp.where(n < max_exact, n, large)


def dsa_block(q, qi, wi, q_pos, k_all, v_all, ki_all, k_pos, rel_bias, topk):
    B, T, H, Dh = q.shape
    G = k_all.shape[2]
    R = H // G
    s = jnp.einsum('bthd,bsd->bths', qi.astype(jnp.float32), ki_all.astype(jnp.float32)) * IDX_DIM ** -0.5
    idx = jnp.einsum('bth,bths->bts', wi.astype(jnp.float32) * N_IDX_HEADS ** -0.5, jax.nn.relu(s))
    visible = (k_pos[None, :] // CHUNK) <= (q_pos[:, None] // CHUNK)
    idx = jnp.where(visible[None], idx, -jnp.inf)
    top_val, top_idx = lax.top_k(idx, topk)
    valid = jnp.isfinite(top_val)
    take = jax.vmap(lambda a, i: a[i])
    k_sel = take(k_all, top_idx)
    v_sel = take(v_all, top_idx)
    rel = k_pos[top_idx] - q_pos[None, :, None]
    bias = rel_bias[t5_bucket(rel)]
    bias = jnp.moveaxis(bias, -1, 2).reshape(B, T, G, R, topk)
    logits = (jnp.einsum('btgrd,btkgd->btgrk', q.reshape(B, T, G, R, Dh), k_sel).astype(jnp.float32)
              * Dh ** -0.5 + bias.astype(jnp.float32))
    logits = jnp.where(valid[:, :, None, None, :], logits, -jnp.inf)
    p = jax.nn.softmax(logits, axis=-1).astype(v_all.dtype)
    out = jnp.einsum('btgrk,btkgd->btgrd', p, v_sel)
    return out.reshape(B, T, H * Dh)


def sparse_attention(q, qi, wi, q_pos, k_all, v_all, ki_all, k_pos, rel_bias):
    B, T = q.shape[:2]
    L = k_all.shape[1]
    topk = min(TOPK_MAX, L // 4)

    def block(args):
        qb, qib, wib, pb = args
        return dsa_block(qb, qib, wib, pb, k_all, v_all, ki_all, k_pos, rel_bias, topk)

    if T > Q_BLOCK and T % Q_BLOCK == 0:
        nb = T // Q_BLOCK
        split = lambda a: jnp.moveaxis(a.reshape((B, nb, Q_BLOCK) + a.shape[2:]), 1, 0)
        out = lax.map(block, (split(q), split(qi), split(wi), q_pos.reshape(nb, Q_BLOCK)))
        return jnp.moveaxis(out, 0, 1).reshape(B, T, -1)
    return block((q, qi, wi, q_pos))


def pool_mixer(u, u_hist, pos0, w_pool, pool_scale):
    B, T, C = u.shape
    Hn = u_hist.shape[1]
    ext = jnp.concatenate([u_hist, u], axis=1)
    c = jnp.cumsum(ext.astype(jnp.float32), axis=1)
    c = jnp.concatenate([jnp.zeros((B, 1, C), jnp.float32), c], axis=1)
    pos = pos0 + jnp.arange(T, dtype=jnp.int32)
    means = []
    for gi, w in enumerate(POOL_WINDOWS):
        sl = slice(gi * POOL_GROUP_DIM, (gi + 1) * POOL_GROUP_DIM)
        wsum = c[:, Hn + 1:Hn + 1 + T, sl] - c[:, Hn + 1 - w:Hn + 1 - w + T, sl]
        cnt = jnp.minimum(pos + 1, w).astype(jnp.float32)[None, :, None]
        means.append(wsum / cnt)
    mean = jnp.stack(means, axis=2)
    d = mean - u.reshape(B, T, len(POOL_WINDOWS), POOL_GROUP_DIM).astype(jnp.float32)
    y = jnp.einsum('btgc,gce->btge', d.astype(w_pool.dtype), w_pool).reshape(B, T, C) * pool_scale
    return y, ext[:, -Hn:]


def memory_attention(h, mem_k, mem_v, w_xq, w_xo):
    B, T, _ = h.shape
    q = jnp.einsum('btd,de->bte', h, w_xq).reshape(B, T, N_XHEADS, XHEAD_DIM)
    logits = jnp.einsum('bthd,bmhd->bhtm', q, mem_k).astype(jnp.float32) * XHEAD_DIM ** -0.5
    p = jax.nn.softmax(logits, axis=-1).astype(mem_v.dtype)
    o = jnp.einsum('bhtm,bmhd->bthd', p, mem_v).reshape(B, T, D_X)
    return jnp.einsum('bte,ed->btd', o, w_xo)


def hier_moe(h, w_rg, b_rg, w_re, b_re, w_gate, w_up, w_down):
    B, T, _ = h.shape
    gl = (jnp.einsum('btd,dg->btg', h, w_rg) + b_rg).astype(jnp.float32)
    g_sel = jnp.argmax(gl, axis=-1)
    g_prob = jnp.max(jax.nn.softmax(gl, axis=-1), axis=-1)
    el = (jnp.einsum('btd,de->bte', h, w_re) + b_re).astype(jnp.float32)
    el = el.reshape(B, T, N_GROUPS, EXPERTS_PER_GROUP)
    el_sel = jnp.einsum('btge,btg->bte', el, jax.nn.one_hot(g_sel, N_GROUPS, dtype=jnp.float32))
    tv, ti = lax.top_k(el_sel, TOP_K_EXPERTS)
    tw = jax.nn.softmax(tv, axis=-1) * g_prob[..., None]
    e_idx = g_sel[..., None] * EXPERTS_PER_GROUP + ti
    combine = jnp.sum(jax.nn.one_hot(e_idx, N_EXPERTS, dtype=jnp.float32) * tw[..., None], axis=2)
    a = jnp.einsum('btd,edf->btef', h, w_gate)
    b = jnp.einsum('btd,edf->btef', h, w_up)
    hid = jax.nn.silu(a) * b * combine[..., None].astype(h.dtype)
    return jnp.einsum('btef,efd->btd', hid, w_down)


def layer(x, pos0, k_hist, v_hist, ki_hist, pool_hist, mem_k, mem_v, rel_bias,
          g_mix, w_in, w_pool, pool_scale, w_out, g_x, w_xq, w_xo,
          g_ffn, w_rg, b_rg, w_re, b_re, w_gate, w_up, w_down):
    B, T, _ = x.shape
    h = rms_norm(x, g_mix)
    proj = jnp.einsum('btd,de->bte', h, w_in)
    q, k, v, qi, ki, wi, u = jnp.split(proj, np.cumsum(IN_SPLITS)[:-1].tolist(), axis=-1)
    q = q.reshape(B, T, N_HEADS_A, HEAD_DIM_A)
    k = k.reshape(B, T, N_KV_A, HEAD_DIM_A)
    v = v.reshape(B, T, N_KV_A, HEAD_DIM_A)
    qi = qi.reshape(B, T, N_IDX_HEADS, IDX_DIM)
    k_all = jnp.concatenate([k_hist, k], axis=1)
    v_all = jnp.concatenate([v_hist, v], axis=1)
    ki_all = jnp.concatenate([ki_hist, ki], axis=1)
    q_pos = pos0 + jnp.arange(T, dtype=jnp.int32)
    k_pos = jnp.arange(k_all.shape[1], dtype=jnp.int32)
    a_out = sparse_attention(q, qi, wi, q_pos, k_all, v_all, ki_all, k_pos, rel_bias)
    p_out, new_pool = pool_mixer(u, pool_hist, pos0, w_pool, pool_scale)
    mixed = jnp.concatenate([a_out, p_out.astype(a_out.dtype)], axis=-1)
    x = x + jnp.einsum('bte,ed->btd', mixed, w_out)
    x = x + memory_attention(rms_norm(x, g_x), mem_k, mem_v, w_xq, w_xo)
    x = x + hier_moe(rms_norm(x, g_ffn), w_rg, b_rg, w_re, b_re, w_gate, w_up, w_down)
    return x, k, v, ki, new_pool


def setup_inputs(seed: int = 0) -> dict:
    key = jax.random.key(seed)
    ks = jax.random.split(key, 40)
    nrm = lambda k, shape, scale: jax.random.normal(k, shape, jnp.float32) * scale
    gain = lambda k, shape: 1.0 + 0.05 * jax.random.normal(k, shape, jnp.float32)
    L = DEPTH
    return {
        'x_prompt': nrm(ks[0], (BATCH, SEQ, D_MODEL), 1.0),
        'x_sample': nrm(ks[1], (DEC_BATCH, DEC_SEQ, D_MODEL), 1.0),
        'mem_prompt': nrm(ks[2], (BATCH, N_MEM, D_MODEL), 1.0),
        'cache_k': nrm(ks[3], (L, DEC_BATCH, PAST_LEN, N_KV_A, HEAD_DIM_A), 1.0),
        'cache_v': nrm(ks[4], (L, DEC_BATCH, PAST_LEN, N_KV_A, HEAD_DIM_A), 1.0),
        'cache_k_idx': nrm(ks[5], (L, DEC_BATCH, PAST_LEN, IDX_DIM), 1.0),
        'cache_pool': nrm(ks[6], (L, DEC_BATCH, POOL_HIST, D_POOL), 1.0),
        'cache_mem_k': nrm(ks[7], (L, DEC_BATCH, N_MEM, N_XHEADS, XHEAD_DIM), 1.0),
        'cache_mem_v': nrm(ks[8], (L, DEC_BATCH, N_MEM, N_XHEADS, XHEAD_DIM), 1.0),
        'rel_bias': nrm(ks[9], (N_BUCKETS, N_HEADS_A), 0.5),
        'g_mix': gain(ks[10], (L, D_MODEL)),
        'w_in': nrm(ks[11], (L, D_MODEL, D_IN), D_MODEL ** -0.5),
        'w_pool': nrm(ks[12], (L, len(POOL_WINDOWS), POOL_GROUP_DIM, POOL_GROUP_DIM), POOL_GROUP_DIM ** -0.5),
        'pool_scale': gain(ks[13], (L, D_POOL)),
        'w_out': nrm(ks[14], (L, D_MIX, D_MODEL), D_MIX ** -0.5),
        'g_mem': gain(ks[15], (L, D_MODEL)),
        'w_mk': nrm(ks[16], (L, D_MODEL, D_X), D_MODEL ** -0.5),
        'w_mv': nrm(ks[17], (L, D_MODEL, D_X), D_MODEL ** -0.5),
        'g_x': gain(ks[18], (L, D_MODEL)),
        'w_xq': nrm(ks[19], (L, D_MODEL, D_X), D_MODEL ** -0.5),
        'w_xo': nrm(ks[20], (L, D_X, D_MODEL), D_X ** -0.5),
        'g_ffn': gain(ks[21], (L, D_MODEL)),
        'w_rg': nrm(ks[22], (L, D_MODEL, N_GROUPS), D_MODEL ** -0.5),
        'b_rg': nrm(ks[23], (L, N_GROUPS), 0.01),
        'w_re': nrm(ks[24], (L, D_MODEL, N_EXPERTS), D_MODEL ** -0.5),
        'b_re': nrm(ks[25], (L, N_EXPERTS), 0.01),
        'w_gate': nrm(ks[26], (L, N_EXPERTS, D_MODEL, D_EXPERT), D_MODEL ** -0.5),
        'w_up': nrm(ks[27], (L, N_EXPERTS, D_MODEL, D_EXPERT), D_MODEL ** -0.5),
        'w_down': nrm(ks[28], (L, N_EXPERTS, D_EXPERT, D_MODEL), D_EXPERT ** -0.5),
        'g_final': gain(ks[29], (D_MODEL,)),
    }


def reference(x_prompt, x_sample, mem_prompt, cache_k, cache_v, cache_k_idx, cache_pool,
              cache_mem_k, cache_mem_v, rel_bias, g_mix, w_in, w_pool, pool_scale, w_out,
              g_mem, w_mk, w_mv, g_x, w_xq, w_xo, g_ffn, w_rg, b_rg, w_re, b_re,
              w_gate, w_up, w_down, g_final):
    B = x_prompt.shape[0]
    n_mem = mem_prompt.shape[1]
    past = cache_k.shape[2]
    xp, xs = x_prompt, x_sample
    nk_p, nv_p, nki_p, npool_p, nmk_p, nmv_p = [], [], [], [], [], []
    nk_s, nv_s, nki_s, npool_s = [], [], [], []
    for l in range(DEPTH):
        lw = (g_mix[l], w_in[l], w_pool[l], pool_scale[l], w_out[l], g_x[l], w_xq[l], w_xo[l],
              g_ffn[l], w_rg[l], b_rg[l], w_re[l], b_re[l], w_gate[l], w_up[l], w_down[l])
        mn = rms_norm(mem_prompt, g_mem[l])
        mk = jnp.einsum('bmd,de->bme', mn, w_mk[l]).reshape(B, n_mem, N_XHEADS, XHEAD_DIM)
        mv = jnp.einsum('bmd,de->bme', mn, w_mv[l]).reshape(B, n_mem, N_XHEADS, XHEAD_DIM)
        k0 = jnp.zeros((B, 0, N_KV_A, HEAD_DIM_A), xp.dtype)
        ki0 = jnp.zeros((B, 0, IDX_DIM), xp.dtype)
        pool0 = jnp.zeros((B, POOL_HIST, D_POOL), xp.dtype)
        xp, kp, vp, kip, pp = layer(xp, 0, k0, k0, ki0, pool0, mk, mv, rel_bias, *lw)
        xs, ksm, vsm, kis, ps = layer(xs, past, cache_k[l], cache_v[l], cache_k_idx[l], cache_pool[l],
                                      cache_mem_k[l], cache_mem_v[l], rel_bias, *lw)
        nk_p.append(kp); nv_p.append(vp); nki_p.append(kip); npool_p.append(pp)
        nmk_p.append(mk); nmv_p.append(mv)
        nk_s.append(ksm); nv_s.append(vsm); nki_s.append(kis); npool_s.append(ps)
    y_prompt = rms_norm(xp, g_final)
    y_sample = rms_norm(xs, g_final)
    return (y_prompt, y_sample,
            jnp.stack(nk_p), jnp.stack(nv_p), jnp.stack(nki_p), jnp.stack(npool_p),
            jnp.stack(nmk_p), jnp.stack(nmv_p),
            jnp.stack(nk_s), jnp.stack(nv_s), jnp.stack(nki_s), jnp.stack(npool_s))
```

```python
import functools
import math

import numpy as np
import jax
import jax.numpy as jnp
from jax import lax
from jax.experimental import pallas as pl
from jax.experimental.pallas import tpu as pltpu

F32 = jnp.float32
BF16 = jnp.bfloat16

CHUNK = 64
N_HEADS_A = 8
HEAD_DIM_A = 128
N_KV_A = 2
D_ATTN = N_HEADS_A * HEAD_DIM_A
D_KV = N_KV_A * HEAD_DIM_A
N_IDX_HEADS = 8
IDX_DIM = 64
TOPK_MAX = 256
POOL_WINDOWS = (2, 4, 8, 16)
POOL_GROUP_DIM = 256
D_POOL = len(POOL_WINDOWS) * POOL_GROUP_DIM
POOL_HIST = max(POOL_WINDOWS) - 1
N_BUCKETS = 32
MAX_DISTANCE = 128
N_XHEADS = 4
XHEAD_DIM = 128
D_X = N_XHEADS * XHEAD_DIM
N_GROUPS = 4
EXPERTS_PER_GROUP = 8
N_EXPERTS = N_GROUPS * EXPERTS_PER_GROUP
D_EXPERT = 256
EPS = 1e-6

LANES = 128
INT_MIN = -(2 ** 31)
NEG = -1e30
VMEM_LIMIT = 48 * 1024 * 1024


def _rms(x, g):
    ms = jnp.mean(x * x, axis=-1, keepdims=True)
    return x * lax.rsqrt(ms + EPS) * g


def _dot(a, b):
    return jnp.dot(a, b, preferred_element_type=F32)


def _dot_nt(a, b):
    return lax.dot_general(a, b, (((1,), (1,)), ((), ())), preferred_element_type=F32)


def _const_spec(shape):
    nd = len(shape)
    return pl.BlockSpec(shape, lambda *_: (0,) * nd, pipeline_mode=pl.Buffered(1))


def _norm_matmul_kernel(x_ref, g_ref, w_ref, *out_refs, groups):
    h = _rms(x_ref[...], g_ref[...]).astype(BF16)
    k = 0
    for c0, c1, scale, n_out in groups:
        acc = _dot(h, w_ref[:, c0:c1])
        if scale != 1.0:
            acc = acc * scale
        for _ in range(n_out):
            out_refs[k][...] = acc.astype(out_refs[k].dtype)
            k += 1


def _norm_matmul(x, g, w, groups, out_dtypes, tm):
    n, d = x.shape
    assert n % tm == 0
    widths = []
    for c0, c1, _, n_out in groups:
        widths += [c1 - c0] * n_out
    return pl.pallas_call(
        functools.partial(_norm_matmul_kernel, groups=tuple(groups)),
        grid=(n // tm,),
        in_specs=[pl.BlockSpec((tm, d), lambda i: (i, 0)),
                  _const_spec((1, d)),
                  _const_spec(w.shape)],
        out_specs=[pl.BlockSpec((tm, wd), lambda i: (i, 0)) for wd in widths],
        out_shape=[jax.ShapeDtypeStruct((n, wd), dt) for wd, dt in zip(widths, out_dtypes)],
        compiler_params=pltpu.CompilerParams(dimension_semantics=("arbitrary",),
                                             vmem_limit_bytes=VMEM_LIMIT),
        name="norm_matmul",
    )(x, g.reshape(1, d), w)


def _bias_table_kernel(relb_ref, bucket_ref, o_ref):
    bucket = bucket_ref[...]
    for h in range(N_HEADS_A):
        acc = jnp.zeros(bucket.shape, F32)
        for b in range(N_BUCKETS):
            acc = jnp.where(bucket == b, relb_ref[b, h], acc)
        o_ref[h] = acc


def _bias_table(rel_bias, bucket):
    tq, w = bucket.shape
    return pl.pallas_call(
        _bias_table_kernel,
        in_specs=[pl.BlockSpec(memory_space=pltpu.SMEM),
                  pl.BlockSpec((tq, w), lambda: (0, 0))],
        out_specs=pl.BlockSpec((N_HEADS_A, tq, w), lambda: (0, 0, 0)),
        out_shape=jax.ShapeDtypeStruct((N_HEADS_A, tq, w), F32),
        name="bias_table",
    )(rel_bias, bucket)


def _t5_bucket(rel):
    nb = N_BUCKETS // 2
    max_exact = nb // 2
    bucket = (rel > 0).astype(jnp.int32) * nb
    n = jnp.abs(rel)
    nf = jnp.maximum(n, 1).astype(F32)
    large = max_exact + (jnp.log(nf / max_exact) / math.log(MAX_DISTANCE / max_exact)
                         * (nb - max_exact)).astype(jnp.int32)
    large = jnp.minimum(large, nb - 1)
    return bucket + jnp.where(n < max_exact, n, large)


def _far_bucket(min_dist, max_dist):
    nb = N_BUCKETS // 2
    max_exact = nb // 2
    n = np.arange(min_dist, max_dist + 1, dtype=np.float64)
    large = max_exact + np.floor(np.log(n / max_exact) / math.log(MAX_DISTANCE / max_exact)
                                 * (nb - max_exact) * (1 - 1e-6)).astype(np.int64)
    assert large.min() >= nb - 1, "far key blocks must sit in the saturated distance bucket"
    return nb - 1


def _dsa_kernel(relb_ref, q_ref, qi_ref, kiwi_ref, k_ref, v_ref, kie_ref, kio_ref, bnear_ref, vis_ref,
                o_ref, keys, madd, wb, qs, xcut, m_s, l_s, acc_s,
                *, TQ, KB, RB, kb0, kstep, topk, far_bucket, tie_bits):
    i = pl.program_id(1)
    nkb = kb0 + kstep * i
    n_far = jnp.maximum(nkb - 2, 0)
    n_lc = KB // LANES
    HPG = N_HEADS_A // N_KV_A

    kiwi = kiwi_ref[0]
    w_scale = (N_IDX_HEADS ** -0.5) * (IDX_DIM ** -0.5)
    for h in range(N_IDX_HEADS):
        wb[h] = jnp.broadcast_to(kiwi[:, IDX_DIM + h:IDX_DIM + h + 1] * w_scale, (TQ, LANES))
    for g in range(N_KV_A):
        for r in range(HPG):
            h = g * HPG + r
            qs[g, r * TQ:(r + 1) * TQ, :] = q_ref[0, :, h * HEAD_DIM_A:(h + 1) * HEAD_DIM_A]

    def idx_block(j, near_col):
        c0 = pl.multiple_of(j * KB, KB)
        ke = kie_ref[0, pl.ds(c0, KB), :]
        ko = kio_ref[0, pl.ds(c0, KB), :]
        acc = [jnp.zeros((TQ, LANES), F32) for _ in range(n_lc)]
        for hp in range(N_IDX_HEADS // 2):
            qh = qi_ref[0, :, hp * LANES:(hp + 1) * LANES]
            se = _dot_nt(qh, ke)
            so = _dot_nt(qh, ko)
            we = wb[2 * hp]
            wo = wb[2 * hp + 1]
            for c in range(n_lc):
                cs = slice(c * LANES, (c + 1) * LANES)
                acc[c] = acc[c] + we * jnp.maximum(se[:, cs], 0.0) + wo * jnp.maximum(so[:, cs], 0.0)
        for c in range(n_lc):
            bits = lax.bitcast_convert_type(acc[c], jnp.int32)
            key = bits ^ ((bits >> 31) & jnp.int32(0x7FFFFFFF))
            if near_col is not None:
                vis = vis_ref[:, near_col + c * LANES:near_col + (c + 1) * LANES]
                key = jnp.where(vis > 0.0, key, jnp.int32(INT_MIN))
            keys[j, :, c * LANES:(c + 1) * LANES] = key

    def far_idx(j, carry):
        idx_block(j, None)
        return carry

    lax.fori_loop(0, n_far, far_idx, 0)

    @pl.when(nkb >= 2)
    def _():
        idx_block(nkb - 2, 0)

    idx_block(nkb - 1, KB)

    kf = float(topk)
    lane_f = lax.broadcasted_iota(jnp.int32, (RB, LANES), 1).astype(F32)
    for rb in range(TQ // RB):
        rows = slice(rb * RB, (rb + 1) * RB)

        def count(pred):
            def body(jj, acc):
                blk = keys[jj, rows, :]
                for c in range(n_lc):
                    acc = acc + jnp.where(pred(blk[:, c * LANES:(c + 1) * LANES], jj, c), 1.0, 0.0)
                return acc
            acc = lax.fori_loop(0, nkb, body, jnp.zeros((RB, LANES), F32))
            return jnp.sum(acc, axis=1, keepdims=True)

        def count_ge(cand):
            candb = jnp.broadcast_to(cand, (RB, LANES))
            return count(lambda kblk, jj, c: kblk >= candb)

        def bis_body(p, thr):
            cand = thr + lax.shift_left(jnp.int32(1), 31 - p)
            return jnp.where(count_ge(cand) >= kf, cand, thr)

        thr = lax.fori_loop(0, 32, bis_body, jnp.full((RB, 1), INT_MIN, jnp.int32))
        thr = jnp.maximum(thr, jnp.int32(INT_MIN + 1))
        cnt_ge = count_ge(thr)
        cnt_gt = count_ge(thr + 1)
        need = kf - cnt_gt
        thrb = jnp.broadcast_to(thr, (RB, LANES))

        xcut[...] = jnp.full((RB, 1), 2.0 ** 24, F32)

        @pl.when(jnp.max(cnt_ge - kf) > 0.0)
        def _():
            def count_eq_lt(x):
                xb = jnp.broadcast_to(x, (RB, LANES))

                def pred(kblk, jj, c):
                    col = lane_f + (jj * KB + c * LANES).astype(F32)
                    return jnp.where(kblk == thrb, col, 2.0 ** 25) < xb
                return count(pred)

            def tie_body(p, x):
                cand = x + lax.shift_left(jnp.int32(1), tie_bits - 1 - p).astype(F32)
                return jnp.where(count_eq_lt(cand) < need, cand, x)

            xcut[...] = lax.fori_loop(0, tie_bits, tie_body, jnp.zeros((RB, 1), F32))

        xb = jnp.broadcast_to(xcut[...], (RB, LANES))

        def mask_body(jj, carry):
            blk = keys[jj, rows, :]
            for c in range(n_lc):
                kblk = blk[:, c * LANES:(c + 1) * LANES]
                col = lane_f + (jj * KB + c * LANES).astype(F32)
                keep = (kblk > thrb) | ((kblk == thrb) & (col <= xb))
                madd[jj, rows, c * LANES:(c + 1) * LANES] = jnp.where(keep, 0.0, NEG)
            return carry

        lax.fori_loop(0, nkb, mask_body, 0)

    m_s[...] = jnp.full(m_s.shape, NEG, F32)
    l_s[...] = jnp.zeros(l_s.shape, F32)
    acc_s[...] = jnp.zeros(acc_s.shape, F32)

    def attn_block(j, near_col):
        c0 = pl.multiple_of(j * KB, KB)
        mk = madd[j]
        for g in range(N_KV_A):
            kg = k_ref[0, pl.ds(c0, KB), g * HEAD_DIM_A:(g + 1) * HEAD_DIM_A]
            vg = v_ref[0, pl.ds(c0, KB), g * HEAD_DIM_A:(g + 1) * HEAD_DIM_A]
            s = _dot_nt(qs[g], kg)
            parts = []
            for r in range(HPG):
                h = g * HPG + r
                if near_col is None:
                    bias = relb_ref[far_bucket, h]
                else:
                    bias = bnear_ref[h, :, near_col:near_col + KB]
                parts.append(s[r * TQ:(r + 1) * TQ] + bias + mk)
            s = jnp.concatenate(parts, axis=0)
            m_old = m_s[g]
            m_new = jnp.maximum(m_old, jnp.max(s, axis=1, keepdims=True))
            alpha = jnp.exp(m_old - m_new)
            p = jnp.exp(s - m_new)
            l_s[g] = alpha * l_s[g] + jnp.sum(p, axis=1, keepdims=True)
            acc_s[g] = alpha * acc_s[g] + _dot(p.astype(BF16), vg)
            m_s[g] = m_new

    def far_attn(j, carry):
        attn_block(j, None)
        return carry

    lax.fori_loop(0, n_far, far_attn, 0)

    @pl.when(nkb >= 2)
    def _():
        attn_block(nkb - 2, 0)

    attn_block(nkb - 1, KB)

    for g in range(N_KV_A):
        out = acc_s[g] / l_s[g]
        for r in range(HPG):
            h = g * HPG + r
            o_ref[0, :, h * HEAD_DIM_A:(h + 1) * HEAD_DIM_A] = out[r * TQ:(r + 1) * TQ].astype(o_ref.dtype)


def _dsa(q, qi, kiwi, kb, vb, kie, kio, rel_bias, *, n_keys, pos0, tq, kblk):
    B, T, _ = q.shape
    Lp = kb.shape[1]
    assert T % tq == 0 and Lp % kblk == 0 and tq % CHUNK == 0 or T == tq
    n_qt = T // tq
    if n_qt == 1:
        kb0, kstep = Lp // kblk, 0
        assert ((pos0 + tq - 1) // CHUNK + 1) * CHUNK >= n_keys
    else:
        assert pos0 == 0 and tq == kblk and n_keys == Lp == T
        kb0, kstep = 1, 1
    topk = min(TOPK_MAX, n_keys // 4)
    rb = min(128, tq)

    t = jnp.arange(tq, dtype=jnp.int32)[:, None]
    c = jnp.arange(2 * kblk, dtype=jnp.int32)[None, :]
    if n_qt == 1:
        q_abs = pos0 + t
        s_abs = (kb0 - 2) * kblk + c
    else:
        q_abs = kblk + t
        s_abs = c
    vis = ((s_abs // CHUNK) <= (q_abs // CHUNK)) & (s_abs < (n_keys if n_qt == 1 else 2 * kblk))
    bias_near = _bias_table(rel_bias, _t5_bucket(s_abs - q_abs))
    far_bucket = _far_bucket(kblk + 1, max(Lp, kblk + 2))

    kern = functools.partial(_dsa_kernel, TQ=tq, KB=kblk, RB=rb, kb0=kb0, kstep=kstep, topk=topk,
                             far_bucket=far_bucket, tie_bits=int(math.ceil(math.log2(Lp))) + 1)
    n_kb = Lp // kblk
    hpg = N_HEADS_A // N_KV_A
    return pl.pallas_call(
        kern,
        grid=(B, n_qt),
        in_specs=[pl.BlockSpec(memory_space=pltpu.SMEM),
                  pl.BlockSpec((1, tq, D_ATTN), lambda b, i: (b, i, 0)),
                  pl.BlockSpec((1, tq, N_IDX_HEADS * IDX_DIM), lambda b, i: (b, i, 0)),
                  pl.BlockSpec((1, tq, LANES), lambda b, i: (b, i, 0)),
                  pl.BlockSpec((1, Lp, D_KV), lambda b, i: (b, 0, 0)),
                  pl.BlockSpec((1, Lp, D_KV), lambda b, i: (b, 0, 0)),
                  pl.BlockSpec((1, Lp, LANES), lambda b, i: (b, 0, 0)),
                  pl.BlockSpec((1, Lp, LANES), lambda b, i: (b, 0, 0)),
                  _const_spec((N_HEADS_A, tq, 2 * kblk)),
                  _const_spec((tq, 2 * kblk))],
        out_specs=pl.BlockSpec((1, tq, D_ATTN), lambda b, i: (b, i, 0)),
        out_shape=jax.ShapeDtypeStruct((B, T, D_ATTN), BF16),
        scratch_shapes=[pltpu.VMEM((n_kb, tq, kblk), jnp.int32),
                        pltpu.VMEM((n_kb, tq, kblk), F32),
                        pltpu.VMEM((N_IDX_HEADS, tq, LANES), F32),
                        pltpu.VMEM((N_KV_A, hpg * tq, HEAD_DIM_A), BF16),
                        pltpu.VMEM((rb, 1), F32),
                        pltpu.VMEM((N_KV_A, hpg * tq, 1), F32),
                        pltpu.VMEM((N_KV_A, hpg * tq, 1), F32),
                        pltpu.VMEM((N_KV_A, hpg * tq, HEAD_DIM_A), F32)],
        compiler_params=pltpu.CompilerParams(dimension_semantics=("arbitrary", "arbitrary"),
                                             vmem_limit_bytes=VMEM_LIMIT),
        name="dsa",
    )(rel_bias, q, qi, kiwi, kb, vb, kie, kio, bias_near, vis.astype(F32))


def _mix_kernel(x_ref, a_ref, u_ref, hist_ref, mk_ref, mv_ref, wpool_ref, pscale_ref, wout_ref, gx_ref,
                wxq_ref, wxo_ref, o_ref, ext, *, tm, pos0):
    t = pl.program_id(1)
    H = POOL_HIST + 1

    @pl.when(t == 0)
    def _():
        ext[0:H, :] = hist_ref[0]

    @pl.when(t > 0)
    def _():
        ext[0:H, :] = ext[tm:tm + H, :]

    ext[H:H + tm, :] = u_ref[0]

    pos = pos0 + t * tm + lax.broadcasted_iota(jnp.int32, (tm, 1), 0)
    pouts = []
    for gi, w in enumerate(POOL_WINDOWS):
        c0, c1 = gi * POOL_GROUP_DIM, (gi + 1) * POOL_GROUP_DIM
        wsum = ext[H:H + tm, c0:c1]
        for s in range(1, w):
            wsum = wsum + ext[H - s:H - s + tm, c0:c1]
        cnt = jnp.minimum(pos + 1, w).astype(F32)
        d = wsum / cnt - u_ref[0, :, c0:c1]
        y = _dot(d.astype(BF16), wpool_ref[gi]) * pscale_ref[:, c0:c1]
        pouts.append(y.astype(BF16))
    pcat = jnp.concatenate(pouts, axis=1)

    x1 = x_ref[0] + _dot(a_ref[0], wout_ref[0:D_ATTN, :]) + _dot(pcat, wout_ref[D_ATTN:D_ATTN + D_POOL, :])

    h = _rms(x1, gx_ref[...]).astype(BF16)
    qx = _dot(h, wxq_ref[...])
    outs = []
    for hh in range(N_XHEADS):
        hs = slice(hh * XHEAD_DIM, (hh + 1) * XHEAD_DIM)
        logits = _dot_nt(qx[:, hs].astype(BF16), mk_ref[0, :, hs]) * (XHEAD_DIM ** -0.5)
        m = jnp.max(logits, axis=1, keepdims=True)
        p = jnp.exp(logits - m)
        l = jnp.sum(p, axis=1, keepdims=True)
        outs.append((_dot(p.astype(BF16), mv_ref[0, :, hs]) / l).astype(BF16))
    o = jnp.concatenate(outs, axis=1)
    o_ref[0] = x1 + _dot(o, wxo_ref[...])


def _mix(x, a, u, hist, mk, mv, w_pool, pool_scale, w_out, g_x, w_xq, w_xo, *, pos0, tm):
    B, T, D = x.shape
    n_mem = mk.shape[1]
    H = POOL_HIST + 1
    assert T % tm == 0 and tm >= H
    hist16 = jnp.concatenate([jnp.zeros((B, 1, D_POOL), F32), hist], axis=1)
    return pl.pallas_call(
        functools.partial(_mix_kernel, tm=tm, pos0=pos0),
        grid=(B, T // tm),
        in_specs=[pl.BlockSpec((1, tm, D), lambda b, t: (b, t, 0)),
                  pl.BlockSpec((1, tm, D_ATTN), lambda b, t: (b, t, 0)),
                  pl.BlockSpec((1, tm, D_POOL), lambda b, t: (b, t, 0)),
                  pl.BlockSpec((1, H, D_POOL), lambda b, t: (b, 0, 0)),
                  pl.BlockSpec((1, n_mem, D_X), lambda b, t: (b, 0, 0)),
                  pl.BlockSpec((1, n_mem, D_X), lambda b, t: (b, 0, 0)),
                  _const_spec(w_pool.shape),
                  _const_spec((1, D_POOL)),
                  _const_spec(w_out.shape),
                  _const_spec((1, D)),
                  _const_spec(w_xq.shape),
                  _const_spec(w_xo.shape)],
        out_specs=pl.BlockSpec((1, tm, D), lambda b, t: (b, t, 0)),
        out_shape=jax.ShapeDtypeStruct((B, T, D), F32),
        scratch_shapes=[pltpu.VMEM((H + tm, D_POOL), F32)],
        compiler_params=pltpu.CompilerParams(dimension_semantics=("arbitrary", "arbitrary"),
                                             vmem_limit_bytes=VMEM_LIMIT),
        name="mix",
    )(x, a, u, hist16, mk, mv, w_pool, pool_scale.reshape(1, D_POOL), w_out, g_x.reshape(1, D), w_xq, w_xo)


R_OFF = N_GROUPS


def _route(logits):
    tm = logits.shape[0]
    lane = lax.broadcasted_iota(jnp.int32, (tm, LANES), 1).astype(F32)
    ninf = -jnp.inf
    big = float(LANES)
    gl = jnp.where(lane < N_GROUPS, logits, ninf)
    gmax = jnp.max(gl, axis=1, keepdims=True)
    g_sel = jnp.min(jnp.where(gl == gmax, lane, big), axis=1, keepdims=True)
    g_prob = 1.0 / jnp.sum(jnp.exp(gl - gmax), axis=1, keepdims=True)
    lo = R_OFF + g_sel * EXPERTS_PER_GROUP
    el = jnp.where((lane >= lo) & (lane < lo + EXPERTS_PER_GROUP), logits, ninf)
    tv0 = jnp.max(el, axis=1, keepdims=True)
    ti0 = jnp.min(jnp.where(el == tv0, lane, big), axis=1, keepdims=True)
    el2 = jnp.where(lane == ti0, ninf, el)
    tv1 = jnp.max(el2, axis=1, keepdims=True)
    ti1 = jnp.min(jnp.where(el2 == tv1, lane, big), axis=1, keepdims=True)
    e1 = jnp.exp(tv1 - tv0)
    den = 1.0 + e1
    w0 = g_prob / den
    w1 = g_prob * e1 / den
    return jnp.where(lane == ti0, w0, 0.0) + jnp.where(lane == ti1, w1, 0.0)


def _moe_kernel(x_ref, gffn_ref, wrh_ref, wrl_ref, br_ref, wg_ref, wu_ref, wd_ref, gfin_ref, o_ref,
                h_s, comb_s, acc_s):
    e = pl.program_id(1)

    @pl.when(e == 0)
    def _():
        h = _rms(x_ref[...], gffn_ref[...])
        h_hi = h.astype(BF16)
        h_lo = (h - h_hi.astype(F32)).astype(BF16)
        logits = (_dot(h_hi, wrh_ref[...]) + _dot(h_lo, wrh_ref[...]) + _dot(h_hi, wrl_ref[...])
                  + br_ref[...])
        comb_s[...] = _route(logits)
        h_s[...] = h_hi
        acc_s[...] = jnp.zeros(acc_s.shape, F32)

    hb = h_s[...]
    a = _dot(hb, wg_ref[0])
    b = _dot(hb, wu_ref[0])
    lane = lax.broadcasted_iota(jnp.int32, comb_s.shape, 1)
    cw = jnp.sum(jnp.where(lane == e + R_OFF, comb_s[...], 0.0), axis=1, keepdims=True)
    hid = a * jax.nn.sigmoid(a) * b * cw
    acc_s[...] += _dot(hid.astype(BF16), wd_ref[0])

    @pl.when(e == pl.num_programs(1) - 1)
    def _():
        o_ref[...] = _rms(x_ref[...] + acc_s[...], gfin_ref[...])


def _moe(x, g_ffn, wr_hi, wr_lo, b_r, w_gate, w_up, w_down, g_final, *, tm):
    n, d = x.shape
    assert n % tm == 0
    return pl.pallas_call(
        _moe_kernel,
        grid=(n // tm, N_EXPERTS),
        in_specs=[pl.BlockSpec((tm, d), lambda i, e: (i, 0)),
                  _const_spec((1, d)),
                  _const_spec(wr_hi.shape),
                  _const_spec(wr_lo.shape),
                  _const_spec((1, LANES)),
                  pl.BlockSpec((1, d, D_EXPERT), lambda i, e: (e, 0, 0)),
                  pl.BlockSpec((1, d, D_EXPERT), lambda i, e: (e, 0, 0)),
                  pl.BlockSpec((1, D_EXPERT, d), lambda i, e: (e, 0, 0)),
                  _const_spec((1, d))],
        out_specs=pl.BlockSpec((tm, d), lambda i, e: (i, 0)),
        out_shape=jax.ShapeDtypeStruct((n, d), F32),
        scratch_shapes=[pltpu.VMEM((tm, d), BF16),
                        pltpu.VMEM((tm, LANES), F32),
                        pltpu.VMEM((tm, d), F32)],
        compiler_params=pltpu.CompilerParams(dimension_semantics=("arbitrary", "arbitrary"),
                                             vmem_limit_bytes=VMEM_LIMIT),
        name="moe",
    )(x, g_ffn.reshape(1, d), wr_hi, wr_lo, b_r, w_gate, w_up, w_down, g_final.reshape(1, d))


C_Q = 0
C_QI = C_Q + D_ATTN
C_K = C_QI + N_IDX_HEADS * IDX_DIM
C_V = C_K + D_KV
C_KIWI = C_V + D_KV
C_KIE = C_KIWI + LANES
C_KIO = C_KIE + LANES
C_U = C_KIO + LANES
C_END = C_U + D_POOL


def _layout_w_in(w_in):
    d = w_in.shape[0]
    offs = np.cumsum((0, D_ATTN, D_KV, D_KV, N_IDX_HEADS * IDX_DIM, IDX_DIM, N_IDX_HEADS, D_POOL))
    wq, wk, wv, wqi, wki, wwi, wu = (w_in[:, offs[n]:offs[n + 1]] for n in range(7))
    z = lambda n: jnp.zeros((d, n), w_in.dtype)
    cat = jnp.concatenate([wq, wqi, wk, wv,
                           wki, wwi, z(LANES - IDX_DIM - N_IDX_HEADS),
                           wki, z(LANES - IDX_DIM),
                           z(LANES - IDX_DIM), wki,
                           wu], axis=1)
    assert cat.shape[1] == C_END
    return cat.astype(BF16)


IN_GROUPS = ((C_Q, C_QI, HEAD_DIM_A ** -0.5, 1),
             (C_QI, C_K, 1.0, 1),
             (C_K, C_V, 1.0, 2),
             (C_V, C_KIWI, 1.0, 2),
             (C_KIWI, C_KIE, 1.0, 1),
             (C_KIE, C_KIO, 1.0, 1),
             (C_KIO, C_U, 1.0, 1),
             (C_U, C_END, 1.0, 1))
IN_DTYPES = (BF16, BF16, F32, BF16, F32, BF16, F32, BF16, BF16, F32)


def _pad_rows(a, n):
    return jnp.pad(a, ((0, 0), (0, n - a.shape[1]), (0, 0)))


def _layer(x, pos0, caches, pool_hist, mk, mv, rel_bias, W, *, tm_in, tq, tm_mix, tm_moe):
    B, T, D = x.shape
    n = B * T
    q, qi, k, kb, v, vb, kiwi, kie, kio, u = _norm_matmul(
        x.reshape(n, D), W["g_mix"], W["w_in"], IN_GROUPS, IN_DTYPES, tm_in)
    r3 = lambda a: a.reshape(B, T, a.shape[-1])
    kblk = 256
    if caches is None:
        n_keys = T
        kb3, vb3, kie3, kio3 = r3(kb), r3(vb), r3(kie), r3(kio)
    else:
        k_hist, v_hist, ki_hist = caches
        past = k_hist.shape[1]
        n_keys = past + T
        lp = -(-n_keys // kblk) * kblk
        ki = r3(kiwi)[:, :, :IDX_DIM]
        ki_all = jnp.concatenate([ki_hist, ki], axis=1)
        zeros = jnp.zeros_like(ki_all)
        kb3 = _pad_rows(jnp.concatenate([k_hist.reshape(B, past, D_KV), r3(k)], axis=1), lp).astype(BF16)
        vb3 = _pad_rows(jnp.concatenate([v_hist.reshape(B, past, D_KV), r3(v)], axis=1), lp).astype(BF16)
        kie3 = _pad_rows(jnp.concatenate([ki_all, zeros], axis=2), lp).astype(BF16)
        kio3 = _pad_rows(jnp.concatenate([zeros, ki_all], axis=2), lp).astype(BF16)
    a_out = _dsa(r3(q), r3(qi), r3(kiwi), kb3, vb3, kie3, kio3, rel_bias,
                 n_keys=n_keys, pos0=pos0, tq=tq, kblk=kblk)
    x2 = _mix(x, a_out, r3(u), pool_hist, mk.astype(BF16), mv.astype(BF16), W["w_pool"], W["pool_scale"],
              W["w_out"], W["g_x"], W["w_xq"], W["w_xo"], pos0=pos0, tm=tm_mix)
    y = _moe(x2.reshape(n, D), W["g_ffn"], W["wr_hi"], W["wr_lo"], W["b_r"], W["w_gate"], W["w_up"],
             W["w_down"], W["g_final"], tm=tm_moe)
    new_pool = jnp.concatenate([pool_hist, r3(u)], axis=1)[:, -POOL_HIST:]
    return (y.reshape(B, T, D), k.reshape(B, T, N_KV_A, HEAD_DIM_A), v.reshape(B, T, N_KV_A, HEAD_DIM_A),
            r3(kiwi)[:, :, :IDX_DIM], new_pool)


def kernel(x_prompt, x_sample, mem_prompt, cache_k, cache_v, cache_k_idx, cache_pool, cache_mem_k, cache_mem_v, rel_bias, g_mix, w_in, w_pool, pool_scale, w_out, g_mem, w_mk, w_mv, g_x, w_xq, w_xo, g_ffn, w_rg, b_rg, w_re, b_re, w_gate, w_up, w_down, g_final):
    depth = g_mix.shape[0]
    assert depth == 1
    l = 0
    B, T, D = x_prompt.shape
    Bs, Ts, _ = x_sample.shape
    n_mem = mem_prompt.shape[1]
    past = cache_k.shape[2]

    w_r = jnp.concatenate([w_rg[l], w_re[l], jnp.zeros((D, LANES - N_GROUPS - N_EXPERTS), F32)], axis=1)
    wr_hi = w_r.astype(BF16)
    W = dict(
        g_mix=g_mix[l], w_in=_layout_w_in(w_in[l]), w_pool=w_pool[l].astype(BF16), pool_scale=pool_scale[l],
        w_out=w_out[l].astype(BF16), g_x=g_x[l], w_xq=w_xq[l].astype(BF16), w_xo=w_xo[l].astype(BF16),
        g_ffn=g_ffn[l], wr_hi=wr_hi, wr_lo=(w_r - wr_hi.astype(F32)).astype(BF16),
        b_r=jnp.concatenate([b_rg[l], b_re[l], jnp.zeros((LANES - N_GROUPS - N_EXPERTS,), F32)]).reshape(1, LANES),
        w_gate=w_gate[l].astype(BF16), w_up=w_up[l].astype(BF16), w_down=w_down[l].astype(BF16),
        g_final=g_final)

    w_m = jnp.concatenate([w_mk[l], w_mv[l]], axis=1).astype(BF16)
    mk, mv = _norm_matmul(mem_prompt.reshape(B * n_mem, D), g_mem[l], w_m,
                          ((0, D_X, 1.0, 1), (D_X, 2 * D_X, 1.0, 1)), (F32, F32), 256)
    mk = mk.reshape(B, n_mem, D_X)
    mv = mv.reshape(B, n_mem, D_X)

    yp, kp, vp, kip, pp = _layer(x_prompt, 0, None, jnp.zeros((B, POOL_HIST, D_POOL), F32), mk, mv, rel_bias, W,
                                 tm_in=256, tq=256, tm_mix=256, tm_moe=512)
    ys, ks, vs, kis, ps = _layer(x_sample, past,
                                 (cache_k[l], cache_v[l], cache_k_idx[l]), cache_pool[l],
                                 cache_mem_k[l].reshape(Bs, n_mem, D_X), cache_mem_v[l].reshape(Bs, n_mem, D_X),
                                 rel_bias, W, tm_in=Bs * Ts, tq=Ts, tm_mix=Ts, tm_moe=Bs * Ts)
    st = lambda a: a[None]
    return (yp, ys, st(kp), st(vp), st(kip), st(pp),
            st(mk.reshape(B, n_mem, N_XHEADS, XHEAD_DIM)), st(mv.reshape(B, n_mem, N_XHEADS, XHEAD_DIM)),
            st(ks), st(vs), st(kis), st(ps))
```

```python
import functools
import math

import numpy as np
import jax
import jax.numpy as jnp
from jax import lax
from jax.experimental import pallas as pl
from jax.experimental.pallas import tpu as pltpu

F32 = jnp.float32
BF16 = jnp.bfloat16

CHUNK = 64
N_HEADS_A = 8
HEAD_DIM_A = 128
N_KV_A = 2
D_ATTN = N_HEADS_A * HEAD_DIM_A
D_KV = N_KV_A * HEAD_DIM_A
N_IDX_HEADS = 8
IDX_DIM = 64
TOPK_MAX = 256
POOL_WINDOWS = (2, 4, 8, 16)
POOL_GROUP_DIM = 256
D_POOL = len(POOL_WINDOWS) * POOL_GROUP_DIM
POOL_HIST = max(POOL_WINDOWS) - 1
N_BUCKETS = 32
MAX_DISTANCE = 128
N_XHEADS = 4
XHEAD_DIM = 128
D_X = N_XHEADS * XHEAD_DIM
N_GROUPS = 4
EXPERTS_PER_GROUP = 8
N_EXPERTS = N_GROUPS * EXPERTS_PER_GROUP
D_EXPERT = 256
EPS = 1e-6

LANES = 128
INT_MIN = -(2 ** 31)
NEG = -1e30
LOG2E = math.log2(math.e)
VMEM_LIMIT = 48 * 1024 * 1024


def _rms(x, g):
    ms = jnp.mean(x * x, axis=-1, keepdims=True)
    return x * lax.rsqrt(ms + EPS) * g


def _dot(a, b):
    return jnp.dot(a, b, preferred_element_type=F32)


def _dot_nt(a, b):
    return lax.dot_general(a, b, (((1,), (1,)), ((), ())), preferred_element_type=F32)


def _const_spec(shape):
    nd = len(shape)
    return pl.BlockSpec(shape, lambda *_: (0,) * nd, pipeline_mode=pl.Buffered(1))


def _norm_matmul_kernel(x_ref, g_ref, w_ref, *out_refs, groups):
    h = _rms(x_ref[...], g_ref[...]).astype(BF16)
    k = 0
    for c0, c1, scale, n_out in groups:
        acc = _dot(h, w_ref[:, c0:c1])
        if scale != 1.0:
            acc = acc * scale
        for _ in range(n_out):
            out_refs[k][...] = acc.astype(out_refs[k].dtype)
            k += 1


def _norm_matmul(x, g, w, groups, out_dtypes, tm):
    n, d = x.shape
    assert n % tm == 0
    widths = []
    for c0, c1, _, n_out in groups:
        widths += [c1 - c0] * n_out
    return pl.pallas_call(
        functools.partial(_norm_matmul_kernel, groups=tuple(groups)),
        grid=(n // tm,),
        in_specs=[pl.BlockSpec((tm, d), lambda i: (i, 0)),
                  _const_spec((1, d)),
                  _const_spec(w.shape)],
        out_specs=[pl.BlockSpec((tm, wd), lambda i: (i, 0)) for wd in widths],
        out_shape=[jax.ShapeDtypeStruct((n, wd), dt) for wd, dt in zip(widths, out_dtypes)],
        compiler_params=pltpu.CompilerParams(dimension_semantics=("arbitrary",),
                                             vmem_limit_bytes=VMEM_LIMIT),
        name="norm_matmul",
    )(x, g.reshape(1, d), w)


def _bias_table_kernel(relb_ref, bucket_ref, o_ref):
    bucket = bucket_ref[...]
    for h in range(N_HEADS_A):
        acc = jnp.zeros(bucket.shape, F32)
        for b in range(N_BUCKETS):
            acc = jnp.where(bucket == b, relb_ref[b, h], acc)
        o_ref[h] = acc * LOG2E


def _bias_table(rel_bias, bucket):
    tq, w = bucket.shape
    return pl.pallas_call(
        _bias_table_kernel,
        in_specs=[pl.BlockSpec(memory_space=pltpu.SMEM),
                  pl.BlockSpec((tq, w), lambda: (0, 0))],
        out_specs=pl.BlockSpec((N_HEADS_A, tq, w), lambda: (0, 0, 0)),
        out_shape=jax.ShapeDtypeStruct((N_HEADS_A, tq, w), F32),
        name="bias_table",
    )(rel_bias, bucket)


def _t5_bucket(rel):
    nb = N_BUCKETS // 2
    max_exact = nb // 2
    bucket = (rel > 0).astype(jnp.int32) * nb
    n = jnp.abs(rel)
    nf = jnp.maximum(n, 1).astype(F32)
    large = max_exact + (jnp.log(nf / max_exact) / math.log(MAX_DISTANCE / max_exact)
                         * (nb - max_exact)).astype(jnp.int32)
    large = jnp.minimum(large, nb - 1)
    return bucket + jnp.where(n < max_exact, n, large)


def _far_bucket(min_dist, max_dist):
    nb = N_BUCKETS // 2
    max_exact = nb // 2
    n = np.arange(min_dist, max_dist + 1, dtype=np.float64)
    large = max_exact + np.floor(np.log(n / max_exact) / math.log(MAX_DISTANCE / max_exact)
                                 * (nb - max_exact) * (1 - 1e-6)).astype(np.int64)
    assert large.min() >= nb - 1, "far key blocks must sit in the saturated distance bucket"
    return nb - 1


def _dsa_kernel(relb_ref, q_ref, qi_ref, kiwi_ref, k_ref, v_ref, kie_ref, kio_ref, bnear_ref, vis_ref,
                o_ref, keys, madd, wb, qs, xcut, mrun, lrun, acc_s,
                *, TQ, KB, RB, kb0, kstep, topk, far_bucket, tie_bits):
    i = pl.program_id(1)
    nkb = kb0 + kstep * i
    n_far = jnp.maximum(nkb - 2, 0)
    n_lc = KB // LANES
    HPG = N_HEADS_A // N_KV_A

    kiwi = kiwi_ref[0]
    w_scale = (N_IDX_HEADS ** -0.5) * (IDX_DIM ** -0.5)
    for h in range(N_IDX_HEADS):
        wb[h] = jnp.broadcast_to(kiwi[:, IDX_DIM + h:IDX_DIM + h + 1] * w_scale, (TQ, LANES))
    for g in range(N_KV_A):
        for r in range(HPG):
            h = g * HPG + r
            qs[g, r * TQ:(r + 1) * TQ, :] = q_ref[0, :, h * HEAD_DIM_A:(h + 1) * HEAD_DIM_A]

    def idx_block(j, near_col):
        c0 = pl.multiple_of(j * KB, KB)
        ke = kie_ref[0, pl.ds(c0, KB), :]
        ko = kio_ref[0, pl.ds(c0, KB), :]
        acc = [jnp.zeros((TQ, LANES), F32) for _ in range(n_lc)]
        for hp in range(N_IDX_HEADS // 2):
            qh = qi_ref[0, :, hp * LANES:(hp + 1) * LANES]
            se = _dot_nt(qh, ke)
            so = _dot_nt(qh, ko)
            we = wb[2 * hp]
            wo = wb[2 * hp + 1]
            for c in range(n_lc):
                cs = slice(c * LANES, (c + 1) * LANES)
                acc[c] = acc[c] + we * jnp.maximum(se[:, cs], 0.0) + wo * jnp.maximum(so[:, cs], 0.0)
        for c in range(n_lc):
            bits = lax.bitcast_convert_type(acc[c], jnp.int32)
            key = bits ^ ((bits >> 31) & jnp.int32(0x7FFFFFFF))
            if near_col is not None:
                vis = vis_ref[:, near_col + c * LANES:near_col + (c + 1) * LANES]
                key = jnp.where(vis > 0.0, key, jnp.int32(INT_MIN))
            keys[j, :, c * LANES:(c + 1) * LANES] = key

    def far_idx(j, carry):
        idx_block(j, None)
        return carry

    lax.fori_loop(0, n_far, far_idx, 0)

    @pl.when(nkb >= 2)
    def _():
        idx_block(nkb - 2, 0)

    idx_block(nkb - 1, KB)

    kf = float(topk)
    lane_f = lax.broadcasted_iota(jnp.int32, (RB, LANES), 1).astype(F32)
    NRB = TQ // RB
    rbs = [slice(rb * RB, (rb + 1) * RB) for rb in range(NRB)]

    def count(pred):
        def body(jj, accs):
            out = []
            for rb in range(NRB):
                blk = keys[jj, rbs[rb], :]
                acc = accs[rb]
                for c in range(n_lc):
                    acc = acc + jnp.where(pred(rb, blk[:, c * LANES:(c + 1) * LANES], jj, c), 1.0, 0.0)
                out.append(acc)
            return tuple(out)
        accs = lax.fori_loop(0, nkb, body, tuple(jnp.zeros((RB, LANES), F32) for _ in range(NRB)))
        return [jnp.sum(acc, axis=1, keepdims=True) for acc in accs]

    def count_ge(cands):
        cb = [jnp.broadcast_to(cand, (RB, LANES)) for cand in cands]
        return count(lambda rb, kblk, jj, c: kblk >= cb[rb])

    def bis_body(p, thrs):
        cands = [thr + lax.shift_left(jnp.int32(1), 31 - p) for thr in thrs]
        cnts = count_ge(cands)
        return tuple(jnp.where(cnt >= kf, cand, thr) for cnt, cand, thr in zip(cnts, cands, thrs))

    thrs = lax.fori_loop(0, 32, bis_body, tuple(jnp.full((RB, 1), INT_MIN, jnp.int32) for _ in range(NRB)))
    thrs = [jnp.maximum(thr, jnp.int32(INT_MIN + 1)) for thr in thrs]
    cnt_ge = count_ge(thrs)
    cnt_gt = count_ge([thr + 1 for thr in thrs])
    needs = [kf - c for c in cnt_gt]
    thrb = [jnp.broadcast_to(thr, (RB, LANES)) for thr in thrs]

    xcut[...] = jnp.full(xcut.shape, 2.0 ** 24, F32)
    excess = cnt_ge[0] - kf
    for c in cnt_ge[1:]:
        excess = jnp.maximum(excess, c - kf)

    @pl.when(jnp.max(excess) > 0.0)
    def _():
        def count_eq_lt(xs):
            xb = [jnp.broadcast_to(x, (RB, LANES)) for x in xs]

            def pred(rb, kblk, jj, c):
                col = lane_f + (jj * KB + c * LANES).astype(F32)
                return jnp.where(kblk == thrb[rb], col, 2.0 ** 25) < xb[rb]
            return count(pred)

        def tie_body(p, xs):
            cands = [x + lax.shift_left(jnp.int32(1), tie_bits - 1 - p).astype(F32) for x in xs]
            cnts = count_eq_lt(cands)
            return tuple(jnp.where(cnt < need, cand, x) for cnt, need, cand, x in zip(cnts, needs, cands, xs))

        xs = lax.fori_loop(0, tie_bits, tie_body, tuple(jnp.zeros((RB, 1), F32) for _ in range(NRB)))
        for rb in range(NRB):
            xcut[rbs[rb], :] = xs[rb]

    xb = [jnp.broadcast_to(xcut[rbs[rb], :], (RB, LANES)) for rb in range(NRB)]

    def mask_body(jj, carry):
        for rb in range(NRB):
            blk = keys[jj, rbs[rb], :]
            for c in range(n_lc):
                kblk = blk[:, c * LANES:(c + 1) * LANES]
                col = lane_f + (jj * KB + c * LANES).astype(F32)
                keep = (kblk > thrb[rb]) | ((kblk == thrb[rb]) & (col <= xb[rb]))
                madd[jj, rbs[rb], c * LANES:(c + 1) * LANES] = jnp.where(keep, 0.0, NEG)
        return carry

    lax.fori_loop(0, nkb, mask_body, 0)

    def logits(j, near_col, g):
        c0 = pl.multiple_of(j * KB, KB)
        kg = k_ref[0, pl.ds(c0, KB), g * HEAD_DIM_A:(g + 1) * HEAD_DIM_A]
        s = _dot_nt(qs[g], kg)
        mk = madd[j]
        out = []
        for r in range(HPG):
            h = g * HPG + r
            if near_col is None:
                bias = relb_ref[far_bucket, h] * LOG2E
            else:
                bias = bnear_ref[h, :, near_col:near_col + KB]
            out.append(s[r * TQ:(r + 1) * TQ] + bias + mk)
        return out

    def over_blocks(fn):
        def far(j, carry):
            fn(j, None)
            return carry

        lax.fori_loop(0, n_far, far, 0)

        @pl.when(nkb >= 2)
        def _():
            fn(nkb - 2, 0)

        fn(nkb - 1, KB)

    mrun[...] = jnp.full(mrun.shape, NEG, F32)

    def max_block(j, near_col):
        for g in range(N_KV_A):
            for r, sh in enumerate(logits(j, near_col, g)):
                rs = slice(r * TQ, (r + 1) * TQ)
                m = mrun[g, rs, :]
                for c in range(n_lc):
                    m = jnp.maximum(m, sh[:, c * LANES:(c + 1) * LANES])
                mrun[g, rs, :] = m

    over_blocks(max_block)

    for g in range(N_KV_A):
        mrun[g] = jnp.broadcast_to(jnp.max(mrun[g], axis=1, keepdims=True), mrun.shape[1:])
    lrun[...] = jnp.zeros(lrun.shape, F32)
    acc_s[...] = jnp.zeros(acc_s.shape, F32)

    def pv_block(j, near_col):
        c0 = pl.multiple_of(j * KB, KB)
        for g in range(N_KV_A):
            vg = v_ref[0, pl.ds(c0, KB), g * HEAD_DIM_A:(g + 1) * HEAD_DIM_A]
            parts = []
            for r, sh in enumerate(logits(j, near_col, g)):
                rs = slice(r * TQ, (r + 1) * TQ)
                mb = mrun[g, rs, :]
                l = lrun[g, rs, :]
                pcs = []
                for c in range(n_lc):
                    p = jnp.exp2(sh[:, c * LANES:(c + 1) * LANES] - mb)
                    l = l + p
                    pcs.append(p.astype(BF16))
                lrun[g, rs, :] = l
                parts.append(jnp.concatenate(pcs, axis=1))
            acc_s[g] += _dot(jnp.concatenate(parts, axis=0), vg)

    over_blocks(pv_block)

    for g in range(N_KV_A):
        out = acc_s[g] / jnp.sum(lrun[g], axis=1, keepdims=True)
        for r in range(HPG):
            h = g * HPG + r
            o_ref[0, :, h * HEAD_DIM_A:(h + 1) * HEAD_DIM_A] = out[r * TQ:(r + 1) * TQ].astype(o_ref.dtype)


def _dsa(q, qi, kiwi, kb, vb, kie, kio, rel_bias, *, n_keys, pos0, tq, kblk):
    B, T, _ = q.shape
    Lp = kb.shape[1]
    assert T % tq == 0 and Lp % kblk == 0 and tq % CHUNK == 0 or T == tq
    n_qt = T // tq
    if n_qt == 1:
        kb0, kstep = Lp // kblk, 0
        assert ((pos0 + tq - 1) // CHUNK + 1) * CHUNK >= n_keys
    else:
        assert pos0 == 0 and tq == kblk and n_keys == Lp == T
        kb0, kstep = 1, 1
    topk = min(TOPK_MAX, n_keys // 4)
    rb = min(128, tq)

    t = jnp.arange(tq, dtype=jnp.int32)[:, None]
    c = jnp.arange(2 * kblk, dtype=jnp.int32)[None, :]
    if n_qt == 1:
        q_abs = pos0 + t
        s_abs = (kb0 - 2) * kblk + c
    else:
        q_abs = kblk + t
        s_abs = c
    vis = ((s_abs // CHUNK) <= (q_abs // CHUNK)) & (s_abs < (n_keys if n_qt == 1 else 2 * kblk))
    bias_near = _bias_table(rel_bias, _t5_bucket(s_abs - q_abs))
    far_bucket = _far_bucket(kblk + 1, max(Lp, kblk + 2))

    kern = functools.partial(_dsa_kernel, TQ=tq, KB=kblk, RB=rb, kb0=kb0, kstep=kstep, topk=topk,
                             far_bucket=far_bucket, tie_bits=int(math.ceil(math.log2(Lp))) + 1)
    n_kb = Lp // kblk
    hpg = N_HEADS_A // N_KV_A
    return pl.pallas_call(
        kern,
        grid=(B, n_qt),
        in_specs=[pl.BlockSpec(memory_space=pltpu.SMEM),
                  pl.BlockSpec((1, tq, D_ATTN), lambda b, i: (b, i, 0)),
                  pl.BlockSpec((1, tq, N_IDX_HEADS * IDX_DIM), lambda b, i: (b, i, 0)),
                  pl.BlockSpec((1, tq, LANES), lambda b, i: (b, i, 0)),
                  pl.BlockSpec((1, Lp, D_KV), lambda b, i: (b, 0, 0)),
                  pl.BlockSpec((1, Lp, D_KV), lambda b, i: (b, 0, 0)),
                  pl.BlockSpec((1, Lp, LANES), lambda b, i: (b, 0, 0)),
                  pl.BlockSpec((1, Lp, LANES), lambda b, i: (b, 0, 0)),
                  _const_spec((N_HEADS_A, tq, 2 * kblk)),
                  _const_spec((tq, 2 * kblk))],
        out_specs=pl.BlockSpec((1, tq, D_ATTN), lambda b, i: (b, i, 0)),
        out_shape=jax.ShapeDtypeStruct((B, T, D_ATTN), BF16),
        scratch_shapes=[pltpu.VMEM((n_kb, tq, kblk), jnp.int32),
                        pltpu.VMEM((n_kb, tq, kblk), F32),
                        pltpu.VMEM((N_IDX_HEADS, tq, LANES), F32),
                        pltpu.VMEM((N_KV_A, hpg * tq, HEAD_DIM_A), BF16),
                        pltpu.VMEM((tq, 1), F32),
                        pltpu.VMEM((N_KV_A, hpg * tq, LANES), F32),
                        pltpu.VMEM((N_KV_A, hpg * tq, LANES), F32),
                        pltpu.VMEM((N_KV_A, hpg * tq, HEAD_DIM_A), F32)],
        compiler_params=pltpu.CompilerParams(dimension_semantics=("arbitrary", "arbitrary"),
                                             vmem_limit_bytes=VMEM_LIMIT),
        name="dsa",
    )(rel_bias, q, qi, kiwi, kb, vb, kie, kio, bias_near, vis.astype(F32))


def _mix_kernel(x_ref, a_ref, u_ref, hist_ref, mk_ref, mv_ref, wpool_ref, pscale_ref, wout_ref, gx_ref,
                wxq_ref, wxo_ref, o_ref, ext, *, tm, pos0):
    t = pl.program_id(1)
    H = POOL_HIST + 1

    @pl.when(t == 0)
    def _():
        ext[0:H, :] = hist_ref[0]

    @pl.when(t > 0)
    def _():
        ext[0:H, :] = ext[tm:tm + H, :]

    ext[H:H + tm, :] = u_ref[0]

    pos = pos0 + t * tm + lax.broadcasted_iota(jnp.int32, (tm, 1), 0)
    pouts = []
    for gi, w in enumerate(POOL_WINDOWS):
        c0, c1 = gi * POOL_GROUP_DIM, (gi + 1) * POOL_GROUP_DIM
        wsum = ext[H:H + tm, c0:c1]
        for s in range(1, w):
            wsum = wsum + ext[H - s:H - s + tm, c0:c1]
        cnt = jnp.minimum(pos + 1, w).astype(F32)
        d = wsum / cnt - u_ref[0, :, c0:c1]
        y = _dot(d.astype(BF16), wpool_ref[gi]) * pscale_ref[:, c0:c1]
        pouts.append(y.astype(BF16))
    pcat = jnp.concatenate(pouts, axis=1)

    x1 = x_ref[0] + _dot(a_ref[0], wout_ref[0:D_ATTN, :]) + _dot(pcat, wout_ref[D_ATTN:D_ATTN + D_POOL, :])

    h = _rms(x1, gx_ref[...]).astype(BF16)
    qx = _dot(h, wxq_ref[...])
    outs = []
    for hh in range(N_XHEADS):
        hs = slice(hh * XHEAD_DIM, (hh + 1) * XHEAD_DIM)
        logits = _dot_nt(qx[:, hs].astype(BF16), mk_ref[0, :, hs]) * (XHEAD_DIM ** -0.5)
        m = jnp.max(logits, axis=1, keepdims=True)
        p = jnp.exp(logits - m)
        l = jnp.sum(p, axis=1, keepdims=True)
        outs.append((_dot(p.astype(BF16), mv_ref[0, :, hs]) / l).astype(BF16))
    o = jnp.concatenate(outs, axis=1)
    o_ref[0] = x1 + _dot(o, wxo_ref[...])


def _mix(x, a, u, hist, mk, mv, w_pool, pool_scale, w_out, g_x, w_xq, w_xo, *, pos0, tm):
    B, T, D = x.shape
    n_mem = mk.shape[1]
    H = POOL_HIST + 1
    assert T % tm == 0 and tm >= H
    hist16 = jnp.concatenate([jnp.zeros((B, 1, D_POOL), F32), hist], axis=1)
    return pl.pallas_call(
        functools.partial(_mix_kernel, tm=tm, pos0=pos0),
        grid=(B, T // tm),
        in_specs=[pl.BlockSpec((1, tm, D), lambda b, t: (b, t, 0)),
                  pl.BlockSpec((1, tm, D_ATTN), lambda b, t: (b, t, 0)),
                  pl.BlockSpec((1, tm, D_POOL), lambda b, t: (b, t, 0)),
                  pl.BlockSpec((1, H, D_POOL), lambda b, t: (b, 0, 0)),
                  pl.BlockSpec((1, n_mem, D_X), lambda b, t: (b, 0, 0)),
                  pl.BlockSpec((1, n_mem, D_X), lambda b, t: (b, 0, 0)),
                  _const_spec(w_pool.shape),
                  _const_spec((1, D_POOL)),
                  _const_spec(w_out.shape),
                  _const_spec((1, D)),
                  _const_spec(w_xq.shape),
                  _const_spec(w_xo.shape)],
        out_specs=pl.BlockSpec((1, tm, D), lambda b, t: (b, t, 0)),
        out_shape=jax.ShapeDtypeStruct((B, T, D), F32),
        scratch_shapes=[pltpu.VMEM((H + tm, D_POOL), F32)],
        compiler_params=pltpu.CompilerParams(dimension_semantics=("arbitrary", "arbitrary"),
                                             vmem_limit_bytes=VMEM_LIMIT),
        name="mix",
    )(x, a, u, hist16, mk, mv, w_pool, pool_scale.reshape(1, D_POOL), w_out, g_x.reshape(1, D), w_xq, w_xo)


R_OFF = N_GROUPS


def _route(logits):
    tm = logits.shape[0]
    lane = lax.broadcasted_iota(jnp.int32, (tm, LANES), 1).astype(F32)
    ninf = -jnp.inf
    big = float(LANES)
    gl = jnp.where(lane < N_GROUPS, logits, ninf)
    gmax = jnp.max(gl, axis=1, keepdims=True)
    g_sel = jnp.min(jnp.where(gl == gmax, lane, big), axis=1, keepdims=True)
    g_prob = 1.0 / jnp.sum(jnp.exp(gl - gmax), axis=1, keepdims=True)
    lo = R_OFF + g_sel * EXPERTS_PER_GROUP
    el = jnp.where((lane >= lo) & (lane < lo + EXPERTS_PER_GROUP), logits, ninf)
    tv0 = jnp.max(el, axis=1, keepdims=True)
    ti0 = jnp.min(jnp.where(el == tv0, lane, big), axis=1, keepdims=True)
    el2 = jnp.where(lane == ti0, ninf, el)
    tv1 = jnp.max(el2, axis=1, keepdims=True)
    ti1 = jnp.min(jnp.where(el2 == tv1, lane, big), axis=1, keepdims=True)
    e1 = jnp.exp(tv1 - tv0)
    den = 1.0 + e1
    w0 = g_prob / den
    w1 = g_prob * e1 / den
    return jnp.where(lane == ti0, w0, 0.0) + jnp.where(lane == ti1, w1, 0.0)


def _moe_kernel(x_ref, gffn_ref, wrh_ref, wrl_ref, br_ref, wg_ref, wu_ref, wd_ref, gfin_ref, o_ref,
                h_s, comb_s, acc_s):
    e = pl.program_id(1)

    @pl.when(e == 0)
    def _():
        h = _rms(x_ref[...], gffn_ref[...])
        h_hi = h.astype(BF16)
        h_lo = (h - h_hi.astype(F32)).astype(BF16)
        logits = (_dot(h_hi, wrh_ref[...]) + _dot(h_lo, wrh_ref[...]) + _dot(h_hi, wrl_ref[...])
                  + br_ref[...])
        comb_s[...] = _route(logits)
        h_s[...] = h_hi
        acc_s[...] = jnp.zeros(acc_s.shape, F32)

    hb = h_s[...]
    a = _dot(hb, wg_ref[0])
    b = _dot(hb, wu_ref[0])
    lane = lax.broadcasted_iota(jnp.int32, comb_s.shape, 1)
    cw = jnp.sum(jnp.where(lane == e + R_OFF, comb_s[...], 0.0), axis=1, keepdims=True)
    hid = a * jax.nn.sigmoid(a) * b * cw
    acc_s[...] += _dot(hid.astype(BF16), wd_ref[0])

    @pl.when(e == pl.num_programs(1) - 1)
    def _():
        o_ref[...] = _rms(x_ref[...] + acc_s[...], gfin_ref[...])


def _moe(x, g_ffn, wr_hi, wr_lo, b_r, w_gate, w_up, w_down, g_final, *, tm):
    n, d = x.shape
    assert n % tm == 0
    return pl.pallas_call(
        _moe_kernel,
        grid=(n // tm, N_EXPERTS),
        in_specs=[pl.BlockSpec((tm, d), lambda i, e: (i, 0)),
                  _const_spec((1, d)),
                  _const_spec(wr_hi.shape),
                  _const_spec(wr_lo.shape),
                  _const_spec((1, LANES)),
                  pl.BlockSpec((1, d, D_EXPERT), lambda i, e: (e, 0, 0)),
                  pl.BlockSpec((1, d, D_EXPERT), lambda i, e: (e, 0, 0)),
                  pl.BlockSpec((1, D_EXPERT, d), lambda i, e: (e, 0, 0)),
                  _const_spec((1, d))],
        out_specs=pl.BlockSpec((tm, d), lambda i, e: (i, 0)),
        out_shape=jax.ShapeDtypeStruct((n, d), F32),
        scratch_shapes=[pltpu.VMEM((tm, d), BF16),
                        pltpu.VMEM((tm, LANES), F32),
                        pltpu.VMEM((tm, d), F32)],
        compiler_params=pltpu.CompilerParams(dimension_semantics=("arbitrary", "arbitrary"),
                                             vmem_limit_bytes=VMEM_LIMIT),
        name="moe",
    )(x, g_ffn.reshape(1, d), wr_hi, wr_lo, b_r, w_gate, w_up, w_down, g_final.reshape(1, d))


C_Q = 0
C_QI = C_Q + D_ATTN
C_K = C_QI + N_IDX_HEADS * IDX_DIM
C_V = C_K + D_KV
C_KIWI = C_V + D_KV
C_KIE = C_KIWI + LANES
C_KIO = C_KIE + LANES
C_U = C_KIO + LANES
C_END = C_U + D_POOL


def _layout_w_in(w_in):
    d = w_in.shape[0]
    offs = np.cumsum((0, D_ATTN, D_KV, D_KV, N_IDX_HEADS * IDX_DIM, IDX_DIM, N_IDX_HEADS, D_POOL))
    wq, wk, wv, wqi, wki, wwi, wu = (w_in[:, offs[n]:offs[n + 1]] for n in range(7))
    z = lambda n: jnp.zeros((d, n), w_in.dtype)
    cat = jnp.concatenate([wq, wqi, wk, wv,
                           wki, wwi, z(LANES - IDX_DIM - N_IDX_HEADS),
                           wki, z(LANES - IDX_DIM),
                           z(LANES - IDX_DIM), wki,
                           wu], axis=1)
    assert cat.shape[1] == C_END
    return cat.astype(BF16)


IN_GROUPS = ((C_Q, C_QI, HEAD_DIM_A ** -0.5 * LOG2E, 1),
             (C_QI, C_K, 1.0, 1),
             (C_K, C_V, 1.0, 2),
             (C_V, C_KIWI, 1.0, 2),
             (C_KIWI, C_KIE, 1.0, 1),
             (C_KIE, C_KIO, 1.0, 1),
             (C_KIO, C_U, 1.0, 1),
             (C_U, C_END, 1.0, 1))
IN_DTYPES = (BF16, BF16, F32, BF16, F32, BF16, F32, BF16, BF16, F32)


def _pad_rows(a, n):
    return jnp.pad(a, ((0, 0), (0, n - a.shape[1]), (0, 0)))


def _layer(x, pos0, caches, pool_hist, mk, mv, rel_bias, W, *, tm_in, tq, tm_mix, tm_moe):
    B, T, D = x.shape
    n = B * T
    q, qi, k, kb, v, vb, kiwi, kie, kio, u = _norm_matmul(
        x.reshape(n, D), W["g_mix"], W["w_in"], IN_GROUPS, IN_DTYPES, tm_in)
    r3 = lambda a: a.reshape(B, T, a.shape[-1])
    kblk = 256
    if caches is None:
        n_keys = T
        kb3, vb3, kie3, kio3 = r3(kb), r3(vb), r3(kie), r3(kio)
    else:
        k_hist, v_hist, ki_hist = caches
        past = k_hist.shape[1]
        n_keys = past + T
        lp = -(-n_keys // kblk) * kblk
        ki = r3(kiwi)[:, :, :IDX_DIM]
        ki_all = jnp.concatenate([ki_hist, ki], axis=1)
        zeros = jnp.zeros_like(ki_all)
        kb3 = _pad_rows(jnp.concatenate([k_hist.reshape(B, past, D_KV), r3(k)], axis=1), lp).astype(BF16)
        vb3 = _pad_rows(jnp.concatenate([v_hist.reshape(B, past, D_KV), r3(v)], axis=1), lp).astype(BF16)
        kie3 = _pad_rows(jnp.concatenate([ki_all, zeros], axis=2), lp).astype(BF16)
        kio3 = _pad_rows(jnp.concatenate([zeros, ki_all], axis=2), lp).astype(BF16)
    a_out = _dsa(r3(q), r3(qi), r3(kiwi), kb3, vb3, kie3, kio3, rel_bias,
                 n_keys=n_keys, pos0=pos0, tq=tq, kblk=kblk)
    x2 = _mix(x, a_out, r3(u), pool_hist, mk.astype(BF16), mv.astype(BF16), W["w_pool"], W["pool_scale"],
              W["w_out"], W["g_x"], W["w_xq"], W["w_xo"], pos0=pos0, tm=tm_mix)
    y = _moe(x2.reshape(n, D), W["g_ffn"], W["wr_hi"], W["wr_lo"], W["b_r"], W["w_gate"], W["w_up"],
             W["w_down"], W["g_final"], tm=tm_moe)
    new_pool = jnp.concatenate([pool_hist, r3(u)], axis=1)[:, -POOL_HIST:]
    return (y.reshape(B, T, D), k.reshape(B, T, N_KV_A, HEAD_DIM_A), v.reshape(B, T, N_KV_A, HEAD_DIM_A),
            r3(kiwi)[:, :, :IDX_DIM], new_pool)


def kernel(x_prompt, x_sample, mem_prompt, cache_k, cache_v, cache_k_idx, cache_pool, cache_mem_k, cache_mem_v, rel_bias, g_mix, w_in, w_pool, pool_scale, w_out, g_mem, w_mk, w_mv, g_x, w_xq, w_xo, g_ffn, w_rg, b_rg, w_re, b_re, w_gate, w_up, w_down, g_final):
    depth = g_mix.shape[0]
    assert depth == 1
    l = 0
    B, T, D = x_prompt.shape
    Bs, Ts, _ = x_sample.shape
    n_mem = mem_prompt.shape[1]
    past = cache_k.shape[2]

    w_r = jnp.concatenate([w_rg[l], w_re[l], jnp.zeros((D, LANES - N_GROUPS - N_EXPERTS), F32)], axis=1)
    wr_hi = w_r.astype(BF16)
    W = dict(
        g_mix=g_mix[l], w_in=_layout_w_in(w_in[l]), w_pool=w_pool[l].astype(BF16), pool_scale=pool_scale[l],
        w_out=w_out[l].astype(BF16), g_x=g_x[l], w_xq=w_xq[l].astype(BF16), w_xo=w_xo[l].astype(BF16),
        g_ffn=g_ffn[l], wr_hi=wr_hi, wr_lo=(w_r - wr_hi.astype(F32)).astype(BF16),
        b_r=jnp.concatenate([b_rg[l], b_re[l], jnp.zeros((LANES - N_GROUPS - N_EXPERTS,), F32)]).reshape(1, LANES),
        w_gate=w_gate[l].astype(BF16), w_up=w_up[l].astype(BF16), w_down=w_down[l].astype(BF16),
        g_final=g_final)

    w_m = jnp.concatenate([w_mk[l], w_mv[l]], axis=1).astype(BF16)
    mk, mv = _norm_matmul(mem_prompt.reshape(B * n_mem, D), g_mem[l], w_m,
                          ((0, D_X, 1.0, 1), (D_X, 2 * D_X, 1.0, 1)), (F32, F32), 256)
    mk = mk.reshape(B, n_mem, D_X)
    mv = mv.reshape(B, n_mem, D_X)

    yp, kp, vp, kip, pp = _layer(x_prompt, 0, None, jnp.zeros((B, POOL_HIST, D_POOL), F32), mk, mv, rel_bias, W,
                                 tm_in=256, tq=256, tm_mix=256, tm_moe=512)
    ys, ks, vs, kis, ps = _layer(x_sample, past,
                                 (cache_k[l], cache_v[l], cache_k_idx[l]), cache_pool[l],
                                 cache_mem_k[l].reshape(Bs, n_mem, D_X), cache_mem_v[l].reshape(Bs, n_mem, D_X),
                                 rel_bias, W, tm_in=Bs * Ts, tq=Ts, tm_mix=Ts, tm_moe=Bs * Ts)
    st = lambda a: a[None]
    return (yp, ys, st(kp), st(vp), st(kip), st(pp),
            st(mk.reshape(B, n_mem, N_XHEADS, XHEAD_DIM)), st(mv.reshape(B, n_mem, N_XHEADS, XHEAD_DIM)),
            st(ks), st(vs), st(kis), st(ps))
```

```python
import functools
import math

import numpy as np
import jax
import jax.numpy as jnp
from jax import lax
from jax.experimental import pallas as pl
from jax.experimental.pallas import tpu as pltpu

F32 = jnp.float32
BF16 = jnp.bfloat16

CHUNK = 64
N_HEADS_A = 8
HEAD_DIM_A = 128
N_KV_A = 2
D_ATTN = N_HEADS_A * HEAD_DIM_A
D_KV = N_KV_A * HEAD_DIM_A
N_IDX_HEADS = 8
IDX_DIM = 64
TOPK_MAX = 256
POOL_WINDOWS = (2, 4, 8, 16)
POOL_GROUP_DIM = 256
D_POOL = len(POOL_WINDOWS) * POOL_GROUP_DIM
POOL_HIST = max(POOL_WINDOWS) - 1
N_BUCKETS = 32
MAX_DISTANCE = 128
N_XHEADS = 4
XHEAD_DIM = 128
D_X = N_XHEADS * XHEAD_DIM
N_GROUPS = 4
EXPERTS_PER_GROUP = 8
N_EXPERTS = N_GROUPS * EXPERTS_PER_GROUP
D_EXPERT = 256
EPS = 1e-6

LANES = 128
INT_MIN = -(2 ** 31)
NEG = -1e30
LOG2E = math.log2(math.e)
VMEM_LIMIT = 48 * 1024 * 1024


def _rms(x, g):
    ms = jnp.mean(x * x, axis=-1, keepdims=True)
    return x * lax.rsqrt(ms + EPS) * g


def _dot(a, b):
    return jnp.dot(a, b, preferred_element_type=F32)


def _dot_nt(a, b):
    return lax.dot_general(a, b, (((1,), (1,)), ((), ())), preferred_element_type=F32)


def _const_spec(shape):
    nd = len(shape)
    return pl.BlockSpec(shape, lambda *_: (0,) * nd, pipeline_mode=pl.Buffered(1))


def _norm_matmul_kernel(x_ref, g_ref, w_ref, *out_refs, groups):
    h = _rms(x_ref[...], g_ref[...]).astype(BF16)
    k = 0
    for c0, c1, scale, n_out in groups:
        acc = _dot(h, w_ref[:, c0:c1])
        if scale != 1.0:
            acc = acc * scale
        for _ in range(n_out):
            out_refs[k][...] = acc.astype(out_refs[k].dtype)
            k += 1


def _norm_matmul(x, g, w, groups, out_dtypes, tm):
    n, d = x.shape
    assert n % tm == 0
    widths = []
    for c0, c1, _, n_out in groups:
        widths += [c1 - c0] * n_out
    return pl.pallas_call(
        functools.partial(_norm_matmul_kernel, groups=tuple(groups)),
        grid=(n // tm,),
        in_specs=[pl.BlockSpec((tm, d), lambda i: (i, 0)),
                  _const_spec((1, d)),
                  _const_spec(w.shape)],
        out_specs=[pl.BlockSpec((tm, wd), lambda i: (i, 0)) for wd in widths],
        out_shape=[jax.ShapeDtypeStruct((n, wd), dt) for wd, dt in zip(widths, out_dtypes)],
        compiler_params=pltpu.CompilerParams(dimension_semantics=("arbitrary",),
                                             vmem_limit_bytes=VMEM_LIMIT),
        name="norm_matmul",
    )(x, g.reshape(1, d), w)


def _bias_table_kernel(relb_ref, bucket_ref, o_ref):
    bucket = bucket_ref[...]
    for h in range(N_HEADS_A):
        acc = jnp.zeros(bucket.shape, F32)
        for b in range(N_BUCKETS):
            acc = jnp.where(bucket == b, relb_ref[b, h], acc)
        o_ref[h] = acc * LOG2E


def _bias_table(rel_bias, bucket):
    tq, w = bucket.shape
    return pl.pallas_call(
        _bias_table_kernel,
        in_specs=[pl.BlockSpec(memory_space=pltpu.SMEM),
                  pl.BlockSpec((tq, w), lambda: (0, 0))],
        out_specs=pl.BlockSpec((N_HEADS_A, tq, w), lambda: (0, 0, 0)),
        out_shape=jax.ShapeDtypeStruct((N_HEADS_A, tq, w), F32),
        name="bias_table",
    )(rel_bias, bucket)


def _t5_bucket(rel):
    nb = N_BUCKETS // 2
    max_exact = nb // 2
    bucket = (rel > 0).astype(jnp.int32) * nb
    n = jnp.abs(rel)
    nf = jnp.maximum(n, 1).astype(F32)
    large = max_exact + (jnp.log(nf / max_exact) / math.log(MAX_DISTANCE / max_exact)
                         * (nb - max_exact)).astype(jnp.int32)
    large = jnp.minimum(large, nb - 1)
    return bucket + jnp.where(n < max_exact, n, large)


def _far_bucket(min_dist, max_dist):
    nb = N_BUCKETS // 2
    max_exact = nb // 2
    n = np.arange(min_dist, max_dist + 1, dtype=np.float64)
    large = max_exact + np.floor(np.log(n / max_exact) / math.log(MAX_DISTANCE / max_exact)
                                 * (nb - max_exact) * (1 - 1e-6)).astype(np.int64)
    assert large.min() >= nb - 1, "far key blocks must sit in the saturated distance bucket"
    return nb - 1


def _dsa_kernel(relb_ref, q_ref, qi_ref, kiwi_ref, k_ref, v_ref, kie_ref, kio_ref, bnear_ref, vis_ref,
                o_ref, keys, madd, wb, qs, xcut, mrun, lrun, acc_s,
                *, TQ, KB, RB, kb0, kstep, topk, far_bucket, tie_bits):
    i = pl.program_id(1)
    nkb = kb0 + kstep * i
    n_far = jnp.maximum(nkb - 2, 0)
    n_lc = KB // LANES
    HPG = N_HEADS_A // N_KV_A

    kiwi = kiwi_ref[0]
    for h in range(N_IDX_HEADS):
        w = (kiwi[:, IDX_DIM + h:IDX_DIM + h + 1] * (N_IDX_HEADS ** -0.5)).astype(BF16).astype(F32)
        wb[h] = jnp.broadcast_to(w * (IDX_DIM ** -0.5), (TQ, LANES))
    for g in range(N_KV_A):
        for r in range(HPG):
            h = g * HPG + r
            qs[g, r * TQ:(r + 1) * TQ, :] = q_ref[0, :, h * HEAD_DIM_A:(h + 1) * HEAD_DIM_A]

    def idx_block(j, near_col):
        c0 = pl.multiple_of(j * KB, KB)
        ke = kie_ref[0, pl.ds(c0, KB), :]
        ko = kio_ref[0, pl.ds(c0, KB), :]
        acc = [jnp.zeros((TQ, LANES), F32) for _ in range(n_lc)]
        for hp in range(N_IDX_HEADS // 2):
            qh = qi_ref[0, :, hp * LANES:(hp + 1) * LANES]
            se = _dot_nt(qh, ke)
            so = _dot_nt(qh, ko)
            we = wb[2 * hp]
            wo = wb[2 * hp + 1]
            for c in range(n_lc):
                cs = slice(c * LANES, (c + 1) * LANES)
                re = jnp.maximum(se[:, cs], 0.0).astype(BF16).astype(F32)
                ro = jnp.maximum(so[:, cs], 0.0).astype(BF16).astype(F32)
                acc[c] = acc[c] + we * re + wo * ro
        for c in range(n_lc):
            bits = lax.bitcast_convert_type(acc[c], jnp.int32)
            key = bits ^ ((bits >> 31) & jnp.int32(0x7FFFFFFF))
            if near_col is not None:
                vis = vis_ref[:, near_col + c * LANES:near_col + (c + 1) * LANES]
                key = jnp.where(vis > 0.0, key, jnp.int32(INT_MIN))
            keys[j, :, c * LANES:(c + 1) * LANES] = key

    def far_idx(j, carry):
        idx_block(j, None)
        return carry

    lax.fori_loop(0, n_far, far_idx, 0)

    @pl.when(nkb >= 2)
    def _():
        idx_block(nkb - 2, 0)

    idx_block(nkb - 1, KB)

    kf = float(topk)
    lane_f = lax.broadcasted_iota(jnp.int32, (RB, LANES), 1).astype(F32)
    NRB = TQ // RB
    rbs = [slice(rb * RB, (rb + 1) * RB) for rb in range(NRB)]

    def count(pred):
        def body(jj, accs):
            out = []
            for rb in range(NRB):
                blk = keys[jj, rbs[rb], :]
                acc = accs[rb]
                for c in range(n_lc):
                    acc = acc + jnp.where(pred(rb, blk[:, c * LANES:(c + 1) * LANES], jj, c), 1.0, 0.0)
                out.append(acc)
            return tuple(out)
        accs = lax.fori_loop(0, nkb, body, tuple(jnp.zeros((RB, LANES), F32) for _ in range(NRB)))
        return [jnp.sum(acc, axis=1, keepdims=True) for acc in accs]

    def count_ge(cands):
        cb = [jnp.broadcast_to(cand, (RB, LANES)) for cand in cands]
        return count(lambda rb, kblk, jj, c: kblk >= cb[rb])

    def bis_body(p, thrs):
        cands = [thr + lax.shift_left(jnp.int32(1), 31 - p) for thr in thrs]
        cnts = count_ge(cands)
        return tuple(jnp.where(cnt >= kf, cand, thr) for cnt, cand, thr in zip(cnts, cands, thrs))

    thrs = lax.fori_loop(0, 32, bis_body, tuple(jnp.full((RB, 1), INT_MIN, jnp.int32) for _ in range(NRB)))
    thrs = [jnp.maximum(thr, jnp.int32(INT_MIN + 1)) for thr in thrs]
    cnt_ge = count_ge(thrs)
    cnt_gt = count_ge([thr + 1 for thr in thrs])
    needs = [kf - c for c in cnt_gt]
    thrb = [jnp.broadcast_to(thr, (RB, LANES)) for thr in thrs]

    xcut[...] = jnp.full(xcut.shape, 2.0 ** 24, F32)
    excess = cnt_ge[0] - kf
    for c in cnt_ge[1:]:
        excess = jnp.maximum(excess, c - kf)

    @pl.when(jnp.max(excess) > 0.0)
    def _():
        def count_eq_lt(xs):
            xb = [jnp.broadcast_to(x, (RB, LANES)) for x in xs]

            def pred(rb, kblk, jj, c):
                col = lane_f + (jj * KB + c * LANES).astype(F32)
                return jnp.where(kblk == thrb[rb], col, 2.0 ** 25) < xb[rb]
            return count(pred)

        def tie_body(p, xs):
            cands = [x + lax.shift_left(jnp.int32(1), tie_bits - 1 - p).astype(F32) for x in xs]
            cnts = count_eq_lt(cands)
            return tuple(jnp.where(cnt < need, cand, x) for cnt, need, cand, x in zip(cnts, needs, cands, xs))

        xs = lax.fori_loop(0, tie_bits, tie_body, tuple(jnp.zeros((RB, 1), F32) for _ in range(NRB)))
        for rb in range(NRB):
            xcut[rbs[rb], :] = xs[rb]

    xb = [jnp.broadcast_to(xcut[rbs[rb], :], (RB, LANES)) for rb in range(NRB)]

    def mask_body(jj, carry):
        for rb in range(NRB):
            blk = keys[jj, rbs[rb], :]
            for c in range(n_lc):
                kblk = blk[:, c * LANES:(c + 1) * LANES]
                col = lane_f + (jj * KB + c * LANES).astype(F32)
                keep = (kblk > thrb[rb]) | ((kblk == thrb[rb]) & (col <= xb[rb]))
                madd[jj, rbs[rb], c * LANES:(c + 1) * LANES] = jnp.where(keep, 0.0, NEG)
        return carry

    lax.fori_loop(0, nkb, mask_body, 0)

    def logits(j, near_col, g):
        c0 = pl.multiple_of(j * KB, KB)
        kg = k_ref[0, pl.ds(c0, KB), g * HEAD_DIM_A:(g + 1) * HEAD_DIM_A]
        s = _dot_nt(qs[g], kg)
        mk = madd[j]
        out = []
        for r in range(HPG):
            h = g * HPG + r
            if near_col is None:
                bias = relb_ref[far_bucket, h] * LOG2E
            else:
                bias = bnear_ref[h, :, near_col:near_col + KB]
            out.append(s[r * TQ:(r + 1) * TQ] * (HEAD_DIM_A ** -0.5 * LOG2E) + bias + mk)
        return out

    def over_blocks(fn):
        def far(j, carry):
            fn(j, None)
            return carry

        lax.fori_loop(0, n_far, far, 0)

        @pl.when(nkb >= 2)
        def _():
            fn(nkb - 2, 0)

        fn(nkb - 1, KB)

    mrun[...] = jnp.full(mrun.shape, NEG, F32)

    def max_block(j, near_col):
        for g in range(N_KV_A):
            for r, sh in enumerate(logits(j, near_col, g)):
                rs = slice(r * TQ, (r + 1) * TQ)
                m = mrun[g, rs, :]
                for c in range(n_lc):
                    m = jnp.maximum(m, sh[:, c * LANES:(c + 1) * LANES])
                mrun[g, rs, :] = m

    over_blocks(max_block)

    for g in range(N_KV_A):
        mrun[g] = jnp.broadcast_to(jnp.max(mrun[g], axis=1, keepdims=True), mrun.shape[1:])
    lrun[...] = jnp.zeros(lrun.shape, F32)
    acc_s[...] = jnp.zeros(acc_s.shape, F32)

    def pv_block(j, near_col):
        c0 = pl.multiple_of(j * KB, KB)
        for g in range(N_KV_A):
            vg = v_ref[0, pl.ds(c0, KB), g * HEAD_DIM_A:(g + 1) * HEAD_DIM_A]
            parts = []
            for r, sh in enumerate(logits(j, near_col, g)):
                rs = slice(r * TQ, (r + 1) * TQ)
                mb = mrun[g, rs, :]
                l = lrun[g, rs, :]
                pcs = []
                for c in range(n_lc):
                    p = jnp.exp2(sh[:, c * LANES:(c + 1) * LANES] - mb)
                    l = l + p
                    pcs.append(p.astype(BF16))
                lrun[g, rs, :] = l
                parts.append(jnp.concatenate(pcs, axis=1))
            acc_s[g] += _dot(jnp.concatenate(parts, axis=0), vg)

    over_blocks(pv_block)

    for g in range(N_KV_A):
        out = acc_s[g] / jnp.sum(lrun[g], axis=1, keepdims=True)
        for r in range(HPG):
            h = g * HPG + r
            o_ref[0, :, h * HEAD_DIM_A:(h + 1) * HEAD_DIM_A] = out[r * TQ:(r + 1) * TQ].astype(o_ref.dtype)


def _dsa(q, qi, kiwi, kb, vb, kie, kio, rel_bias, *, n_keys, pos0, tq, kblk):
    B, T, _ = q.shape
    Lp = kb.shape[1]
    assert T % tq == 0 and Lp % kblk == 0 and tq % CHUNK == 0 or T == tq
    n_qt = T // tq
    if n_qt == 1:
        kb0, kstep = Lp // kblk, 0
        assert ((pos0 + tq - 1) // CHUNK + 1) * CHUNK >= n_keys
    else:
        assert pos0 == 0 and tq == kblk and n_keys == Lp == T
        kb0, kstep = 1, 1
    topk = min(TOPK_MAX, n_keys // 4)
    rb = min(128, tq)

    t = jnp.arange(tq, dtype=jnp.int32)[:, None]
    c = jnp.arange(2 * kblk, dtype=jnp.int32)[None, :]
    if n_qt == 1:
        q_abs = pos0 + t
        s_abs = (kb0 - 2) * kblk + c
    else:
        q_abs = kblk + t
        s_abs = c
    vis = ((s_abs // CHUNK) <= (q_abs // CHUNK)) & (s_abs < (n_keys if n_qt == 1 else 2 * kblk))
    bias_near = _bias_table(rel_bias, _t5_bucket(s_abs - q_abs))
    far_bucket = _far_bucket(kblk + 1, max(Lp, kblk + 2))

    kern = functools.partial(_dsa_kernel, TQ=tq, KB=kblk, RB=rb, kb0=kb0, kstep=kstep, topk=topk,
                             far_bucket=far_bucket, tie_bits=int(math.ceil(math.log2(Lp))) + 1)
    n_kb = Lp // kblk
    hpg = N_HEADS_A // N_KV_A
    return pl.pallas_call(
        kern,
        grid=(B, n_qt),
        in_specs=[pl.BlockSpec(memory_space=pltpu.SMEM),
                  pl.BlockSpec((1, tq, D_ATTN), lambda b, i: (b, i, 0)),
                  pl.BlockSpec((1, tq, N_IDX_HEADS * IDX_DIM), lambda b, i: (b, i, 0)),
                  pl.BlockSpec((1, tq, LANES), lambda b, i: (b, i, 0)),
                  pl.BlockSpec((1, Lp, D_KV), lambda b, i: (b, 0, 0)),
                  pl.BlockSpec((1, Lp, D_KV), lambda b, i: (b, 0, 0)),
                  pl.BlockSpec((1, Lp, LANES), lambda b, i: (b, 0, 0)),
                  pl.BlockSpec((1, Lp, LANES), lambda b, i: (b, 0, 0)),
                  _const_spec((N_HEADS_A, tq, 2 * kblk)),
                  _const_spec((tq, 2 * kblk))],
        out_specs=pl.BlockSpec((1, tq, D_ATTN), lambda b, i: (b, i, 0)),
        out_shape=jax.ShapeDtypeStruct((B, T, D_ATTN), BF16),
        scratch_shapes=[pltpu.VMEM((n_kb, tq, kblk), jnp.int32),
                        pltpu.VMEM((n_kb, tq, kblk), F32),
                        pltpu.VMEM((N_IDX_HEADS, tq, LANES), F32),
                        pltpu.VMEM((N_KV_A, hpg * tq, HEAD_DIM_A), BF16),
                        pltpu.VMEM((tq, 1), F32),
                        pltpu.VMEM((N_KV_A, hpg * tq, LANES), F32),
                        pltpu.VMEM((N_KV_A, hpg * tq, LANES), F32),
                        pltpu.VMEM((N_KV_A, hpg * tq, HEAD_DIM_A), F32)],
        compiler_params=pltpu.CompilerParams(dimension_semantics=("arbitrary", "arbitrary"),
                                             vmem_limit_bytes=VMEM_LIMIT),
        name="dsa",
    )(rel_bias, q, qi, kiwi, kb, vb, kie, kio, bias_near, vis.astype(F32))


def _mix_kernel(x_ref, a_ref, u_ref, hist_ref, mk_ref, mv_ref, wpool_ref, pscale_ref, wout_ref, gx_ref,
                wxq_ref, wxo_ref, gffn_ref, wr_ref, br_ref, o_ref, ext, *, tm, pos0):
    t = pl.program_id(1)
    H = POOL_HIST + 1

    @pl.when(t == 0)
    def _():
        ext[0:H, :] = hist_ref[0]

    @pl.when(t > 0)
    def _():
        ext[0:H, :] = ext[tm:tm + H, :]

    ext[H:H + tm, :] = u_ref[0]

    pos = pos0 + t * tm + lax.broadcasted_iota(jnp.int32, (tm, 1), 0)
    pouts = []
    for gi, w in enumerate(POOL_WINDOWS):
        c0, c1 = gi * POOL_GROUP_DIM, (gi + 1) * POOL_GROUP_DIM
        wsum = ext[H:H + tm, c0:c1]
        for s in range(1, w):
            wsum = wsum + ext[H - s:H - s + tm, c0:c1]
        cnt = jnp.minimum(pos + 1, w).astype(F32)
        d = wsum / cnt - u_ref[0, :, c0:c1]
        y = _dot(d.astype(BF16), wpool_ref[gi]) * pscale_ref[:, c0:c1]
        pouts.append(y.astype(BF16))
    pcat = jnp.concatenate(pouts, axis=1)

    x1 = x_ref[0] + _dot(a_ref[0], wout_ref[0:D_ATTN, :]) + _dot(pcat, wout_ref[D_ATTN:D_ATTN + D_POOL, :])

    h = _rms(x1, gx_ref[...]).astype(BF16)
    qx = _dot(h, wxq_ref[...])
    outs = []
    for hh in range(N_XHEADS):
        hs = slice(hh * XHEAD_DIM, (hh + 1) * XHEAD_DIM)
        logits = _dot_nt(qx[:, hs].astype(BF16), mk_ref[0, :, hs]) * (XHEAD_DIM ** -0.5)
        m = jnp.max(logits, axis=1, keepdims=True)
        p = jnp.exp(logits - m)
        l = jnp.sum(p, axis=1, keepdims=True)
        outs.append((_dot(p.astype(BF16), mv_ref[0, :, hs]) / l).astype(BF16))
    o = jnp.concatenate(outs, axis=1)
    x2 = x1 + _dot(o, wxo_ref[...])
    D = x2.shape[1]
    o_ref[0, :, 0:D] = x2
    hf = _rms(x2, gffn_ref[...]).astype(BF16)
    o_ref[0, :, D:D + LANES] = _route(_dot(hf, wr_ref[...]) + br_ref[...])


def _mix(x, a, u, hist, mk, mv, w_pool, pool_scale, w_out, g_x, w_xq, w_xo, g_ffn, w_r, b_r, *, pos0, tm):
    B, T, D = x.shape
    n_mem = mk.shape[1]
    H = POOL_HIST + 1
    assert T % tm == 0 and tm >= H
    hist16 = jnp.concatenate([jnp.zeros((B, 1, D_POOL), F32), hist], axis=1)
    return pl.pallas_call(
        functools.partial(_mix_kernel, tm=tm, pos0=pos0),
        grid=(B, T // tm),
        in_specs=[pl.BlockSpec((1, tm, D), lambda b, t: (b, t, 0)),
                  pl.BlockSpec((1, tm, D_ATTN), lambda b, t: (b, t, 0)),
                  pl.BlockSpec((1, tm, D_POOL), lambda b, t: (b, t, 0)),
                  pl.BlockSpec((1, H, D_POOL), lambda b, t: (b, 0, 0)),
                  pl.BlockSpec((1, n_mem, D_X), lambda b, t: (b, 0, 0)),
                  pl.BlockSpec((1, n_mem, D_X), lambda b, t: (b, 0, 0)),
                  _const_spec(w_pool.shape),
                  _const_spec((1, D_POOL)),
                  _const_spec(w_out.shape),
                  _const_spec((1, D)),
                  _const_spec(w_xq.shape),
                  _const_spec(w_xo.shape),
                  _const_spec((1, D)),
                  _const_spec(w_r.shape),
                  _const_spec((1, LANES))],
        out_specs=pl.BlockSpec((1, tm, D + LANES), lambda b, t: (b, t, 0)),
        out_shape=jax.ShapeDtypeStruct((B, T, D + LANES), F32),
        scratch_shapes=[pltpu.VMEM((H + tm, D_POOL), F32)],
        compiler_params=pltpu.CompilerParams(dimension_semantics=("arbitrary", "arbitrary"),
                                             vmem_limit_bytes=VMEM_LIMIT),
        name="mix",
    )(x, a, u, hist16, mk, mv, w_pool, pool_scale.reshape(1, D_POOL), w_out, g_x.reshape(1, D), w_xq, w_xo,
      g_ffn.reshape(1, D), w_r, b_r)


R_OFF = N_GROUPS


def _route(logits):
    tm = logits.shape[0]
    lane = lax.broadcasted_iota(jnp.int32, (tm, LANES), 1).astype(F32)
    ninf = -jnp.inf
    big = float(LANES)
    gl = jnp.where(lane < N_GROUPS, logits, ninf)
    gmax = jnp.max(gl, axis=1, keepdims=True)
    g_sel = jnp.min(jnp.where(gl == gmax, lane, big), axis=1, keepdims=True)
    g_prob = 1.0 / jnp.sum(jnp.exp(gl - gmax), axis=1, keepdims=True)
    lo = R_OFF + g_sel * EXPERTS_PER_GROUP
    el = jnp.where((lane >= lo) & (lane < lo + EXPERTS_PER_GROUP), logits, ninf)
    tv0 = jnp.max(el, axis=1, keepdims=True)
    ti0 = jnp.min(jnp.where(el == tv0, lane, big), axis=1, keepdims=True)
    el2 = jnp.where(lane == ti0, ninf, el)
    tv1 = jnp.max(el2, axis=1, keepdims=True)
    ti1 = jnp.min(jnp.where(el2 == tv1, lane, big), axis=1, keepdims=True)
    e1 = jnp.exp(tv1 - tv0)
    den = 1.0 + e1
    w0 = g_prob / den
    w1 = g_prob * e1 / den
    ids = jnp.where(lane == 0.0, jnp.minimum(ti0, ti1) - R_OFF,
                    jnp.where(lane == 1.0, jnp.maximum(ti0, ti1) - R_OFF, 0.0))
    return jnp.where(lane == ti0, w0, 0.0) + jnp.where(lane == ti1, w1, 0.0) + ids


def _expert(h, rec, e, wg, wu, wd):
    a = _dot(h, wg)
    b = _dot(h, wu)
    lane = lax.broadcasted_iota(jnp.int32, rec.shape, 1)
    cw = jnp.sum(jnp.where(lane == e + R_OFF, rec, 0.0), axis=1, keepdims=True)
    hid = a * jax.nn.sigmoid(a) * b * cw
    return _dot(hid.astype(BF16), wd)


def _moe_kernel(x_ref, gffn_ref, wg_ref, wu_ref, wd_ref, gfin_ref, o_ref, h_s, acc_s):
    e = pl.program_id(1)
    d = o_ref.shape[1]

    @pl.when(e == 0)
    def _():
        h_s[...] = _rms(x_ref[:, 0:d], gffn_ref[...]).astype(BF16)
        acc_s[...] = jnp.zeros(acc_s.shape, F32)

    acc_s[...] += _expert(h_s[...], x_ref[:, d:d + LANES], e, wg_ref[0], wu_ref[0], wd_ref[0])

    @pl.when(e == pl.num_programs(1) - 1)
    def _():
        o_ref[...] = _rms(x_ref[:, 0:d] + acc_s[...], gfin_ref[...])


def _moe(x, g_ffn, w_gate, w_up, w_down, g_final, *, tm):
    n, de = x.shape
    d = de - LANES
    assert n % tm == 0
    return pl.pallas_call(
        _moe_kernel,
        grid=(n // tm, N_EXPERTS),
        in_specs=[pl.BlockSpec((tm, de), lambda i, e: (i, 0)),
                  _const_spec((1, d)),
                  pl.BlockSpec((1, d, D_EXPERT), lambda i, e: (e, 0, 0)),
                  pl.BlockSpec((1, d, D_EXPERT), lambda i, e: (e, 0, 0)),
                  pl.BlockSpec((1, D_EXPERT, d), lambda i, e: (e, 0, 0)),
                  _const_spec((1, d))],
        out_specs=pl.BlockSpec((tm, d), lambda i, e: (i, 0)),
        out_shape=jax.ShapeDtypeStruct((n, d), F32),
        scratch_shapes=[pltpu.VMEM((tm, d), BF16),
                        pltpu.VMEM((tm, d), F32)],
        compiler_params=pltpu.CompilerParams(dimension_semantics=("arbitrary", "arbitrary"),
                                             vmem_limit_bytes=VMEM_LIMIT),
        name="moe",
    )(x, g_ffn.reshape(1, d), w_gate, w_up, w_down, g_final.reshape(1, d))


def _moe_sparse_kernel(ea_ref, eb_ref, nv_ref, idx_ref, idxn_ref, x_hbm, gffn_ref, gfin_ref,
                       wga_ref, wua_ref, wda_ref, wgb_ref, wub_ref, wdb_ref, o_hbm,
                       xbuf, ybuf, gsem, ssem, *, tm):
    t = pl.program_id(0)
    nt = pl.num_programs(0)
    d = o_hbm.shape[1]
    slot = lax.rem(t, 2)

    def gather_start(ids, s):
        def body(r, carry):
            row = ids[0, 0, r]
            pltpu.make_async_copy(x_hbm.at[pl.ds(row, 1), :], xbuf.at[s, pl.ds(r, 1), :], gsem.at[s]).start()
            return carry
        lax.fori_loop(0, tm, body, 0)

    def scatter_row(s, r, row):
        return pltpu.make_async_copy(ybuf.at[s, pl.ds(r, 1), :], o_hbm.at[pl.ds(row, 1), :], ssem.at[s])

    def scatter_wait(s, n):
        def body(r, carry):
            scatter_row(s, 0, 0).wait()
            return carry
        lax.fori_loop(0, n, body, 0)

    @pl.when((t == 0) & (nv_ref[0] > 0))
    def _():
        gather_start(idx_ref, 0)

    tn = jnp.minimum(t + 1, nt - 1)

    @pl.when((t + 1 < nt) & (nv_ref[tn] > 0))
    def _():
        gather_start(idxn_ref, 1 - slot)

    @pl.when(t >= 2)
    def _():
        scatter_wait(slot, nv_ref[jnp.maximum(t - 2, 0)])

    nv = nv_ref[t]

    @pl.when(nv > 0)
    def _():
        pltpu.make_async_copy(x_hbm.at[pl.ds(0, tm), :], xbuf.at[slot], gsem.at[slot]).wait()
        x = xbuf[slot]
        xr = x[:, 0:d]
        rec = x[:, d:d + LANES]
        h = _rms(xr, gffn_ref[...]).astype(BF16)
        acc = _expert(h, rec, ea_ref[t], wga_ref[0], wua_ref[0], wda_ref[0])
        acc = acc + _expert(h, rec, eb_ref[t], wgb_ref[0], wub_ref[0], wdb_ref[0])
        ybuf[slot] = _rms(xr + acc, gfin_ref[...])

        def body(r, carry):
            scatter_row(slot, r, idx_ref[0, 0, r]).start()
            return carry
        lax.fori_loop(0, nv, body, 0)

    @pl.when(t == nt - 1)
    def _():
        @pl.when(t >= 1)
        def _():
            scatter_wait(1 - slot, nv_ref[jnp.maximum(t - 1, 0)])
        scatter_wait(slot, nv)


def _moe_sparse(x, g_ffn, w_gate, w_up, w_down, g_final, *, tm):
    n, de = x.shape
    d = de - LANES
    n_pairs = N_GROUPS * (EXPERTS_PER_GROUP * (EXPERTS_PER_GROUP - 1) // 2)
    nt = -(-n // tm) + n_pairs
    n_keys = N_EXPERTS * N_EXPERTS

    ids = x[:, d:d + 2].astype(jnp.int32)
    key = ids[:, 0] * N_EXPERTS + ids[:, 1]
    order = jnp.argsort(key).astype(jnp.int32)
    skey = key[order]
    edges = jnp.searchsorted(skey, jnp.arange(n_keys + 1, dtype=jnp.int32), side="left").astype(jnp.int32)
    starts, ends = edges[:-1], edges[1:]
    tiles = (ends - starts + tm - 1) // tm
    cum = jnp.cumsum(tiles)
    total = cum[-1]
    tt = jnp.arange(nt, dtype=jnp.int32)
    tc = jnp.minimum(tt, total - 1)
    cls = jnp.searchsorted(cum, tc, side="right").astype(jnp.int32)
    first = cum[cls] - tiles[cls]
    tstart = starts[cls] + (tc - first) * tm
    nv = jnp.where(tt < total, jnp.clip(ends[cls] - tstart, 0, tm), 0).astype(jnp.int32)
    ea = (cls // N_EXPERTS).astype(jnp.int32)
    eb = (cls % N_EXPERTS).astype(jnp.int32)
    rows = jnp.clip(tstart[:, None] + jnp.arange(tm, dtype=jnp.int32)[None, :], 0, n - 1)
    idx = order[rows].reshape(nt, 1, tm)

    wspec_a = lambda shape: pl.BlockSpec(shape, lambda t, ea, eb, nv: (ea[t], 0, 0))
    wspec_b = lambda shape: pl.BlockSpec(shape, lambda t, ea, eb, nv: (eb[t], 0, 0))
    cspec = lambda shape: pl.BlockSpec(shape, lambda t, ea, eb, nv: (0,) * len(shape),
                                       pipeline_mode=pl.Buffered(1))
    grid_spec = pltpu.PrefetchScalarGridSpec(
        num_scalar_prefetch=3,
        grid=(nt,),
        in_specs=[pl.BlockSpec((1, 1, tm), lambda t, ea, eb, nv: (t, 0, 0), memory_space=pltpu.SMEM),
                  pl.BlockSpec((1, 1, tm), lambda t, ea, eb, nv: (jnp.minimum(t + 1, nt - 1), 0, 0),
                               memory_space=pltpu.SMEM),
                  pl.BlockSpec(memory_space=pl.ANY),
                  cspec((1, d)), cspec((1, d)),
                  wspec_a((1, d, D_EXPERT)), wspec_a((1, d, D_EXPERT)), wspec_a((1, D_EXPERT, d)),
                  wspec_b((1, d, D_EXPERT)), wspec_b((1, d, D_EXPERT)), wspec_b((1, D_EXPERT, d))],
        out_specs=pl.BlockSpec(memory_space=pl.ANY),
        scratch_shapes=[pltpu.VMEM((2, tm, de), F32),
                        pltpu.VMEM((2, tm, d), F32),
                        pltpu.SemaphoreType.DMA((2,)),
                        pltpu.SemaphoreType.DMA((2,))])
    return pl.pallas_call(
        functools.partial(_moe_sparse_kernel, tm=tm),
        grid_spec=grid_spec,
        out_shape=jax.ShapeDtypeStruct((n, d), F32),
        compiler_params=pltpu.CompilerParams(dimension_semantics=("arbitrary",),
                                             vmem_limit_bytes=VMEM_LIMIT),
        name="moe_sparse",
    )(ea, eb, nv, idx, idx, x, g_ffn.reshape(1, d), g_final.reshape(1, d),
      w_gate, w_up, w_down, w_gate, w_up, w_down)


C_Q = 0
C_QI = C_Q + D_ATTN
C_K = C_QI + N_IDX_HEADS * IDX_DIM
C_V = C_K + D_KV
C_KIWI = C_V + D_KV
C_KIE = C_KIWI + LANES
C_KIO = C_KIE + LANES
C_U = C_KIO + LANES
C_END = C_U + D_POOL


def _layout_w_in(w_in):
    d = w_in.shape[0]
    offs = np.cumsum((0, D_ATTN, D_KV, D_KV, N_IDX_HEADS * IDX_DIM, IDX_DIM, N_IDX_HEADS, D_POOL))
    wq, wk, wv, wqi, wki, wwi, wu = (w_in[:, offs[n]:offs[n + 1]] for n in range(7))
    z = lambda n: jnp.zeros((d, n), w_in.dtype)
    cat = jnp.concatenate([wq, wqi, wk, wv,
                           wki, wwi, z(LANES - IDX_DIM - N_IDX_HEADS),
                           wki, z(LANES - IDX_DIM),
                           z(LANES - IDX_DIM), wki,
                           wu], axis=1)
    assert cat.shape[1] == C_END
    return cat.astype(BF16)


IN_GROUPS = ((C_Q, C_QI, 1.0, 1),
             (C_QI, C_K, 1.0, 1),
             (C_K, C_V, 1.0, 2),
             (C_V, C_KIWI, 1.0, 2),
             (C_KIWI, C_KIE, 1.0, 1),
             (C_KIE, C_KIO, 1.0, 1),
             (C_KIO, C_U, 1.0, 1),
             (C_U, C_END, 1.0, 1))
IN_DTYPES = (BF16, BF16, F32, BF16, F32, BF16, F32, BF16, BF16, F32)


def _pad_rows(a, n):
    return jnp.pad(a, ((0, 0), (0, n - a.shape[1]), (0, 0)))


def _layer(x, pos0, caches, pool_hist, mk, mv, rel_bias, W, *, tm_in, tq, tm_mix, tm_moe, sparse_moe):
    B, T, D = x.shape
    n = B * T
    q, qi, k, kb, v, vb, kiwi, kie, kio, u = _norm_matmul(
        x.reshape(n, D), W["g_mix"], W["w_in"], IN_GROUPS, IN_DTYPES, tm_in)
    r3 = lambda a: a.reshape(B, T, a.shape[-1])
    kblk = 256
    if caches is None:
        n_keys = T
        kb3, vb3, kie3, kio3 = r3(kb), r3(vb), r3(kie), r3(kio)
    else:
        k_hist, v_hist, ki_hist = caches
        past = k_hist.shape[1]
        n_keys = past + T
        lp = -(-n_keys // kblk) * kblk
        ki = r3(kiwi)[:, :, :IDX_DIM]
        ki_all = jnp.concatenate([ki_hist, ki], axis=1)
        zeros = jnp.zeros_like(ki_all)
        kb3 = _pad_rows(jnp.concatenate([k_hist.reshape(B, past, D_KV), r3(k)], axis=1), lp).astype(BF16)
        vb3 = _pad_rows(jnp.concatenate([v_hist.reshape(B, past, D_KV), r3(v)], axis=1), lp).astype(BF16)
        kie3 = _pad_rows(jnp.concatenate([ki_all, zeros], axis=2), lp).astype(BF16)
        kio3 = _pad_rows(jnp.concatenate([zeros, ki_all], axis=2), lp).astype(BF16)
    a_out = _dsa(r3(q), r3(qi), r3(kiwi), kb3, vb3, kie3, kio3, rel_bias,
                 n_keys=n_keys, pos0=pos0, tq=tq, kblk=kblk)
    x2 = _mix(x, a_out, r3(u), pool_hist, mk.astype(BF16), mv.astype(BF16), W["w_pool"], W["pool_scale"],
              W["w_out"], W["g_x"], W["w_xq"], W["w_xo"], W["g_ffn"], W["w_r"], W["b_r"], pos0=pos0, tm=tm_mix)
    moe = _moe_sparse if sparse_moe else _moe
    y = moe(x2.reshape(n, D + LANES), W["g_ffn"], W["w_gate"], W["w_up"], W["w_down"], W["g_final"], tm=tm_moe)
    new_pool = jnp.concatenate([pool_hist, r3(u)], axis=1)[:, -POOL_HIST:]
    return (y.reshape(B, T, D), k.reshape(B, T, N_KV_A, HEAD_DIM_A), v.reshape(B, T, N_KV_A, HEAD_DIM_A),
            r3(kiwi)[:, :, :IDX_DIM], new_pool)


def kernel(x_prompt, x_sample, mem_prompt, cache_k, cache_v, cache_k_idx, cache_pool, cache_mem_k, cache_mem_v, rel_bias, g_mix, w_in, w_pool, pool_scale, w_out, g_mem, w_mk, w_mv, g_x, w_xq, w_xo, g_ffn, w_rg, b_rg, w_re, b_re, w_gate, w_up, w_down, g_final):
    depth = g_mix.shape[0]
    assert depth == 1
    l = 0
    B, T, D = x_prompt.shape
    Bs, Ts, _ = x_sample.shape
    n_mem = mem_prompt.shape[1]
    past = cache_k.shape[2]

    w_r = jnp.concatenate([w_rg[l], w_re[l], jnp.zeros((D, LANES - N_GROUPS - N_EXPERTS), F32)], axis=1)
    W = dict(
        g_mix=g_mix[l], w_in=_layout_w_in(w_in[l]), w_pool=w_pool[l].astype(BF16), pool_scale=pool_scale[l],
        w_out=w_out[l].astype(BF16), g_x=g_x[l], w_xq=w_xq[l].astype(BF16), w_xo=w_xo[l].astype(BF16),
        g_ffn=g_ffn[l], w_r=w_r.astype(BF16),
        b_r=jnp.concatenate([b_rg[l], b_re[l], jnp.zeros((LANES - N_GROUPS - N_EXPERTS,), F32)]).reshape(1, LANES),
        w_gate=w_gate[l].astype(BF16), w_up=w_up[l].astype(BF16), w_down=w_down[l].astype(BF16),
        g_final=g_final)

    w_m = jnp.concatenate([w_mk[l], w_mv[l]], axis=1).astype(BF16)
    mk, mv = _norm_matmul(mem_prompt.reshape(B * n_mem, D), g_mem[l], w_m,
                          ((0, D_X, 1.0, 1), (D_X, 2 * D_X, 1.0, 1)), (F32, F32), 256)
    mk = mk.reshape(B, n_mem, D_X)
    mv = mv.reshape(B, n_mem, D_X)

    yp, kp, vp, kip, pp = _layer(x_prompt, 0, None, jnp.zeros((B, POOL_HIST, D_POOL), F32), mk, mv, rel_bias, W,
                                 tm_in=256, tq=256, tm_mix=256, tm_moe=128, sparse_moe=True)
    ys, ks, vs, kis, ps = _layer(x_sample, past,
                                 (cache_k[l], cache_v[l], cache_k_idx[l]), cache_pool[l],
                                 cache_mem_k[l].reshape(Bs, n_mem, D_X), cache_mem_v[l].reshape(Bs, n_mem, D_X),
                                 rel_bias, W, tm_in=Bs * Ts, tq=Ts, tm_mix=Ts, tm_moe=Bs * Ts, sparse_moe=False)
    st = lambda a: a[None]
    return (yp, ys, st(kp), st(vp), st(kip), st(pp),
            st(mk.reshape(B, n_mem, N_XHEADS, XHEAD_DIM)), st(mv.reshape(B, n_mem, N_XHEADS, XHEAD_DIM)),
            st(ks), st(vs), st(kis), st(ps))
```

```python
import functools
import math

import numpy as np
import jax
import jax.numpy as jnp
from jax import lax
from jax.experimental import pallas as pl
from jax.experimental.pallas import tpu as pltpu

F32 = jnp.float32
BF16 = jnp.bfloat16

CHUNK = 64
N_HEADS_A = 8
HEAD_DIM_A = 128
N_KV_A = 2
D_ATTN = N_HEADS_A * HEAD_DIM_A
D_KV = N_KV_A * HEAD_DIM_A
N_IDX_HEADS = 8
IDX_DIM = 64
TOPK_MAX = 256
POOL_WINDOWS = (2, 4, 8, 16)
POOL_GROUP_DIM = 256
D_POOL = len(POOL_WINDOWS) * POOL_GROUP_DIM
POOL_HIST = max(POOL_WINDOWS) - 1
N_BUCKETS = 32
MAX_DISTANCE = 128
N_XHEADS = 4
XHEAD_DIM = 128
D_X = N_XHEADS * XHEAD_DIM
N_GROUPS = 4
EXPERTS_PER_GROUP = 8
N_EXPERTS = N_GROUPS * EXPERTS_PER_GROUP
D_EXPERT = 256
EPS = 1e-6

LANES = 128
INT_MIN = -(2 ** 31)
NEG = -1e30
LOG2E = math.log2(math.e)
VMEM_LIMIT = 48 * 1024 * 1024


def _rms(x, g):
    ms = jnp.mean(x * x, axis=-1, keepdims=True)
    return x * lax.rsqrt(ms + EPS) * g


def _dot(a, b):
    return jnp.dot(a, b, preferred_element_type=F32)


def _dot_nt(a, b):
    return lax.dot_general(a, b, (((1,), (1,)), ((), ())), preferred_element_type=F32)


def _const_spec(shape):
    nd = len(shape)
    return pl.BlockSpec(shape, lambda *_: (0,) * nd, pipeline_mode=pl.Buffered(1))


def _norm_matmul_kernel(x_ref, g_ref, w_ref, *out_refs, groups):
    h = _rms(x_ref[...], g_ref[...]).astype(BF16)
    k = 0
    for c0, c1, scale, n_out in groups:
        acc = _dot(h, w_ref[:, c0:c1])
        if scale != 1.0:
            acc = acc * scale
        for _ in range(n_out):
            out_refs[k][...] = acc.astype(out_refs[k].dtype)
            k += 1


def _norm_matmul(x, g, w, groups, out_dtypes, tm):
    n, d = x.shape
    assert n % tm == 0
    widths = []
    for c0, c1, _, n_out in groups:
        widths += [c1 - c0] * n_out
    return pl.pallas_call(
        functools.partial(_norm_matmul_kernel, groups=tuple(groups)),
        grid=(n // tm,),
        in_specs=[pl.BlockSpec((tm, d), lambda i: (i, 0)),
                  _const_spec((1, d)),
                  _const_spec(w.shape)],
        out_specs=[pl.BlockSpec((tm, wd), lambda i: (i, 0)) for wd in widths],
        out_shape=[jax.ShapeDtypeStruct((n, wd), dt) for wd, dt in zip(widths, out_dtypes)],
        compiler_params=pltpu.CompilerParams(dimension_semantics=("arbitrary",),
                                             vmem_limit_bytes=VMEM_LIMIT),
        name="norm_matmul",
    )(x, g.reshape(1, d), w)


def _bias_table_kernel(relb_ref, bucket_ref, o_ref):
    bucket = bucket_ref[...]
    for h in range(N_HEADS_A):
        acc = jnp.zeros(bucket.shape, F32)
        for b in range(N_BUCKETS):
            acc = jnp.where(bucket == b, relb_ref[b, h], acc)
        o_ref[h] = acc * LOG2E


def _bias_table(rel_bias, bucket):
    tq, w = bucket.shape
    return pl.pallas_call(
        _bias_table_kernel,
        in_specs=[pl.BlockSpec(memory_space=pltpu.SMEM),
                  pl.BlockSpec((tq, w), lambda: (0, 0))],
        out_specs=pl.BlockSpec((N_HEADS_A, tq, w), lambda: (0, 0, 0)),
        out_shape=jax.ShapeDtypeStruct((N_HEADS_A, tq, w), F32),
        name="bias_table",
    )(rel_bias, bucket)


def _t5_bucket(rel):
    nb = N_BUCKETS // 2
    max_exact = nb // 2
    bucket = (rel > 0).astype(jnp.int32) * nb
    n = jnp.abs(rel)
    nf = jnp.maximum(n, 1).astype(F32)
    large = max_exact + (jnp.log(nf / max_exact) / math.log(MAX_DISTANCE / max_exact)
                         * (nb - max_exact)).astype(jnp.int32)
    large = jnp.minimum(large, nb - 1)
    return bucket + jnp.where(n < max_exact, n, large)


def _far_bucket(min_dist, max_dist):
    nb = N_BUCKETS // 2
    max_exact = nb // 2
    n = np.arange(min_dist, max_dist + 1, dtype=np.float64)
    large = max_exact + np.floor(np.log(n / max_exact) / math.log(MAX_DISTANCE / max_exact)
                                 * (nb - max_exact) * (1 - 1e-6)).astype(np.int64)
    assert large.min() >= nb - 1, "far key blocks must sit in the saturated distance bucket"
    return nb - 1


def _dsa_kernel(relb_ref, q_ref, qi_ref, kiwi_ref, k_ref, v_ref, kie_ref, kio_ref, bnear_ref, vis_ref,
                o_ref, keys, madd, wb, qs, xcut, mrun, lrun, acc_s,
                *, TQ, KB, RB, kb0, kstep, topk, far_bucket, tie_bits):
    i = pl.program_id(1)
    nkb = kb0 + kstep * i
    n_far = jnp.maximum(nkb - 2, 0)
    n_lc = KB // LANES
    HPG = N_HEADS_A // N_KV_A

    kiwi = kiwi_ref[0]
    for h in range(N_IDX_HEADS):
        w = (kiwi[:, IDX_DIM + h:IDX_DIM + h + 1] * (N_IDX_HEADS ** -0.5)).astype(BF16).astype(F32)
        wb[h] = jnp.broadcast_to(w * (IDX_DIM ** -0.5), (TQ, LANES))
    for g in range(N_KV_A):
        for r in range(HPG):
            h = g * HPG + r
            qs[g, r * TQ:(r + 1) * TQ, :] = q_ref[0, :, h * HEAD_DIM_A:(h + 1) * HEAD_DIM_A]

    def idx_block(j, near_col):
        c0 = pl.multiple_of(j * KB, KB)
        ke = kie_ref[0, pl.ds(c0, KB), :]
        ko = kio_ref[0, pl.ds(c0, KB), :]
        acc = [jnp.zeros((TQ, LANES), F32) for _ in range(n_lc)]
        for hp in range(N_IDX_HEADS // 2):
            qh = qi_ref[0, :, hp * LANES:(hp + 1) * LANES]
            se = _dot_nt(qh, ke)
            so = _dot_nt(qh, ko)
            we = wb[2 * hp]
            wo = wb[2 * hp + 1]
            for c in range(n_lc):
                cs = slice(c * LANES, (c + 1) * LANES)
                re = jnp.maximum(se[:, cs], 0.0).astype(BF16).astype(F32)
                ro = jnp.maximum(so[:, cs], 0.0).astype(BF16).astype(F32)
                acc[c] = acc[c] + we * re + wo * ro
        for c in range(n_lc):
            bits = lax.bitcast_convert_type(acc[c], jnp.int32)
            key = bits ^ ((bits >> 31) & jnp.int32(0x7FFFFFFF))
            if near_col is not None:
                vis = vis_ref[:, near_col + c * LANES:near_col + (c + 1) * LANES]
                key = jnp.where(vis > 0.0, key, jnp.int32(INT_MIN))
            keys[j, :, c * LANES:(c + 1) * LANES] = key

    def far_idx(j, carry):
        idx_block(j, None)
        return carry

    lax.fori_loop(0, n_far, far_idx, 0)

    @pl.when(nkb >= 2)
    def _():
        idx_block(nkb - 2, 0)

    idx_block(nkb - 1, KB)

    kf = float(topk)
    lane_f = lax.broadcasted_iota(jnp.int32, (RB, LANES), 1).astype(F32)
    NRB = TQ // RB
    rbs = [slice(rb * RB, (rb + 1) * RB) for rb in range(NRB)]

    def count(pred):
        def body(jj, accs):
            out = []
            for rb in range(NRB):
                blk = keys[jj, rbs[rb], :]
                acc = accs[rb]
                for c in range(n_lc):
                    acc = acc + jnp.where(pred(rb, blk[:, c * LANES:(c + 1) * LANES], jj, c), 1.0, 0.0)
                out.append(acc)
            return tuple(out)
        accs = lax.fori_loop(0, nkb, body, tuple(jnp.zeros((RB, LANES), F32) for _ in range(NRB)))
        return [jnp.sum(acc, axis=1, keepdims=True) for acc in accs]

    def count_ge(cands):
        cb = [jnp.broadcast_to(cand, (RB, LANES)) for cand in cands]
        return count(lambda rb, kblk, jj, c: kblk >= cb[rb])

    def bis_body(p, thrs):
        cands = [thr + lax.shift_left(jnp.int32(1), 31 - p) for thr in thrs]
        cnts = count_ge(cands)
        return tuple(jnp.where(cnt >= kf, cand, thr) for cnt, cand, thr in zip(cnts, cands, thrs))

    thrs = lax.fori_loop(0, 32, bis_body, tuple(jnp.full((RB, 1), INT_MIN, jnp.int32) for _ in range(NRB)))
    thrs = [jnp.maximum(thr, jnp.int32(INT_MIN + 1)) for thr in thrs]
    cnt_ge = count_ge(thrs)
    cnt_gt = count_ge([thr + 1 for thr in thrs])
    needs = [kf - c for c in cnt_gt]
    thrb = [jnp.broadcast_to(thr, (RB, LANES)) for thr in thrs]

    xcut[...] = jnp.full(xcut.shape, 2.0 ** 24, F32)
    excess = cnt_ge[0] - kf
    for c in cnt_ge[1:]:
        excess = jnp.maximum(excess, c - kf)

    @pl.when(jnp.max(excess) > 0.0)
    def _():
        def count_eq_lt(xs):
            xb = [jnp.broadcast_to(x, (RB, LANES)) for x in xs]

            def pred(rb, kblk, jj, c):
                col = lane_f + (jj * KB + c * LANES).astype(F32)
                return jnp.where(kblk == thrb[rb], col, 2.0 ** 25) < xb[rb]
            return count(pred)

        def tie_body(p, xs):
            cands = [x + lax.shift_left(jnp.int32(1), tie_bits - 1 - p).astype(F32) for x in xs]
            cnts = count_eq_lt(cands)
            return tuple(jnp.where(cnt < need, cand, x) for cnt, need, cand, x in zip(cnts, needs, cands, xs))

        xs = lax.fori_loop(0, tie_bits, tie_body, tuple(jnp.zeros((RB, 1), F32) for _ in range(NRB)))
        for rb in range(NRB):
            xcut[rbs[rb], :] = xs[rb]

    xb = [jnp.broadcast_to(xcut[rbs[rb], :], (RB, LANES)) for rb in range(NRB)]

    def mask_body(jj, carry):
        for rb in range(NRB):
            blk = keys[jj, rbs[rb], :]
            for c in range(n_lc):
                kblk = blk[:, c * LANES:(c + 1) * LANES]
                col = lane_f + (jj * KB + c * LANES).astype(F32)
                keep = (kblk > thrb[rb]) | ((kblk == thrb[rb]) & (col <= xb[rb]))
                madd[jj, rbs[rb], c * LANES:(c + 1) * LANES] = jnp.where(keep, 0.0, NEG)
        return carry

    lax.fori_loop(0, nkb, mask_body, 0)

    def logits(j, near_col, g):
        c0 = pl.multiple_of(j * KB, KB)
        kg = k_ref[0, pl.ds(c0, KB), g * HEAD_DIM_A:(g + 1) * HEAD_DIM_A]
        s = _dot_nt(qs[g], kg)
        mk = madd[j]
        out = []
        for r in range(HPG):
            h = g * HPG + r
            if near_col is None:
                bias = relb_ref[far_bucket, h] * LOG2E
            else:
                bias = bnear_ref[h, :, near_col:near_col + KB]
            out.append(s[r * TQ:(r + 1) * TQ] * (HEAD_DIM_A ** -0.5 * LOG2E) + bias + mk)
        return out

    def over_blocks(fn):
        def far(j, carry):
            fn(j, None)
            return carry

        lax.fori_loop(0, n_far, far, 0)

        @pl.when(nkb >= 2)
        def _():
            fn(nkb - 2, 0)

        fn(nkb - 1, KB)

    mrun[...] = jnp.full(mrun.shape, NEG, F32)

    def max_block(j, near_col):
        for g in range(N_KV_A):
            for r, sh in enumerate(logits(j, near_col, g)):
                rs = slice(r * TQ, (r + 1) * TQ)
                m = mrun[g, rs, :]
                for c in range(n_lc):
                    m = jnp.maximum(m, sh[:, c * LANES:(c + 1) * LANES])
                mrun[g, rs, :] = m

    over_blocks(max_block)

    for g in range(N_KV_A):
        mrun[g] = jnp.broadcast_to(jnp.max(mrun[g], axis=1, keepdims=True), mrun.shape[1:])
    lrun[...] = jnp.zeros(lrun.shape, F32)
    acc_s[...] = jnp.zeros(acc_s.shape, F32)

    def pv_block(j, near_col):
        c0 = pl.multiple_of(j * KB, KB)
        for g in range(N_KV_A):
            vg = v_ref[0, pl.ds(c0, KB), g * HEAD_DIM_A:(g + 1) * HEAD_DIM_A]
            parts = []
            for r, sh in enumerate(logits(j, near_col, g)):
                rs = slice(r * TQ, (r + 1) * TQ)
                mb = mrun[g, rs, :]
                l = lrun[g, rs, :]
                pcs = []
                for c in range(n_lc):
                    p = jnp.exp2(sh[:, c * LANES:(c + 1) * LANES] - mb)
                    l = l + p
                    pcs.append(p.astype(BF16))
                lrun[g, rs, :] = l
                parts.append(jnp.concatenate(pcs, axis=1))
            acc_s[g] += _dot(jnp.concatenate(parts, axis=0), vg)

    over_blocks(pv_block)

    for g in range(N_KV_A):
        out = acc_s[g] / jnp.sum(lrun[g], axis=1, keepdims=True)
        for r in range(HPG):
            h = g * HPG + r
            o_ref[0, :, h * HEAD_DIM_A:(h + 1) * HEAD_DIM_A] = out[r * TQ:(r + 1) * TQ].astype(o_ref.dtype)


def _dsa(q, qi, kiwi, kb, vb, kie, kio, rel_bias, *, n_keys, pos0, tq, kblk):
    B, T, _ = q.shape
    Lp = kb.shape[1]
    assert T % tq == 0 and Lp % kblk == 0 and tq % CHUNK == 0 or T == tq
    n_qt = T // tq
    if n_qt == 1:
        kb0, kstep = Lp // kblk, 0
        assert ((pos0 + tq - 1) // CHUNK + 1) * CHUNK >= n_keys
    else:
        assert pos0 == 0 and tq == kblk and n_keys == Lp == T
        kb0, kstep = 1, 1
    topk = min(TOPK_MAX, n_keys // 4)
    rb = min(128, tq)

    t = jnp.arange(tq, dtype=jnp.int32)[:, None]
    c = jnp.arange(2 * kblk, dtype=jnp.int32)[None, :]
    if n_qt == 1:
        q_abs = pos0 + t
        s_abs = (kb0 - 2) * kblk + c
    else:
        q_abs = kblk + t
        s_abs = c
    vis = ((s_abs // CHUNK) <= (q_abs // CHUNK)) & (s_abs < (n_keys if n_qt == 1 else 2 * kblk))
    bias_near = _bias_table(rel_bias, _t5_bucket(s_abs - q_abs))
    far_bucket = _far_bucket(kblk + 1, max(Lp, kblk + 2))

    kern = functools.partial(_dsa_kernel, TQ=tq, KB=kblk, RB=rb, kb0=kb0, kstep=kstep, topk=topk,
                             far_bucket=far_bucket, tie_bits=int(math.ceil(math.log2(Lp))) + 1)
    n_kb = Lp // kblk
    hpg = N_HEADS_A // N_KV_A
    return pl.pallas_call(
        kern,
        grid=(B, n_qt),
        in_specs=[pl.BlockSpec(memory_space=pltpu.SMEM),
                  pl.BlockSpec((1, tq, D_ATTN), lambda b, i: (b, i, 0)),
                  pl.BlockSpec((1, tq, N_IDX_HEADS * IDX_DIM), lambda b, i: (b, i, 0)),
                  pl.BlockSpec((1, tq, LANES), lambda b, i: (b, i, 0)),
                  pl.BlockSpec((1, Lp, D_KV), lambda b, i: (b, 0, 0)),
                  pl.BlockSpec((1, Lp, D_KV), lambda b, i: (b, 0, 0)),
                  pl.BlockSpec((1, Lp, LANES), lambda b, i: (b, 0, 0)),
                  pl.BlockSpec((1, Lp, LANES), lambda b, i: (b, 0, 0)),
                  _const_spec((N_HEADS_A, tq, 2 * kblk)),
                  _const_spec((tq, 2 * kblk))],
        out_specs=pl.BlockSpec((1, tq, D_ATTN), lambda b, i: (b, i, 0)),
        out_shape=jax.ShapeDtypeStruct((B, T, D_ATTN), BF16),
        scratch_shapes=[pltpu.VMEM((n_kb, tq, kblk), jnp.int32),
                        pltpu.VMEM((n_kb, tq, kblk), F32),
                        pltpu.VMEM((N_IDX_HEADS, tq, LANES), F32),
                        pltpu.VMEM((N_KV_A, hpg * tq, HEAD_DIM_A), BF16),
                        pltpu.VMEM((tq, 1), F32),
                        pltpu.VMEM((N_KV_A, hpg * tq, LANES), F32),
                        pltpu.VMEM((N_KV_A, hpg * tq, LANES), F32),
                        pltpu.VMEM((N_KV_A, hpg * tq, HEAD_DIM_A), F32)],
        compiler_params=pltpu.CompilerParams(dimension_semantics=("arbitrary", "arbitrary"),
                                             vmem_limit_bytes=VMEM_LIMIT),
        name="dsa",
    )(rel_bias, q, qi, kiwi, kb, vb, kie, kio, bias_near, vis.astype(F32))


def _mix_kernel(x_ref, a_ref, u_ref, hist_ref, mk_ref, mv_ref, wpool_ref, pscale_ref, wout_ref, gx_ref,
                wxq_ref, wxo_ref, gffn_ref, wr_ref, br_ref, o_ref, ext, *, tm, pos0):
    t = pl.program_id(1)
    H = POOL_HIST + 1

    @pl.when(t == 0)
    def _():
        ext[0:H, :] = hist_ref[0]

    @pl.when(t > 0)
    def _():
        ext[0:H, :] = ext[tm:tm + H, :]

    ext[H:H + tm, :] = u_ref[0]

    pos = pos0 + t * tm + lax.broadcasted_iota(jnp.int32, (tm, 1), 0)
    pouts = []
    for gi, w in enumerate(POOL_WINDOWS):
        c0, c1 = gi * POOL_GROUP_DIM, (gi + 1) * POOL_GROUP_DIM
        wsum = ext[H:H + tm, c0:c1]
        for s in range(1, w):
            wsum = wsum + ext[H - s:H - s + tm, c0:c1]
        cnt = jnp.minimum(pos + 1, w).astype(F32)
        d = wsum / cnt - u_ref[0, :, c0:c1]
        y = _dot(d.astype(BF16), wpool_ref[gi]) * pscale_ref[:, c0:c1]
        pouts.append(y.astype(BF16))
    pcat = jnp.concatenate(pouts, axis=1)

    x1 = x_ref[0] + _dot(a_ref[0], wout_ref[0:D_ATTN, :]) + _dot(pcat, wout_ref[D_ATTN:D_ATTN + D_POOL, :])

    h = _rms(x1, gx_ref[...]).astype(BF16)
    qx = _dot(h, wxq_ref[...])
    outs = []
    for hh in range(N_XHEADS):
        hs = slice(hh * XHEAD_DIM, (hh + 1) * XHEAD_DIM)
        logits = _dot_nt(qx[:, hs].astype(BF16), mk_ref[0, :, hs]) * (XHEAD_DIM ** -0.5)
        m = jnp.max(logits, axis=1, keepdims=True)
        p = jnp.exp(logits - m)
        l = jnp.sum(p, axis=1, keepdims=True)
        outs.append((_dot(p.astype(BF16), mv_ref[0, :, hs]) / l).astype(BF16))
    o = jnp.concatenate(outs, axis=1)
    x2 = x1 + _dot(o, wxo_ref[...])
    D = x2.shape[1]
    o_ref[0, :, 0:D] = x2
    hf = _rms(x2, gffn_ref[...]).astype(BF16)
    o_ref[0, :, D:D + LANES] = _route(_dot(hf, wr_ref[...]) + br_ref[...])


def _mix(x, a, u, hist, mk, mv, w_pool, pool_scale, w_out, g_x, w_xq, w_xo, g_ffn, w_r, b_r, *, pos0, tm):
    B, T, D = x.shape
    n_mem = mk.shape[1]
    H = POOL_HIST + 1
    assert T % tm == 0 and tm >= H
    hist16 = jnp.concatenate([jnp.zeros((B, 1, D_POOL), F32), hist], axis=1)
    return pl.pallas_call(
        functools.partial(_mix_kernel, tm=tm, pos0=pos0),
        grid=(B, T // tm),
        in_specs=[pl.BlockSpec((1, tm, D), lambda b, t: (b, t, 0)),
                  pl.BlockSpec((1, tm, D_ATTN), lambda b, t: (b, t, 0)),
                  pl.BlockSpec((1, tm, D_POOL), lambda b, t: (b, t, 0)),
                  pl.BlockSpec((1, H, D_POOL), lambda b, t: (b, 0, 0)),
                  pl.BlockSpec((1, n_mem, D_X), lambda b, t: (b, 0, 0)),
                  pl.BlockSpec((1, n_mem, D_X), lambda b, t: (b, 0, 0)),
                  _const_spec(w_pool.shape),
                  _const_spec((1, D_POOL)),
                  _const_spec(w_out.shape),
                  _const_spec((1, D)),
                  _const_spec(w_xq.shape),
                  _const_spec(w_xo.shape),
                  _const_spec((1, D)),
                  _const_spec(w_r.shape),
                  _const_spec((1, LANES))],
        out_specs=pl.BlockSpec((1, tm, D + LANES), lambda b, t: (b, t, 0)),
        out_shape=jax.ShapeDtypeStruct((B, T, D + LANES), F32),
        scratch_shapes=[pltpu.VMEM((H + tm, D_POOL), F32)],
        compiler_params=pltpu.CompilerParams(dimension_semantics=("arbitrary", "arbitrary"),
                                             vmem_limit_bytes=VMEM_LIMIT),
        name="mix",
    )(x, a, u, hist16, mk, mv, w_pool, pool_scale.reshape(1, D_POOL), w_out, g_x.reshape(1, D), w_xq, w_xo,
      g_ffn.reshape(1, D), w_r, b_r)


R_OFF = N_GROUPS


def _route(logits):
    tm = logits.shape[0]
    lane = lax.broadcasted_iota(jnp.int32, (tm, LANES), 1).astype(F32)
    ninf = -jnp.inf
    big = float(LANES)
    gl = jnp.where(lane < N_GROUPS, logits, ninf)
    gmax = jnp.max(gl, axis=1, keepdims=True)
    g_sel = jnp.min(jnp.where(gl == gmax, lane, big), axis=1, keepdims=True)
    g_prob = 1.0 / jnp.sum(jnp.exp(gl - gmax), axis=1, keepdims=True)
    lo = R_OFF + g_sel * EXPERTS_PER_GROUP
    el = jnp.where((lane >= lo) & (lane < lo + EXPERTS_PER_GROUP), logits, ninf)
    tv0 = jnp.max(el, axis=1, keepdims=True)
    ti0 = jnp.min(jnp.where(el == tv0, lane, big), axis=1, keepdims=True)
    el2 = jnp.where(lane == ti0, ninf, el)
    tv1 = jnp.max(el2, axis=1, keepdims=True)
    ti1 = jnp.min(jnp.where(el2 == tv1, lane, big), axis=1, keepdims=True)
    e1 = jnp.exp(tv1 - tv0)
    den = 1.0 + e1
    w0 = g_prob / den
    w1 = g_prob * e1 / den
    ids = jnp.where(lane == 0.0, jnp.minimum(ti0, ti1) - R_OFF,
                    jnp.where(lane == 1.0, jnp.maximum(ti0, ti1) - R_OFF, 0.0))
    return jnp.where(lane == ti0, w0, 0.0) + jnp.where(lane == ti1, w1, 0.0) + ids


def _expert(h, rec, e, wg, wu, wd):
    a = _dot(h, wg)
    b = _dot(h, wu)
    lane = lax.broadcasted_iota(jnp.int32, rec.shape, 1)
    cw = jnp.sum(jnp.where(lane == e + R_OFF, rec, 0.0), axis=1, keepdims=True)
    hid = a * jax.nn.sigmoid(a) * b * cw
    return _dot(hid.astype(BF16), wd)


def _moe_kernel(x_ref, gffn_ref, wg_ref, wu_ref, wd_ref, gfin_ref, o_ref, h_s, acc_s):
    e = pl.program_id(1)
    d = o_ref.shape[1]

    @pl.when(e == 0)
    def _():
        h_s[...] = _rms(x_ref[:, 0:d], gffn_ref[...]).astype(BF16)
        acc_s[...] = jnp.zeros(acc_s.shape, F32)

    acc_s[...] += _expert(h_s[...], x_ref[:, d:d + LANES], e, wg_ref[0], wu_ref[0], wd_ref[0])

    @pl.when(e == pl.num_programs(1) - 1)
    def _():
        o_ref[...] = _rms(x_ref[:, 0:d] + acc_s[...], gfin_ref[...])


def _moe(x, g_ffn, w_gate, w_up, w_down, g_final, *, tm):
    n, de = x.shape
    d = de - LANES
    assert n % tm == 0
    return pl.pallas_call(
        _moe_kernel,
        grid=(n // tm, N_EXPERTS),
        in_specs=[pl.BlockSpec((tm, de), lambda i, e: (i, 0)),
                  _const_spec((1, d)),
                  pl.BlockSpec((1, d, D_EXPERT), lambda i, e: (e, 0, 0)),
                  pl.BlockSpec((1, d, D_EXPERT), lambda i, e: (e, 0, 0)),
                  pl.BlockSpec((1, D_EXPERT, d), lambda i, e: (e, 0, 0)),
                  _const_spec((1, d))],
        out_specs=pl.BlockSpec((tm, d), lambda i, e: (i, 0)),
        out_shape=jax.ShapeDtypeStruct((n, d), F32),
        scratch_shapes=[pltpu.VMEM((tm, d), BF16),
                        pltpu.VMEM((tm, d), F32)],
        compiler_params=pltpu.CompilerParams(dimension_semantics=("arbitrary", "arbitrary"),
                                             vmem_limit_bytes=VMEM_LIMIT),
        name="moe",
    )(x, g_ffn.reshape(1, d), w_gate, w_up, w_down, g_final.reshape(1, d))


def _moe_sparse_kernel(ea_ref, eb_ref, nv_ref, idx_ref, idxn_ref, x_hbm, gffn_ref, gfin_ref,
                       wga_ref, wua_ref, wda_ref, wgb_ref, wub_ref, wdb_ref, o_hbm,
                       xbuf, ybuf, gsem, ssem, *, tm):
    t = pl.program_id(0)
    nt = pl.num_programs(0)
    d = o_hbm.shape[1]
    slot = lax.rem(t, 2)
    nslot = 1 - slot
    nv = nv_ref[t]
    WAIT_ROWS = 8

    def gather_row(ids, s, r):
        return pltpu.make_async_copy(x_hbm.at[pl.ds(ids[0, 0, r], 1), :], xbuf.at[s, pl.ds(r, 1), :], gsem.at[s])

    def gather_loop(ids, s):
        def body(r, carry):
            gather_row(ids, s, r).start()
            return carry
        lax.fori_loop(0, tm, body, 0)

    def gather_wait(s):
        pltpu.make_async_copy(x_hbm.at[pl.ds(0, tm), :], xbuf.at[s], gsem.at[s]).wait()

    def scatter_rows(s, r, row, n):
        return pltpu.make_async_copy(ybuf.at[s, pl.ds(r, n), :], o_hbm.at[pl.ds(row, n), :], ssem.at[s])

    def scatter_wait(s, n):
        def body_many(r, carry):
            scatter_rows(s, 0, 0, WAIT_ROWS).wait()
            return carry
        lax.fori_loop(0, n // WAIT_ROWS, body_many, 0)

        def body_one(r, carry):
            scatter_rows(s, 0, 0, 1).wait()
            return carry
        lax.fori_loop(0, lax.rem(n, WAIT_ROWS), body_one, 0)

    @pl.when(t == 0)
    def _():
        gather_loop(idx_ref, 0)

    @pl.when(t >= 2)
    def _():
        scatter_wait(slot, nv_ref[jnp.maximum(t - 2, 0)])

    gather_wait(slot)

    @pl.when(nv > 0)
    def _():
        x = xbuf[slot]
        xr = x[:, 0:d]
        rec = x[:, d:d + LANES]
        h = _rms(xr, gffn_ref[...]).astype(BF16)
        for r in range(tm):
            gather_row(idxn_ref, nslot, r).start()
        acc = _expert(h, rec, ea_ref[t], wga_ref[0], wua_ref[0], wda_ref[0])
        acc = acc + _expert(h, rec, eb_ref[t], wgb_ref[0], wub_ref[0], wdb_ref[0])
        ybuf[slot] = _rms(xr + acc, gfin_ref[...])

        def body(r, carry):
            scatter_rows(slot, r, idx_ref[0, 0, r], 1).start()
            return carry
        lax.fori_loop(0, nv, body, 0)

    @pl.when(nv <= 0)
    def _():
        gather_loop(idxn_ref, nslot)

    @pl.when(t == nt - 1)
    def _():
        gather_wait(nslot)

        @pl.when(t >= 1)
        def _():
            scatter_wait(nslot, nv_ref[jnp.maximum(t - 1, 0)])
        scatter_wait(slot, nv)


def _moe_sparse(x, g_ffn, w_gate, w_up, w_down, g_final, *, tm):
    n, de = x.shape
    d = de - LANES
    n_pairs = N_GROUPS * (EXPERTS_PER_GROUP * (EXPERTS_PER_GROUP - 1) // 2)
    nt = -(-n // tm) + n_pairs
    n_keys = N_EXPERTS * N_EXPERTS

    ids = x[:, d:d + 2].astype(jnp.int32)
    key = ids[:, 0] * N_EXPERTS + ids[:, 1]
    order = jnp.argsort(key).astype(jnp.int32)
    skey = key[order]
    edges = jnp.searchsorted(skey, jnp.arange(n_keys + 1, dtype=jnp.int32), side="left",
                             method="compare_all").astype(jnp.int32)
    starts, ends = edges[:-1], edges[1:]
    tiles = (ends - starts + tm - 1) // tm
    cum = jnp.cumsum(tiles)
    total = cum[-1]
    tt = jnp.arange(nt, dtype=jnp.int32)
    tc = jnp.minimum(tt, total - 1)
    cls = jnp.searchsorted(cum, tc, side="right", method="compare_all").astype(jnp.int32)
    first = cum[cls] - tiles[cls]
    tstart = starts[cls] + (tc - first) * tm
    nv = jnp.where(tt < total, jnp.clip(ends[cls] - tstart, 0, tm), 0).astype(jnp.int32)
    ea = (cls // N_EXPERTS).astype(jnp.int32)
    eb = (cls % N_EXPERTS).astype(jnp.int32)
    rows = jnp.clip(tstart[:, None] + jnp.arange(tm, dtype=jnp.int32)[None, :], 0, n - 1)
    idx = order[rows].reshape(nt, 1, tm)

    wspec_a = lambda shape: pl.BlockSpec(shape, lambda t, ea, eb, nv: (ea[t], 0, 0))
    wspec_b = lambda shape: pl.BlockSpec(shape, lambda t, ea, eb, nv: (eb[t], 0, 0))
    cspec = lambda shape: pl.BlockSpec(shape, lambda t, ea, eb, nv: (0,) * len(shape),
                                       pipeline_mode=pl.Buffered(1))
    grid_spec = pltpu.PrefetchScalarGridSpec(
        num_scalar_prefetch=3,
        grid=(nt,),
        in_specs=[pl.BlockSpec((1, 1, tm), lambda t, ea, eb, nv: (t, 0, 0), memory_space=pltpu.SMEM),
                  pl.BlockSpec((1, 1, tm), lambda t, ea, eb, nv: (jnp.minimum(t + 1, nt - 1), 0, 0),
                               memory_space=pltpu.SMEM),
                  pl.BlockSpec(memory_space=pl.ANY),
                  cspec((1, d)), cspec((1, d)),
                  wspec_a((1, d, D_EXPERT)), wspec_a((1, d, D_EXPERT)), wspec_a((1, D_EXPERT, d)),
                  wspec_b((1, d, D_EXPERT)), wspec_b((1, d, D_EXPERT)), wspec_b((1, D_EXPERT, d))],
        out_specs=pl.BlockSpec(memory_space=pl.ANY),
        scratch_shapes=[pltpu.VMEM((2, tm, de), F32),
                        pltpu.VMEM((2, tm, d), F32),
                        pltpu.SemaphoreType.DMA((2,)),
                        pltpu.SemaphoreType.DMA((2,))])
    return pl.pallas_call(
        functools.partial(_moe_sparse_kernel, tm=tm),
        grid_spec=grid_spec,
        out_shape=jax.ShapeDtypeStruct((n, d), F32),
        compiler_params=pltpu.CompilerParams(dimension_semantics=("arbitrary",),
                                             vmem_limit_bytes=VMEM_LIMIT),
        name="moe_sparse",
    )(ea, eb, nv, idx, idx, x, g_ffn.reshape(1, d), g_final.reshape(1, d),
      w_gate, w_up, w_down, w_gate, w_up, w_down)


C_Q = 0
C_QI = C_Q + D_ATTN
C_K = C_QI + N_IDX_HEADS * IDX_DIM
C_V = C_K + D_KV
C_KIWI = C_V + D_KV
C_KIE = C_KIWI + LANES
C_KIO = C_KIE + LANES
C_U = C_KIO + LANES
C_END = C_U + D_POOL


def _layout_w_in(w_in):
    d = w_in.shape[0]
    offs = np.cumsum((0, D_ATTN, D_KV, D_KV, N_IDX_HEADS * IDX_DIM, IDX_DIM, N_IDX_HEADS, D_POOL))
    wq, wk, wv, wqi, wki, wwi, wu = (w_in[:, offs[n]:offs[n + 1]] for n in range(7))
    z = lambda n: jnp.zeros((d, n), w_in.dtype)
    cat = jnp.concatenate([wq, wqi, wk, wv,
                           wki, wwi, z(LANES - IDX_DIM - N_IDX_HEADS),
                           wki, z(LANES - IDX_DIM),
                           z(LANES - IDX_DIM), wki,
                           wu], axis=1)
    assert cat.shape[1] == C_END
    return cat.astype(BF16)


IN_GROUPS = ((C_Q, C_QI, 1.0, 1),
             (C_QI, C_K, 1.0, 1),
             (C_K, C_V, 1.0, 2),
             (C_V, C_KIWI, 1.0, 2),
             (C_KIWI, C_KIE, 1.0, 1),
             (C_KIE, C_KIO, 1.0, 1),
             (C_KIO, C_U, 1.0, 1),
             (C_U, C_END, 1.0, 1))
IN_DTYPES = (BF16, BF16, F32, BF16, F32, BF16, F32, BF16, BF16, F32)


def _pad_rows(a, n):
    return jnp.pad(a, ((0, 0), (0, n - a.shape[1]), (0, 0)))


def _layer(x, pos0, caches, pool_hist, mk, mv, rel_bias, W, *, tm_in, tq, tm_mix, tm_moe, sparse_moe):
    B, T, D = x.shape
    n = B * T
    q, qi, k, kb, v, vb, kiwi, kie, kio, u = _norm_matmul(
        x.reshape(n, D), W["g_mix"], W["w_in"], IN_GROUPS, IN_DTYPES, tm_in)
    r3 = lambda a: a.reshape(B, T, a.shape[-1])
    kblk = 256
    if caches is None:
        n_keys = T
        kb3, vb3, kie3, kio3 = r3(kb), r3(vb), r3(kie), r3(kio)
    else:
        k_hist, v_hist, ki_hist = caches
        past = k_hist.shape[1]
        n_keys = past + T
        lp = -(-n_keys // kblk) * kblk
        ki = r3(kiwi)[:, :, :IDX_DIM]
        ki_all = jnp.concatenate([ki_hist, ki], axis=1)
        zeros = jnp.zeros_like(ki_all)
        kb3 = _pad_rows(jnp.concatenate([k_hist.reshape(B, past, D_KV), r3(k)], axis=1), lp).astype(BF16)
        vb3 = _pad_rows(jnp.concatenate([v_hist.reshape(B, past, D_KV), r3(v)], axis=1), lp).astype(BF16)
        kie3 = _pad_rows(jnp.concatenate([ki_all, zeros], axis=2), lp).astype(BF16)
        kio3 = _pad_rows(jnp.concatenate([zeros, ki_all], axis=2), lp).astype(BF16)
    a_out = _dsa(r3(q), r3(qi), r3(kiwi), kb3, vb3, kie3, kio3, rel_bias,
                 n_keys=n_keys, pos0=pos0, tq=tq, kblk=kblk)
    x2 = _mix(x, a_out, r3(u), pool_hist, mk.astype(BF16), mv.astype(BF16), W["w_pool"], W["pool_scale"],
              W["w_out"], W["g_x"], W["w_xq"], W["w_xo"], W["g_ffn"], W["w_r"], W["b_r"], pos0=pos0, tm=tm_mix)
    moe = _moe_sparse if sparse_moe else _moe
    y = moe(x2.reshape(n, D + LANES), W["g_ffn"], W["w_gate"], W["w_up"], W["w_down"], W["g_final"], tm=tm_moe)
    new_pool = jnp.concatenate([pool_hist, r3(u)], axis=1)[:, -POOL_HIST:]
    return (y.reshape(B, T, D), k.reshape(B, T, N_KV_A, HEAD_DIM_A), v.reshape(B, T, N_KV_A, HEAD_DIM_A),
            r3(kiwi)[:, :, :IDX_DIM], new_pool)


def kernel(x_prompt, x_sample, mem_prompt, cache_k, cache_v, cache_k_idx, cache_pool, cache_mem_k, cache_mem_v, rel_bias, g_mix, w_in, w_pool, pool_scale, w_out, g_mem, w_mk, w_mv, g_x, w_xq, w_xo, g_ffn, w_rg, b_rg, w_re, b_re, w_gate, w_up, w_down, g_final):
    depth = g_mix.shape[0]
    assert depth == 1
    l = 0
    B, T, D = x_prompt.shape
    Bs, Ts, _ = x_sample.shape
    n_mem = mem_prompt.shape[1]
    past = cache_k.shape[2]

    w_r = jnp.concatenate([w_rg[l], w_re[l], jnp.zeros((D, LANES - N_GROUPS - N_EXPERTS), F32)], axis=1)
    W = dict(
        g_mix=g_mix[l], w_in=_layout_w_in(w_in[l]), w_pool=w_pool[l].astype(BF16), pool_scale=pool_scale[l],
        w_out=w_out[l].astype(BF16), g_x=g_x[l], w_xq=w_xq[l].astype(BF16), w_xo=w_xo[l].astype(BF16),
        g_ffn=g_ffn[l], w_r=w_r.astype(BF16),
        b_r=jnp.concatenate([b_rg[l], b_re[l], jnp.zeros((LANES - N_GROUPS - N_EXPERTS,), F32)]).reshape(1, LANES),
        w_gate=w_gate[l].astype(BF16), w_up=w_up[l].astype(BF16), w_down=w_down[l].astype(BF16),
        g_final=g_final)

    w_m = jnp.concatenate([w_mk[l], w_mv[l]], axis=1).astype(BF16)
    mk, mv = _norm_matmul(mem_prompt.reshape(B * n_mem, D), g_mem[l], w_m,
                          ((0, D_X, 1.0, 1), (D_X, 2 * D_X, 1.0, 1)), (F32, F32), 256)
    mk = mk.reshape(B, n_mem, D_X)
    mv = mv.reshape(B, n_mem, D_X)

    yp, kp, vp, kip, pp = _layer(x_prompt, 0, None, jnp.zeros((B, POOL_HIST, D_POOL), F32), mk, mv, rel_bias, W,
                                 tm_in=256, tq=256, tm_mix=256, tm_moe=128, sparse_moe=True)
    ys, ks, vs, kis, ps = _layer(x_sample, past,
                                 (cache_k[l], cache_v[l], cache_k_idx[l]), cache_pool[l],
                                 cache_mem_k[l].reshape(Bs, n_mem, D_X), cache_mem_v[l].reshape(Bs, n_mem, D_X),
                                 rel_bias, W, tm_in=Bs * Ts, tq=Ts, tm_mix=Ts, tm_moe=Bs * Ts, sparse_moe=False)
    st = lambda a: a[None]
    return (yp, ys, st(kp), st(vp), st(kip), st(pp),
            st(mk.reshape(B, n_mem, N_XHEADS, XHEAD_DIM)), st(mv.reshape(B, n_mem, N_XHEADS, XHEAD_DIM)),
            st(ks), st(vs), st(kis), st(ps))
```

```python
import functools
import math

import numpy as np
import jax
import jax.numpy as jnp
from jax import lax
from jax.experimental import pallas as pl
from jax.experimental.pallas import tpu as pltpu

F32 = jnp.float32
BF16 = jnp.bfloat16

CHUNK = 64
N_HEADS_A = 8
HEAD_DIM_A = 128
N_KV_A = 2
D_ATTN = N_HEADS_A * HEAD_DIM_A
D_KV = N_KV_A * HEAD_DIM_A
N_IDX_HEADS = 8
IDX_DIM = 64
TOPK_MAX = 256
POOL_WINDOWS = (2, 4, 8, 16)
POOL_GROUP_DIM = 256
D_POOL = len(POOL_WINDOWS) * POOL_GROUP_DIM
POOL_HIST = max(POOL_WINDOWS) - 1
N_BUCKETS = 32
MAX_DISTANCE = 128
N_XHEADS = 4
XHEAD_DIM = 128
D_X = N_XHEADS * XHEAD_DIM
N_GROUPS = 4
EXPERTS_PER_GROUP = 8
N_EXPERTS = N_GROUPS * EXPERTS_PER_GROUP
D_EXPERT = 256
EPS = 1e-6

LANES = 128
INT_MIN = -(2 ** 31)
NEG = -1e30
LOG2E = math.log2(math.e)
VMEM_LIMIT = 48 * 1024 * 1024


def _rms(x, g):
    ms = jnp.mean(x * x, axis=-1, keepdims=True)
    return x * lax.rsqrt(ms + EPS) * g


def _dot(a, b):
    return jnp.dot(a, b, preferred_element_type=F32)


def _dot_nt(a, b):
    return lax.dot_general(a, b, (((1,), (1,)), ((), ())), preferred_element_type=F32)


def _const_spec(shape):
    nd = len(shape)
    return pl.BlockSpec(shape, lambda *_: (0,) * nd, pipeline_mode=pl.Buffered(1))


def _norm_matmul_kernel(x_ref, g_ref, w_ref, *out_refs, groups):
    h = _rms(x_ref[...], g_ref[...]).astype(BF16)
    k = 0
    for c0, c1, scale, n_out in groups:
        acc = _dot(h, w_ref[:, c0:c1])
        if scale != 1.0:
            acc = acc * scale
        for _ in range(n_out):
            out_refs[k][...] = acc.astype(out_refs[k].dtype)
            k += 1


def _norm_matmul(x, g, w, groups, out_dtypes, tm):
    n, d = x.shape
    assert n % tm == 0
    widths = []
    for c0, c1, _, n_out in groups:
        widths += [c1 - c0] * n_out
    return pl.pallas_call(
        functools.partial(_norm_matmul_kernel, groups=tuple(groups)),
        grid=(n // tm,),
        in_specs=[pl.BlockSpec((tm, d), lambda i: (i, 0)),
                  _const_spec((1, d)),
                  _const_spec(w.shape)],
        out_specs=[pl.BlockSpec((tm, wd), lambda i: (i, 0)) for wd in widths],
        out_shape=[jax.ShapeDtypeStruct((n, wd), dt) for wd, dt in zip(widths, out_dtypes)],
        compiler_params=pltpu.CompilerParams(dimension_semantics=("arbitrary",),
                                             vmem_limit_bytes=VMEM_LIMIT),
        name="norm_matmul",
    )(x, g.reshape(1, d), w)


def _bias_table_kernel(relb_ref, bucket_ref, o_ref):
    bucket = bucket_ref[...]
    for h in range(N_HEADS_A):
        acc = jnp.zeros(bucket.shape, F32)
        for b in range(N_BUCKETS):
            acc = jnp.where(bucket == b, relb_ref[b, h], acc)
        o_ref[h] = acc * LOG2E


def _bias_table(rel_bias, bucket):
    tq, w = bucket.shape
    return pl.pallas_call(
        _bias_table_kernel,
        in_specs=[pl.BlockSpec(memory_space=pltpu.SMEM),
                  pl.BlockSpec((tq, w), lambda: (0, 0))],
        out_specs=pl.BlockSpec((N_HEADS_A, tq, w), lambda: (0, 0, 0)),
        out_shape=jax.ShapeDtypeStruct((N_HEADS_A, tq, w), F32),
        name="bias_table",
    )(rel_bias, bucket)


def _t5_bucket(rel):
    nb = N_BUCKETS // 2
    max_exact = nb // 2
    bucket = (rel > 0).astype(jnp.int32) * nb
    n = jnp.abs(rel)
    nf = jnp.maximum(n, 1).astype(F32)
    large = max_exact + (jnp.log(nf / max_exact) / math.log(MAX_DISTANCE / max_exact)
                         * (nb - max_exact)).astype(jnp.int32)
    large = jnp.minimum(large, nb - 1)
    return bucket + jnp.where(n < max_exact, n, large)


def _far_bucket(min_dist, max_dist):
    nb = N_BUCKETS // 2
    max_exact = nb // 2
    n = np.arange(min_dist, max_dist + 1, dtype=np.float64)
    large = max_exact + np.floor(np.log(n / max_exact) / math.log(MAX_DISTANCE / max_exact)
                                 * (nb - max_exact) * (1 - 1e-6)).astype(np.int64)
    assert large.min() >= nb - 1, "far key blocks must sit in the saturated distance bucket"
    return nb - 1


def _dsa_kernel(relb_ref, q_ref, qi_ref, kiwi_ref, k_ref, v_ref, kie_ref, kio_ref, bnear_ref, vis_ref,
                o_ref, keys, madd, wb, qs, xcut, mrun, lrun, acc_s,
                *, TQ, KB, RB, kb0, kstep, topk, far_bucket, tie_bits):
    i = pl.program_id(1)
    nkb = kb0 + kstep * i
    n_far = jnp.maximum(nkb - 2, 0)
    n_lc = KB // LANES
    HPG = N_HEADS_A // N_KV_A

    kiwi = kiwi_ref[0]
    for h in range(N_IDX_HEADS):
        w = (kiwi[:, IDX_DIM + h:IDX_DIM + h + 1] * (N_IDX_HEADS ** -0.5)).astype(BF16).astype(F32)
        wb[h] = jnp.broadcast_to(w * (IDX_DIM ** -0.5), (TQ, LANES))

    def idx_block(j, near_col):
        c0 = pl.multiple_of(j * KB, KB)
        ke = kie_ref[0, pl.ds(c0, KB), :]
        ko = kio_ref[0, pl.ds(c0, KB), :]
        acc = [jnp.zeros((TQ, LANES), F32) for _ in range(n_lc)]
        for hp in range(N_IDX_HEADS // 2):
            qh = qi_ref[0, :, hp * LANES:(hp + 1) * LANES]
            se = _dot_nt(qh, ke)
            so = _dot_nt(qh, ko)
            we = wb[2 * hp]
            wo = wb[2 * hp + 1]
            for c in range(n_lc):
                cs = slice(c * LANES, (c + 1) * LANES)
                re = jnp.maximum(se[:, cs], 0.0).astype(BF16).astype(F32)
                ro = jnp.maximum(so[:, cs], 0.0).astype(BF16).astype(F32)
                acc[c] = acc[c] + we * re + wo * ro
        for c in range(n_lc):
            bits = lax.bitcast_convert_type(acc[c], jnp.int32)
            key = bits ^ ((bits >> 31) & jnp.int32(0x7FFFFFFF))
            if near_col is not None:
                vis = vis_ref[:, near_col + c * LANES:near_col + (c + 1) * LANES]
                key = jnp.where(vis > 0.0, key, jnp.int32(INT_MIN))
            keys[j, :, c * LANES:(c + 1) * LANES] = key

    def far_idx(j, carry):
        idx_block(j, None)
        return carry

    lax.fori_loop(0, n_far, far_idx, 0)

    @pl.when(nkb >= 2)
    def _():
        idx_block(nkb - 2, 0)

    idx_block(nkb - 1, KB)

    kf = float(topk)
    lane_f = lax.broadcasted_iota(jnp.int32, (RB, LANES), 1).astype(F32)
    NRB = TQ // RB
    rbs = [slice(rb * RB, (rb + 1) * RB) for rb in range(NRB)]

    def count(pred):
        def body(jj, accs):
            out = []
            for rb in range(NRB):
                blk = keys[jj, rbs[rb], :]
                acc = accs[rb]
                for c in range(n_lc):
                    acc = acc + jnp.where(pred(rb, blk[:, c * LANES:(c + 1) * LANES], jj, c), 1.0, 0.0)
                out.append(acc)
            return tuple(out)
        accs = lax.fori_loop(0, nkb, body, tuple(jnp.zeros((RB, LANES), F32) for _ in range(NRB)))
        return [jnp.sum(acc, axis=1, keepdims=True) for acc in accs]

    def count_ge(cands):
        cb = [jnp.broadcast_to(cand, (RB, LANES)) for cand in cands]
        return count(lambda rb, kblk, jj, c: kblk >= cb[rb])

    def bis_body(p, thrs):
        cands = [thr + lax.shift_left(jnp.int32(1), 31 - p) for thr in thrs]
        cnts = count_ge(cands)
        return tuple(jnp.where(cnt >= kf, cand, thr) for cnt, cand, thr in zip(cnts, cands, thrs))

    thrs = lax.fori_loop(0, 32, bis_body, tuple(jnp.full((RB, 1), INT_MIN, jnp.int32) for _ in range(NRB)))
    thrs = [jnp.maximum(thr, jnp.int32(INT_MIN + 1)) for thr in thrs]
    cnt_ge = count_ge(thrs)
    cnt_gt = count_ge([thr + 1 for thr in thrs])
    needs = [kf - c for c in cnt_gt]
    thrb = [jnp.broadcast_to(thr, (RB, LANES)) for thr in thrs]

    xcut[...] = jnp.full(xcut.shape, 2.0 ** 24, F32)
    excess = cnt_ge[0] - kf
    for c in cnt_ge[1:]:
        excess = jnp.maximum(excess, c - kf)

    @pl.when(jnp.max(excess) > 0.0)
    def _():
        def count_eq_lt(xs):
            xb = [jnp.broadcast_to(x, (RB, LANES)) for x in xs]

            def pred(rb, kblk, jj, c):
                col = lane_f + (jj * KB + c * LANES).astype(F32)
                return jnp.where(kblk == thrb[rb], col, 2.0 ** 25) < xb[rb]
            return count(pred)

        def tie_body(p, xs):
            cands = [x + lax.shift_left(jnp.int32(1), tie_bits - 1 - p).astype(F32) for x in xs]
            cnts = count_eq_lt(cands)
            return tuple(jnp.where(cnt < need, cand, x) for cnt, need, cand, x in zip(cnts, needs, cands, xs))

        xs = lax.fori_loop(0, tie_bits, tie_body, tuple(jnp.zeros((RB, 1), F32) for _ in range(NRB)))
        for rb in range(NRB):
            xcut[rbs[rb], :] = xs[rb]

    xb = [jnp.broadcast_to(xcut[rbs[rb], :], (RB, LANES)) for rb in range(NRB)]

    def mask_body(jj, carry):
        for rb in range(NRB):
            blk = keys[jj, rbs[rb], :]
            for c in range(n_lc):
                kblk = blk[:, c * LANES:(c + 1) * LANES]
                col = lane_f + (jj * KB + c * LANES).astype(F32)
                keep = (kblk > thrb[rb]) | ((kblk == thrb[rb]) & (col <= xb[rb]))
                madd[jj, rbs[rb], c * LANES:(c + 1) * LANES] = jnp.where(keep, 0.0, NEG)
        return carry

    lax.fori_loop(0, nkb, mask_body, 0)

    _attend(relb_ref, q_ref, k_ref, v_ref, bnear_ref, o_ref, madd, qs, mrun, lrun, acc_s,
            TQ=TQ, KB=KB, nkb=nkb, n_far=n_far, far_bucket=far_bucket)


def _attend(relb_ref, q_ref, k_ref, v_ref, bnear_ref, o_ref, madd, qs, mrun, lrun, acc_s,
            *, TQ, KB, nkb, n_far, far_bucket):
    n_lc = KB // LANES
    HPG = N_HEADS_A // N_KV_A
    for g in range(N_KV_A):
        for r in range(HPG):
            h = g * HPG + r
            qs[g, r * TQ:(r + 1) * TQ, :] = q_ref[0, :, h * HEAD_DIM_A:(h + 1) * HEAD_DIM_A]

    def logits(j, near_col, g):
        c0 = pl.multiple_of(j * KB, KB)
        kg = k_ref[0, pl.ds(c0, KB), g * HEAD_DIM_A:(g + 1) * HEAD_DIM_A]
        s = _dot_nt(qs[g], kg)
        mk = madd[j]
        out = []
        for r in range(HPG):
            h = g * HPG + r
            if near_col is None:
                bias = relb_ref[far_bucket, h] * LOG2E
            else:
                bias = bnear_ref[h, :, near_col:near_col + KB]
            out.append(s[r * TQ:(r + 1) * TQ] * (HEAD_DIM_A ** -0.5 * LOG2E) + bias + mk)
        return out

    def over_blocks(fn):
        def far(j, carry):
            fn(j, None)
            return carry

        lax.fori_loop(0, n_far, far, 0)

        @pl.when(nkb >= 2)
        def _():
            fn(nkb - 2, 0)

        fn(nkb - 1, KB)

    mrun[...] = jnp.full(mrun.shape, NEG, F32)

    def max_block(j, near_col):
        for g in range(N_KV_A):
            for r, sh in enumerate(logits(j, near_col, g)):
                rs = slice(r * TQ, (r + 1) * TQ)
                m = mrun[g, rs, :]
                for c in range(n_lc):
                    m = jnp.maximum(m, sh[:, c * LANES:(c + 1) * LANES])
                mrun[g, rs, :] = m

    over_blocks(max_block)

    for g in range(N_KV_A):
        mrun[g] = jnp.broadcast_to(jnp.max(mrun[g], axis=1, keepdims=True), mrun.shape[1:])
    lrun[...] = jnp.zeros(lrun.shape, F32)
    acc_s[...] = jnp.zeros(acc_s.shape, F32)

    def pv_block(j, near_col):
        c0 = pl.multiple_of(j * KB, KB)
        for g in range(N_KV_A):
            vg = v_ref[0, pl.ds(c0, KB), g * HEAD_DIM_A:(g + 1) * HEAD_DIM_A]
            parts = []
            for r, sh in enumerate(logits(j, near_col, g)):
                rs = slice(r * TQ, (r + 1) * TQ)
                mb = mrun[g, rs, :]
                l = lrun[g, rs, :]
                pcs = []
                for c in range(n_lc):
                    p = jnp.exp2(sh[:, c * LANES:(c + 1) * LANES] - mb)
                    l = l + p
                    pcs.append(p.astype(BF16))
                lrun[g, rs, :] = l
                parts.append(jnp.concatenate(pcs, axis=1))
            acc_s[g] += _dot(jnp.concatenate(parts, axis=0), vg)

    over_blocks(pv_block)

    for g in range(N_KV_A):
        out = acc_s[g] / jnp.sum(lrun[g], axis=1, keepdims=True)
        for r in range(HPG):
            h = g * HPG + r
            o_ref[0, :, h * HEAD_DIM_A:(h + 1) * HEAD_DIM_A] = out[r * TQ:(r + 1) * TQ].astype(o_ref.dtype)


DIGIT_BITS = 8
N_DIGITS = 32 // DIGIT_BITS
SUB_BF16 = 16


def _dsa_kernel_t(relb_ref, q_ref, qi_ref, kiwi_ref, k_ref, v_ref, kie_ref, kio_ref, bnear_ref, vist_ref,
                  o_ref, keys, dig, madd, qs, xcut, mrun, lrun, acc_s,
                  *, TQ, KB, kb0, kstep, topk, far_bucket, tie_bits):
    i = pl.program_id(1)
    nkb = kb0 + kstep * i
    n_far = jnp.maximum(nkb - 2, 0)
    n_slab = KB // SUB_BF16

    w_t = kiwi_ref[0].T
    wrow = [(w_t[IDX_DIM + h:IDX_DIM + h + 1, :] * (N_IDX_HEADS ** -0.5)).astype(BF16).astype(F32)
            * (IDX_DIM ** -0.5) for h in range(N_IDX_HEADS)]

    def idx_block(j, near_row):
        c0 = pl.multiple_of(j * KB, KB)
        ke = kie_ref[0, pl.ds(c0, KB), :]
        ko = kio_ref[0, pl.ds(c0, KB), :]
        acc = jnp.zeros((KB, TQ), F32)
        for hp in range(N_IDX_HEADS // 2):
            qh = qi_ref[0, :, hp * LANES:(hp + 1) * LANES]
            re = jnp.maximum(_dot_nt(ke, qh), 0.0).astype(BF16).astype(F32)
            ro = jnp.maximum(_dot_nt(ko, qh), 0.0).astype(BF16).astype(F32)
            acc = acc + wrow[2 * hp] * re + wrow[2 * hp + 1] * ro
        bits = lax.bitcast_convert_type(acc, jnp.int32)
        key = bits ^ ((bits >> 31) & jnp.int32(0x7FFFFFFF))
        if near_row is not None:
            key = jnp.where(vist_ref[near_row:near_row + KB, :] > 0.0, key, jnp.int32(INT_MIN))
        keys[j] = key
        for dk in range(N_DIGITS):
            sh = 32 - DIGIT_BITS * (dk + 1)
            d = (key >> sh) if dk == 0 else ((key >> sh) & jnp.int32(2 ** DIGIT_BITS - 1))
            dig[dk, j] = d.astype(F32).astype(BF16)

    def far_idx(j, carry):
        idx_block(j, None)
        return carry

    lax.fori_loop(0, n_far, far_idx, 0)

    @pl.when(nkb >= 2)
    def _():
        idx_block(nkb - 2, 0)

    idx_block(nkb - 1, KB)

    one = jnp.ones((SUB_BF16, TQ), BF16)
    zero = jnp.zeros((SUB_BF16, TQ), BF16)

    def count_ge(dk, cand):
        cb = jnp.broadcast_to(cand.astype(BF16), (SUB_BF16, TQ))

        def body(jj, tot):
            blk = dig[dk, jj]
            parts = [jnp.where(blk[s * SUB_BF16:(s + 1) * SUB_BF16, :] >= cb, one, zero) for s in range(n_slab)]
            while len(parts) > 1:
                parts = [parts[n] + parts[n + 1] for n in range(0, len(parts), 2)]
            return tot + parts[0].astype(F32)

        tot = lax.fori_loop(0, nkb, body, jnp.zeros((SUB_BF16, TQ), F32))
        return jnp.sum(tot, axis=0, keepdims=True)

    kneed = jnp.full((1, TQ), float(topk), F32)
    thr = jnp.zeros((1, TQ), jnp.int32)
    cnt_eq = None
    for dk in range(N_DIGITS):
        lo = -float(2 ** (DIGIT_BITS - 1)) if dk == 0 else 0.0

        def bis_body(p, t, dk=dk, kneed=kneed):
            cand = t + lax.shift_left(jnp.int32(1), DIGIT_BITS - 1 - p).astype(F32)
            return jnp.where(count_ge(dk, cand) >= kneed, cand, t)

        t = lax.fori_loop(0, DIGIT_BITS, bis_body, jnp.full((1, TQ), lo, F32))
        n_gt = count_ge(dk, t + 1.0)
        if dk == N_DIGITS - 1:
            cnt_eq = count_ge(dk, t) - n_gt
        kneed = kneed - n_gt
        thr = thr + lax.shift_left(t.astype(jnp.int32), 32 - DIGIT_BITS * (dk + 1))
        if dk + 1 < N_DIGITS:
            tb = t.astype(BF16)

            def prep(jj, carry, dk=dk, tb=tb):
                dig[dk + 1, jj] = jnp.where(dig[dk, jj] == tb, dig[dk + 1, jj], jnp.asarray(-1.0, BF16))
                return carry

            lax.fori_loop(0, nkb, prep, 0)

    thr = jnp.maximum(thr, jnp.int32(INT_MIN + 1))
    need = kneed
    sub_f = lax.broadcasted_iota(jnp.int32, (KB, TQ), 0).astype(F32)

    xcut[...] = jnp.full(xcut.shape, 2.0 ** 24, F32)

    @pl.when(jnp.max(cnt_eq - need) > 0.0)
    def _():
        def count_eq_lt(x):
            def body(jj, tot):
                col = sub_f + (jj * KB).astype(F32)
                hit = jnp.where(jnp.where(keys[jj] == thr, col, 2.0 ** 25) < x, 1.0, 0.0)
                for s in range(KB // 8):
                    tot = tot + hit[s * 8:(s + 1) * 8, :]
                return tot
            tot = lax.fori_loop(0, nkb, body, jnp.zeros((8, TQ), F32))
            return jnp.sum(tot, axis=0, keepdims=True)

        def tie_body(p, x):
            cand = x + lax.shift_left(jnp.int32(1), tie_bits - 1 - p).astype(F32)
            return jnp.where(count_eq_lt(cand) < need, cand, x)

        x = lax.fori_loop(0, tie_bits, tie_body, jnp.zeros((1, TQ), F32))
        xcut[...] = jnp.broadcast_to(x, xcut.shape)

    xc = xcut[0:1, :]

    def mask_body(jj, carry):
        kblk = keys[jj]
        col = sub_f + (jj * KB).astype(F32)
        at_thr = jnp.where(col <= xc, 0.0, NEG)
        m_t = jnp.where(kblk > thr, 0.0, jnp.where(kblk == thr, at_thr, NEG))
        madd[jj] = m_t.T
        return carry

    lax.fori_loop(0, nkb, mask_body, 0)

    _attend(relb_ref, q_ref, k_ref, v_ref, bnear_ref, o_ref, madd, qs, mrun, lrun, acc_s,
            TQ=TQ, KB=KB, nkb=nkb, n_far=n_far, far_bucket=far_bucket)


def _dsa(q, qi, kiwi, kb, vb, kie, kio, rel_bias, *, n_keys, pos0, tq, kblk):
    B, T, _ = q.shape
    Lp = kb.shape[1]
    assert T % tq == 0 and Lp % kblk == 0 and tq % CHUNK == 0 or T == tq
    n_qt = T // tq
    if n_qt == 1:
        kb0, kstep = Lp // kblk, 0
        assert ((pos0 + tq - 1) // CHUNK + 1) * CHUNK >= n_keys
    else:
        assert pos0 == 0 and tq == kblk and n_keys == Lp == T
        kb0, kstep = 1, 1
    topk = min(TOPK_MAX, n_keys // 4)
    rb = min(128, tq)

    t = jnp.arange(tq, dtype=jnp.int32)[:, None]
    c = jnp.arange(2 * kblk, dtype=jnp.int32)[None, :]
    if n_qt == 1:
        q_abs = pos0 + t
        s_abs = (kb0 - 2) * kblk + c
    else:
        q_abs = kblk + t
        s_abs = c
    vis = ((s_abs // CHUNK) <= (q_abs // CHUNK)) & (s_abs < (n_keys if n_qt == 1 else 2 * kblk))
    bias_near = _bias_table(rel_bias, _t5_bucket(s_abs - q_abs))
    far_bucket = _far_bucket(kblk + 1, max(Lp, kblk + 2))

    n_kb = Lp // kblk
    hpg = N_HEADS_A // N_KV_A
    common = dict(TQ=tq, KB=kblk, kb0=kb0, kstep=kstep, topk=topk, far_bucket=far_bucket,
                  tie_bits=int(math.ceil(math.log2(Lp))) + 1)
    attend_scratch = [pltpu.VMEM((N_KV_A, hpg * tq, HEAD_DIM_A), BF16),
                      pltpu.VMEM((N_KV_A, hpg * tq, LANES), F32),
                      pltpu.VMEM((N_KV_A, hpg * tq, LANES), F32),
                      pltpu.VMEM((N_KV_A, hpg * tq, HEAD_DIM_A), F32)]
    if tq % LANES == 0:
        return pl.pallas_call(
            functools.partial(_dsa_kernel_t, **common),
            grid=(B, n_qt),
            in_specs=[pl.BlockSpec(memory_space=pltpu.SMEM),
                      pl.BlockSpec((1, tq, D_ATTN), lambda b, i: (b, i, 0)),
                      pl.BlockSpec((1, tq, N_IDX_HEADS * IDX_DIM), lambda b, i: (b, i, 0)),
                      pl.BlockSpec((1, tq, LANES), lambda b, i: (b, i, 0)),
                      pl.BlockSpec((1, Lp, D_KV), lambda b, i: (b, 0, 0)),
                      pl.BlockSpec((1, Lp, D_KV), lambda b, i: (b, 0, 0)),
                      pl.BlockSpec((1, Lp, LANES), lambda b, i: (b, 0, 0)),
                      pl.BlockSpec((1, Lp, LANES), lambda b, i: (b, 0, 0)),
                      _const_spec((N_HEADS_A, tq, 2 * kblk)),
                      _const_spec((2 * kblk, tq))],
            out_specs=pl.BlockSpec((1, tq, D_ATTN), lambda b, i: (b, i, 0)),
            out_shape=jax.ShapeDtypeStruct((B, T, D_ATTN), BF16),
            scratch_shapes=[pltpu.VMEM((n_kb, kblk, tq), jnp.int32),
                            pltpu.VMEM((N_DIGITS, n_kb, kblk, tq), BF16),
                            pltpu.VMEM((n_kb, tq, kblk), F32),
                            attend_scratch[0],
                            pltpu.VMEM((8, tq), F32)] + attend_scratch[1:],
            compiler_params=pltpu.CompilerParams(dimension_semantics=("arbitrary", "arbitrary"),
                                                 vmem_limit_bytes=VMEM_LIMIT),
            name="dsa_t",
        )(rel_bias, q, qi, kiwi, kb, vb, kie, kio, bias_near, vis.astype(F32).T)

    kern = functools.partial(_dsa_kernel, RB=rb, **common)
    return pl.pallas_call(
        kern,
        grid=(B, n_qt),
        in_specs=[pl.BlockSpec(memory_space=pltpu.SMEM),
                  pl.BlockSpec((1, tq, D_ATTN), lambda b, i: (b, i, 0)),
                  pl.BlockSpec((1, tq, N_IDX_HEADS * IDX_DIM), lambda b, i: (b, i, 0)),
                  pl.BlockSpec((1, tq, LANES), lambda b, i: (b, i, 0)),
                  pl.BlockSpec((1, Lp, D_KV), lambda b, i: (b, 0, 0)),
                  pl.BlockSpec((1, Lp, D_KV), lambda b, i: (b, 0, 0)),
                  pl.BlockSpec((1, Lp, LANES), lambda b, i: (b, 0, 0)),
                  pl.BlockSpec((1, Lp, LANES), lambda b, i: (b, 0, 0)),
                  _const_spec((N_HEADS_A, tq, 2 * kblk)),
                  _const_spec((tq, 2 * kblk))],
        out_specs=pl.BlockSpec((1, tq, D_ATTN), lambda b, i: (b, i, 0)),
        out_shape=jax.ShapeDtypeStruct((B, T, D_ATTN), BF16),
        scratch_shapes=[pltpu.VMEM((n_kb, tq, kblk), jnp.int32),
                        pltpu.VMEM((n_kb, tq, kblk), F32),
                        pltpu.VMEM((N_IDX_HEADS, tq, LANES), F32),
                        pltpu.VMEM((N_KV_A, hpg * tq, HEAD_DIM_A), BF16),
                        pltpu.VMEM((tq, 1), F32),
                        pltpu.VMEM((N_KV_A, hpg * tq, LANES), F32),
                        pltpu.VMEM((N_KV_A, hpg * tq, LANES), F32),
                        pltpu.VMEM((N_KV_A, hpg * tq, HEAD_DIM_A), F32)],
        compiler_params=pltpu.CompilerParams(dimension_semantics=("arbitrary", "arbitrary"),
                                             vmem_limit_bytes=VMEM_LIMIT),
        name="dsa",
    )(rel_bias, q, qi, kiwi, kb, vb, kie, kio, bias_near, vis.astype(F32))


def _mix_kernel(x_ref, a_ref, u_ref, hist_ref, mk_ref, mv_ref, wpool_ref, pscale_ref, wout_ref, gx_ref,
                wxq_ref, wxo_ref, gffn_ref, wr_ref, br_ref, o_ref, ext, *, tm, pos0):
    t = pl.program_id(1)
    H = POOL_HIST + 1

    @pl.when(t == 0)
    def _():
        ext[0:H, :] = hist_ref[0]

    @pl.when(t > 0)
    def _():
        ext[0:H, :] = ext[tm:tm + H, :]

    ext[H:H + tm, :] = u_ref[0]

    pos = pos0 + t * tm + lax.broadcasted_iota(jnp.int32, (tm, 1), 0)
    pouts = []
    for gi, w in enumerate(POOL_WINDOWS):
        c0, c1 = gi * POOL_GROUP_DIM, (gi + 1) * POOL_GROUP_DIM
        wsum = ext[H:H + tm, c0:c1]
        for s in range(1, w):
            wsum = wsum + ext[H - s:H - s + tm, c0:c1]
        cnt = jnp.minimum(pos + 1, w).astype(F32)
        d = wsum / cnt - u_ref[0, :, c0:c1]
        y = _dot(d.astype(BF16), wpool_ref[gi]) * pscale_ref[:, c0:c1]
        pouts.append(y.astype(BF16))
    pcat = jnp.concatenate(pouts, axis=1)

    x1 = x_ref[0] + _dot(a_ref[0], wout_ref[0:D_ATTN, :]) + _dot(pcat, wout_ref[D_ATTN:D_ATTN + D_POOL, :])

    h = _rms(x1, gx_ref[...]).astype(BF16)
    qx = _dot(h, wxq_ref[...])
    outs = []
    for hh in range(N_XHEADS):
        hs = slice(hh * XHEAD_DIM, (hh + 1) * XHEAD_DIM)
        logits = _dot_nt(qx[:, hs].astype(BF16), mk_ref[0, :, hs]) * (XHEAD_DIM ** -0.5)
        m = jnp.max(logits, axis=1, keepdims=True)
        p = jnp.exp(logits - m)
        l = jnp.sum(p, axis=1, keepdims=True)
        outs.append((_dot(p.astype(BF16), mv_ref[0, :, hs]) / l).astype(BF16))
    o = jnp.concatenate(outs, axis=1)
    x2 = x1 + _dot(o, wxo_ref[...])
    D = x2.shape[1]
    o_ref[0, :, 0:D] = x2
    hf = _rms(x2, gffn_ref[...]).astype(BF16)
    o_ref[0, :, D:D + LANES] = _route(_dot(hf, wr_ref[...]) + br_ref[...])


def _mix(x, a, u, hist, mk, mv, w_pool, pool_scale, w_out, g_x, w_xq, w_xo, g_ffn, w_r, b_r, *, pos0, tm):
    B, T, D = x.shape
    n_mem = mk.shape[1]
    H = POOL_HIST + 1
    assert T % tm == 0 and tm >= H
    hist16 = jnp.concatenate([jnp.zeros((B, 1, D_POOL), F32), hist], axis=1)
    return pl.pallas_call(
        functools.partial(_mix_kernel, tm=tm, pos0=pos0),
        grid=(B, T // tm),
        in_specs=[pl.BlockSpec((1, tm, D), lambda b, t: (b, t, 0)),
                  pl.BlockSpec((1, tm, D_ATTN), lambda b, t: (b, t, 0)),
                  pl.BlockSpec((1, tm, D_POOL), lambda b, t: (b, t, 0)),
                  pl.BlockSpec((1, H, D_POOL), lambda b, t: (b, 0, 0)),
                  pl.BlockSpec((1, n_mem, D_X), lambda b, t: (b, 0, 0)),
                  pl.BlockSpec((1, n_mem, D_X), lambda b, t: (b, 0, 0)),
                  _const_spec(w_pool.shape),
                  _const_spec((1, D_POOL)),
                  _const_spec(w_out.shape),
                  _const_spec((1, D)),
                  _const_spec(w_xq.shape),
                  _const_spec(w_xo.shape),
                  _const_spec((1, D)),
                  _const_spec(w_r.shape),
                  _const_spec((1, LANES))],
        out_specs=pl.BlockSpec((1, tm, D + LANES), lambda b, t: (b, t, 0)),
        out_shape=jax.ShapeDtypeStruct((B, T, D + LANES), F32),
        scratch_shapes=[pltpu.VMEM((H + tm, D_POOL), F32)],
        compiler_params=pltpu.CompilerParams(dimension_semantics=("arbitrary", "arbitrary"),
                                             vmem_limit_bytes=VMEM_LIMIT),
        name="mix",
    )(x, a, u, hist16, mk, mv, w_pool, pool_scale.reshape(1, D_POOL), w_out, g_x.reshape(1, D), w_xq, w_xo,
      g_ffn.reshape(1, D), w_r, b_r)


R_OFF = N_GROUPS


def _route(logits):
    tm = logits.shape[0]
    lane = lax.broadcasted_iota(jnp.int32, (tm, LANES), 1).astype(F32)
    ninf = -jnp.inf
    big = float(LANES)
    gl = jnp.where(lane < N_GROUPS, logits, ninf)
    gmax = jnp.max(gl, axis=1, keepdims=True)
    g_sel = jnp.min(jnp.where(gl == gmax, lane, big), axis=1, keepdims=True)
    g_prob = 1.0 / jnp.sum(jnp.exp(gl - gmax), axis=1, keepdims=True)
    lo = R_OFF + g_sel * EXPERTS_PER_GROUP
    el = jnp.where((lane >= lo) & (lane < lo + EXPERTS_PER_GROUP), logits, ninf)
    tv0 = jnp.max(el, axis=1, keepdims=True)
    ti0 = jnp.min(jnp.where(el == tv0, lane, big), axis=1, keepdims=True)
    el2 = jnp.where(lane == ti0, ninf, el)
    tv1 = jnp.max(el2, axis=1, keepdims=True)
    ti1 = jnp.min(jnp.where(el2 == tv1, lane, big), axis=1, keepdims=True)
    e1 = jnp.exp(tv1 - tv0)
    den = 1.0 + e1
    w0 = g_prob / den
    w1 = g_prob * e1 / den
    ids = jnp.where(lane == 0.0, jnp.minimum(ti0, ti1) - R_OFF,
                    jnp.where(lane == 1.0, jnp.maximum(ti0, ti1) - R_OFF, 0.0))
    return jnp.where(lane == ti0, w0, 0.0) + jnp.where(lane == ti1, w1, 0.0) + ids


def _expert(h, rec, e, wg, wu, wd):
    a = _dot(h, wg)
    b = _dot(h, wu)
    lane = lax.broadcasted_iota(jnp.int32, rec.shape, 1)
    cw = jnp.sum(jnp.where(lane == e + R_OFF, rec, 0.0), axis=1, keepdims=True)
    hid = a * jax.nn.sigmoid(a) * b * cw
    return _dot(hid.astype(BF16), wd)


def _moe_kernel(x_ref, gffn_ref, wg_ref, wu_ref, wd_ref, gfin_ref, o_ref, h_s, acc_s):
    e = pl.program_id(1)
    d = o_ref.shape[1]

    @pl.when(e == 0)
    def _():
        h_s[...] = _rms(x_ref[:, 0:d], gffn_ref[...]).astype(BF16)
        acc_s[...] = jnp.zeros(acc_s.shape, F32)

    acc_s[...] += _expert(h_s[...], x_ref[:, d:d + LANES], e, wg_ref[0], wu_ref[0], wd_ref[0])

    @pl.when(e == pl.num_programs(1) - 1)
    def _():
        o_ref[...] = _rms(x_ref[:, 0:d] + acc_s[...], gfin_ref[...])


def _moe(x, g_ffn, w_gate, w_up, w_down, g_final, *, tm):
    n, de = x.shape
    d = de - LANES
    assert n % tm == 0
    return pl.pallas_call(
        _moe_kernel,
        grid=(n // tm, N_EXPERTS),
        in_specs=[pl.BlockSpec((tm, de), lambda i, e: (i, 0)),
                  _const_spec((1, d)),
                  pl.BlockSpec((1, d, D_EXPERT), lambda i, e: (e, 0, 0)),
                  pl.BlockSpec((1, d, D_EXPERT), lambda i, e: (e, 0, 0)),
                  pl.BlockSpec((1, D_EXPERT, d), lambda i, e: (e, 0, 0)),
                  _const_spec((1, d))],
        out_specs=pl.BlockSpec((tm, d), lambda i, e: (i, 0)),
        out_shape=jax.ShapeDtypeStruct((n, d), F32),
        scratch_shapes=[pltpu.VMEM((tm, d), BF16),
                        pltpu.VMEM((tm, d), F32)],
        compiler_params=pltpu.CompilerParams(dimension_semantics=("arbitrary", "arbitrary"),
                                             vmem_limit_bytes=VMEM_LIMIT),
        name="moe",
    )(x, g_ffn.reshape(1, d), w_gate, w_up, w_down, g_final.reshape(1, d))


def _moe_sparse_kernel(ea_ref, eb_ref, nv_ref, idx_ref, idxn_ref, x_hbm, gffn_ref, gfin_ref,
                       wga_ref, wua_ref, wda_ref, wgb_ref, wub_ref, wdb_ref, o_hbm,
                       xbuf, ybuf, gsem, ssem, *, tm):
    t = pl.program_id(0)
    nt = pl.num_programs(0)
    d = o_hbm.shape[1]
    slot = lax.rem(t, 2)
    nslot = 1 - slot
    nv = nv_ref[t]
    WAIT_ROWS = 8

    def gather_row(ids, s, r):
        return pltpu.make_async_copy(x_hbm.at[pl.ds(ids[0, 0, r], 1), :], xbuf.at[s, pl.ds(r, 1), :], gsem.at[s])

    def gather_loop(ids, s):
        def body(r, carry):
            gather_row(ids, s, r).start()
            return carry
        lax.fori_loop(0, tm, body, 0)

    def gather_wait(s):
        pltpu.make_async_copy(x_hbm.at[pl.ds(0, tm), :], xbuf.at[s], gsem.at[s]).wait()

    def scatter_rows(s, r, row, n):
        return pltpu.make_async_copy(ybuf.at[s, pl.ds(r, n), :], o_hbm.at[pl.ds(row, n), :], ssem.at[s])

    def scatter_wait(s, n):
        def body_many(r, carry):
            scatter_rows(s, 0, 0, WAIT_ROWS).wait()
            return carry
        lax.fori_loop(0, n // WAIT_ROWS, body_many, 0)

        def body_one(r, carry):
            scatter_rows(s, 0, 0, 1).wait()
            return carry
        lax.fori_loop(0, lax.rem(n, WAIT_ROWS), body_one, 0)

    @pl.when(t == 0)
    def _():
        gather_loop(idx_ref, 0)

    @pl.when(t >= 2)
    def _():
        scatter_wait(slot, nv_ref[jnp.maximum(t - 2, 0)])

    gather_wait(slot)

    @pl.when(nv > 0)
    def _():
        x = xbuf[slot]
        xr = x[:, 0:d]
        rec = x[:, d:d + LANES]
        h = _rms(xr, gffn_ref[...]).astype(BF16)
        for r in range(tm):
            gather_row(idxn_ref, nslot, r).start()
        acc = _expert(h, rec, ea_ref[t], wga_ref[0], wua_ref[0], wda_ref[0])
        acc = acc + _expert(h, rec, eb_ref[t], wgb_ref[0], wub_ref[0], wdb_ref[0])
        ybuf[slot] = _rms(xr + acc, gfin_ref[...])

        def body(r, carry):
            scatter_rows(slot, r, idx_ref[0, 0, r], 1).start()
            return carry
        lax.fori_loop(0, nv, body, 0)

    @pl.when(nv <= 0)
    def _():
        gather_loop(idxn_ref, nslot)

    @pl.when(t == nt - 1)
    def _():
        gather_wait(nslot)

        @pl.when(t >= 1)
        def _():
            scatter_wait(nslot, nv_ref[jnp.maximum(t - 1, 0)])
        scatter_wait(slot, nv)


def _moe_sparse(x, g_ffn, w_gate, w_up, w_down, g_final, *, tm):
    n, de = x.shape
    d = de - LANES
    n_pairs = N_GROUPS * (EXPERTS_PER_GROUP * (EXPERTS_PER_GROUP - 1) // 2)
    nt = -(-n // tm) + n_pairs
    n_keys = N_EXPERTS * N_EXPERTS

    ids = x[:, d:d + 2].astype(jnp.int32)
    key = ids[:, 0] * N_EXPERTS + ids[:, 1]
    order = jnp.argsort(key).astype(jnp.int32)
    skey = key[order]
    count_below = lambda sorted_vals, q: jnp.sum((sorted_vals[None, :] < q[:, None]).astype(jnp.int32), axis=1)
    edges = count_below(skey, jnp.arange(n_keys + 1, dtype=jnp.int32))
    starts, ends = edges[:-1], edges[1:]
    tiles = (ends - starts + tm - 1) // tm
    cum = jnp.cumsum(tiles)
    total = cum[-1]
    tt = jnp.arange(nt, dtype=jnp.int32)
    tc = jnp.minimum(tt, total - 1)
    cls = count_below(cum, tc + 1)
    first = cum[cls] - tiles[cls]
    tstart = starts[cls] + (tc - first) * tm
    nv = jnp.where(tt < total, jnp.clip(ends[cls] - tstart, 0, tm), 0).astype(jnp.int32)
    ea = (cls // N_EXPERTS).astype(jnp.int32)
    eb = (cls % N_EXPERTS).astype(jnp.int32)
    rows = jnp.clip(tstart[:, None] + jnp.arange(tm, dtype=jnp.int32)[None, :], 0, n - 1)
    idx = order[rows].reshape(nt, 1, tm)

    wspec_a = lambda shape: pl.BlockSpec(shape, lambda t, ea, eb, nv: (ea[t], 0, 0))
    wspec_b = lambda shape: pl.BlockSpec(shape, lambda t, ea, eb, nv: (eb[t], 0, 0))
    cspec = lambda shape: pl.BlockSpec(shape, lambda t, ea, eb, nv: (0,) * len(shape),
                                       pipeline_mode=pl.Buffered(1))
    grid_spec = pltpu.PrefetchScalarGridSpec(
        num_scalar_prefetch=3,
        grid=(nt,),
        in_specs=[pl.BlockSpec((1, 1, tm), lambda t, ea, eb, nv: (t, 0, 0), memory_space=pltpu.SMEM),
                  pl.BlockSpec((1, 1, tm), lambda t, ea, eb, nv: (jnp.minimum(t + 1, nt - 1), 0, 0),
                               memory_space=pltpu.SMEM),
                  pl.BlockSpec(memory_space=pl.ANY),
                  cspec((1, d)), cspec((1, d)),
                  wspec_a((1, d, D_EXPERT)), wspec_a((1, d, D_EXPERT)), wspec_a((1, D_EXPERT, d)),
                  wspec_b((1, d, D_EXPERT)), wspec_b((1, d, D_EXPERT)), wspec_b((1, D_EXPERT, d))],
        out_specs=pl.BlockSpec(memory_space=pl.ANY),
        scratch_shapes=[pltpu.VMEM((2, tm, de), F32),
                        pltpu.VMEM((2, tm, d), F32),
                        pltpu.SemaphoreType.DMA((2,)),
                        pltpu.SemaphoreType.DMA((2,))])
    return pl.pallas_call(
        functools.partial(_moe_sparse_kernel, tm=tm),
        grid_spec=grid_spec,
        out_shape=jax.ShapeDtypeStruct((n, d), F32),
        compiler_params=pltpu.CompilerParams(dimension_semantics=("arbitrary",),
                                             vmem_limit_bytes=VMEM_LIMIT),
        name="moe_sparse",
    )(ea, eb, nv, idx, idx, x, g_ffn.reshape(1, d), g_final.reshape(1, d),
      w_gate, w_up, w_down, w_gate, w_up, w_down)


C_Q = 0
C_QI = C_Q + D_ATTN
C_K = C_QI + N_IDX_HEADS * IDX_DIM
C_V = C_K + D_KV
C_KIWI = C_V + D_KV
C_KIE = C_KIWI + LANES
C_KIO = C_KIE + LANES
C_U = C_KIO + LANES
C_END = C_U + D_POOL


def _layout_w_in(w_in):
    d = w_in.shape[0]
    offs = np.cumsum((0, D_ATTN, D_KV, D_KV, N_IDX_HEADS * IDX_DIM, IDX_DIM, N_IDX_HEADS, D_POOL))
    wq, wk, wv, wqi, wki, wwi, wu = (w_in[:, offs[n]:offs[n + 1]] for n in range(7))
    z = lambda n: jnp.zeros((d, n), w_in.dtype)
    cat = jnp.concatenate([wq, wqi, wk, wv,
                           wki, wwi, z(LANES - IDX_DIM - N_IDX_HEADS),
                           wki, z(LANES - IDX_DIM),
                           z(LANES - IDX_DIM), wki,
                           wu], axis=1)
    assert cat.shape[1] == C_END
    return cat.astype(BF16)


IN_GROUPS = ((C_Q, C_QI, 1.0, 1),
             (C_QI, C_K, 1.0, 1),
             (C_K, C_V, 1.0, 2),
             (C_V, C_KIWI, 1.0, 2),
             (C_KIWI, C_KIE, 1.0, 1),
             (C_KIE, C_KIO, 1.0, 1),
             (C_KIO, C_U, 1.0, 1),
             (C_U, C_END, 1.0, 1))
IN_DTYPES = (BF16, BF16, F32, BF16, F32, BF16, F32, BF16, BF16, F32)


def _pad_rows(a, n):
    return jnp.pad(a, ((0, 0), (0, n - a.shape[1]), (0, 0)))


def _layer(x, pos0, caches, pool_hist, mk, mv, rel_bias, W, *, tm_in, tq, tm_mix, tm_moe, sparse_moe):
    B, T, D = x.shape
    n = B * T
    q, qi, k, kb, v, vb, kiwi, kie, kio, u = _norm_matmul(
        x.reshape(n, D), W["g_mix"], W["w_in"], IN_GROUPS, IN_DTYPES, tm_in)
    r3 = lambda a: a.reshape(B, T, a.shape[-1])
    kblk = 256
    if caches is None:
        n_keys = T
        kb3, vb3, kie3, kio3 = r3(kb), r3(vb), r3(kie), r3(kio)
    else:
        k_hist, v_hist, ki_hist = caches
        past = k_hist.shape[1]
        n_keys = past + T
        lp = -(-n_keys // kblk) * kblk
        ki = r3(kiwi)[:, :, :IDX_DIM]
        ki_all = jnp.concatenate([ki_hist, ki], axis=1)
        zeros = jnp.zeros_like(ki_all)
        kb3 = _pad_rows(jnp.concatenate([k_hist.reshape(B, past, D_KV), r3(k)], axis=1), lp).astype(BF16)
        vb3 = _pad_rows(jnp.concatenate([v_hist.reshape(B, past, D_KV), r3(v)], axis=1), lp).astype(BF16)
        kie3 = _pad_rows(jnp.concatenate([ki_all, zeros], axis=2), lp).astype(BF16)
        kio3 = _pad_rows(jnp.concatenate([zeros, ki_all], axis=2), lp).astype(BF16)
    a_out = _dsa(r3(q), r3(qi), r3(kiwi), kb3, vb3, kie3, kio3, rel_bias,
                 n_keys=n_keys, pos0=pos0, tq=tq, kblk=kblk)
    x2 = _mix(x, a_out, r3(u), pool_hist, mk.astype(BF16), mv.astype(BF16), W["w_pool"], W["pool_scale"],
              W["w_out"], W["g_x"], W["w_xq"], W["w_xo"], W["g_ffn"], W["w_r"], W["b_r"], pos0=pos0, tm=tm_mix)
    moe = _moe_sparse if sparse_moe else _moe
    y = moe(x2.reshape(n, D + LANES), W["g_ffn"], W["w_gate"], W["w_up"], W["w_down"], W["g_final"], tm=tm_moe)
    new_pool = jnp.concatenate([pool_hist, r3(u)], axis=1)[:, -POOL_HIST:]
    return (y.reshape(B, T, D), k.reshape(B, T, N_KV_A, HEAD_DIM_A), v.reshape(B, T, N_KV_A, HEAD_DIM_A),
            r3(kiwi)[:, :, :IDX_DIM], new_pool)


def kernel(x_prompt, x_sample, mem_prompt, cache_k, cache_v, cache_k_idx, cache_pool, cache_mem_k, cache_mem_v, rel_bias, g_mix, w_in, w_pool, pool_scale, w_out, g_mem, w_mk, w_mv, g_x, w_xq, w_xo, g_ffn, w_rg, b_rg, w_re, b_re, w_gate, w_up, w_down, g_final):
    depth = g_mix.shape[0]
    assert depth == 1
    l = 0
    B, T, D = x_prompt.shape
    Bs, Ts, _ = x_sample.shape
    n_mem = mem_prompt.shape[1]
    past = cache_k.shape[2]

    w_r = jnp.concatenate([w_rg[l], w_re[l], jnp.zeros((D, LANES - N_GROUPS - N_EXPERTS), F32)], axis=1)
    W = dict(
        g_mix=g_mix[l], w_in=_layout_w_in(w_in[l]), w_pool=w_pool[l].astype(BF16), pool_scale=pool_scale[l],
        w_out=w_out[l].astype(BF16), g_x=g_x[l], w_xq=w_xq[l].astype(BF16), w_xo=w_xo[l].astype(BF16),
        g_ffn=g_ffn[l], w_r=w_r.astype(BF16),
        b_r=jnp.concatenate([b_rg[l], b_re[l], jnp.zeros((LANES - N_GROUPS - N_EXPERTS,), F32)]).reshape(1, LANES),
        w_gate=w_gate[l].astype(BF16), w_up=w_up[l].astype(BF16), w_down=w_down[l].astype(BF16),
        g_final=g_final)

    w_m = jnp.concatenate([w_mk[l], w_mv[l]], axis=1).astype(BF16)
    mk, mv = _norm_matmul(mem_prompt.reshape(B * n_mem, D), g_mem[l], w_m,
                          ((0, D_X, 1.0, 1), (D_X, 2 * D_X, 1.0, 1)), (F32, F32), 256)
    mk = mk.reshape(B, n_mem, D_X)
    mv = mv.reshape(B, n_mem, D_X)

    yp, kp, vp, kip, pp = _layer(x_prompt, 0, None, jnp.zeros((B, POOL_HIST, D_POOL), F32), mk, mv, rel_bias, W,
                                 tm_in=256, tq=256, tm_mix=256, tm_moe=128, sparse_moe=True)
    ys, ks, vs, kis, ps = _layer(x_sample, past,
                                 (cache_k[l], cache_v[l], cache_k_idx[l]), cache_pool[l],
                                 cache_mem_k[l].reshape(Bs, n_mem, D_X), cache_mem_v[l].reshape(Bs, n_mem, D_X),
                                 rel_bias, W, tm_in=Bs * Ts, tq=Ts, tm_mix=Ts, tm_moe=Bs * Ts, sparse_moe=False)
    st = lambda a: a[None]
    return (yp, ys, st(kp), st(vp), st(kip), st(pp),
            st(mk.reshape(B, n_mem, N_XHEADS, XHEAD_DIM)), st(mv.reshape(B, n_mem, N_XHEADS, XHEAD_DIM)),
            st(ks), st(vs), st(kis), st(ps))
```

```python
import functools
import math

import numpy as np
import jax
import jax.numpy as jnp
from jax import lax
from jax.experimental import pallas as pl
from jax.experimental.pallas import tpu as pltpu

F32 = jnp.float32
BF16 = jnp.bfloat16

CHUNK = 64
N_HEADS_A = 8
HEAD_DIM_A = 128
N_KV_A = 2
D_ATTN = N_HEADS_A * HEAD_DIM_A
D_KV = N_KV_A * HEAD_DIM_A
N_IDX_HEADS = 8
IDX_DIM = 64
TOPK_MAX = 256
POOL_WINDOWS = (2, 4, 8, 16)
POOL_GROUP_DIM = 256
D_POOL = len(POOL_WINDOWS) * POOL_GROUP_DIM
POOL_HIST = max(POOL_WINDOWS) - 1
N_BUCKETS = 32
MAX_DISTANCE = 128
N_XHEADS = 4
XHEAD_DIM = 128
D_X = N_XHEADS * XHEAD_DIM
N_GROUPS = 4
EXPERTS_PER_GROUP = 8
N_EXPERTS = N_GROUPS * EXPERTS_PER_GROUP
D_EXPERT = 256
EPS = 1e-6

LANES = 128
INT_MIN = -(2 ** 31)
NEG = -1e30
LOG2E = math.log2(math.e)
VMEM_LIMIT = 48 * 1024 * 1024


def _rms(x, g):
    ms = jnp.mean(x * x, axis=-1, keepdims=True)
    return x * lax.rsqrt(ms + EPS) * g


def _dot(a, b):
    return jnp.dot(a, b, preferred_element_type=F32)


def _dot_nt(a, b):
    return lax.dot_general(a, b, (((1,), (1,)), ((), ())), preferred_element_type=F32)


def _const_spec(shape):
    nd = len(shape)
    return pl.BlockSpec(shape, lambda *_: (0,) * nd, pipeline_mode=pl.Buffered(1))


def _norm_matmul_kernel(x_ref, g_ref, w_ref, *out_refs, groups):
    h = _rms(x_ref[...], g_ref[...]).astype(BF16)
    k = 0
    for c0, c1, scale, n_out in groups:
        acc = _dot(h, w_ref[:, c0:c1])
        if scale != 1.0:
            acc = acc * scale
        for _ in range(n_out):
            out_refs[k][...] = acc.astype(out_refs[k].dtype)
            k += 1


def _norm_matmul(x, g, w, groups, out_dtypes, tm):
    n, d = x.shape
    assert n % tm == 0
    widths = []
    for c0, c1, _, n_out in groups:
        widths += [c1 - c0] * n_out
    return pl.pallas_call(
        functools.partial(_norm_matmul_kernel, groups=tuple(groups)),
        grid=(n // tm,),
        in_specs=[pl.BlockSpec((tm, d), lambda i: (i, 0)),
                  _const_spec((1, d)),
                  _const_spec(w.shape)],
        out_specs=[pl.BlockSpec((tm, wd), lambda i: (i, 0)) for wd in widths],
        out_shape=[jax.ShapeDtypeStruct((n, wd), dt) for wd, dt in zip(widths, out_dtypes)],
        compiler_params=pltpu.CompilerParams(dimension_semantics=("arbitrary",),
                                             vmem_limit_bytes=VMEM_LIMIT),
        name="norm_matmul",
    )(x, g.reshape(1, d), w)


def _bias_table_kernel(relb_ref, bucket_ref, o_ref, *, shift_bucket):
    bucket = bucket_ref[...]
    for h in range(N_HEADS_A):
        acc = jnp.zeros(bucket.shape, F32)
        for b in range(N_BUCKETS):
            acc = jnp.where(bucket == b, relb_ref[b, h], acc)
        if shift_bucket is None:
            o_ref[h] = acc * LOG2E
        else:
            o_ref[h] = (acc - relb_ref[shift_bucket, h]) * (HEAD_DIM_A ** 0.5)


def _bias_table(rel_bias, bucket, shift_bucket=None):
    tq, w = bucket.shape
    return pl.pallas_call(
        functools.partial(_bias_table_kernel, shift_bucket=shift_bucket),
        in_specs=[pl.BlockSpec(memory_space=pltpu.SMEM),
                  pl.BlockSpec((tq, w), lambda: (0, 0))],
        out_specs=pl.BlockSpec((N_HEADS_A, tq, w), lambda: (0, 0, 0)),
        out_shape=jax.ShapeDtypeStruct((N_HEADS_A, tq, w), F32),
        name="bias_table",
    )(rel_bias, bucket)


def _t5_bucket(rel):
    nb = N_BUCKETS // 2
    max_exact = nb // 2
    bucket = (rel > 0).astype(jnp.int32) * nb
    n = jnp.abs(rel)
    nf = jnp.maximum(n, 1).astype(F32)
    large = max_exact + (jnp.log(nf / max_exact) / math.log(MAX_DISTANCE / max_exact)
                         * (nb - max_exact)).astype(jnp.int32)
    large = jnp.minimum(large, nb - 1)
    return bucket + jnp.where(n < max_exact, n, large)


def _far_bucket(min_dist, max_dist):
    nb = N_BUCKETS // 2
    max_exact = nb // 2
    n = np.arange(min_dist, max_dist + 1, dtype=np.float64)
    large = max_exact + np.floor(np.log(n / max_exact) / math.log(MAX_DISTANCE / max_exact)
                                 * (nb - max_exact) * (1 - 1e-6)).astype(np.int64)
    assert large.min() >= nb - 1, "far key blocks must sit in the saturated distance bucket"
    return nb - 1


def _dsa_kernel(relb_ref, q_ref, qi_ref, kiwi_ref, k_ref, v_ref, kie_ref, kio_ref, bnear_ref, vis_ref,
                o_ref, keys, madd, wb, qs, xcut, mrun, lrun, acc_s,
                *, TQ, KB, RB, kb0, kstep, topk, far_bucket, tie_bits):
    i = pl.program_id(1)
    nkb = kb0 + kstep * i
    n_far = jnp.maximum(nkb - 2, 0)
    n_lc = KB // LANES
    HPG = N_HEADS_A // N_KV_A

    kiwi = kiwi_ref[0]
    for h in range(N_IDX_HEADS):
        w = (kiwi[:, IDX_DIM + h:IDX_DIM + h + 1] * (N_IDX_HEADS ** -0.5)).astype(BF16).astype(F32)
        wb[h] = jnp.broadcast_to(w * (IDX_DIM ** -0.5), (TQ, LANES))

    def idx_block(j, near_col):
        c0 = pl.multiple_of(j * KB, KB)
        ke = kie_ref[0, pl.ds(c0, KB), :]
        ko = kio_ref[0, pl.ds(c0, KB), :]
        acc = [jnp.zeros((TQ, LANES), F32) for _ in range(n_lc)]
        for hp in range(N_IDX_HEADS // 2):
            qh = qi_ref[0, :, hp * LANES:(hp + 1) * LANES]
            se = _dot_nt(qh, ke)
            so = _dot_nt(qh, ko)
            we = wb[2 * hp]
            wo = wb[2 * hp + 1]
            for c in range(n_lc):
                cs = slice(c * LANES, (c + 1) * LANES)
                re = jnp.maximum(se[:, cs], 0.0).astype(BF16).astype(F32)
                ro = jnp.maximum(so[:, cs], 0.0).astype(BF16).astype(F32)
                acc[c] = acc[c] + we * re + wo * ro
        for c in range(n_lc):
            bits = lax.bitcast_convert_type(acc[c], jnp.int32)
            key = bits ^ ((bits >> 31) & jnp.int32(0x7FFFFFFF))
            if near_col is not None:
                vis = vis_ref[:, near_col + c * LANES:near_col + (c + 1) * LANES]
                key = jnp.where(vis > 0.0, key, jnp.int32(INT_MIN))
            keys[j, :, c * LANES:(c + 1) * LANES] = key

    def far_idx(j, carry):
        idx_block(j, None)
        return carry

    lax.fori_loop(0, n_far, far_idx, 0)

    @pl.when(nkb >= 2)
    def _():
        idx_block(nkb - 2, 0)

    idx_block(nkb - 1, KB)

    kf = float(topk)
    lane_f = lax.broadcasted_iota(jnp.int32, (RB, LANES), 1).astype(F32)
    NRB = TQ // RB
    rbs = [slice(rb * RB, (rb + 1) * RB) for rb in range(NRB)]

    def count(pred):
        def body(jj, accs):
            out = []
            for rb in range(NRB):
                blk = keys[jj, rbs[rb], :]
                acc = accs[rb]
                for c in range(n_lc):
                    acc = acc + jnp.where(pred(rb, blk[:, c * LANES:(c + 1) * LANES], jj, c), 1.0, 0.0)
                out.append(acc)
            return tuple(out)
        accs = lax.fori_loop(0, nkb, body, tuple(jnp.zeros((RB, LANES), F32) for _ in range(NRB)))
        return [jnp.sum(acc, axis=1, keepdims=True) for acc in accs]

    def count_ge(cands):
        cb = [jnp.broadcast_to(cand, (RB, LANES)) for cand in cands]
        return count(lambda rb, kblk, jj, c: kblk >= cb[rb])

    def bis_body(p, thrs):
        cands = [thr + lax.shift_left(jnp.int32(1), 31 - p) for thr in thrs]
        cnts = count_ge(cands)
        return tuple(jnp.where(cnt >= kf, cand, thr) for cnt, cand, thr in zip(cnts, cands, thrs))

    thrs = lax.fori_loop(0, 32, bis_body, tuple(jnp.full((RB, 1), INT_MIN, jnp.int32) for _ in range(NRB)))
    thrs = [jnp.maximum(thr, jnp.int32(INT_MIN + 1)) for thr in thrs]
    cnt_ge = count_ge(thrs)
    cnt_gt = count_ge([thr + 1 for thr in thrs])
    needs = [kf - c for c in cnt_gt]
    thrb = [jnp.broadcast_to(thr, (RB, LANES)) for thr in thrs]

    xcut[...] = jnp.full(xcut.shape, 2.0 ** 24, F32)
    excess = cnt_ge[0] - kf
    for c in cnt_ge[1:]:
        excess = jnp.maximum(excess, c - kf)

    @pl.when(jnp.max(excess) > 0.0)
    def _():
        def count_eq_lt(xs):
            xb = [jnp.broadcast_to(x, (RB, LANES)) for x in xs]

            def pred(rb, kblk, jj, c):
                col = lane_f + (jj * KB + c * LANES).astype(F32)
                return jnp.where(kblk == thrb[rb], col, 2.0 ** 25) < xb[rb]
            return count(pred)

        def tie_body(p, xs):
            cands = [x + lax.shift_left(jnp.int32(1), tie_bits - 1 - p).astype(F32) for x in xs]
            cnts = count_eq_lt(cands)
            return tuple(jnp.where(cnt < need, cand, x) for cnt, need, cand, x in zip(cnts, needs, cands, xs))

        xs = lax.fori_loop(0, tie_bits, tie_body, tuple(jnp.zeros((RB, 1), F32) for _ in range(NRB)))
        for rb in range(NRB):
            xcut[rbs[rb], :] = xs[rb]

    xb = [jnp.broadcast_to(xcut[rbs[rb], :], (RB, LANES)) for rb in range(NRB)]

    def mask_body(jj, carry):
        for rb in range(NRB):
            blk = keys[jj, rbs[rb], :]
            for c in range(n_lc):
                kblk = blk[:, c * LANES:(c + 1) * LANES]
                col = lane_f + (jj * KB + c * LANES).astype(F32)
                keep = (kblk > thrb[rb]) | ((kblk == thrb[rb]) & (col <= xb[rb]))
                madd[jj, rbs[rb], c * LANES:(c + 1) * LANES] = jnp.where(keep, 0.0, NEG)
        return carry

    lax.fori_loop(0, nkb, mask_body, 0)

    _attend(relb_ref, q_ref, k_ref, v_ref, bnear_ref, o_ref, madd, qs, mrun, lrun, acc_s,
            TQ=TQ, KB=KB, nkb=nkb, n_far=n_far, far_bucket=far_bucket)


def _attend(relb_ref, q_ref, k_ref, v_ref, bnear_ref, o_ref, madd, qs, mrun, lrun, acc_s,
            *, TQ, KB, nkb, n_far, far_bucket):
    n_lc = KB // LANES
    HPG = N_HEADS_A // N_KV_A
    for g in range(N_KV_A):
        for r in range(HPG):
            h = g * HPG + r
            qs[g, r * TQ:(r + 1) * TQ, :] = q_ref[0, :, h * HEAD_DIM_A:(h + 1) * HEAD_DIM_A]

    def logits(j, near_col, g):
        c0 = pl.multiple_of(j * KB, KB)
        kg = k_ref[0, pl.ds(c0, KB), g * HEAD_DIM_A:(g + 1) * HEAD_DIM_A]
        s = _dot_nt(qs[g], kg)
        mk = madd[j]
        out = []
        for r in range(HPG):
            h = g * HPG + r
            if near_col is None:
                bias = relb_ref[far_bucket, h] * LOG2E
            else:
                bias = bnear_ref[h, :, near_col:near_col + KB]
            out.append(s[r * TQ:(r + 1) * TQ] * (HEAD_DIM_A ** -0.5 * LOG2E) + bias + mk)
        return out

    def over_blocks(fn):
        def far(j, carry):
            fn(j, None)
            return carry

        lax.fori_loop(0, n_far, far, 0)

        @pl.when(nkb >= 2)
        def _():
            fn(nkb - 2, 0)

        fn(nkb - 1, KB)

    mrun[...] = jnp.full(mrun.shape, NEG, F32)

    def max_block(j, near_col):
        for g in range(N_KV_A):
            for r, sh in enumerate(logits(j, near_col, g)):
                rs = slice(r * TQ, (r + 1) * TQ)
                m = mrun[g, rs, :]
                for c in range(n_lc):
                    m = jnp.maximum(m, sh[:, c * LANES:(c + 1) * LANES])
                mrun[g, rs, :] = m

    over_blocks(max_block)

    for g in range(N_KV_A):
        mrun[g] = jnp.broadcast_to(jnp.max(mrun[g], axis=1, keepdims=True), mrun.shape[1:])
    lrun[...] = jnp.zeros(lrun.shape, F32)
    acc_s[...] = jnp.zeros(acc_s.shape, F32)

    def pv_block(j, near_col):
        c0 = pl.multiple_of(j * KB, KB)
        for g in range(N_KV_A):
            vg = v_ref[0, pl.ds(c0, KB), g * HEAD_DIM_A:(g + 1) * HEAD_DIM_A]
            parts = []
            for r, sh in enumerate(logits(j, near_col, g)):
                rs = slice(r * TQ, (r + 1) * TQ)
                mb = mrun[g, rs, :]
                l = lrun[g, rs, :]
                pcs = []
                for c in range(n_lc):
                    p = jnp.exp2(sh[:, c * LANES:(c + 1) * LANES] - mb)
                    l = l + p
                    pcs.append(p.astype(BF16))
                lrun[g, rs, :] = l
                parts.append(jnp.concatenate(pcs, axis=1))
            acc_s[g] += _dot(jnp.concatenate(parts, axis=0), vg)

    over_blocks(pv_block)

    for g in range(N_KV_A):
        out = acc_s[g] / jnp.sum(lrun[g], axis=1, keepdims=True)
        for r in range(HPG):
            h = g * HPG + r
            o_ref[0, :, h * HEAD_DIM_A:(h + 1) * HEAD_DIM_A] = out[r * TQ:(r + 1) * TQ].astype(o_ref.dtype)


DIGIT_BITS = 8
N_DIGITS = 32 // DIGIT_BITS
SUB_BF16 = 16


def _dsa_kernel_t(q_ref, qi_ref, kiwi_ref, k_ref, vt_ref, kie_ref, kio_ref, bnear_ref, vist_ref,
                  o_ref, keys, dig, madd, qs, xcut, acc_s,
                  *, TQ, KB, kb0, kstep, topk, tie_bits):
    i = pl.program_id(1)
    nkb = kb0 + kstep * i
    n_far = jnp.maximum(nkb - 2, 0)
    n_slab = KB // SUB_BF16

    w_t = kiwi_ref[0].T
    wrow = [(w_t[IDX_DIM + h:IDX_DIM + h + 1, :] * (N_IDX_HEADS ** -0.5)).astype(BF16).astype(F32)
            * (IDX_DIM ** -0.5) for h in range(N_IDX_HEADS)]

    def idx_block(j, near_row):
        c0 = pl.multiple_of(j * KB, KB)
        ke = kie_ref[0, pl.ds(c0, KB), :]
        ko = kio_ref[0, pl.ds(c0, KB), :]
        acc = jnp.zeros((KB, TQ), F32)
        for hp in range(N_IDX_HEADS // 2):
            qh = qi_ref[0, :, hp * LANES:(hp + 1) * LANES]
            re = jnp.maximum(_dot_nt(ke, qh), 0.0).astype(BF16).astype(F32)
            ro = jnp.maximum(_dot_nt(ko, qh), 0.0).astype(BF16).astype(F32)
            acc = acc + wrow[2 * hp] * re + wrow[2 * hp + 1] * ro
        bits = lax.bitcast_convert_type(acc, jnp.int32)
        key = bits ^ ((bits >> 31) & jnp.int32(0x7FFFFFFF))
        if near_row is not None:
            key = jnp.where(vist_ref[near_row:near_row + KB, :] > 0.0, key, jnp.int32(INT_MIN))
        keys[j] = key
        for dk in range(N_DIGITS):
            sh = 32 - DIGIT_BITS * (dk + 1)
            d = (key >> sh) if dk == 0 else ((key >> sh) & jnp.int32(2 ** DIGIT_BITS - 1))
            dig[dk, j] = d.astype(F32).astype(BF16)

    def far_idx(j, carry):
        idx_block(j, None)
        return carry

    lax.fori_loop(0, n_far, far_idx, 0)

    @pl.when(nkb >= 2)
    def _():
        idx_block(nkb - 2, 0)

    idx_block(nkb - 1, KB)

    one = jnp.ones((SUB_BF16, TQ), BF16)
    zero = jnp.zeros((SUB_BF16, TQ), BF16)

    def count_ge(dk, cand):
        cb = jnp.broadcast_to(cand.astype(BF16), (SUB_BF16, TQ))

        def body(jj, tot):
            blk = dig[dk, jj]
            parts = [jnp.where(blk[s * SUB_BF16:(s + 1) * SUB_BF16, :] >= cb, one, zero) for s in range(n_slab)]
            while len(parts) > 1:
                parts = [parts[n] + parts[n + 1] for n in range(0, len(parts), 2)]
            return tot + parts[0].astype(F32)

        tot = lax.fori_loop(0, nkb, body, jnp.zeros((SUB_BF16, TQ), F32))
        return jnp.sum(tot, axis=0, keepdims=True)

    kneed = jnp.full((1, TQ), float(topk), F32)
    thr = jnp.zeros((1, TQ), jnp.int32)
    cnt_eq = None
    for dk in range(N_DIGITS):
        lo = -float(2 ** (DIGIT_BITS - 1)) if dk == 0 else 0.0

        def bis_body(p, t, dk=dk, kneed=kneed):
            cand = t + lax.shift_left(jnp.int32(1), DIGIT_BITS - 1 - p).astype(F32)
            return jnp.where(count_ge(dk, cand) >= kneed, cand, t)

        t = lax.fori_loop(0, DIGIT_BITS, bis_body, jnp.full((1, TQ), lo, F32))
        n_gt = count_ge(dk, t + 1.0)
        if dk == N_DIGITS - 1:
            cnt_eq = count_ge(dk, t) - n_gt
        kneed = kneed - n_gt
        thr = thr + lax.shift_left(t.astype(jnp.int32), 32 - DIGIT_BITS * (dk + 1))
        if dk + 1 < N_DIGITS:
            tb = t.astype(BF16)

            def prep(jj, carry, dk=dk, tb=tb):
                dig[dk + 1, jj] = jnp.where(dig[dk, jj] == tb, dig[dk + 1, jj], jnp.asarray(-1.0, BF16))
                return carry

            lax.fori_loop(0, nkb, prep, 0)

    thr = jnp.maximum(thr, jnp.int32(INT_MIN + 1))
    need = kneed
    sub_f = lax.broadcasted_iota(jnp.int32, (KB, TQ), 0).astype(F32)

    xcut[...] = jnp.full(xcut.shape, 2.0 ** 24, F32)

    @pl.when(jnp.max(cnt_eq - need) > 0.0)
    def _():
        def count_eq_lt(x):
            def body(jj, tot):
                col = sub_f + (jj * KB).astype(F32)
                hit = jnp.where(jnp.where(keys[jj] == thr, col, 2.0 ** 25) < x, 1.0, 0.0)
                for s in range(KB // 8):
                    tot = tot + hit[s * 8:(s + 1) * 8, :]
                return tot
            tot = lax.fori_loop(0, nkb, body, jnp.zeros((8, TQ), F32))
            return jnp.sum(tot, axis=0, keepdims=True)

        def tie_body(p, x):
            cand = x + lax.shift_left(jnp.int32(1), tie_bits - 1 - p).astype(F32)
            return jnp.where(count_eq_lt(cand) < need, cand, x)

        x = lax.fori_loop(0, tie_bits, tie_body, jnp.zeros((1, TQ), F32))
        xcut[...] = jnp.broadcast_to(x, xcut.shape)

    xc = xcut[0:1, :]

    def mask_body(jj, carry):
        kblk = keys[jj]
        col = sub_f + (jj * KB).astype(F32)
        at_thr = jnp.where(col <= xc, 0.0, NEG)
        madd[jj] = jnp.where(kblk > thr, 0.0, jnp.where(kblk == thr, at_thr, NEG))
        return carry

    lax.fori_loop(0, nkb, mask_body, 0)

    HPG = N_HEADS_A // N_KV_A
    c2 = HEAD_DIM_A ** -0.5 * LOG2E
    for g in range(N_KV_A):
        for r in range(HPG):
            h = g * HPG + r
            qs[g, r * TQ:(r + 1) * TQ, :] = q_ref[0, :, h * HEAD_DIM_A:(h + 1) * HEAD_DIM_A]
    acc_s[...] = jnp.zeros(acc_s.shape, F32)

    def attn_block(j, near_row, carry):
        ms, ls = carry
        c0 = pl.multiple_of(j * KB, KB)
        mk = madd[j]
        ms_out, ls_out = [], []
        for g in range(N_KV_A):
            kg = k_ref[0, pl.ds(c0, KB), g * HEAD_DIM_A:(g + 1) * HEAD_DIM_A]
            vt = vt_ref[0, j, g * HEAD_DIM_A:(g + 1) * HEAD_DIM_A, :]
            s_t = _dot_nt(kg, qs[g])
            for r in range(HPG):
                h = g * HPG + r
                cs = slice(r * TQ, (r + 1) * TQ)
                t = s_t[:, cs] + mk
                if near_row is not None:
                    t = t + bnear_ref[h, near_row:near_row + KB, :]
                m_new = jnp.maximum(ms[h], jnp.max(t, axis=0, keepdims=True))
                alpha = jnp.exp2((ms[h] - m_new) * c2)
                p = jnp.exp2((t - m_new) * c2)
                ls_out.append(alpha * ls[h] + jnp.sum(p, axis=0, keepdims=True))
                ms_out.append(m_new)
                acc_s[g, :, cs] = alpha * acc_s[g, :, cs] + _dot(vt, p.astype(BF16))
        return tuple(ms_out), tuple(ls_out)

    carry = (tuple(jnp.full((1, TQ), NEG, F32) for _ in range(N_HEADS_A)),
             tuple(jnp.zeros((1, TQ), F32) for _ in range(N_HEADS_A)))
    carry = lax.fori_loop(0, n_far, lambda j, c: attn_block(j, None, c), carry)
    carry = lax.cond(nkb >= 2, lambda c: attn_block(nkb - 2, 0, c), lambda c: c, carry)
    ms, ls = attn_block(nkb - 1, KB, carry)

    for g in range(N_KV_A):
        for r in range(HPG):
            h = g * HPG + r
            out_t = acc_s[g, :, r * TQ:(r + 1) * TQ] / ls[h]
            o_ref[0, :, h * HEAD_DIM_A:(h + 1) * HEAD_DIM_A] = out_t.T.astype(o_ref.dtype)


def _dsa(q, qi, kiwi, kb, vb, kie, kio, rel_bias, *, n_keys, pos0, tq, kblk):
    B, T, _ = q.shape
    Lp = kb.shape[1]
    assert T % tq == 0 and Lp % kblk == 0 and tq % CHUNK == 0 or T == tq
    n_qt = T // tq
    if n_qt == 1:
        kb0, kstep = Lp // kblk, 0
        assert ((pos0 + tq - 1) // CHUNK + 1) * CHUNK >= n_keys
    else:
        assert pos0 == 0 and tq == kblk and n_keys == Lp == T
        kb0, kstep = 1, 1
    topk = min(TOPK_MAX, n_keys // 4)
    rb = min(128, tq)

    t = jnp.arange(tq, dtype=jnp.int32)[:, None]
    c = jnp.arange(2 * kblk, dtype=jnp.int32)[None, :]
    if n_qt == 1:
        q_abs = pos0 + t
        s_abs = (kb0 - 2) * kblk + c
    else:
        q_abs = kblk + t
        s_abs = c
    vis = ((s_abs // CHUNK) <= (q_abs // CHUNK)) & (s_abs < (n_keys if n_qt == 1 else 2 * kblk))
    far_bucket = _far_bucket(kblk + 1, max(Lp, kblk + 2))

    n_kb = Lp // kblk
    hpg = N_HEADS_A // N_KV_A
    common = dict(TQ=tq, KB=kblk, kb0=kb0, kstep=kstep, topk=topk, far_bucket=far_bucket,
                  tie_bits=int(math.ceil(math.log2(Lp))) + 1)
    attend_scratch = [pltpu.VMEM((N_KV_A, hpg * tq, HEAD_DIM_A), BF16),
                      pltpu.VMEM((N_KV_A, hpg * tq, LANES), F32),
                      pltpu.VMEM((N_KV_A, hpg * tq, LANES), F32),
                      pltpu.VMEM((N_KV_A, hpg * tq, HEAD_DIM_A), F32)]
    if tq % LANES == 0:
        tab_t = _bias_table(rel_bias, _t5_bucket(s_abs - q_abs).T, shift_bucket=far_bucket)
        vt = jnp.swapaxes(vb.reshape(B, n_kb, kblk, D_KV), 2, 3)
        common.pop("far_bucket")
        return pl.pallas_call(
            functools.partial(_dsa_kernel_t, **common),
            grid=(B, n_qt),
            in_specs=[pl.BlockSpec((1, tq, D_ATTN), lambda b, i: (b, i, 0)),
                      pl.BlockSpec((1, tq, N_IDX_HEADS * IDX_DIM), lambda b, i: (b, i, 0)),
                      pl.BlockSpec((1, tq, LANES), lambda b, i: (b, i, 0)),
                      pl.BlockSpec((1, Lp, D_KV), lambda b, i: (b, 0, 0)),
                      pl.BlockSpec((1, n_kb, D_KV, kblk), lambda b, i: (b, 0, 0, 0)),
                      pl.BlockSpec((1, Lp, LANES), lambda b, i: (b, 0, 0)),
                      pl.BlockSpec((1, Lp, LANES), lambda b, i: (b, 0, 0)),
                      _const_spec((N_HEADS_A, 2 * kblk, tq)),
                      _const_spec((2 * kblk, tq))],
            out_specs=pl.BlockSpec((1, tq, D_ATTN), lambda b, i: (b, i, 0)),
            out_shape=jax.ShapeDtypeStruct((B, T, D_ATTN), BF16),
            scratch_shapes=[pltpu.VMEM((n_kb, kblk, tq), jnp.int32),
                            pltpu.VMEM((N_DIGITS, n_kb, kblk, tq), BF16),
                            pltpu.VMEM((n_kb, kblk, tq), F32),
                            pltpu.VMEM((N_KV_A, hpg * tq, HEAD_DIM_A), BF16),
                            pltpu.VMEM((8, tq), F32),
                            pltpu.VMEM((N_KV_A, HEAD_DIM_A, hpg * tq), F32)],
            compiler_params=pltpu.CompilerParams(dimension_semantics=("arbitrary", "arbitrary"),
                                                 vmem_limit_bytes=VMEM_LIMIT),
            name="dsa_t",
        )(q, qi, kiwi, kb, vt, kie, kio, tab_t, vis.astype(F32).T)

    bias_near = _bias_table(rel_bias, _t5_bucket(s_abs - q_abs))
    kern = functools.partial(_dsa_kernel, RB=rb, **common)
    return pl.pallas_call(
        kern,
        grid=(B, n_qt),
        in_specs=[pl.BlockSpec(memory_space=pltpu.SMEM),
                  pl.BlockSpec((1, tq, D_ATTN), lambda b, i: (b, i, 0)),
                  pl.BlockSpec((1, tq, N_IDX_HEADS * IDX_DIM), lambda b, i: (b, i, 0)),
                  pl.BlockSpec((1, tq, LANES), lambda b, i: (b, i, 0)),
                  pl.BlockSpec((1, Lp, D_KV), lambda b, i: (b, 0, 0)),
                  pl.BlockSpec((1, Lp, D_KV), lambda b, i: (b, 0, 0)),
                  pl.BlockSpec((1, Lp, LANES), lambda b, i: (b, 0, 0)),
                  pl.BlockSpec((1, Lp, LANES), lambda b, i: (b, 0, 0)),
                  _const_spec((N_HEADS_A, tq, 2 * kblk)),
                  _const_spec((tq, 2 * kblk))],
        out_specs=pl.BlockSpec((1, tq, D_ATTN), lambda b, i: (b, i, 0)),
        out_shape=jax.ShapeDtypeStruct((B, T, D_ATTN), BF16),
        scratch_shapes=[pltpu.VMEM((n_kb, tq, kblk), jnp.int32),
                        pltpu.VMEM((n_kb, tq, kblk), F32),
                        pltpu.VMEM((N_IDX_HEADS, tq, LANES), F32),
                        pltpu.VMEM((N_KV_A, hpg * tq, HEAD_DIM_A), BF16),
                        pltpu.VMEM((tq, 1), F32),
                        pltpu.VMEM((N_KV_A, hpg * tq, LANES), F32),
                        pltpu.VMEM((N_KV_A, hpg * tq, LANES), F32),
                        pltpu.VMEM((N_KV_A, hpg * tq, HEAD_DIM_A), F32)],
        compiler_params=pltpu.CompilerParams(dimension_semantics=("arbitrary", "arbitrary"),
                                             vmem_limit_bytes=VMEM_LIMIT),
        name="dsa",
    )(rel_bias, q, qi, kiwi, kb, vb, kie, kio, bias_near, vis.astype(F32))


def _mix_kernel(x_ref, a_ref, u_ref, hist_ref, mk_ref, mv_ref, wpool_ref, pscale_ref, wout_ref, gx_ref,
                wxq_ref, wxo_ref, gffn_ref, wr_ref, br_ref, o_ref, ext, *, tm, pos0):
    t = pl.program_id(1)
    H = POOL_HIST + 1

    @pl.when(t == 0)
    def _():
        ext[0:H, :] = hist_ref[0]

    @pl.when(t > 0)
    def _():
        ext[0:H, :] = ext[tm:tm + H, :]

    ext[H:H + tm, :] = u_ref[0]

    pos = pos0 + t * tm + lax.broadcasted_iota(jnp.int32, (tm, 1), 0)
    pouts = []
    for gi, w in enumerate(POOL_WINDOWS):
        c0, c1 = gi * POOL_GROUP_DIM, (gi + 1) * POOL_GROUP_DIM
        wsum = ext[H:H + tm, c0:c1]
        for s in range(1, w):
            wsum = wsum + ext[H - s:H - s + tm, c0:c1]
        cnt = jnp.minimum(pos + 1, w).astype(F32)
        d = wsum / cnt - u_ref[0, :, c0:c1]
        y = _dot(d.astype(BF16), wpool_ref[gi]) * pscale_ref[:, c0:c1]
        pouts.append(y.astype(BF16))
    pcat = jnp.concatenate(pouts, axis=1)

    x1 = x_ref[0] + _dot(a_ref[0], wout_ref[0:D_ATTN, :]) + _dot(pcat, wout_ref[D_ATTN:D_ATTN + D_POOL, :])

    h = _rms(x1, gx_ref[...]).astype(BF16)
    qx = _dot(h, wxq_ref[...])
    outs = []
    for hh in range(N_XHEADS):
        hs = slice(hh * XHEAD_DIM, (hh + 1) * XHEAD_DIM)
        logits = _dot_nt(qx[:, hs].astype(BF16), mk_ref[0, :, hs]) * (XHEAD_DIM ** -0.5)
        m = jnp.max(logits, axis=1, keepdims=True)
        p = jnp.exp(logits - m)
        l = jnp.sum(p, axis=1, keepdims=True)
        outs.append((_dot(p.astype(BF16), mv_ref[0, :, hs]) / l).astype(BF16))
    o = jnp.concatenate(outs, axis=1)
    x2 = x1 + _dot(o, wxo_ref[...])
    D = x2.shape[1]
    o_ref[0, :, 0:D] = x2
    hf = _rms(x2, gffn_ref[...]).astype(BF16)
    o_ref[0, :, D:D + LANES] = _route(_dot(hf, wr_ref[...]) + br_ref[...])


def _mix(x, a, u, hist, mk, mv, w_pool, pool_scale, w_out, g_x, w_xq, w_xo, g_ffn, w_r, b_r, *, pos0, tm):
    B, T, D = x.shape
    n_mem = mk.shape[1]
    H = POOL_HIST + 1
    assert T % tm == 0 and tm >= H
    hist16 = jnp.concatenate([jnp.zeros((B, 1, D_POOL), F32), hist], axis=1)
    return pl.pallas_call(
        functools.partial(_mix_kernel, tm=tm, pos0=pos0),
        grid=(B, T // tm),
        in_specs=[pl.BlockSpec((1, tm, D), lambda b, t: (b, t, 0)),
                  pl.BlockSpec((1, tm, D_ATTN), lambda b, t: (b, t, 0)),
                  pl.BlockSpec((1, tm, D_POOL), lambda b, t: (b, t, 0)),
                  pl.BlockSpec((1, H, D_POOL), lambda b, t: (b, 0, 0)),
                  pl.BlockSpec((1, n_mem, D_X), lambda b, t: (b, 0, 0)),
                  pl.BlockSpec((1, n_mem, D_X), lambda b, t: (b, 0, 0)),
                  _const_spec(w_pool.shape),
                  _const_spec((1, D_POOL)),
                  _const_spec(w_out.shape),
                  _const_spec((1, D)),
                  _const_spec(w_xq.shape),
                  _const_spec(w_xo.shape),
                  _const_spec((1, D)),
                  _const_spec(w_r.shape),
                  _const_spec((1, LANES))],
        out_specs=pl.BlockSpec((1, tm, D + LANES), lambda b, t: (b, t, 0)),
        out_shape=jax.ShapeDtypeStruct((B, T, D + LANES), F32),
        scratch_shapes=[pltpu.VMEM((H + tm, D_POOL), F32)],
        compiler_params=pltpu.CompilerParams(dimension_semantics=("arbitrary", "arbitrary"),
                                             vmem_limit_bytes=VMEM_LIMIT),
        name="mix",
    )(x, a, u, hist16, mk, mv, w_pool, pool_scale.reshape(1, D_POOL), w_out, g_x.reshape(1, D), w_xq, w_xo,
      g_ffn.reshape(1, D), w_r, b_r)


R_OFF = N_GROUPS


def _route(logits):
    tm = logits.shape[0]
    lane = lax.broadcasted_iota(jnp.int32, (tm, LANES), 1).astype(F32)
    ninf = -jnp.inf
    big = float(LANES)
    gl = jnp.where(lane < N_GROUPS, logits, ninf)
    gmax = jnp.max(gl, axis=1, keepdims=True)
    g_sel = jnp.min(jnp.where(gl == gmax, lane, big), axis=1, keepdims=True)
    g_prob = 1.0 / jnp.sum(jnp.exp(gl - gmax), axis=1, keepdims=True)
    lo = R_OFF + g_sel * EXPERTS_PER_GROUP
    el = jnp.where((lane >= lo) & (lane < lo + EXPERTS_PER_GROUP), logits, ninf)
    tv0 = jnp.max(el, axis=1, keepdims=True)
    ti0 = jnp.min(jnp.where(el == tv0, lane, big), axis=1, keepdims=True)
    el2 = jnp.where(lane == ti0, ninf, el)
    tv1 = jnp.max(el2, axis=1, keepdims=True)
    ti1 = jnp.min(jnp.where(el2 == tv1, lane, big), axis=1, keepdims=True)
    e1 = jnp.exp(tv1 - tv0)
    den = 1.0 + e1
    w0 = g_prob / den
    w1 = g_prob * e1 / den
    ids = jnp.where(lane == 0.0, jnp.minimum(ti0, ti1) - R_OFF,
                    jnp.where(lane == 1.0, jnp.maximum(ti0, ti1) - R_OFF, 0.0))
    return jnp.where(lane == ti0, w0, 0.0) + jnp.where(lane == ti1, w1, 0.0) + ids


def _expert(h, rec, e, wg, wu, wd):
    a = _dot(h, wg)
    b = _dot(h, wu)
    lane = lax.broadcasted_iota(jnp.int32, rec.shape, 1)
    cw = jnp.sum(jnp.where(lane == e + R_OFF, rec, 0.0), axis=1, keepdims=True)
    hid = a * jax.nn.sigmoid(a) * b * cw
    return _dot(hid.astype(BF16), wd)


def _moe_kernel(x_ref, gffn_ref, wg_ref, wu_ref, wd_ref, gfin_ref, o_ref, h_s, acc_s):
    e = pl.program_id(1)
    d = o_ref.shape[1]

    @pl.when(e == 0)
    def _():
        h_s[...] = _rms(x_ref[:, 0:d], gffn_ref[...]).astype(BF16)
        acc_s[...] = jnp.zeros(acc_s.shape, F32)

    acc_s[...] += _expert(h_s[...], x_ref[:, d:d + LANES], e, wg_ref[0], wu_ref[0], wd_ref[0])

    @pl.when(e == pl.num_programs(1) - 1)
    def _():
        o_ref[...] = _rms(x_ref[:, 0:d] + acc_s[...], gfin_ref[...])


def _moe(x, g_ffn, w_gate, w_up, w_down, g_final, *, tm):
    n, de = x.shape
    d = de - LANES
    assert n % tm == 0
    return pl.pallas_call(
        _moe_kernel,
        grid=(n // tm, N_EXPERTS),
        in_specs=[pl.BlockSpec((tm, de), lambda i, e: (i, 0)),
                  _const_spec((1, d)),
                  pl.BlockSpec((1, d, D_EXPERT), lambda i, e: (e, 0, 0)),
                  pl.BlockSpec((1, d, D_EXPERT), lambda i, e: (e, 0, 0)),
                  pl.BlockSpec((1, D_EXPERT, d), lambda i, e: (e, 0, 0)),
                  _const_spec((1, d))],
        out_specs=pl.BlockSpec((tm, d), lambda i, e: (i, 0)),
        out_shape=jax.ShapeDtypeStruct((n, d), F32),
        scratch_shapes=[pltpu.VMEM((tm, d), BF16),
                        pltpu.VMEM((tm, d), F32)],
        compiler_params=pltpu.CompilerParams(dimension_semantics=("arbitrary", "arbitrary"),
                                             vmem_limit_bytes=VMEM_LIMIT),
        name="moe",
    )(x, g_ffn.reshape(1, d), w_gate, w_up, w_down, g_final.reshape(1, d))


def _moe_sparse_kernel(ea_ref, eb_ref, nv_ref, idx_ref, idxn_ref, x_hbm, gffn_ref, gfin_ref,
                       wga_ref, wua_ref, wda_ref, wgb_ref, wub_ref, wdb_ref, o_hbm,
                       xbuf, ybuf, gsem, ssem, *, tm):
    t = pl.program_id(0)
    nt = pl.num_programs(0)
    d = o_hbm.shape[1]
    slot = lax.rem(t, 2)
    nslot = 1 - slot
    nv = nv_ref[t]
    WAIT_ROWS = 8

    def gather_row(ids, s, r):
        return pltpu.make_async_copy(x_hbm.at[pl.ds(ids[0, 0, r], 1), :], xbuf.at[s, pl.ds(r, 1), :], gsem.at[s])

    def gather_loop(ids, s):
        def body(r, carry):
            gather_row(ids, s, r).start()
            return carry
        lax.fori_loop(0, tm, body, 0)

    def gather_wait(s):
        pltpu.make_async_copy(x_hbm.at[pl.ds(0, tm), :], xbuf.at[s], gsem.at[s]).wait()

    def scatter_rows(s, r, row, n):
        return pltpu.make_async_copy(ybuf.at[s, pl.ds(r, n), :], o_hbm.at[pl.ds(row, n), :], ssem.at[s])

    def scatter_wait(s, n):
        def body_many(r, carry):
            scatter_rows(s, 0, 0, WAIT_ROWS).wait()
            return carry
        lax.fori_loop(0, n // WAIT_ROWS, body_many, 0)

        def body_one(r, carry):
            scatter_rows(s, 0, 0, 1).wait()
            return carry
        lax.fori_loop(0, lax.rem(n, WAIT_ROWS), body_one, 0)

    @pl.when(t == 0)
    def _():
        gather_loop(idx_ref, 0)

    @pl.when(t >= 2)
    def _():
        scatter_wait(slot, nv_ref[jnp.maximum(t - 2, 0)])

    gather_wait(slot)

    @pl.when(nv > 0)
    def _():
        x = xbuf[slot]
        xr = x[:, 0:d]
        rec = x[:, d:d + LANES]
        h = _rms(xr, gffn_ref[...]).astype(BF16)
        for r in range(tm):
            gather_row(idxn_ref, nslot, r).start()
        acc = _expert(h, rec, ea_ref[t], wga_ref[0], wua_ref[0], wda_ref[0])
        acc = acc + _expert(h, rec, eb_ref[t], wgb_ref[0], wub_ref[0], wdb_ref[0])
        ybuf[slot] = _rms(xr + acc, gfin_ref[...])

        def body(r, carry):
            scatter_rows(slot, r, idx_ref[0, 0, r], 1).start()
            return carry
        lax.fori_loop(0, nv, body, 0)

    @pl.when(nv <= 0)
    def _():
        gather_loop(idxn_ref, nslot)

    @pl.when(t == nt - 1)
    def _():
        gather_wait(nslot)

        @pl.when(t >= 1)
        def _():
            scatter_wait(nslot, nv_ref[jnp.maximum(t - 1, 0)])
        scatter_wait(slot, nv)


def _moe_sparse(x, g_ffn, w_gate, w_up, w_down, g_final, *, tm):
    n, de = x.shape
    d = de - LANES
    n_pairs = N_GROUPS * (EXPERTS_PER_GROUP * (EXPERTS_PER_GROUP - 1) // 2)
    nt = -(-n // tm) + n_pairs
    n_keys = N_EXPERTS * N_EXPERTS

    ids = x[:, d:d + 2].astype(jnp.int32)
    key = ids[:, 0] * N_EXPERTS + ids[:, 1]
    order = jnp.argsort(key).astype(jnp.int32)
    skey = key[order]
    count_below = lambda sorted_vals, q: jnp.sum((sorted_vals[None, :] < q[:, None]).astype(jnp.int32), axis=1)
    edges = count_below(skey, jnp.arange(n_keys + 1, dtype=jnp.int32))
    starts, ends = edges[:-1], edges[1:]
    tiles = (ends - starts + tm - 1) // tm
    cum = jnp.cumsum(tiles)
    total = cum[-1]
    tt = jnp.arange(nt, dtype=jnp.int32)
    tc = jnp.minimum(tt, total - 1)
    cls = count_below(cum, tc + 1)
    first = cum[cls] - tiles[cls]
    tstart = starts[cls] + (tc - first) * tm
    nv = jnp.where(tt < total, jnp.clip(ends[cls] - tstart, 0, tm), 0).astype(jnp.int32)
    ea = (cls // N_EXPERTS).astype(jnp.int32)
    eb = (cls % N_EXPERTS).astype(jnp.int32)
    rows = jnp.clip(tstart[:, None] + jnp.arange(tm, dtype=jnp.int32)[None, :], 0, n - 1)
    idx = order[rows].reshape(nt, 1, tm)

    wspec_a = lambda shape: pl.BlockSpec(shape, lambda t, ea, eb, nv: (ea[t], 0, 0))
    wspec_b = lambda shape: pl.BlockSpec(shape, lambda t, ea, eb, nv: (eb[t], 0, 0))
    cspec = lambda shape: pl.BlockSpec(shape, lambda t, ea, eb, nv: (0,) * len(shape),
                                       pipeline_mode=pl.Buffered(1))
    grid_spec = pltpu.PrefetchScalarGridSpec(
        num_scalar_prefetch=3,
        grid=(nt,),
        in_specs=[pl.BlockSpec((1, 1, tm), lambda t, ea, eb, nv: (t, 0, 0), memory_space=pltpu.SMEM),
                  pl.BlockSpec((1, 1, tm), lambda t, ea, eb, nv: (jnp.minimum(t + 1, nt - 1), 0, 0),
                               memory_space=pltpu.SMEM),
                  pl.BlockSpec(memory_space=pl.ANY),
                  cspec((1, d)), cspec((1, d)),
                  wspec_a((1, d, D_EXPERT)), wspec_a((1, d, D_EXPERT)), wspec_a((1, D_EXPERT, d)),
                  wspec_b((1, d, D_EXPERT)), wspec_b((1, d, D_EXPERT)), wspec_b((1, D_EXPERT, d))],
        out_specs=pl.BlockSpec(memory_space=pl.ANY),
        scratch_shapes=[pltpu.VMEM((2, tm, de), F32),
                        pltpu.VMEM((2, tm, d), F32),
                        pltpu.SemaphoreType.DMA((2,)),
                        pltpu.SemaphoreType.DMA((2,))])
    return pl.pallas_call(
        functools.partial(_moe_sparse_kernel, tm=tm),
        grid_spec=grid_spec,
        out_shape=jax.ShapeDtypeStruct((n, d), F32),
        compiler_params=pltpu.CompilerParams(dimension_semantics=("arbitrary",),
                                             vmem_limit_bytes=VMEM_LIMIT),
        name="moe_sparse",
    )(ea, eb, nv, idx, idx, x, g_ffn.reshape(1, d), g_final.reshape(1, d),
      w_gate, w_up, w_down, w_gate, w_up, w_down)


C_Q = 0
C_QI = C_Q + D_ATTN
C_K = C_QI + N_IDX_HEADS * IDX_DIM
C_V = C_K + D_KV
C_KIWI = C_V + D_KV
C_KIE = C_KIWI + LANES
C_KIO = C_KIE + LANES
C_U = C_KIO + LANES
C_END = C_U + D_POOL


def _layout_w_in(w_in):
    d = w_in.shape[0]
    offs = np.cumsum((0, D_ATTN, D_KV, D_KV, N_IDX_HEADS * IDX_DIM, IDX_DIM, N_IDX_HEADS, D_POOL))
    wq, wk, wv, wqi, wki, wwi, wu = (w_in[:, offs[n]:offs[n + 1]] for n in range(7))
    z = lambda n: jnp.zeros((d, n), w_in.dtype)
    cat = jnp.concatenate([wq, wqi, wk, wv,
                           wki, wwi, z(LANES - IDX_DIM - N_IDX_HEADS),
                           wki, z(LANES - IDX_DIM),
                           z(LANES - IDX_DIM), wki,
                           wu], axis=1)
    assert cat.shape[1] == C_END
    return cat.astype(BF16)


IN_GROUPS = ((C_Q, C_QI, 1.0, 1),
             (C_QI, C_K, 1.0, 1),
             (C_K, C_V, 1.0, 2),
             (C_V, C_KIWI, 1.0, 2),
             (C_KIWI, C_KIE, 1.0, 1),
             (C_KIE, C_KIO, 1.0, 1),
             (C_KIO, C_U, 1.0, 1),
             (C_U, C_END, 1.0, 1))
IN_DTYPES = (BF16, BF16, F32, BF16, F32, BF16, F32, BF16, BF16, F32)


def _pad_rows(a, n):
    return jnp.pad(a, ((0, 0), (0, n - a.shape[1]), (0, 0)))


def _layer(x, pos0, caches, pool_hist, mk, mv, rel_bias, W, *, tm_in, tq, tm_mix, tm_moe, sparse_moe):
    B, T, D = x.shape
    n = B * T
    q, qi, k, kb, v, vb, kiwi, kie, kio, u = _norm_matmul(
        x.reshape(n, D), W["g_mix"], W["w_in"], IN_GROUPS, IN_DTYPES, tm_in)
    r3 = lambda a: a.reshape(B, T, a.shape[-1])
    kblk = 256
    if caches is None:
        n_keys = T
        kb3, vb3, kie3, kio3 = r3(kb), r3(vb), r3(kie), r3(kio)
    else:
        k_hist, v_hist, ki_hist = caches
        past = k_hist.shape[1]
        n_keys = past + T
        lp = -(-n_keys // kblk) * kblk
        ki = r3(kiwi)[:, :, :IDX_DIM]
        ki_all = jnp.concatenate([ki_hist, ki], axis=1)
        zeros = jnp.zeros_like(ki_all)
        kb3 = _pad_rows(jnp.concatenate([k_hist.reshape(B, past, D_KV), r3(k)], axis=1), lp).astype(BF16)
        vb3 = _pad_rows(jnp.concatenate([v_hist.reshape(B, past, D_KV), r3(v)], axis=1), lp).astype(BF16)
        kie3 = _pad_rows(jnp.concatenate([ki_all, zeros], axis=2), lp).astype(BF16)
        kio3 = _pad_rows(jnp.concatenate([zeros, ki_all], axis=2), lp).astype(BF16)
    a_out = _dsa(r3(q), r3(qi), r3(kiwi), kb3, vb3, kie3, kio3, rel_bias,
                 n_keys=n_keys, pos0=pos0, tq=tq, kblk=kblk)
    x2 = _mix(x, a_out, r3(u), pool_hist, mk.astype(BF16), mv.astype(BF16), W["w_pool"], W["pool_scale"],
              W["w_out"], W["g_x"], W["w_xq"], W["w_xo"], W["g_ffn"], W["w_r"], W["b_r"], pos0=pos0, tm=tm_mix)
    moe = _moe_sparse if sparse_moe else _moe
    y = moe(x2.reshape(n, D + LANES), W["g_ffn"], W["w_gate"], W["w_up"], W["w_down"], W["g_final"], tm=tm_moe)
    new_pool = jnp.concatenate([pool_hist, r3(u)], axis=1)[:, -POOL_HIST:]
    return (y.reshape(B, T, D), k.reshape(B, T, N_KV_A, HEAD_DIM_A), v.reshape(B, T, N_KV_A, HEAD_DIM_A),
            r3(kiwi)[:, :, :IDX_DIM], new_pool)


def kernel(x_prompt, x_sample, mem_prompt, cache_k, cache_v, cache_k_idx, cache_pool, cache_mem_k, cache_mem_v, rel_bias, g_mix, w_in, w_pool, pool_scale, w_out, g_mem, w_mk, w_mv, g_x, w_xq, w_xo, g_ffn, w_rg, b_rg, w_re, b_re, w_gate, w_up, w_down, g_final):
    depth = g_mix.shape[0]
    assert depth == 1
    l = 0
    B, T, D = x_prompt.shape
    Bs, Ts, _ = x_sample.shape
    n_mem = mem_prompt.shape[1]
    past = cache_k.shape[2]

    w_r = jnp.concatenate([w_rg[l], w_re[l], jnp.zeros((D, LANES - N_GROUPS - N_EXPERTS), F32)], axis=1)
    W = dict(
        g_mix=g_mix[l], w_in=_layout_w_in(w_in[l]), w_pool=w_pool[l].astype(BF16), pool_scale=pool_scale[l],
        w_out=w_out[l].astype(BF16), g_x=g_x[l], w_xq=w_xq[l].astype(BF16), w_xo=w_xo[l].astype(BF16),
        g_ffn=g_ffn[l], w_r=w_r.astype(BF16),
        b_r=jnp.concatenate([b_rg[l], b_re[l], jnp.zeros((LANES - N_GROUPS - N_EXPERTS,), F32)]).reshape(1, LANES),
        w_gate=w_gate[l].astype(BF16), w_up=w_up[l].astype(BF16), w_down=w_down[l].astype(BF16),
        g_final=g_final)

    w_m = jnp.concatenate([w_mk[l], w_mv[l]], axis=1).astype(BF16)
    mk, mv = _norm_matmul(mem_prompt.reshape(B * n_mem, D), g_mem[l], w_m,
                          ((0, D_X, 1.0, 1), (D_X, 2 * D_X, 1.0, 1)), (F32, F32), 256)
    mk = mk.reshape(B, n_mem, D_X)
    mv = mv.reshape(B, n_mem, D_X)

    yp, kp, vp, kip, pp = _layer(x_prompt, 0, None, jnp.zeros((B, POOL_HIST, D_POOL), F32), mk, mv, rel_bias, W,
                                 tm_in=256, tq=256, tm_mix=256, tm_moe=128, sparse_moe=True)
    ys, ks, vs, kis, ps = _layer(x_sample, past,
                                 (cache_k[l], cache_v[l], cache_k_idx[l]), cache_pool[l],
                                 cache_mem_k[l].reshape(Bs, n_mem, D_X), cache_mem_v[l].reshape(Bs, n_mem, D_X),
                                 rel_bias, W, tm_in=Bs * Ts, tq=Ts, tm_mix=Ts, tm_moe=Bs * Ts, sparse_moe=False)
    st = lambda a: a[None]
    return (yp, ys, st(kp), st(vp), st(kip), st(pp),
            st(mk.reshape(B, n_mem, N_XHEADS, XHEAD_DIM)), st(mv.reshape(B, n_mem, N_XHEADS, XHEAD_DIM)),
            st(ks), st(vs), st(kis), st(ps))
```

```python
import functools
import math

import numpy as np
import jax
import jax.numpy as jnp
from jax import lax
from jax.experimental import pallas as pl
from jax.experimental.pallas import tpu as pltpu

F32 = jnp.float32
BF16 = jnp.bfloat16

CHUNK = 64
N_HEADS_A = 8
HEAD_DIM_A = 128
N_KV_A = 2
D_ATTN = N_HEADS_A * HEAD_DIM_A
D_KV = N_KV_A * HEAD_DIM_A
N_IDX_HEADS = 8
IDX_DIM = 64
TOPK_MAX = 256
POOL_WINDOWS = (2, 4, 8, 16)
POOL_GROUP_DIM = 256
D_POOL = len(POOL_WINDOWS) * POOL_GROUP_DIM
POOL_HIST = max(POOL_WINDOWS) - 1
N_BUCKETS = 32
MAX_DISTANCE = 128
N_XHEADS = 4
XHEAD_DIM = 128
D_X = N_XHEADS * XHEAD_DIM
N_GROUPS = 4
EXPERTS_PER_GROUP = 8
N_EXPERTS = N_GROUPS * EXPERTS_PER_GROUP
D_EXPERT = 256
EPS = 1e-6

LANES = 128
INT_MIN = -(2 ** 31)
NEG = -1e30
LOG2E = math.log2(math.e)
VMEM_LIMIT = 48 * 1024 * 1024


def _rms(x, g):
    ms = jnp.mean(x * x, axis=-1, keepdims=True)
    return x * lax.rsqrt(ms + EPS) * g


def _dot(a, b):
    return jnp.dot(a, b, preferred_element_type=F32)


def _dot_nt(a, b):
    return lax.dot_general(a, b, (((1,), (1,)), ((), ())), preferred_element_type=F32)


def _const_spec(shape):
    nd = len(shape)
    return pl.BlockSpec(shape, lambda *_: (0,) * nd, pipeline_mode=pl.Buffered(1))


def _norm_matmul_kernel(x_ref, g_ref, w_ref, *out_refs, groups):
    h = _rms(x_ref[...], g_ref[...]).astype(BF16)
    k = 0
    for c0, c1, scale, n_out in groups:
        acc = _dot(h, w_ref[:, c0:c1])
        if scale != 1.0:
            acc = acc * scale
        for _ in range(n_out):
            out_refs[k][...] = acc.astype(out_refs[k].dtype)
            k += 1


def _norm_matmul(x, g, w, groups, out_dtypes, tm):
    n, d = x.shape
    assert n % tm == 0
    widths = []
    for c0, c1, _, n_out in groups:
        widths += [c1 - c0] * n_out
    return pl.pallas_call(
        functools.partial(_norm_matmul_kernel, groups=tuple(groups)),
        grid=(n // tm,),
        in_specs=[pl.BlockSpec((tm, d), lambda i: (i, 0)),
                  _const_spec((1, d)),
                  _const_spec(w.shape)],
        out_specs=[pl.BlockSpec((tm, wd), lambda i: (i, 0)) for wd in widths],
        out_shape=[jax.ShapeDtypeStruct((n, wd), dt) for wd, dt in zip(widths, out_dtypes)],
        compiler_params=pltpu.CompilerParams(dimension_semantics=("arbitrary",),
                                             vmem_limit_bytes=VMEM_LIMIT),
        name="norm_matmul",
    )(x, g.reshape(1, d), w)


def _bias_table_kernel(relb_ref, bucket_ref, o_ref, *, shift_bucket):
    bucket = bucket_ref[...]
    for h in range(N_HEADS_A):
        acc = jnp.zeros(bucket.shape, F32)
        for b in range(N_BUCKETS):
            acc = jnp.where(bucket == b, relb_ref[b, h], acc)
        if shift_bucket is None:
            o_ref[h] = acc * LOG2E
        else:
            o_ref[h] = (acc - relb_ref[shift_bucket, h]) * (HEAD_DIM_A ** 0.5)


def _bias_table(rel_bias, bucket, shift_bucket=None):
    tq, w = bucket.shape
    return pl.pallas_call(
        functools.partial(_bias_table_kernel, shift_bucket=shift_bucket),
        in_specs=[pl.BlockSpec(memory_space=pltpu.SMEM),
                  pl.BlockSpec((tq, w), lambda: (0, 0))],
        out_specs=pl.BlockSpec((N_HEADS_A, tq, w), lambda: (0, 0, 0)),
        out_shape=jax.ShapeDtypeStruct((N_HEADS_A, tq, w), F32),
        name="bias_table",
    )(rel_bias, bucket)


def _t5_bucket(rel):
    nb = N_BUCKETS // 2
    max_exact = nb // 2
    bucket = (rel > 0).astype(jnp.int32) * nb
    n = jnp.abs(rel)
    nf = jnp.maximum(n, 1).astype(F32)
    large = max_exact + (jnp.log(nf / max_exact) / math.log(MAX_DISTANCE / max_exact)
                         * (nb - max_exact)).astype(jnp.int32)
    large = jnp.minimum(large, nb - 1)
    return bucket + jnp.where(n < max_exact, n, large)


def _far_bucket(min_dist, max_dist):
    nb = N_BUCKETS // 2
    max_exact = nb // 2
    n = np.arange(min_dist, max_dist + 1, dtype=np.float64)
    large = max_exact + np.floor(np.log(n / max_exact) / math.log(MAX_DISTANCE / max_exact)
                                 * (nb - max_exact) * (1 - 1e-6)).astype(np.int64)
    assert large.min() >= nb - 1, "far key blocks must sit in the saturated distance bucket"
    return nb - 1


def _dsa_kernel(relb_ref, q_ref, qi_ref, kiwi_ref, k_ref, v_ref, kie_ref, kio_ref, bnear_ref, vis_ref,
                o_ref, keys, madd, wb, qs, xcut, mrun, lrun, acc_s,
                *, TQ, KB, RB, kb0, kstep, topk, far_bucket, tie_bits):
    i = pl.program_id(1)
    nkb = kb0 + kstep * i
    n_far = jnp.maximum(nkb - 2, 0)
    n_lc = KB // LANES
    HPG = N_HEADS_A // N_KV_A

    kiwi = kiwi_ref[0]
    for h in range(N_IDX_HEADS):
        w = (kiwi[:, IDX_DIM + h:IDX_DIM + h + 1] * (N_IDX_HEADS ** -0.5)).astype(BF16).astype(F32)
        wb[h] = jnp.broadcast_to(w * (IDX_DIM ** -0.5), (TQ, LANES))

    def idx_block(j, near_col):
        c0 = pl.multiple_of(j * KB, KB)
        ke = kie_ref[0, pl.ds(c0, KB), :]
        ko = kio_ref[0, pl.ds(c0, KB), :]
        acc = [jnp.zeros((TQ, LANES), F32) for _ in range(n_lc)]
        for hp in range(N_IDX_HEADS // 2):
            qh = qi_ref[0, :, hp * LANES:(hp + 1) * LANES]
            se = _dot_nt(qh, ke)
            so = _dot_nt(qh, ko)
            we = wb[2 * hp]
            wo = wb[2 * hp + 1]
            for c in range(n_lc):
                cs = slice(c * LANES, (c + 1) * LANES)
                re = jnp.maximum(se[:, cs], 0.0).astype(BF16).astype(F32)
                ro = jnp.maximum(so[:, cs], 0.0).astype(BF16).astype(F32)
                acc[c] = acc[c] + we * re + wo * ro
        for c in range(n_lc):
            bits = lax.bitcast_convert_type(acc[c], jnp.int32)
            key = bits ^ ((bits >> 31) & jnp.int32(0x7FFFFFFF))
            if near_col is not None:
                vis = vis_ref[:, near_col + c * LANES:near_col + (c + 1) * LANES]
                key = jnp.where(vis > 0.0, key, jnp.int32(INT_MIN))
            keys[j, :, c * LANES:(c + 1) * LANES] = key

    def far_idx(j, carry):
        idx_block(j, None)
        return carry

    lax.fori_loop(0, n_far, far_idx, 0)

    @pl.when(nkb >= 2)
    def _():
        idx_block(nkb - 2, 0)

    idx_block(nkb - 1, KB)

    kf = float(topk)
    lane_f = lax.broadcasted_iota(jnp.int32, (RB, LANES), 1).astype(F32)
    NRB = TQ // RB
    rbs = [slice(rb * RB, (rb + 1) * RB) for rb in range(NRB)]

    def count(pred):
        def body(jj, accs):
            out = []
            for rb in range(NRB):
                blk = keys[jj, rbs[rb], :]
                acc = accs[rb]
                for c in range(n_lc):
                    acc = acc + jnp.where(pred(rb, blk[:, c * LANES:(c + 1) * LANES], jj, c), 1.0, 0.0)
                out.append(acc)
            return tuple(out)
        accs = lax.fori_loop(0, nkb, body, tuple(jnp.zeros((RB, LANES), F32) for _ in range(NRB)))
        return [jnp.sum(acc, axis=1, keepdims=True) for acc in accs]

    def count_ge(cands):
        cb = [jnp.broadcast_to(cand, (RB, LANES)) for cand in cands]
        return count(lambda rb, kblk, jj, c: kblk >= cb[rb])

    def bis_body(p, thrs):
        cands = [thr + lax.shift_left(jnp.int32(1), 31 - p) for thr in thrs]
        cnts = count_ge(cands)
        return tuple(jnp.where(cnt >= kf, cand, thr) for cnt, cand, thr in zip(cnts, cands, thrs))

    thrs = lax.fori_loop(0, 32, bis_body, tuple(jnp.full((RB, 1), INT_MIN, jnp.int32) for _ in range(NRB)))
    thrs = [jnp.maximum(thr, jnp.int32(INT_MIN + 1)) for thr in thrs]
    cnt_ge = count_ge(thrs)
    cnt_gt = count_ge([thr + 1 for thr in thrs])
    needs = [kf - c for c in cnt_gt]
    thrb = [jnp.broadcast_to(thr, (RB, LANES)) for thr in thrs]

    xcut[...] = jnp.full(xcut.shape, 2.0 ** 24, F32)
    excess = cnt_ge[0] - kf
    for c in cnt_ge[1:]:
        excess = jnp.maximum(excess, c - kf)

    @pl.when(jnp.max(excess) > 0.0)
    def _():
        def count_eq_lt(xs):
            xb = [jnp.broadcast_to(x, (RB, LANES)) for x in xs]

            def pred(rb, kblk, jj, c):
                col = lane_f + (jj * KB + c * LANES).astype(F32)
                return jnp.where(kblk == thrb[rb], col, 2.0 ** 25) < xb[rb]
            return count(pred)

        def tie_body(p, xs):
            cands = [x + lax.shift_left(jnp.int32(1), tie_bits - 1 - p).astype(F32) for x in xs]
            cnts = count_eq_lt(cands)
            return tuple(jnp.where(cnt < need, cand, x) for cnt, need, cand, x in zip(cnts, needs, cands, xs))

        xs = lax.fori_loop(0, tie_bits, tie_body, tuple(jnp.zeros((RB, 1), F32) for _ in range(NRB)))
        for rb in range(NRB):
            xcut[rbs[rb], :] = xs[rb]

    xb = [jnp.broadcast_to(xcut[rbs[rb], :], (RB, LANES)) for rb in range(NRB)]

    def mask_body(jj, carry):
        for rb in range(NRB):
            blk = keys[jj, rbs[rb], :]
            for c in range(n_lc):
                kblk = blk[:, c * LANES:(c + 1) * LANES]
                col = lane_f + (jj * KB + c * LANES).astype(F32)
                keep = (kblk > thrb[rb]) | ((kblk == thrb[rb]) & (col <= xb[rb]))
                madd[jj, rbs[rb], c * LANES:(c + 1) * LANES] = jnp.where(keep, 0.0, NEG)
        return carry

    lax.fori_loop(0, nkb, mask_body, 0)

    _attend(relb_ref, q_ref, k_ref, v_ref, bnear_ref, o_ref, madd, qs, mrun, lrun, acc_s,
            TQ=TQ, KB=KB, nkb=nkb, n_far=n_far, far_bucket=far_bucket)


def _attend(relb_ref, q_ref, k_ref, v_ref, bnear_ref, o_ref, madd, qs, mrun, lrun, acc_s,
            *, TQ, KB, nkb, n_far, far_bucket):
    n_lc = KB // LANES
    HPG = N_HEADS_A // N_KV_A
    for g in range(N_KV_A):
        for r in range(HPG):
            h = g * HPG + r
            qs[g, r * TQ:(r + 1) * TQ, :] = q_ref[0, :, h * HEAD_DIM_A:(h + 1) * HEAD_DIM_A]

    def logits(j, near_col, g):
        c0 = pl.multiple_of(j * KB, KB)
        kg = k_ref[0, pl.ds(c0, KB), g * HEAD_DIM_A:(g + 1) * HEAD_DIM_A]
        s = _dot_nt(qs[g], kg)
        mk = madd[j]
        out = []
        for r in range(HPG):
            h = g * HPG + r
            if near_col is None:
                bias = relb_ref[far_bucket, h] * LOG2E
            else:
                bias = bnear_ref[h, :, near_col:near_col + KB]
            out.append(s[r * TQ:(r + 1) * TQ] * (HEAD_DIM_A ** -0.5 * LOG2E) + bias + mk)
        return out

    def over_blocks(fn):
        def far(j, carry):
            fn(j, None)
            return carry

        lax.fori_loop(0, n_far, far, 0)

        @pl.when(nkb >= 2)
        def _():
            fn(nkb - 2, 0)

        fn(nkb - 1, KB)

    mrun[...] = jnp.full(mrun.shape, NEG, F32)

    def max_block(j, near_col):
        for g in range(N_KV_A):
            for r, sh in enumerate(logits(j, near_col, g)):
                rs = slice(r * TQ, (r + 1) * TQ)
                m = mrun[g, rs, :]
                for c in range(n_lc):
                    m = jnp.maximum(m, sh[:, c * LANES:(c + 1) * LANES])
                mrun[g, rs, :] = m

    over_blocks(max_block)

    for g in range(N_KV_A):
        mrun[g] = jnp.broadcast_to(jnp.max(mrun[g], axis=1, keepdims=True), mrun.shape[1:])
    lrun[...] = jnp.zeros(lrun.shape, F32)
    acc_s[...] = jnp.zeros(acc_s.shape, F32)

    def pv_block(j, near_col):
        c0 = pl.multiple_of(j * KB, KB)
        for g in range(N_KV_A):
            vg = v_ref[0, pl.ds(c0, KB), g * HEAD_DIM_A:(g + 1) * HEAD_DIM_A]
            parts = []
            for r, sh in enumerate(logits(j, near_col, g)):
                rs = slice(r * TQ, (r + 1) * TQ)
                mb = mrun[g, rs, :]
                l = lrun[g, rs, :]
                pcs = []
                for c in range(n_lc):
                    p = jnp.exp2(sh[:, c * LANES:(c + 1) * LANES] - mb)
                    l = l + p
                    pcs.append(p.astype(BF16))
                lrun[g, rs, :] = l
                parts.append(jnp.concatenate(pcs, axis=1))
            acc_s[g] += _dot(jnp.concatenate(parts, axis=0), vg)

    over_blocks(pv_block)

    for g in range(N_KV_A):
        out = acc_s[g] / jnp.sum(lrun[g], axis=1, keepdims=True)
        for r in range(HPG):
            h = g * HPG + r
            o_ref[0, :, h * HEAD_DIM_A:(h + 1) * HEAD_DIM_A] = out[r * TQ:(r + 1) * TQ].astype(o_ref.dtype)


DIGIT_BITS = 8
N_DIGITS = 32 // DIGIT_BITS
SUB_BF16 = 16


def _dsa_kernel_t(q_ref, qi_ref, kiwi_ref, k_ref, vt_ref, kie_ref, kio_ref, bnear_ref, vist_ref,
                  o_ref, keys, dig, madd, qs, xcut, acc_s, tbuf,
                  *, TQ, KB, kb0, kstep, topk, tie_bits):
    i = pl.program_id(1)
    nkb = kb0 + kstep * i
    n_far = jnp.maximum(nkb - 2, 0)
    n_slab = KB // SUB_BF16

    w_t = kiwi_ref[0].T
    wrow = [(w_t[IDX_DIM + h:IDX_DIM + h + 1, :] * (N_IDX_HEADS ** -0.5)).astype(BF16).astype(F32)
            * (IDX_DIM ** -0.5) for h in range(N_IDX_HEADS)]

    def idx_block(j, near_row):
        c0 = pl.multiple_of(j * KB, KB)
        ke = kie_ref[0, pl.ds(c0, KB), :]
        ko = kio_ref[0, pl.ds(c0, KB), :]
        acc = jnp.zeros((KB, TQ), F32)
        for hp in range(N_IDX_HEADS // 2):
            qh = qi_ref[0, :, hp * LANES:(hp + 1) * LANES]
            re = jnp.maximum(_dot_nt(ke, qh).astype(BF16), 0.0).astype(F32)
            ro = jnp.maximum(_dot_nt(ko, qh).astype(BF16), 0.0).astype(F32)
            acc = acc + wrow[2 * hp] * re + wrow[2 * hp + 1] * ro
        bits = lax.bitcast_convert_type(acc, jnp.int32)
        key = bits ^ ((bits >> 31) & jnp.int32(0x7FFFFFFF))
        if near_row is not None:
            key = jnp.where(vist_ref[near_row:near_row + KB, :] > 0.0, key, jnp.int32(INT_MIN))
        keys[j] = key
        for dk in range(N_DIGITS):
            sh = 32 - DIGIT_BITS * (dk + 1)
            d = (key >> sh) if dk == 0 else ((key >> sh) & jnp.int32(2 ** DIGIT_BITS - 1))
            dig[dk, j] = d.astype(F32).astype(BF16)

    def far_idx(j, carry):
        idx_block(j, None)
        return carry

    lax.fori_loop(0, n_far, far_idx, 0)

    @pl.when(nkb >= 2)
    def _():
        idx_block(nkb - 2, 0)

    idx_block(nkb - 1, KB)

    one = jnp.ones((SUB_BF16, TQ), BF16)
    zero = jnp.zeros((SUB_BF16, TQ), BF16)

    def count_ge(dk, cand):
        cb = jnp.broadcast_to(cand.astype(BF16), (SUB_BF16, TQ))

        def body(jj, tot):
            blk = dig[dk, jj]
            parts = [jnp.where(blk[s * SUB_BF16:(s + 1) * SUB_BF16, :] >= cb, one, zero) for s in range(n_slab)]
            while len(parts) > 1:
                parts = [parts[n] + parts[n + 1] for n in range(0, len(parts), 2)]
            return tot + parts[0].astype(F32)

        tot = lax.fori_loop(0, nkb, body, jnp.zeros((SUB_BF16, TQ), F32))
        return jnp.sum(tot, axis=0, keepdims=True)

    kneed = jnp.full((1, TQ), float(topk), F32)
    thr = jnp.zeros((1, TQ), jnp.int32)
    cnt_eq = None
    for dk in range(N_DIGITS):
        lo = -float(2 ** (DIGIT_BITS - 1)) if dk == 0 else 0.0

        def bis_body(p, carry, dk=dk, kneed=kneed):
            t, c_rej = carry
            cand = t + lax.shift_left(jnp.int32(1), DIGIT_BITS - 1 - p).astype(F32)
            cnt = count_ge(dk, cand)
            ok = cnt >= kneed
            return jnp.where(ok, cand, t), jnp.where(ok, c_rej, cnt)

        t, n_gt = lax.fori_loop(0, DIGIT_BITS, bis_body,
                                (jnp.full((1, TQ), lo, F32), jnp.zeros((1, TQ), F32)))
        if dk == N_DIGITS - 1:
            cnt_eq = count_ge(dk, t) - n_gt
        kneed = kneed - n_gt
        thr = thr + lax.shift_left(t.astype(jnp.int32), 32 - DIGIT_BITS * (dk + 1))
        if dk + 1 < N_DIGITS:
            tb = t.astype(BF16)

            def prep(jj, carry, dk=dk, tb=tb):
                dig[dk + 1, jj] = jnp.where(dig[dk, jj] == tb, dig[dk + 1, jj], jnp.asarray(-1.0, BF16))
                return carry

            lax.fori_loop(0, nkb, prep, 0)

    thr = jnp.maximum(thr, jnp.int32(INT_MIN + 1))
    need = kneed
    sub_f = lax.broadcasted_iota(jnp.int32, (KB, TQ), 0).astype(F32)

    xcut[...] = jnp.full(xcut.shape, 2.0 ** 24, F32)

    @pl.when(jnp.max(cnt_eq - need) > 0.0)
    def _():
        def count_eq_lt(x):
            def body(jj, tot):
                col = sub_f + (jj * KB).astype(F32)
                hit = jnp.where(jnp.where(keys[jj] == thr, col, 2.0 ** 25) < x, 1.0, 0.0)
                for s in range(KB // 8):
                    tot = tot + hit[s * 8:(s + 1) * 8, :]
                return tot
            tot = lax.fori_loop(0, nkb, body, jnp.zeros((8, TQ), F32))
            return jnp.sum(tot, axis=0, keepdims=True)

        def tie_body(p, x):
            cand = x + lax.shift_left(jnp.int32(1), tie_bits - 1 - p).astype(F32)
            return jnp.where(count_eq_lt(cand) < need, cand, x)

        x = lax.fori_loop(0, tie_bits, tie_body, jnp.zeros((1, TQ), F32))
        xcut[...] = jnp.broadcast_to(x, xcut.shape)

    xc = xcut[0:1, :]

    def mask_body(jj, carry):
        kblk = keys[jj]
        col = sub_f + (jj * KB).astype(F32)
        at_thr = jnp.where(col <= xc, 0.0, NEG)
        madd[jj] = jnp.where(kblk > thr, 0.0, jnp.where(kblk == thr, at_thr, NEG))
        return carry

    lax.fori_loop(0, nkb, mask_body, 0)

    HPG = N_HEADS_A // N_KV_A
    c2 = HEAD_DIM_A ** -0.5 * LOG2E
    for g in range(N_KV_A):
        for r in range(HPG):
            h = g * HPG + r
            qs[g, r * TQ:(r + 1) * TQ, :] = q_ref[0, :, h * HEAD_DIM_A:(h + 1) * HEAD_DIM_A]
    acc_s[...] = jnp.zeros(acc_s.shape, F32)

    def attn_block(j, near_row, carry):
        ms, ls = carry
        c0 = pl.multiple_of(j * KB, KB)
        mk = madd[j]
        ms_out, ls_out = [], []
        s_ts = [_dot_nt(k_ref[0, pl.ds(c0, KB), g * HEAD_DIM_A:(g + 1) * HEAD_DIM_A], qs[g])
                for g in range(N_KV_A)]
        for g in range(N_KV_A):
            vt = vt_ref[0, j, g * HEAD_DIM_A:(g + 1) * HEAD_DIM_A, :]
            s_t = s_ts[g]
            m_news = []
            for r in range(HPG):
                h = g * HPG + r
                cs = slice(r * TQ, (r + 1) * TQ)
                t = s_t[:, cs] + mk
                if near_row is not None:
                    t = t + bnear_ref[h, near_row:near_row + KB, :]
                tbuf[h] = t
                m_news.append(jnp.maximum(ms[h], jnp.max(t, axis=0, keepdims=True)))
            for r in range(HPG):
                h = g * HPG + r
                cs = slice(r * TQ, (r + 1) * TQ)
                m_new = m_news[r]
                alpha = jnp.exp2((ms[h] - m_new) * c2)
                p = jnp.exp2((tbuf[h] - m_new) * c2)
                ls_out.append(alpha * ls[h] + jnp.sum(p, axis=0, keepdims=True))
                ms_out.append(m_new)
                acc_s[g, :, cs] = alpha * acc_s[g, :, cs] + _dot(vt, p.astype(BF16))
        return tuple(ms_out), tuple(ls_out)

    carry = (tuple(jnp.full((1, TQ), NEG, F32) for _ in range(N_HEADS_A)),
             tuple(jnp.zeros((1, TQ), F32) for _ in range(N_HEADS_A)))
    carry = lax.fori_loop(0, n_far, lambda j, c: attn_block(j, None, c), carry)
    carry = lax.cond(nkb >= 2, lambda c: attn_block(nkb - 2, 0, c), lambda c: c, carry)
    ms, ls = attn_block(nkb - 1, KB, carry)

    for g in range(N_KV_A):
        for r in range(HPG):
            h = g * HPG + r
            out_t = acc_s[g, :, r * TQ:(r + 1) * TQ] / ls[h]
            o_ref[0, :, h * HEAD_DIM_A:(h + 1) * HEAD_DIM_A] = out_t.T.astype(o_ref.dtype)


def _dsa(q, qi, kiwi, kb, vb, kie, kio, rel_bias, *, n_keys, pos0, tq, kblk):
    B, T, _ = q.shape
    Lp = kb.shape[1]
    assert T % tq == 0 and Lp % kblk == 0 and tq % CHUNK == 0 or T == tq
    n_qt = T // tq
    if n_qt == 1:
        kb0, kstep = Lp // kblk, 0
        assert ((pos0 + tq - 1) // CHUNK + 1) * CHUNK >= n_keys
    else:
        assert pos0 == 0 and tq == kblk and n_keys == Lp == T
        kb0, kstep = 1, 1
    topk = min(TOPK_MAX, n_keys // 4)
    rb = min(128, tq)

    t = jnp.arange(tq, dtype=jnp.int32)[:, None]
    c = jnp.arange(2 * kblk, dtype=jnp.int32)[None, :]
    if n_qt == 1:
        q_abs = pos0 + t
        s_abs = (kb0 - 2) * kblk + c
    else:
        q_abs = kblk + t
        s_abs = c
    vis = ((s_abs // CHUNK) <= (q_abs // CHUNK)) & (s_abs < (n_keys if n_qt == 1 else 2 * kblk))
    far_bucket = _far_bucket(kblk + 1, max(Lp, kblk + 2))

    n_kb = Lp // kblk
    hpg = N_HEADS_A // N_KV_A
    common = dict(TQ=tq, KB=kblk, kb0=kb0, kstep=kstep, topk=topk, far_bucket=far_bucket,
                  tie_bits=int(math.ceil(math.log2(Lp))) + 1)
    attend_scratch = [pltpu.VMEM((N_KV_A, hpg * tq, HEAD_DIM_A), BF16),
                      pltpu.VMEM((N_KV_A, hpg * tq, LANES), F32),
                      pltpu.VMEM((N_KV_A, hpg * tq, LANES), F32),
                      pltpu.VMEM((N_KV_A, hpg * tq, HEAD_DIM_A), F32)]
    if tq % LANES == 0:
        tab_t = _bias_table(rel_bias, _t5_bucket(s_abs - q_abs).T, shift_bucket=far_bucket)
        vt = jnp.swapaxes(vb.reshape(B, n_kb, kblk, D_KV), 2, 3)
        common.pop("far_bucket")
        return pl.pallas_call(
            functools.partial(_dsa_kernel_t, **common),
            grid=(B, n_qt),
            in_specs=[pl.BlockSpec((1, tq, D_ATTN), lambda b, i: (b, i, 0)),
                      pl.BlockSpec((1, tq, N_IDX_HEADS * IDX_DIM), lambda b, i: (b, i, 0)),
                      pl.BlockSpec((1, tq, LANES), lambda b, i: (b, i, 0)),
                      pl.BlockSpec((1, Lp, D_KV), lambda b, i: (b, 0, 0)),
                      pl.BlockSpec((1, n_kb, D_KV, kblk), lambda b, i: (b, 0, 0, 0)),
                      pl.BlockSpec((1, Lp, LANES), lambda b, i: (b, 0, 0)),
                      pl.BlockSpec((1, Lp, LANES), lambda b, i: (b, 0, 0)),
                      _const_spec((N_HEADS_A, 2 * kblk, tq)),
                      _const_spec((2 * kblk, tq))],
            out_specs=pl.BlockSpec((1, tq, D_ATTN), lambda b, i: (b, i, 0)),
            out_shape=jax.ShapeDtypeStruct((B, T, D_ATTN), BF16),
            scratch_shapes=[pltpu.VMEM((n_kb, kblk, tq), jnp.int32),
                            pltpu.VMEM((N_DIGITS, n_kb, kblk, tq), BF16),
                            pltpu.VMEM((n_kb, kblk, tq), F32),
                            pltpu.VMEM((N_KV_A, hpg * tq, HEAD_DIM_A), BF16),
                            pltpu.VMEM((8, tq), F32),
                            pltpu.VMEM((N_KV_A, HEAD_DIM_A, hpg * tq), F32),
                            pltpu.VMEM((N_HEADS_A, kblk, tq), F32)],
            compiler_params=pltpu.CompilerParams(dimension_semantics=("arbitrary", "arbitrary"),
                                                 vmem_limit_bytes=VMEM_LIMIT),
            name="dsa_t",
        )(q, qi, kiwi, kb, vt, kie, kio, tab_t, vis.astype(F32).T)

    bias_near = _bias_table(rel_bias, _t5_bucket(s_abs - q_abs))
    kern = functools.partial(_dsa_kernel, RB=rb, **common)
    return pl.pallas_call(
        kern,
        grid=(B, n_qt),
        in_specs=[pl.BlockSpec(memory_space=pltpu.SMEM),
                  pl.BlockSpec((1, tq, D_ATTN), lambda b, i: (b, i, 0)),
                  pl.BlockSpec((1, tq, N_IDX_HEADS * IDX_DIM), lambda b, i: (b, i, 0)),
                  pl.BlockSpec((1, tq, LANES), lambda b, i: (b, i, 0)),
                  pl.BlockSpec((1, Lp, D_KV), lambda b, i: (b, 0, 0)),
                  pl.BlockSpec((1, Lp, D_KV), lambda b, i: (b, 0, 0)),
                  pl.BlockSpec((1, Lp, LANES), lambda b, i: (b, 0, 0)),
                  pl.BlockSpec((1, Lp, LANES), lambda b, i: (b, 0, 0)),
                  _const_spec((N_HEADS_A, tq, 2 * kblk)),
                  _const_spec((tq, 2 * kblk))],
        out_specs=pl.BlockSpec((1, tq, D_ATTN), lambda b, i: (b, i, 0)),
        out_shape=jax.ShapeDtypeStruct((B, T, D_ATTN), BF16),
        scratch_shapes=[pltpu.VMEM((n_kb, tq, kblk), jnp.int32),
                        pltpu.VMEM((n_kb, tq, kblk), F32),
                        pltpu.VMEM((N_IDX_HEADS, tq, LANES), F32),
                        pltpu.VMEM((N_KV_A, hpg * tq, HEAD_DIM_A), BF16),
                        pltpu.VMEM((tq, 1), F32),
                        pltpu.VMEM((N_KV_A, hpg * tq, LANES), F32),
                        pltpu.VMEM((N_KV_A, hpg * tq, LANES), F32),
                        pltpu.VMEM((N_KV_A, hpg * tq, HEAD_DIM_A), F32)],
        compiler_params=pltpu.CompilerParams(dimension_semantics=("arbitrary", "arbitrary"),
                                             vmem_limit_bytes=VMEM_LIMIT),
        name="dsa",
    )(rel_bias, q, qi, kiwi, kb, vb, kie, kio, bias_near, vis.astype(F32))


def _mix_kernel(x_ref, a_ref, u_ref, hist_ref, mk_ref, mv_ref, wpool_ref, pscale_ref, wout_ref, gx_ref,
                wxq_ref, wxo_ref, gffn_ref, wr_ref, br_ref, o_ref, ext, *, tm, pos0):
    t = pl.program_id(1)
    H = POOL_HIST + 1

    @pl.when(t == 0)
    def _():
        ext[0:H, :] = hist_ref[0]

    @pl.when(t > 0)
    def _():
        ext[0:H, :] = ext[tm:tm + H, :]

    ext[H:H + tm, :] = u_ref[0]

    pos = pos0 + t * tm + lax.broadcasted_iota(jnp.int32, (tm, 1), 0)
    pouts = []
    for gi, w in enumerate(POOL_WINDOWS):
        c0, c1 = gi * POOL_GROUP_DIM, (gi + 1) * POOL_GROUP_DIM
        wsum = ext[H:H + tm, c0:c1]
        for s in range(1, w):
            wsum = wsum + ext[H - s:H - s + tm, c0:c1]
        cnt = jnp.minimum(pos + 1, w).astype(F32)
        d = wsum / cnt - u_ref[0, :, c0:c1]
        y = _dot(d.astype(BF16), wpool_ref[gi]) * pscale_ref[:, c0:c1]
        pouts.append(y.astype(BF16))
    pcat = jnp.concatenate(pouts, axis=1)

    x1 = x_ref[0] + _dot(a_ref[0], wout_ref[0:D_ATTN, :]) + _dot(pcat, wout_ref[D_ATTN:D_ATTN + D_POOL, :])

    h = _rms(x1, gx_ref[...]).astype(BF16)
    qx = _dot(h, wxq_ref[...])
    outs = []
    for hh in range(N_XHEADS):
        hs = slice(hh * XHEAD_DIM, (hh + 1) * XHEAD_DIM)
        logits = _dot_nt(qx[:, hs].astype(BF16), mk_ref[0, :, hs]) * (XHEAD_DIM ** -0.5)
        m = jnp.max(logits, axis=1, keepdims=True)
        p = jnp.exp(logits - m)
        l = jnp.sum(p, axis=1, keepdims=True)
        outs.append((_dot(p.astype(BF16), mv_ref[0, :, hs]) / l).astype(BF16))
    o = jnp.concatenate(outs, axis=1)
    x2 = x1 + _dot(o, wxo_ref[...])
    D = x2.shape[1]
    o_ref[0, :, 0:D] = x2
    hf = _rms(x2, gffn_ref[...]).astype(BF16)
    o_ref[0, :, D:D + LANES] = _route(_dot(hf, wr_ref[...]) + br_ref[...])


def _mix(x, a, u, hist, mk, mv, w_pool, pool_scale, w_out, g_x, w_xq, w_xo, g_ffn, w_r, b_r, *, pos0, tm):
    B, T, D = x.shape
    n_mem = mk.shape[1]
    H = POOL_HIST + 1
    assert T % tm == 0 and tm >= H
    hist16 = jnp.concatenate([jnp.zeros((B, 1, D_POOL), F32), hist], axis=1)
    return pl.pallas_call(
        functools.partial(_mix_kernel, tm=tm, pos0=pos0),
        grid=(B, T // tm),
        in_specs=[pl.BlockSpec((1, tm, D), lambda b, t: (b, t, 0)),
                  pl.BlockSpec((1, tm, D_ATTN), lambda b, t: (b, t, 0)),
                  pl.BlockSpec((1, tm, D_POOL), lambda b, t: (b, t, 0)),
                  pl.BlockSpec((1, H, D_POOL), lambda b, t: (b, 0, 0)),
                  pl.BlockSpec((1, n_mem, D_X), lambda b, t: (b, 0, 0)),
                  pl.BlockSpec((1, n_mem, D_X), lambda b, t: (b, 0, 0)),
                  _const_spec(w_pool.shape),
                  _const_spec((1, D_POOL)),
                  _const_spec(w_out.shape),
                  _const_spec((1, D)),
                  _const_spec(w_xq.shape),
                  _const_spec(w_xo.shape),
                  _const_spec((1, D)),
                  _const_spec(w_r.shape),
                  _const_spec((1, LANES))],
        out_specs=pl.BlockSpec((1, tm, D + LANES), lambda b, t: (b, t, 0)),
        out_shape=jax.ShapeDtypeStruct((B, T, D + LANES), F32),
        scratch_shapes=[pltpu.VMEM((H + tm, D_POOL), F32)],
        compiler_params=pltpu.CompilerParams(dimension_semantics=("arbitrary", "arbitrary"),
                                             vmem_limit_bytes=VMEM_LIMIT),
        name="mix",
    )(x, a, u, hist16, mk, mv, w_pool, pool_scale.reshape(1, D_POOL), w_out, g_x.reshape(1, D), w_xq, w_xo,
      g_ffn.reshape(1, D), w_r, b_r)


R_OFF = N_GROUPS


def _route(logits):
    tm = logits.shape[0]
    lane = lax.broadcasted_iota(jnp.int32, (tm, LANES), 1).astype(F32)
    ninf = -jnp.inf
    big = float(LANES)
    gl = jnp.where(lane < N_GROUPS, logits, ninf)
    gmax = jnp.max(gl, axis=1, keepdims=True)
    g_sel = jnp.min(jnp.where(gl == gmax, lane, big), axis=1, keepdims=True)
    g_prob = 1.0 / jnp.sum(jnp.exp(gl - gmax), axis=1, keepdims=True)
    lo = R_OFF + g_sel * EXPERTS_PER_GROUP
    el = jnp.where((lane >= lo) & (lane < lo + EXPERTS_PER_GROUP), logits, ninf)
    tv0 = jnp.max(el, axis=1, keepdims=True)
    ti0 = jnp.min(jnp.where(el == tv0, lane, big), axis=1, keepdims=True)
    el2 = jnp.where(lane == ti0, ninf, el)
    tv1 = jnp.max(el2, axis=1, keepdims=True)
    ti1 = jnp.min(jnp.where(el2 == tv1, lane, big), axis=1, keepdims=True)
    e1 = jnp.exp(tv1 - tv0)
    den = 1.0 + e1
    w0 = g_prob / den
    w1 = g_prob * e1 / den
    ids = jnp.where(lane == 0.0, jnp.minimum(ti0, ti1) - R_OFF,
                    jnp.where(lane == 1.0, jnp.maximum(ti0, ti1) - R_OFF, 0.0))
    return jnp.where(lane == ti0, w0, 0.0) + jnp.where(lane == ti1, w1, 0.0) + ids


def _expert(h, rec, e, wg, wu, wd):
    a = _dot(h, wg)
    b = _dot(h, wu)
    lane = lax.broadcasted_iota(jnp.int32, rec.shape, 1)
    cw = jnp.sum(jnp.where(lane == e + R_OFF, rec, 0.0), axis=1, keepdims=True)
    hid = a * jax.nn.sigmoid(a) * b * cw
    return _dot(hid.astype(BF16), wd)


def _moe_kernel(x_ref, gffn_ref, wg_ref, wu_ref, wd_ref, gfin_ref, o_ref, h_s, acc_s):
    e = pl.program_id(1)
    d = o_ref.shape[1]

    @pl.when(e == 0)
    def _():
        h_s[...] = _rms(x_ref[:, 0:d], gffn_ref[...]).astype(BF16)
        acc_s[...] = jnp.zeros(acc_s.shape, F32)

    acc_s[...] += _expert(h_s[...], x_ref[:, d:d + LANES], e, wg_ref[0], wu_ref[0], wd_ref[0])

    @pl.when(e == pl.num_programs(1) - 1)
    def _():
        o_ref[...] = _rms(x_ref[:, 0:d] + acc_s[...], gfin_ref[...])


def _moe(x, g_ffn, w_gate, w_up, w_down, g_final, *, tm):
    n, de = x.shape
    d = de - LANES
    assert n % tm == 0
    return pl.pallas_call(
        _moe_kernel,
        grid=(n // tm, N_EXPERTS),
        in_specs=[pl.BlockSpec((tm, de), lambda i, e: (i, 0)),
                  _const_spec((1, d)),
                  pl.BlockSpec((1, d, D_EXPERT), lambda i, e: (e, 0, 0)),
                  pl.BlockSpec((1, d, D_EXPERT), lambda i, e: (e, 0, 0)),
                  pl.BlockSpec((1, D_EXPERT, d), lambda i, e: (e, 0, 0)),
                  _const_spec((1, d))],
        out_specs=pl.BlockSpec((tm, d), lambda i, e: (i, 0)),
        out_shape=jax.ShapeDtypeStruct((n, d), F32),
        scratch_shapes=[pltpu.VMEM((tm, d), BF16),
                        pltpu.VMEM((tm, d), F32)],
        compiler_params=pltpu.CompilerParams(dimension_semantics=("arbitrary", "arbitrary"),
                                             vmem_limit_bytes=VMEM_LIMIT),
        name="moe",
    )(x, g_ffn.reshape(1, d), w_gate, w_up, w_down, g_final.reshape(1, d))


def _moe_sparse_kernel(ea_ref, eb_ref, nv_ref, idx_ref, idxn_ref, x_hbm, gffn_ref, gfin_ref,
                       wga_ref, wua_ref, wda_ref, wgb_ref, wub_ref, wdb_ref, o_hbm,
                       xbuf, ybuf, gsem, ssem, *, tm):
    t = pl.program_id(0)
    nt = pl.num_programs(0)
    d = o_hbm.shape[1]
    slot = lax.rem(t, 2)
    nslot = 1 - slot
    nv = nv_ref[t]
    WAIT_ROWS = 8

    def gather_row(ids, s, r):
        return pltpu.make_async_copy(x_hbm.at[pl.ds(ids[0, 0, r], 1), :], xbuf.at[s, pl.ds(r, 1), :], gsem.at[s])

    def gather_loop(ids, s):
        def body(r, carry):
            gather_row(ids, s, r).start()
            return carry
        lax.fori_loop(0, tm, body, 0)

    def gather_wait(s):
        pltpu.make_async_copy(x_hbm.at[pl.ds(0, tm), :], xbuf.at[s], gsem.at[s]).wait()

    def scatter_rows(s, r, row, n):
        return pltpu.make_async_copy(ybuf.at[s, pl.ds(r, n), :], o_hbm.at[pl.ds(row, n), :], ssem.at[s])

    def scatter_wait(s, n):
        def body_many(r, carry):
            scatter_rows(s, 0, 0, WAIT_ROWS).wait()
            return carry
        lax.fori_loop(0, n // WAIT_ROWS, body_many, 0)

        def body_one(r, carry):
            scatter_rows(s, 0, 0, 1).wait()
            return carry
        lax.fori_loop(0, lax.rem(n, WAIT_ROWS), body_one, 0)

    @pl.when(t == 0)
    def _():
        gather_loop(idx_ref, 0)

    @pl.when(t >= 2)
    def _():
        scatter_wait(slot, nv_ref[jnp.maximum(t - 2, 0)])

    gather_wait(slot)

    @pl.when(nv > 0)
    def _():
        x = xbuf[slot]
        xr = x[:, 0:d]
        rec = x[:, d:d + LANES]
        h = _rms(xr, gffn_ref[...]).astype(BF16)
        for r in range(tm):
            gather_row(idxn_ref, nslot, r).start()
        lane = lax.broadcasted_iota(jnp.int32, rec.shape, 1)
        pre = [(_dot(h, wg[0]), _dot(h, wu[0])) for wg, wu in ((wga_ref, wua_ref), (wgb_ref, wub_ref))]
        acc = None
        for (a, b), e, wd in zip(pre, (ea_ref[t], eb_ref[t]), (wda_ref, wdb_ref)):
            cw = jnp.sum(jnp.where(lane == e + R_OFF, rec, 0.0), axis=1, keepdims=True)
            hid = a * jax.nn.sigmoid(a) * b * cw
            out = _dot(hid.astype(BF16), wd[0])
            acc = out if acc is None else acc + out
        ybuf[slot] = _rms(xr + acc, gfin_ref[...])

        def body(r, carry):
            scatter_rows(slot, r, idx_ref[0, 0, r], 1).start()
            return carry
        lax.fori_loop(0, nv, body, 0)

    @pl.when(nv <= 0)
    def _():
        gather_loop(idxn_ref, nslot)

    @pl.when(t == nt - 1)
    def _():
        gather_wait(nslot)

        @pl.when(t >= 1)
        def _():
            scatter_wait(nslot, nv_ref[jnp.maximum(t - 1, 0)])
        scatter_wait(slot, nv)


def _moe_sparse(x, g_ffn, w_gate, w_up, w_down, g_final, *, tm):
    n, de = x.shape
    d = de - LANES
    n_pairs = N_GROUPS * (EXPERTS_PER_GROUP * (EXPERTS_PER_GROUP - 1) // 2)
    nt = -(-n // tm) + n_pairs
    n_keys = N_EXPERTS * N_EXPERTS

    ids = x[:, d:d + 2].astype(jnp.int32)
    key = ids[:, 0] * N_EXPERTS + ids[:, 1]
    order = jnp.argsort(key).astype(jnp.int32)
    skey = key[order]
    count_below = lambda sorted_vals, q: jnp.sum((sorted_vals[None, :] < q[:, None]).astype(jnp.int32), axis=1)
    edges = count_below(skey, jnp.arange(n_keys + 1, dtype=jnp.int32))
    starts, ends = edges[:-1], edges[1:]
    tiles = (ends - starts + tm - 1) // tm
    cum = jnp.cumsum(tiles)
    total = cum[-1]
    tt = jnp.arange(nt, dtype=jnp.int32)
    tc = jnp.minimum(tt, total - 1)
    cls = count_below(cum, tc + 1)
    first = cum[cls] - tiles[cls]
    tstart = starts[cls] + (tc - first) * tm
    nv = jnp.where(tt < total, jnp.clip(ends[cls] - tstart, 0, tm), 0).astype(jnp.int32)
    ea = (cls // N_EXPERTS).astype(jnp.int32)
    eb = (cls % N_EXPERTS).astype(jnp.int32)
    rows = jnp.clip(tstart[:, None] + jnp.arange(tm, dtype=jnp.int32)[None, :], 0, n - 1)
    idx = order[rows].reshape(nt, 1, tm)

    wspec_a = lambda shape: pl.BlockSpec(shape, lambda t, ea, eb, nv: (ea[t], 0, 0))
    wspec_b = lambda shape: pl.BlockSpec(shape, lambda t, ea, eb, nv: (eb[t], 0, 0))
    cspec = lambda shape: pl.BlockSpec(shape, lambda t, ea, eb, nv: (0,) * len(shape),
                                       pipeline_mode=pl.Buffered(1))
    grid_spec = pltpu.PrefetchScalarGridSpec(
        num_scalar_prefetch=3,
        grid=(nt,),
        in_specs=[pl.BlockSpec((1, 1, tm), lambda t, ea, eb, nv: (t, 0, 0), memory_space=pltpu.SMEM),
                  pl.BlockSpec((1, 1, tm), lambda t, ea, eb, nv: (jnp.minimum(t + 1, nt - 1), 0, 0),
                               memory_space=pltpu.SMEM),
                  pl.BlockSpec(memory_space=pl.ANY),
                  cspec((1, d)), cspec((1, d)),
                  wspec_a((1, d, D_EXPERT)), wspec_a((1, d, D_EXPERT)), wspec_a((1, D_EXPERT, d)),
                  wspec_b((1, d, D_EXPERT)), wspec_b((1, d, D_EXPERT)), wspec_b((1, D_EXPERT, d))],
        out_specs=pl.BlockSpec(memory_space=pl.ANY),
        scratch_shapes=[pltpu.VMEM((2, tm, de), F32),
                        pltpu.VMEM((2, tm, d), F32),
                        pltpu.SemaphoreType.DMA((2,)),
                        pltpu.SemaphoreType.DMA((2,))])
    return pl.pallas_call(
        functools.partial(_moe_sparse_kernel, tm=tm),
        grid_spec=grid_spec,
        out_shape=jax.ShapeDtypeStruct((n, d), F32),
        compiler_params=pltpu.CompilerParams(dimension_semantics=("arbitrary",),
                                             vmem_limit_bytes=VMEM_LIMIT),
        name="moe_sparse",
    )(ea, eb, nv, idx, idx, x, g_ffn.reshape(1, d), g_final.reshape(1, d),
      w_gate, w_up, w_down, w_gate, w_up, w_down)


C_Q = 0
C_QI = C_Q + D_ATTN
C_K = C_QI + N_IDX_HEADS * IDX_DIM
C_V = C_K + D_KV
C_KIWI = C_V + D_KV
C_KIE = C_KIWI + LANES
C_KIO = C_KIE + LANES
C_U = C_KIO + LANES
C_END = C_U + D_POOL


def _layout_w_in(w_in):
    d = w_in.shape[0]
    offs = np.cumsum((0, D_ATTN, D_KV, D_KV, N_IDX_HEADS * IDX_DIM, IDX_DIM, N_IDX_HEADS, D_POOL))
    wq, wk, wv, wqi, wki, wwi, wu = (w_in[:, offs[n]:offs[n + 1]] for n in range(7))
    z = lambda n: jnp.zeros((d, n), w_in.dtype)
    cat = jnp.concatenate([wq, wqi, wk, wv,
                           wki, wwi, z(LANES - IDX_DIM - N_IDX_HEADS),
                           wki, z(LANES - IDX_DIM),
                           z(LANES - IDX_DIM), wki,
                           wu], axis=1)
    assert cat.shape[1] == C_END
    return cat.astype(BF16)


IN_GROUPS = ((C_Q, C_QI, 1.0, 1),
             (C_QI, C_K, 1.0, 1),
             (C_K, C_V, 1.0, 2),
             (C_V, C_KIWI, 1.0, 2),
             (C_KIWI, C_KIE, 1.0, 1),
             (C_KIE, C_KIO, 1.0, 1),
             (C_KIO, C_U, 1.0, 1),
             (C_U, C_END, 1.0, 1))
IN_DTYPES = (BF16, BF16, F32, BF16, F32, BF16, F32, BF16, BF16, F32)


def _pad_rows(a, n):
    return jnp.pad(a, ((0, 0), (0, n - a.shape[1]), (0, 0)))


def _layer(x, pos0, caches, pool_hist, mk, mv, rel_bias, W, *, tm_in, tq, tm_mix, tm_moe, sparse_moe):
    B, T, D = x.shape
    n = B * T
    q, qi, k, kb, v, vb, kiwi, kie, kio, u = _norm_matmul(
        x.reshape(n, D), W["g_mix"], W["w_in"], IN_GROUPS, IN_DTYPES, tm_in)
    r3 = lambda a: a.reshape(B, T, a.shape[-1])
    kblk = 256
    if caches is None:
        n_keys = T
        kb3, vb3, kie3, kio3 = r3(kb), r3(vb), r3(kie), r3(kio)
    else:
        k_hist, v_hist, ki_hist = caches
        past = k_hist.shape[1]
        n_keys = past + T
        lp = -(-n_keys // kblk) * kblk
        ki = r3(kiwi)[:, :, :IDX_DIM]
        ki_all = jnp.concatenate([ki_hist, ki], axis=1)
        zeros = jnp.zeros_like(ki_all)
        kb3 = _pad_rows(jnp.concatenate([k_hist.reshape(B, past, D_KV), r3(k)], axis=1), lp).astype(BF16)
        vb3 = _pad_rows(jnp.concatenate([v_hist.reshape(B, past, D_KV), r3(v)], axis=1), lp).astype(BF16)
        kie3 = _pad_rows(jnp.concatenate([ki_all, zeros], axis=2), lp).astype(BF16)
        kio3 = _pad_rows(jnp.concatenate([zeros, ki_all], axis=2), lp).astype(BF16)
    a_out = _dsa(r3(q), r3(qi), r3(kiwi), kb3, vb3, kie3, kio3, rel_bias,
                 n_keys=n_keys, pos0=pos0, tq=tq, kblk=kblk)
    x2 = _mix(x, a_out, r3(u), pool_hist, mk.astype(BF16), mv.astype(BF16), W["w_pool"], W["pool_scale"],
              W["w_out"], W["g_x"], W["w_xq"], W["w_xo"], W["g_ffn"], W["w_r"], W["b_r"], pos0=pos0, tm=tm_mix)
    moe = _moe_sparse if sparse_moe else _moe
    y = moe(x2.reshape(n, D + LANES), W["g_ffn"], W["w_gate"], W["w_up"], W["w_down"], W["g_final"], tm=tm_moe)
    new_pool = jnp.concatenate([pool_hist, r3(u)], axis=1)[:, -POOL_HIST:]
    return (y.reshape(B, T, D), k.reshape(B, T, N_KV_A, HEAD_DIM_A), v.reshape(B, T, N_KV_A, HEAD_DIM_A),
            r3(kiwi)[:, :, :IDX_DIM], new_pool)


def kernel(x_prompt, x_sample, mem_prompt, cache_k, cache_v, cache_k_idx, cache_pool, cache_mem_k, cache_mem_v, rel_bias, g_mix, w_in, w_pool, pool_scale, w_out, g_mem, w_mk, w_mv, g_x, w_xq, w_xo, g_ffn, w_rg, b_rg, w_re, b_re, w_gate, w_up, w_down, g_final):
    depth = g_mix.shape[0]
    assert depth == 1
    l = 0
    B, T, D = x_prompt.shape
    Bs, Ts, _ = x_sample.shape
    n_mem = mem_prompt.shape[1]
    past = cache_k.shape[2]

    w_r = jnp.concatenate([w_rg[l], w_re[l], jnp.zeros((D, LANES - N_GROUPS - N_EXPERTS), F32)], axis=1)
    W = dict(
        g_mix=g_mix[l], w_in=_layout_w_in(w_in[l]), w_pool=w_pool[l].astype(BF16), pool_scale=pool_scale[l],
        w_out=w_out[l].astype(BF16), g_x=g_x[l], w_xq=w_xq[l].astype(BF16), w_xo=w_xo[l].astype(BF16),
        g_ffn=g_ffn[l], w_r=w_r.astype(BF16),
        b_r=jnp.concatenate([b_rg[l], b_re[l], jnp.zeros((LANES - N_GROUPS - N_EXPERTS,), F32)]).reshape(1, LANES),
        w_gate=w_gate[l].astype(BF16), w_up=w_up[l].astype(BF16), w_down=w_down[l].astype(BF16),
        g_final=g_final)

    w_m = jnp.concatenate([w_mk[l], w_mv[l]], axis=1).astype(BF16)
    mk, mv = _norm_matmul(mem_prompt.reshape(B * n_mem, D), g_mem[l], w_m,
                          ((0, D_X, 1.0, 1), (D_X, 2 * D_X, 1.0, 1)), (F32, F32), 256)
    mk = mk.reshape(B, n_mem, D_X)
    mv = mv.reshape(B, n_mem, D_X)

    yp, kp, vp, kip, pp = _layer(x_prompt, 0, None, jnp.zeros((B, POOL_HIST, D_POOL), F32), mk, mv, rel_bias, W,
                                 tm_in=256, tq=256, tm_mix=256, tm_moe=128, sparse_moe=True)
    ys, ks, vs, kis, ps = _layer(x_sample, past,
                                 (cache_k[l], cache_v[l], cache_k_idx[l]), cache_pool[l],
                                 cache_mem_k[l].reshape(Bs, n_mem, D_X), cache_mem_v[l].reshape(Bs, n_mem, D_X),
                                 rel_bias, W, tm_in=Bs * Ts, tq=Ts, tm_mix=Ts, tm_moe=Bs * Ts, sparse_moe=False)
    st = lambda a: a[None]
    return (yp, ys, st(kp), st(vp), st(kip), st(pp),
            st(mk.reshape(B, n_mem, N_XHEADS, XHEAD_DIM)), st(mv.reshape(B, n_mem, N_XHEADS, XHEAD_DIM)),
            st(ks), st(vs), st(kis), st(ps))
```

```python
import functools
import math

import numpy as np
import jax
import jax.numpy as jnp
from jax import lax
from jax.experimental import pallas as pl
from jax.experimental.pallas import tpu as pltpu

F32 = jnp.float32
BF16 = jnp.bfloat16

CHUNK = 64
N_HEADS_A = 8
HEAD_DIM_A = 128
N_KV_A = 2
D_ATTN = N_HEADS_A * HEAD_DIM_A
D_KV = N_KV_A * HEAD_DIM_A
N_IDX_HEADS = 8
IDX_DIM = 64
TOPK_MAX = 256
POOL_WINDOWS = (2, 4, 8, 16)
POOL_GROUP_DIM = 256
D_POOL = len(POOL_WINDOWS) * POOL_GROUP_DIM
POOL_HIST = max(POOL_WINDOWS) - 1
N_BUCKETS = 32
MAX_DISTANCE = 128
N_XHEADS = 4
XHEAD_DIM = 128
D_X = N_XHEADS * XHEAD_DIM
N_GROUPS = 4
EXPERTS_PER_GROUP = 8
N_EXPERTS = N_GROUPS * EXPERTS_PER_GROUP
D_EXPERT = 256
EPS = 1e-6

LANES = 128
INT_MIN = -(2 ** 31)
NEG = -1e30
LOG2E = math.log2(math.e)
VMEM_LIMIT = 48 * 1024 * 1024


def _rms(x, g):
    ms = jnp.mean(x * x, axis=-1, keepdims=True)
    return x * lax.rsqrt(ms + EPS) * g


def _dot(a, b):
    return jnp.dot(a, b, preferred_element_type=F32)


def _dot_nt(a, b):
    return lax.dot_general(a, b, (((1,), (1,)), ((), ())), preferred_element_type=F32)


def _const_spec(shape):
    nd = len(shape)
    return pl.BlockSpec(shape, lambda *_: (0,) * nd, pipeline_mode=pl.Buffered(1))


def _norm_matmul_kernel(x_ref, g_ref, w_ref, *out_refs, groups):
    h = _rms(x_ref[...], g_ref[...]).astype(BF16)
    k = 0
    for c0, c1, scale, n_out in groups:
        acc = _dot(h, w_ref[:, c0:c1])
        if scale != 1.0:
            acc = acc * scale
        for _ in range(n_out):
            out_refs[k][...] = acc.astype(out_refs[k].dtype)
            k += 1


def _norm_matmul(x, g, w, groups, out_dtypes, tm):
    n, d = x.shape
    assert n % tm == 0
    widths = []
    for c0, c1, _, n_out in groups:
        widths += [c1 - c0] * n_out
    return pl.pallas_call(
        functools.partial(_norm_matmul_kernel, groups=tuple(groups)),
        grid=(n // tm,),
        in_specs=[pl.BlockSpec((tm, d), lambda i: (i, 0)),
                  _const_spec((1, d)),
                  _const_spec(w.shape)],
        out_specs=[pl.BlockSpec((tm, wd), lambda i: (i, 0)) for wd in widths],
        out_shape=[jax.ShapeDtypeStruct((n, wd), dt) for wd, dt in zip(widths, out_dtypes)],
        compiler_params=pltpu.CompilerParams(dimension_semantics=("arbitrary",),
                                             vmem_limit_bytes=VMEM_LIMIT),
        name="norm_matmul",
    )(x, g.reshape(1, d), w)


def _bias_table_kernel(relb_ref, bucket_ref, o_ref, *, shift_bucket):
    bucket = bucket_ref[...]
    for h in range(N_HEADS_A):
        acc = jnp.zeros(bucket.shape, F32)
        for b in range(N_BUCKETS):
            acc = jnp.where(bucket == b, relb_ref[b, h], acc)
        if shift_bucket is None:
            o_ref[h] = acc * LOG2E
        else:
            o_ref[h] = (acc - relb_ref[shift_bucket, h]) * (HEAD_DIM_A ** 0.5)


def _bias_table(rel_bias, bucket, shift_bucket=None):
    tq, w = bucket.shape
    return pl.pallas_call(
        functools.partial(_bias_table_kernel, shift_bucket=shift_bucket),
        in_specs=[pl.BlockSpec(memory_space=pltpu.SMEM),
                  pl.BlockSpec((tq, w), lambda: (0, 0))],
        out_specs=pl.BlockSpec((N_HEADS_A, tq, w), lambda: (0, 0, 0)),
        out_shape=jax.ShapeDtypeStruct((N_HEADS_A, tq, w), F32),
        name="bias_table",
    )(rel_bias, bucket)


def _t5_bucket(rel):
    nb = N_BUCKETS // 2
    max_exact = nb // 2
    bucket = (rel > 0).astype(jnp.int32) * nb
    n = jnp.abs(rel)
    nf = jnp.maximum(n, 1).astype(F32)
    large = max_exact + (jnp.log(nf / max_exact) / math.log(MAX_DISTANCE / max_exact)
                         * (nb - max_exact)).astype(jnp.int32)
    large = jnp.minimum(large, nb - 1)
    return bucket + jnp.where(n < max_exact, n, large)


def _far_bucket(min_dist, max_dist):
    nb = N_BUCKETS // 2
    max_exact = nb // 2
    n = np.arange(min_dist, max_dist + 1, dtype=np.float64)
    large = max_exact + np.floor(np.log(n / max_exact) / math.log(MAX_DISTANCE / max_exact)
                                 * (nb - max_exact) * (1 - 1e-6)).astype(np.int64)
    assert large.min() >= nb - 1, "far key blocks must sit in the saturated distance bucket"
    return nb - 1


def _dsa_kernel(relb_ref, q_ref, qi_ref, kiwi_ref, k_ref, v_ref, kie_ref, kio_ref, bnear_ref, vis_ref,
                o_ref, keys, madd, wb, qs, xcut, mrun, lrun, acc_s,
                *, TQ, KB, RB, kb0, kstep, topk, far_bucket, tie_bits):
    i = pl.program_id(1)
    nkb = kb0 + kstep * i
    n_far = jnp.maximum(nkb - 2, 0)
    n_lc = KB // LANES
    HPG = N_HEADS_A // N_KV_A

    kiwi = kiwi_ref[0]
    for h in range(N_IDX_HEADS):
        w = (kiwi[:, IDX_DIM + h:IDX_DIM + h + 1] * (N_IDX_HEADS ** -0.5)).astype(BF16).astype(F32)
        wb[h] = jnp.broadcast_to(w * (IDX_DIM ** -0.5), (TQ, LANES))

    def idx_block(j, near_col):
        c0 = pl.multiple_of(j * KB, KB)
        ke = kie_ref[0, pl.ds(c0, KB), :]
        ko = kio_ref[0, pl.ds(c0, KB), :]
        acc = [jnp.zeros((TQ, LANES), F32) for _ in range(n_lc)]
        for hp in range(N_IDX_HEADS // 2):
            qh = qi_ref[0, :, hp * LANES:(hp + 1) * LANES]
            se = _dot_nt(qh, ke)
            so = _dot_nt(qh, ko)
            we = wb[2 * hp]
            wo = wb[2 * hp + 1]
            for c in range(n_lc):
                cs = slice(c * LANES, (c + 1) * LANES)
                re = jnp.maximum(se[:, cs], 0.0).astype(BF16).astype(F32)
                ro = jnp.maximum(so[:, cs], 0.0).astype(BF16).astype(F32)
                acc[c] = acc[c] + we * re + wo * ro
        for c in range(n_lc):
            bits = lax.bitcast_convert_type(acc[c], jnp.int32)
            key = bits ^ ((bits >> 31) & jnp.int32(0x7FFFFFFF))
            if near_col is not None:
                vis = vis_ref[:, near_col + c * LANES:near_col + (c + 1) * LANES]
                key = jnp.where(vis > 0.0, key, jnp.int32(INT_MIN))
            keys[j, :, c * LANES:(c + 1) * LANES] = key

    def far_idx(j, carry):
        idx_block(j, None)
        return carry

    lax.fori_loop(0, n_far, far_idx, 0)

    @pl.when(nkb >= 2)
    def _():
        idx_block(nkb - 2, 0)

    idx_block(nkb - 1, KB)

    kf = float(topk)
    lane_f = lax.broadcasted_iota(jnp.int32, (RB, LANES), 1).astype(F32)
    NRB = TQ // RB
    rbs = [slice(rb * RB, (rb + 1) * RB) for rb in range(NRB)]

    def count(pred):
        def body(jj, accs):
            out = []
            for rb in range(NRB):
                blk = keys[jj, rbs[rb], :]
                acc = accs[rb]
                for c in range(n_lc):
                    acc = acc + jnp.where(pred(rb, blk[:, c * LANES:(c + 1) * LANES], jj, c), 1.0, 0.0)
                out.append(acc)
            return tuple(out)
        accs = lax.fori_loop(0, nkb, body, tuple(jnp.zeros((RB, LANES), F32) for _ in range(NRB)))
        return [jnp.sum(acc, axis=1, keepdims=True) for acc in accs]

    def count_ge(cands):
        cb = [jnp.broadcast_to(cand, (RB, LANES)) for cand in cands]
        return count(lambda rb, kblk, jj, c: kblk >= cb[rb])

    def bis_body(p, thrs):
        cands = [thr + lax.shift_left(jnp.int32(1), 31 - p) for thr in thrs]
        cnts = count_ge(cands)
        return tuple(jnp.where(cnt >= kf, cand, thr) for cnt, cand, thr in zip(cnts, cands, thrs))

    thrs = lax.fori_loop(0, 32, bis_body, tuple(jnp.full((RB, 1), INT_MIN, jnp.int32) for _ in range(NRB)))
    thrs = [jnp.maximum(thr, jnp.int32(INT_MIN + 1)) for thr in thrs]
    cnt_ge = count_ge(thrs)
    cnt_gt = count_ge([thr + 1 for thr in thrs])
    needs = [kf - c for c in cnt_gt]
    thrb = [jnp.broadcast_to(thr, (RB, LANES)) for thr in thrs]

    xcut[...] = jnp.full(xcut.shape, 2.0 ** 24, F32)
    excess = cnt_ge[0] - kf
    for c in cnt_ge[1:]:
        excess = jnp.maximum(excess, c - kf)

    @pl.when(jnp.max(excess) > 0.0)
    def _():
        def count_eq_lt(xs):
            xb = [jnp.broadcast_to(x, (RB, LANES)) for x in xs]

            def pred(rb, kblk, jj, c):
                col = lane_f + (jj * KB + c * LANES).astype(F32)
                return jnp.where(kblk == thrb[rb], col, 2.0 ** 25) < xb[rb]
            return count(pred)

        def tie_body(p, xs):
            cands = [x + lax.shift_left(jnp.int32(1), tie_bits - 1 - p).astype(F32) for x in xs]
            cnts = count_eq_lt(cands)
            return tuple(jnp.where(cnt < need, cand, x) for cnt, need, cand, x in zip(cnts, needs, cands, xs))

        xs = lax.fori_loop(0, tie_bits, tie_body, tuple(jnp.zeros((RB, 1), F32) for _ in range(NRB)))
        for rb in range(NRB):
            xcut[rbs[rb], :] = xs[rb]

    xb = [jnp.broadcast_to(xcut[rbs[rb], :], (RB, LANES)) for rb in range(NRB)]

    def mask_body(jj, carry):
        for rb in range(NRB):
            blk = keys[jj, rbs[rb], :]
            for c in range(n_lc):
                kblk = blk[:, c * LANES:(c + 1) * LANES]
                col = lane_f + (jj * KB + c * LANES).astype(F32)
                keep = (kblk > thrb[rb]) | ((kblk == thrb[rb]) & (col <= xb[rb]))
                madd[jj, rbs[rb], c * LANES:(c + 1) * LANES] = jnp.where(keep, 0.0, NEG)
        return carry

    lax.fori_loop(0, nkb, mask_body, 0)

    _attend(relb_ref, q_ref, k_ref, v_ref, bnear_ref, o_ref, madd, qs, mrun, lrun, acc_s,
            TQ=TQ, KB=KB, nkb=nkb, n_far=n_far, far_bucket=far_bucket)


def _attend(relb_ref, q_ref, k_ref, v_ref, bnear_ref, o_ref, madd, qs, mrun, lrun, acc_s,
            *, TQ, KB, nkb, n_far, far_bucket):
    n_lc = KB // LANES
    HPG = N_HEADS_A // N_KV_A
    for g in range(N_KV_A):
        for r in range(HPG):
            h = g * HPG + r
            qs[g, r * TQ:(r + 1) * TQ, :] = q_ref[0, :, h * HEAD_DIM_A:(h + 1) * HEAD_DIM_A]

    def logits(j, near_col, g):
        c0 = pl.multiple_of(j * KB, KB)
        kg = k_ref[0, pl.ds(c0, KB), g * HEAD_DIM_A:(g + 1) * HEAD_DIM_A]
        s = _dot_nt(qs[g], kg)
        mk = madd[j]
        out = []
        for r in range(HPG):
            h = g * HPG + r
            if near_col is None:
                bias = relb_ref[far_bucket, h] * LOG2E
            else:
                bias = bnear_ref[h, :, near_col:near_col + KB]
            out.append(s[r * TQ:(r + 1) * TQ] * (HEAD_DIM_A ** -0.5 * LOG2E) + bias + mk)
        return out

    def over_blocks(fn):
        def far(j, carry):
            fn(j, None)
            return carry

        lax.fori_loop(0, n_far, far, 0)

        @pl.when(nkb >= 2)
        def _():
            fn(nkb - 2, 0)

        fn(nkb - 1, KB)

    mrun[...] = jnp.full(mrun.shape, NEG, F32)

    def max_block(j, near_col):
        for g in range(N_KV_A):
            for r, sh in enumerate(logits(j, near_col, g)):
                rs = slice(r * TQ, (r + 1) * TQ)
                m = mrun[g, rs, :]
                for c in range(n_lc):
                    m = jnp.maximum(m, sh[:, c * LANES:(c + 1) * LANES])
                mrun[g, rs, :] = m

    over_blocks(max_block)

    for g in range(N_KV_A):
        mrun[g] = jnp.broadcast_to(jnp.max(mrun[g], axis=1, keepdims=True), mrun.shape[1:])
    lrun[...] = jnp.zeros(lrun.shape, F32)
    acc_s[...] = jnp.zeros(acc_s.shape, F32)

    def pv_block(j, near_col):
        c0 = pl.multiple_of(j * KB, KB)
        for g in range(N_KV_A):
            vg = v_ref[0, pl.ds(c0, KB), g * HEAD_DIM_A:(g + 1) * HEAD_DIM_A]
            parts = []
            for r, sh in enumerate(logits(j, near_col, g)):
                rs = slice(r * TQ, (r + 1) * TQ)
                mb = mrun[g, rs, :]
                l = lrun[g, rs, :]
                pcs = []
                for c in range(n_lc):
                    p = jnp.exp2(sh[:, c * LANES:(c + 1) * LANES] - mb)
                    l = l + p
                    pcs.append(p.astype(BF16))
                lrun[g, rs, :] = l
                parts.append(jnp.concatenate(pcs, axis=1))
            acc_s[g] += _dot(jnp.concatenate(parts, axis=0), vg)

    over_blocks(pv_block)

    for g in range(N_KV_A):
        out = acc_s[g] / jnp.sum(lrun[g], axis=1, keepdims=True)
        for r in range(HPG):
            h = g * HPG + r
            o_ref[0, :, h * HEAD_DIM_A:(h + 1) * HEAD_DIM_A] = out[r * TQ:(r + 1) * TQ].astype(o_ref.dtype)


DIGIT_BITS = 8
N_DIGITS = 32 // DIGIT_BITS
SUB_BF16 = 16


def _dsa_kernel_t(q_ref, qi_ref, kiwi_ref, k_ref, vt_ref, kie_ref, kio_ref, bnear_ref, vist_ref,
                  o_ref, keys, dig, madd, qs, xcut, acc_s, tbuf,
                  *, TQ, KB, kb0, kstep, topk, tie_bits):
    i = pl.program_id(1)
    nkb = kb0 + kstep * i
    n_far = jnp.maximum(nkb - 2, 0)
    n_slab = KB // SUB_BF16

    w_t = kiwi_ref[0].T
    wrow = [(w_t[IDX_DIM + h:IDX_DIM + h + 1, :] * (N_IDX_HEADS ** -0.5)).astype(BF16).astype(F32)
            * (IDX_DIM ** -0.5) for h in range(N_IDX_HEADS)]

    def idx_block(j, near_row):
        c0 = pl.multiple_of(j * KB, KB)
        ke = kie_ref[0, pl.ds(c0, KB), :]
        ko = kio_ref[0, pl.ds(c0, KB), :]
        acc = jnp.zeros((KB, TQ), F32)
        raw = []
        for hp in range(N_IDX_HEADS // 2):
            qh = qi_ref[0, :, hp * LANES:(hp + 1) * LANES]
            raw.append((_dot_nt(ke, qh), _dot_nt(ko, qh)))
        for hp, (se, so) in enumerate(raw):
            re = jnp.maximum(se.astype(BF16), 0.0).astype(F32)
            ro = jnp.maximum(so.astype(BF16), 0.0).astype(F32)
            acc = acc + wrow[2 * hp] * re + wrow[2 * hp + 1] * ro
        bits = lax.bitcast_convert_type(acc, jnp.int32)
        key = bits ^ ((bits >> 31) & jnp.int32(0x7FFFFFFF))
        if near_row is not None:
            key = jnp.where(vist_ref[near_row:near_row + KB, :] > 0.0, key, jnp.int32(INT_MIN))
        keys[j] = key
        for dk in range(N_DIGITS):
            sh = 32 - DIGIT_BITS * (dk + 1)
            d = (key >> sh) if dk == 0 else ((key >> sh) & jnp.int32(2 ** DIGIT_BITS - 1))
            dig[dk, j] = d.astype(F32).astype(BF16)

    def far_idx(j, carry):
        idx_block(j, None)
        return carry

    lax.fori_loop(0, n_far, far_idx, 0)

    @pl.when(nkb >= 2)
    def _():
        idx_block(nkb - 2, 0)

    idx_block(nkb - 1, KB)

    one = jnp.ones((SUB_BF16, TQ), BF16)
    zero = jnp.zeros((SUB_BF16, TQ), BF16)

    def count_ge(dk, cand):
        cb = jnp.broadcast_to(cand.astype(BF16), (SUB_BF16, TQ))

        def body(jj, tot):
            blk = dig[dk, jj]
            parts = [jnp.where(blk[s * SUB_BF16:(s + 1) * SUB_BF16, :] >= cb, one, zero) for s in range(n_slab)]
            while len(parts) > 1:
                parts = [parts[n] + parts[n + 1] for n in range(0, len(parts), 2)]
            return tot + parts[0].astype(F32)

        tot = lax.fori_loop(0, nkb, body, jnp.zeros((SUB_BF16, TQ), F32))
        return jnp.sum(tot, axis=0, keepdims=True)

    kneed = jnp.full((1, TQ), float(topk), F32)
    thr = jnp.zeros((1, TQ), jnp.int32)
    cnt_eq = None
    for dk in range(N_DIGITS):
        lo = -float(2 ** (DIGIT_BITS - 1)) if dk == 0 else 0.0

        def bis_body(p, carry, dk=dk, kneed=kneed):
            t, c_rej = carry
            cand = t + lax.shift_left(jnp.int32(1), DIGIT_BITS - 1 - p).astype(F32)
            cnt = count_ge(dk, cand)
            ok = cnt >= kneed
            return jnp.where(ok, cand, t), jnp.where(ok, c_rej, cnt)

        t, n_gt = lax.fori_loop(0, DIGIT_BITS, bis_body,
                                (jnp.full((1, TQ), lo, F32), jnp.zeros((1, TQ), F32)))
        if dk == N_DIGITS - 1:
            cnt_eq = count_ge(dk, t) - n_gt
        kneed = kneed - n_gt
        thr = thr + lax.shift_left(t.astype(jnp.int32), 32 - DIGIT_BITS * (dk + 1))
        if dk + 1 < N_DIGITS:
            tb = t.astype(BF16)

            def prep(jj, carry, dk=dk, tb=tb):
                dig[dk + 1, jj] = jnp.where(dig[dk, jj] == tb, dig[dk + 1, jj], jnp.asarray(-1.0, BF16))
                return carry

            lax.fori_loop(0, nkb, prep, 0)

    thr = jnp.maximum(thr, jnp.int32(INT_MIN + 1))
    need = kneed
    sub_f = lax.broadcasted_iota(jnp.int32, (KB, TQ), 0).astype(F32)

    xcut[...] = jnp.full(xcut.shape, 2.0 ** 24, F32)

    @pl.when(jnp.max(cnt_eq - need) > 0.0)
    def _():
        def count_eq_lt(x):
            def body(jj, tot):
                col = sub_f + (jj * KB).astype(F32)
                hit = jnp.where(jnp.where(keys[jj] == thr, col, 2.0 ** 25) < x, 1.0, 0.0)
                for s in range(KB // 8):
                    tot = tot + hit[s * 8:(s + 1) * 8, :]
                return tot
            tot = lax.fori_loop(0, nkb, body, jnp.zeros((8, TQ), F32))
            return jnp.sum(tot, axis=0, keepdims=True)

        def tie_body(p, x):
            cand = x + lax.shift_left(jnp.int32(1), tie_bits - 1 - p).astype(F32)
            return jnp.where(count_eq_lt(cand) < need, cand, x)

        x = lax.fori_loop(0, tie_bits, tie_body, jnp.zeros((1, TQ), F32))
        xcut[...] = jnp.broadcast_to(x, xcut.shape)

    xc = xcut[0:1, :]

    def mask_body(jj, carry):
        kblk = keys[jj]
        col = sub_f + (jj * KB).astype(F32)
        at_thr = jnp.where(col <= xc, 0.0, NEG)
        madd[jj] = jnp.where(kblk > thr, 0.0, jnp.where(kblk == thr, at_thr, NEG))
        return carry

    lax.fori_loop(0, nkb, mask_body, 0)

    HPG = N_HEADS_A // N_KV_A
    c2 = HEAD_DIM_A ** -0.5 * LOG2E
    for g in range(N_KV_A):
        for r in range(HPG):
            h = g * HPG + r
            qs[g, r * TQ:(r + 1) * TQ, :] = q_ref[0, :, h * HEAD_DIM_A:(h + 1) * HEAD_DIM_A]
    acc_s[...] = jnp.zeros(acc_s.shape, F32)

    def attn_block(j, near_row, carry):
        ms, ls = carry
        c0 = pl.multiple_of(j * KB, KB)
        mk = madd[j]
        ms_out, ls_out = [], []
        s_ts = [_dot_nt(k_ref[0, pl.ds(c0, KB), g * HEAD_DIM_A:(g + 1) * HEAD_DIM_A], qs[g])
                for g in range(N_KV_A)]
        for g in range(N_KV_A):
            vt = vt_ref[0, j, g * HEAD_DIM_A:(g + 1) * HEAD_DIM_A, :]
            s_t = s_ts[g]
            m_news = []
            for r in range(HPG):
                h = g * HPG + r
                cs = slice(r * TQ, (r + 1) * TQ)
                t = s_t[:, cs] + mk
                if near_row is not None:
                    t = t + bnear_ref[h, near_row:near_row + KB, :]
                tbuf[h] = t
                m_news.append(jnp.maximum(ms[h], jnp.max(t, axis=0, keepdims=True)))
            for r in range(HPG):
                h = g * HPG + r
                cs = slice(r * TQ, (r + 1) * TQ)
                m_new = m_news[r]
                alpha = jnp.exp2((ms[h] - m_new) * c2)
                p = jnp.exp2((tbuf[h] - m_new) * c2)
                ls_out.append(alpha * ls[h] + jnp.sum(p, axis=0, keepdims=True))
                ms_out.append(m_new)
                acc_s[g, :, cs] = alpha * acc_s[g, :, cs] + _dot(vt, p.astype(BF16))
        return tuple(ms_out), tuple(ls_out)

    carry = (tuple(jnp.full((1, TQ), NEG, F32) for _ in range(N_HEADS_A)),
             tuple(jnp.zeros((1, TQ), F32) for _ in range(N_HEADS_A)))
    carry = lax.fori_loop(0, n_far, lambda j, c: attn_block(j, None, c), carry)
    carry = lax.cond(nkb >= 2, lambda c: attn_block(nkb - 2, 0, c), lambda c: c, carry)
    ms, ls = attn_block(nkb - 1, KB, carry)

    for g in range(N_KV_A):
        for r in range(HPG):
            h = g * HPG + r
            out_t = acc_s[g, :, r * TQ:(r + 1) * TQ] / ls[h]
            o_ref[0, :, h * HEAD_DIM_A:(h + 1) * HEAD_DIM_A] = out_t.T.astype(o_ref.dtype)


def _dsa(q, qi, kiwi, kb, vb, kie, kio, rel_bias, *, n_keys, pos0, tq, kblk):
    B, T, _ = q.shape
    Lp = kb.shape[1]
    assert T % tq == 0 and Lp % kblk == 0 and tq % CHUNK == 0 or T == tq
    n_qt = T // tq
    if n_qt == 1:
        kb0, kstep = Lp // kblk, 0
        assert ((pos0 + tq - 1) // CHUNK + 1) * CHUNK >= n_keys
    else:
        assert pos0 == 0 and tq == kblk and n_keys == Lp == T
        kb0, kstep = 1, 1
    topk = min(TOPK_MAX, n_keys // 4)
    rb = min(128, tq)

    t = jnp.arange(tq, dtype=jnp.int32)[:, None]
    c = jnp.arange(2 * kblk, dtype=jnp.int32)[None, :]
    if n_qt == 1:
        q_abs = pos0 + t
        s_abs = (kb0 - 2) * kblk + c
    else:
        q_abs = kblk + t
        s_abs = c
    vis = ((s_abs // CHUNK) <= (q_abs // CHUNK)) & (s_abs < (n_keys if n_qt == 1 else 2 * kblk))
    far_bucket = _far_bucket(kblk + 1, max(Lp, kblk + 2))

    n_kb = Lp // kblk
    hpg = N_HEADS_A // N_KV_A
    common = dict(TQ=tq, KB=kblk, kb0=kb0, kstep=kstep, topk=topk, far_bucket=far_bucket,
                  tie_bits=int(math.ceil(math.log2(Lp))) + 1)
    attend_scratch = [pltpu.VMEM((N_KV_A, hpg * tq, HEAD_DIM_A), BF16),
                      pltpu.VMEM((N_KV_A, hpg * tq, LANES), F32),
                      pltpu.VMEM((N_KV_A, hpg * tq, LANES), F32),
                      pltpu.VMEM((N_KV_A, hpg * tq, HEAD_DIM_A), F32)]
    if tq % LANES == 0:
        tab_t = _bias_table(rel_bias, _t5_bucket(s_abs - q_abs).T, shift_bucket=far_bucket)
        vt = jnp.swapaxes(vb.reshape(B, n_kb, kblk, D_KV), 2, 3)
        common.pop("far_bucket")
        return pl.pallas_call(
            functools.partial(_dsa_kernel_t, **common),
            grid=(B, n_qt),
            in_specs=[pl.BlockSpec((1, tq, D_ATTN), lambda b, i: (b, i, 0)),
                      pl.BlockSpec((1, tq, N_IDX_HEADS * IDX_DIM), lambda b, i: (b, i, 0)),
                      pl.BlockSpec((1, tq, LANES), lambda b, i: (b, i, 0)),
                      pl.BlockSpec((1, Lp, D_KV), lambda b, i: (b, 0, 0)),
                      pl.BlockSpec((1, n_kb, D_KV, kblk), lambda b, i: (b, 0, 0, 0)),
                      pl.BlockSpec((1, Lp, LANES), lambda b, i: (b, 0, 0)),
                      pl.BlockSpec((1, Lp, LANES), lambda b, i: (b, 0, 0)),
                      _const_spec((N_HEADS_A, 2 * kblk, tq)),
                      _const_spec((2 * kblk, tq))],
            out_specs=pl.BlockSpec((1, tq, D_ATTN), lambda b, i: (b, i, 0)),
            out_shape=jax.ShapeDtypeStruct((B, T, D_ATTN), BF16),
            scratch_shapes=[pltpu.VMEM((n_kb, kblk, tq), jnp.int32),
                            pltpu.VMEM((N_DIGITS, n_kb, kblk, tq), BF16),
                            pltpu.VMEM((n_kb, kblk, tq), F32),
                            pltpu.VMEM((N_KV_A, hpg * tq, HEAD_DIM_A), BF16),
                            pltpu.VMEM((8, tq), F32),
                            pltpu.VMEM((N_KV_A, HEAD_DIM_A, hpg * tq), F32),
                            pltpu.VMEM((N_HEADS_A, kblk, tq), F32)],
            compiler_params=pltpu.CompilerParams(dimension_semantics=("arbitrary", "arbitrary"),
                                                 vmem_limit_bytes=VMEM_LIMIT),
            name="dsa_t",
        )(q, qi, kiwi, kb, vt, kie, kio, tab_t, vis.astype(F32).T)

    bias_near = _bias_table(rel_bias, _t5_bucket(s_abs - q_abs))
    kern = functools.partial(_dsa_kernel, RB=rb, **common)
    return pl.pallas_call(
        kern,
        grid=(B, n_qt),
        in_specs=[pl.BlockSpec(memory_space=pltpu.SMEM),
                  pl.BlockSpec((1, tq, D_ATTN), lambda b, i: (b, i, 0)),
                  pl.BlockSpec((1, tq, N_IDX_HEADS * IDX_DIM), lambda b, i: (b, i, 0)),
                  pl.BlockSpec((1, tq, LANES), lambda b, i: (b, i, 0)),
                  pl.BlockSpec((1, Lp, D_KV), lambda b, i: (b, 0, 0)),
                  pl.BlockSpec((1, Lp, D_KV), lambda b, i: (b, 0, 0)),
                  pl.BlockSpec((1, Lp, LANES), lambda b, i: (b, 0, 0)),
                  pl.BlockSpec((1, Lp, LANES), lambda b, i: (b, 0, 0)),
                  _const_spec((N_HEADS_A, tq, 2 * kblk)),
                  _const_spec((tq, 2 * kblk))],
        out_specs=pl.BlockSpec((1, tq, D_ATTN), lambda b, i: (b, i, 0)),
        out_shape=jax.ShapeDtypeStruct((B, T, D_ATTN), BF16),
        scratch_shapes=[pltpu.VMEM((n_kb, tq, kblk), jnp.int32),
                        pltpu.VMEM((n_kb, tq, kblk), F32),
                        pltpu.VMEM((N_IDX_HEADS, tq, LANES), F32),
                        pltpu.VMEM((N_KV_A, hpg * tq, HEAD_DIM_A), BF16),
                        pltpu.VMEM((tq, 1), F32),
                        pltpu.VMEM((N_KV_A, hpg * tq, LANES), F32),
                        pltpu.VMEM((N_KV_A, hpg * tq, LANES), F32),
                        pltpu.VMEM((N_KV_A, hpg * tq, HEAD_DIM_A), F32)],
        compiler_params=pltpu.CompilerParams(dimension_semantics=("arbitrary", "arbitrary"),
                                             vmem_limit_bytes=VMEM_LIMIT),
        name="dsa",
    )(rel_bias, q, qi, kiwi, kb, vb, kie, kio, bias_near, vis.astype(F32))


def _mix_kernel(x_ref, a_ref, u_ref, hist_ref, mk_ref, mv_ref, wpool_ref, pscale_ref, wout_ref, gx_ref,
                wxq_ref, wxo_ref, gffn_ref, wr_ref, br_ref, o_ref, ext, *, tm, pos0):
    t = pl.program_id(1)
    H = POOL_HIST + 1

    @pl.when(t == 0)
    def _():
        ext[0:H, :] = hist_ref[0]

    @pl.when(t > 0)
    def _():
        ext[0:H, :] = ext[tm:tm + H, :]

    ext[H:H + tm, :] = u_ref[0]

    da = _dot(a_ref[0], wout_ref[0:D_ATTN, :])

    pos = pos0 + t * tm + lax.broadcasted_iota(jnp.int32, (tm, 1), 0)
    pouts = []
    for gi, w in enumerate(POOL_WINDOWS):
        c0, c1 = gi * POOL_GROUP_DIM, (gi + 1) * POOL_GROUP_DIM
        wsum = ext[H:H + tm, c0:c1]
        for s in range(1, w):
            wsum = wsum + ext[H - s:H - s + tm, c0:c1]
        cnt = jnp.minimum(pos + 1, w).astype(F32)
        d = wsum / cnt - u_ref[0, :, c0:c1]
        y = _dot(d.astype(BF16), wpool_ref[gi]) * pscale_ref[:, c0:c1]
        pouts.append(y.astype(BF16))
    pcat = jnp.concatenate(pouts, axis=1)

    D = x_ref.shape[2]
    x1 = x_ref[0] + da + _dot(pcat, wout_ref[D_ATTN:D_ATTN + D_POOL, :])
    h = _rms(x1, gx_ref[...]).astype(BF16)
    qx = _dot(h, wxq_ref[...])
    heads = [slice(hh * XHEAD_DIM, (hh + 1) * XHEAD_DIM) for hh in range(N_XHEADS)]
    all_logits = [_dot_nt(qx[:, hs].astype(BF16), mk_ref[0, :, hs]) for hs in heads]
    outs = []
    for hs, logits in zip(heads, all_logits):
        logits = logits * (XHEAD_DIM ** -0.5)
        m = jnp.max(logits, axis=1, keepdims=True)
        p = jnp.exp(logits - m)
        l = jnp.sum(p, axis=1, keepdims=True)
        outs.append((_dot(p.astype(BF16), mv_ref[0, :, hs]) / l).astype(BF16))
    o = jnp.concatenate(outs, axis=1)
    x2 = x1 + _dot(o, wxo_ref[...])
    o_ref[0, :, 0:D] = x2
    hf = _rms(x2, gffn_ref[...]).astype(BF16)
    o_ref[0, :, D:D + LANES] = _route(_dot(hf, wr_ref[...]) + br_ref[...])


def _mix(x, a, u, hist, mk, mv, w_pool, pool_scale, w_out, g_x, w_xq, w_xo, g_ffn, w_r, b_r, *, pos0, tm):
    B, T, D = x.shape
    n_mem = mk.shape[1]
    H = POOL_HIST + 1
    assert T % tm == 0 and tm >= H
    hist16 = jnp.concatenate([jnp.zeros((B, 1, D_POOL), F32), hist], axis=1)
    return pl.pallas_call(
        functools.partial(_mix_kernel, tm=tm, pos0=pos0),
        grid=(B, T // tm),
        in_specs=[pl.BlockSpec((1, tm, D), lambda b, t: (b, t, 0)),
                  pl.BlockSpec((1, tm, D_ATTN), lambda b, t: (b, t, 0)),
                  pl.BlockSpec((1, tm, D_POOL), lambda b, t: (b, t, 0)),
                  pl.BlockSpec((1, H, D_POOL), lambda b, t: (b, 0, 0)),
                  pl.BlockSpec((1, n_mem, D_X), lambda b, t: (b, 0, 0)),
                  pl.BlockSpec((1, n_mem, D_X), lambda b, t: (b, 0, 0)),
                  _const_spec(w_pool.shape),
                  _const_spec((1, D_POOL)),
                  _const_spec(w_out.shape),
                  _const_spec((1, D)),
                  _const_spec(w_xq.shape),
                  _const_spec(w_xo.shape),
                  _const_spec((1, D)),
                  _const_spec(w_r.shape),
                  _const_spec((1, LANES))],
        out_specs=pl.BlockSpec((1, tm, D + LANES), lambda b, t: (b, t, 0)),
        out_shape=jax.ShapeDtypeStruct((B, T, D + LANES), F32),
        scratch_shapes=[pltpu.VMEM((H + tm, D_POOL), F32)],
        compiler_params=pltpu.CompilerParams(dimension_semantics=("arbitrary", "arbitrary"),
                                             vmem_limit_bytes=VMEM_LIMIT),
        name="mix",
    )(x, a, u, hist16, mk, mv, w_pool, pool_scale.reshape(1, D_POOL), w_out, g_x.reshape(1, D), w_xq, w_xo,
      g_ffn.reshape(1, D), w_r, b_r)


R_OFF = N_GROUPS


def _route(logits):
    tm = logits.shape[0]
    lane = lax.broadcasted_iota(jnp.int32, (tm, LANES), 1).astype(F32)
    ninf = -jnp.inf
    big = float(LANES)
    gl = jnp.where(lane < N_GROUPS, logits, ninf)
    gmax = jnp.max(gl, axis=1, keepdims=True)
    g_sel = jnp.min(jnp.where(gl == gmax, lane, big), axis=1, keepdims=True)
    g_prob = 1.0 / jnp.sum(jnp.exp(gl - gmax), axis=1, keepdims=True)
    lo = R_OFF + g_sel * EXPERTS_PER_GROUP
    el = jnp.where((lane >= lo) & (lane < lo + EXPERTS_PER_GROUP), logits, ninf)
    tv0 = jnp.max(el, axis=1, keepdims=True)
    ti0 = jnp.min(jnp.where(el == tv0, lane, big), axis=1, keepdims=True)
    el2 = jnp.where(lane == ti0, ninf, el)
    tv1 = jnp.max(el2, axis=1, keepdims=True)
    ti1 = jnp.min(jnp.where(el2 == tv1, lane, big), axis=1, keepdims=True)
    e1 = jnp.exp(tv1 - tv0)
    den = 1.0 + e1
    w0 = g_prob / den
    w1 = g_prob * e1 / den
    ids = jnp.where(lane == 0.0, jnp.minimum(ti0, ti1) - R_OFF,
                    jnp.where(lane == 1.0, jnp.maximum(ti0, ti1) - R_OFF, 0.0))
    return jnp.where(lane == ti0, w0, 0.0) + jnp.where(lane == ti1, w1, 0.0) + ids


def _expert(h, rec, e, wg, wu, wd):
    a = _dot(h, wg)
    b = _dot(h, wu)
    lane = lax.broadcasted_iota(jnp.int32, rec.shape, 1)
    cw = jnp.sum(jnp.where(lane == e + R_OFF, rec, 0.0), axis=1, keepdims=True)
    hid = a * jax.nn.sigmoid(a) * b * cw
    return _dot(hid.astype(BF16), wd)


def _moe_kernel(x_ref, gffn_ref, wg_ref, wu_ref, wd_ref, gfin_ref, o_ref, h_s, acc_s):
    e = pl.program_id(1)
    d = o_ref.shape[1]

    @pl.when(e == 0)
    def _():
        h_s[...] = _rms(x_ref[:, 0:d], gffn_ref[...]).astype(BF16)
        acc_s[...] = jnp.zeros(acc_s.shape, F32)

    acc_s[...] += _expert(h_s[...], x_ref[:, d:d + LANES], e, wg_ref[0], wu_ref[0], wd_ref[0])

    @pl.when(e == pl.num_programs(1) - 1)
    def _():
        o_ref[...] = _rms(x_ref[:, 0:d] + acc_s[...], gfin_ref[...])


def _moe(x, g_ffn, w_gate, w_up, w_down, g_final, *, tm):
    n, de = x.shape
    d = de - LANES
    assert n % tm == 0
    return pl.pallas_call(
        _moe_kernel,
        grid=(n // tm, N_EXPERTS),
        in_specs=[pl.BlockSpec((tm, de), lambda i, e: (i, 0)),
                  _const_spec((1, d)),
                  pl.BlockSpec((1, d, D_EXPERT), lambda i, e: (e, 0, 0)),
                  pl.BlockSpec((1, d, D_EXPERT), lambda i, e: (e, 0, 0)),
                  pl.BlockSpec((1, D_EXPERT, d), lambda i, e: (e, 0, 0)),
                  _const_spec((1, d))],
        out_specs=pl.BlockSpec((tm, d), lambda i, e: (i, 0)),
        out_shape=jax.ShapeDtypeStruct((n, d), F32),
        scratch_shapes=[pltpu.VMEM((tm, d), BF16),
                        pltpu.VMEM((tm, d), F32)],
        compiler_params=pltpu.CompilerParams(dimension_semantics=("arbitrary", "arbitrary"),
                                             vmem_limit_bytes=VMEM_LIMIT),
        name="moe",
    )(x, g_ffn.reshape(1, d), w_gate, w_up, w_down, g_final.reshape(1, d))


def _moe_sparse_kernel(ea_ref, eb_ref, nv_ref, idx_ref, idxn_ref, x_hbm, gffn_ref, gfin_ref,
                       wga_ref, wua_ref, wda_ref, wgb_ref, wub_ref, wdb_ref, o_hbm,
                       xbuf, ybuf, gsem, ssem, *, tm):
    t = pl.program_id(0)
    nt = pl.num_programs(0)
    d = o_hbm.shape[1]
    slot = lax.rem(t, 2)
    nslot = 1 - slot
    nv = nv_ref[t]
    WAIT_ROWS = 8

    def gather_row(ids, s, r):
        return pltpu.make_async_copy(x_hbm.at[pl.ds(ids[0, 0, r], 1), :], xbuf.at[s, pl.ds(r, 1), :], gsem.at[s])

    def gather_loop(ids, s):
        def body(r, carry):
            gather_row(ids, s, r).start()
            return carry
        lax.fori_loop(0, tm, body, 0)

    def gather_wait(s):
        pltpu.make_async_copy(x_hbm.at[pl.ds(0, tm), :], xbuf.at[s], gsem.at[s]).wait()

    def scatter_rows(s, r, row, n):
        return pltpu.make_async_copy(ybuf.at[s, pl.ds(r, n), :], o_hbm.at[pl.ds(row, n), :], ssem.at[s])

    def scatter_wait(s, n):
        def body_many(r, carry):
            scatter_rows(s, 0, 0, WAIT_ROWS).wait()
            return carry
        lax.fori_loop(0, n // WAIT_ROWS, body_many, 0)

        def body_one(r, carry):
            scatter_rows(s, 0, 0, 1).wait()
            return carry
        lax.fori_loop(0, lax.rem(n, WAIT_ROWS), body_one, 0)

    @pl.when(t == 0)
    def _():
        gather_loop(idx_ref, 0)

    @pl.when(t >= 2)
    def _():
        scatter_wait(slot, nv_ref[jnp.maximum(t - 2, 0)])

    gather_wait(slot)

    @pl.when(nv > 0)
    def _():
        x = xbuf[slot]
        xr = x[:, 0:d]
        rec = x[:, d:d + LANES]
        h = _rms(xr, gffn_ref[...]).astype(BF16)
        for r in range(tm):
            gather_row(idxn_ref, nslot, r).start()
        lane = lax.broadcasted_iota(jnp.int32, rec.shape, 1)
        pre = [(_dot(h, wg[0]), _dot(h, wu[0])) for wg, wu in ((wga_ref, wua_ref), (wgb_ref, wub_ref))]
        acc = None
        for (a, b), e, wd in zip(pre, (ea_ref[t], eb_ref[t]), (wda_ref, wdb_ref)):
            cw = jnp.sum(jnp.where(lane == e + R_OFF, rec, 0.0), axis=1, keepdims=True)
            hid = a * jax.nn.sigmoid(a) * b * cw
            out = _dot(hid.astype(BF16), wd[0])
            acc = out if acc is None else acc + out
        ybuf[slot] = _rms(xr + acc, gfin_ref[...])

        def body_many(i, carry):
            for k in range(WAIT_ROWS):
                r = i * WAIT_ROWS + k
                scatter_rows(slot, r, idx_ref[0, 0, r], 1).start()
            return carry
        lax.fori_loop(0, nv // WAIT_ROWS, body_many, 0)

        def body_one(r, carry):
            scatter_rows(slot, r, idx_ref[0, 0, r], 1).start()
            return carry
        lax.fori_loop((nv // WAIT_ROWS) * WAIT_ROWS, nv, body_one, 0)

    @pl.when(nv <= 0)
    def _():
        gather_loop(idxn_ref, nslot)

    @pl.when(t == nt - 1)
    def _():
        gather_wait(nslot)

        @pl.when(t >= 1)
        def _():
            scatter_wait(nslot, nv_ref[jnp.maximum(t - 1, 0)])
        scatter_wait(slot, nv)


def _moe_tile_rows(n):
    n_pairs = N_GROUPS * (EXPERTS_PER_GROUP * (EXPERTS_PER_GROUP - 1) // 2)
    mean = n / n_pairs
    rows = (mean + 3.0 * math.sqrt(mean)) / 2
    return int(min(256, max(SUB_BF16, -(-rows // SUB_BF16) * SUB_BF16)))


def _moe_sparse(x, g_ffn, w_gate, w_up, w_down, g_final, *, tm):
    n, de = x.shape
    d = de - LANES
    n_pairs = N_GROUPS * (EXPERTS_PER_GROUP * (EXPERTS_PER_GROUP - 1) // 2)
    nt = -(-n // tm) + n_pairs
    n_keys = N_EXPERTS * N_EXPERTS

    ids = x[:, d:d + 2].astype(jnp.int32)
    key = ids[:, 0] * N_EXPERTS + ids[:, 1]
    order = jnp.argsort(key).astype(jnp.int32)
    skey = key[order]
    count_below = lambda sorted_vals, q: jnp.sum((sorted_vals[None, :] < q[:, None]).astype(jnp.int32), axis=1)
    edges = count_below(skey, jnp.arange(n_keys + 1, dtype=jnp.int32))
    starts, ends = edges[:-1], edges[1:]
    tiles = (ends - starts + tm - 1) // tm
    cum = jnp.cumsum(tiles)
    total = cum[-1]
    tt = jnp.arange(nt, dtype=jnp.int32)
    tc = jnp.minimum(tt, total - 1)
    cls = count_below(cum, tc + 1)
    first = cum[cls] - tiles[cls]
    tstart = starts[cls] + (tc - first) * tm
    nv = jnp.where(tt < total, jnp.clip(ends[cls] - tstart, 0, tm), 0).astype(jnp.int32)
    ea = (cls // N_EXPERTS).astype(jnp.int32)
    eb = (cls % N_EXPERTS).astype(jnp.int32)
    rows = jnp.clip(tstart[:, None] + jnp.arange(tm, dtype=jnp.int32)[None, :], 0, n - 1)
    idx = order[rows].reshape(nt, 1, tm)

    wspec_a = lambda shape: pl.BlockSpec(shape, lambda t, ea, eb, nv: (ea[t], 0, 0))
    wspec_b = lambda shape: pl.BlockSpec(shape, lambda t, ea, eb, nv: (eb[t], 0, 0))
    cspec = lambda shape: pl.BlockSpec(shape, lambda t, ea, eb, nv: (0,) * len(shape),
                                       pipeline_mode=pl.Buffered(1))
    grid_spec = pltpu.PrefetchScalarGridSpec(
        num_scalar_prefetch=3,
        grid=(nt,),
        in_specs=[pl.BlockSpec((1, 1, tm), lambda t, ea, eb, nv: (t, 0, 0), memory_space=pltpu.SMEM),
                  pl.BlockSpec((1, 1, tm), lambda t, ea, eb, nv: (jnp.minimum(t + 1, nt - 1), 0, 0),
                               memory_space=pltpu.SMEM),
                  pl.BlockSpec(memory_space=pl.ANY),
                  cspec((1, d)), cspec((1, d)),
                  wspec_a((1, d, D_EXPERT)), wspec_a((1, d, D_EXPERT)), wspec_a((1, D_EXPERT, d)),
                  wspec_b((1, d, D_EXPERT)), wspec_b((1, d, D_EXPERT)), wspec_b((1, D_EXPERT, d))],
        out_specs=pl.BlockSpec(memory_space=pl.ANY),
        scratch_shapes=[pltpu.VMEM((2, tm, de), F32),
                        pltpu.VMEM((2, tm, d), F32),
                        pltpu.SemaphoreType.DMA((2,)),
                        pltpu.SemaphoreType.DMA((2,))])
    return pl.pallas_call(
        functools.partial(_moe_sparse_kernel, tm=tm),
        grid_spec=grid_spec,
        out_shape=jax.ShapeDtypeStruct((n, d), F32),
        compiler_params=pltpu.CompilerParams(dimension_semantics=("arbitrary",),
                                             vmem_limit_bytes=VMEM_LIMIT),
        name="moe_sparse",
    )(ea, eb, nv, idx, idx, x, g_ffn.reshape(1, d), g_final.reshape(1, d),
      w_gate, w_up, w_down, w_gate, w_up, w_down)


C_Q = 0
C_QI = C_Q + D_ATTN
C_K = C_QI + N_IDX_HEADS * IDX_DIM
C_V = C_K + D_KV
C_KIWI = C_V + D_KV
C_KIE = C_KIWI + LANES
C_KIO = C_KIE + LANES
C_U = C_KIO + LANES
C_END = C_U + D_POOL


def _layout_w_in(w_in):
    d = w_in.shape[0]
    offs = np.cumsum((0, D_ATTN, D_KV, D_KV, N_IDX_HEADS * IDX_DIM, IDX_DIM, N_IDX_HEADS, D_POOL))
    wq, wk, wv, wqi, wki, wwi, wu = (w_in[:, offs[n]:offs[n + 1]] for n in range(7))
    z = lambda n: jnp.zeros((d, n), w_in.dtype)
    cat = jnp.concatenate([wq, wqi, wk, wv,
                           wki, wwi, z(LANES - IDX_DIM - N_IDX_HEADS),
                           wki, z(LANES - IDX_DIM),
                           z(LANES - IDX_DIM), wki,
                           wu], axis=1)
    assert cat.shape[1] == C_END
    return cat.astype(BF16)


IN_GROUPS = ((C_Q, C_QI, 1.0, 1),
             (C_QI, C_K, 1.0, 1),
             (C_K, C_V, 1.0, 2),
             (C_V, C_KIWI, 1.0, 2),
             (C_KIWI, C_KIE, 1.0, 1),
             (C_KIE, C_KIO, 1.0, 1),
             (C_KIO, C_U, 1.0, 1),
             (C_U, C_END, 1.0, 1))
IN_DTYPES = (BF16, BF16, F32, BF16, F32, BF16, F32, BF16, BF16, F32)


def _pad_rows(a, n):
    return jnp.pad(a, ((0, 0), (0, n - a.shape[1]), (0, 0)))


def _layer(x, pos0, caches, pool_hist, mk, mv, rel_bias, W, *, tm_in, tq, tm_mix, tm_moe, sparse_moe):
    B, T, D = x.shape
    n = B * T
    q, qi, k, kb, v, vb, kiwi, kie, kio, u = _norm_matmul(
        x.reshape(n, D), W["g_mix"], W["w_in"], IN_GROUPS, IN_DTYPES, tm_in)
    r3 = lambda a: a.reshape(B, T, a.shape[-1])
    kblk = 256
    if caches is None:
        n_keys = T
        kb3, vb3, kie3, kio3 = r3(kb), r3(vb), r3(kie), r3(kio)
    else:
        k_hist, v_hist, ki_hist = caches
        past = k_hist.shape[1]
        n_keys = past + T
        lp = -(-n_keys // kblk) * kblk
        ki = r3(kiwi)[:, :, :IDX_DIM]
        ki_all = jnp.concatenate([ki_hist, ki], axis=1)
        zeros = jnp.zeros_like(ki_all)
        kb3 = _pad_rows(jnp.concatenate([k_hist.reshape(B, past, D_KV), r3(k)], axis=1), lp).astype(BF16)
        vb3 = _pad_rows(jnp.concatenate([v_hist.reshape(B, past, D_KV), r3(v)], axis=1), lp).astype(BF16)
        kie3 = _pad_rows(jnp.concatenate([ki_all, zeros], axis=2), lp).astype(BF16)
        kio3 = _pad_rows(jnp.concatenate([zeros, ki_all], axis=2), lp).astype(BF16)
    a_out = _dsa(r3(q), r3(qi), r3(kiwi), kb3, vb3, kie3, kio3, rel_bias,
                 n_keys=n_keys, pos0=pos0, tq=tq, kblk=kblk)
    x2 = _mix(x, a_out, r3(u), pool_hist, mk.astype(BF16), mv.astype(BF16), W["w_pool"], W["pool_scale"],
              W["w_out"], W["g_x"], W["w_xq"], W["w_xo"], W["g_ffn"], W["w_r"], W["b_r"], pos0=pos0, tm=tm_mix)
    moe = _moe_sparse if sparse_moe else _moe
    y = moe(x2.reshape(n, D + LANES), W["g_ffn"], W["w_gate"], W["w_up"], W["w_down"], W["g_final"], tm=tm_moe)
    new_pool = jnp.concatenate([pool_hist, r3(u)], axis=1)[:, -POOL_HIST:]
    return (y.reshape(B, T, D), k.reshape(B, T, N_KV_A, HEAD_DIM_A), v.reshape(B, T, N_KV_A, HEAD_DIM_A),
            r3(kiwi)[:, :, :IDX_DIM], new_pool)


def kernel(x_prompt, x_sample, mem_prompt, cache_k, cache_v, cache_k_idx, cache_pool, cache_mem_k, cache_mem_v, rel_bias, g_mix, w_in, w_pool, pool_scale, w_out, g_mem, w_mk, w_mv, g_x, w_xq, w_xo, g_ffn, w_rg, b_rg, w_re, b_re, w_gate, w_up, w_down, g_final):
    depth = g_mix.shape[0]
    assert depth == 1
    l = 0
    B, T, D = x_prompt.shape
    Bs, Ts, _ = x_sample.shape
    n_mem = mem_prompt.shape[1]
    past = cache_k.shape[2]

    w_r = jnp.concatenate([w_rg[l], w_re[l], jnp.zeros((D, LANES - N_GROUPS - N_EXPERTS), F32)], axis=1)
    W = dict(
        g_mix=g_mix[l], w_in=_layout_w_in(w_in[l]), w_pool=w_pool[l].astype(BF16), pool_scale=pool_scale[l],
        w_out=w_out[l].astype(BF16), g_x=g_x[l], w_xq=w_xq[l].astype(BF16), w_xo=w_xo[l].astype(BF16),
        g_ffn=g_ffn[l], w_r=w_r.astype(BF16),
        b_r=jnp.concatenate([b_rg[l], b_re[l], jnp.zeros((LANES - N_GROUPS - N_EXPERTS,), F32)]).reshape(1, LANES),
        w_gate=w_gate[l].astype(BF16), w_up=w_up[l].astype(BF16), w_down=w_down[l].astype(BF16),
        g_final=g_final)

    w_m = jnp.concatenate([w_mk[l], w_mv[l]], axis=1).astype(BF16)
    mk, mv = _norm_matmul(mem_prompt.reshape(B * n_mem, D), g_mem[l], w_m,
                          ((0, D_X, 1.0, 1), (D_X, 2 * D_X, 1.0, 1)), (F32, F32), 256)
    mk = mk.reshape(B, n_mem, D_X)
    mv = mv.reshape(B, n_mem, D_X)

    yp, kp, vp, kip, pp = _layer(x_prompt, 0, None, jnp.zeros((B, POOL_HIST, D_POOL), F32), mk, mv, rel_bias, W,
                                 tm_in=256, tq=256, tm_mix=256, tm_moe=_moe_tile_rows(B * T), sparse_moe=True)
    ys, ks, vs, kis, ps = _layer(x_sample, past,
                                 (cache_k[l], cache_v[l], cache_k_idx[l]), cache_pool[l],
                                 cache_mem_k[l].reshape(Bs, n_mem, D_X), cache_mem_v[l].reshape(Bs, n_mem, D_X),
                                 rel_bias, W, tm_in=Bs * Ts, tq=Ts, tm_mix=Ts, tm_moe=Bs * Ts, sparse_moe=False)
    st = lambda a: a[None]
    return (yp, ys, st(kp), st(vp), st(kip), st(pp),
            st(mk.reshape(B, n_mem, N_XHEADS, XHEAD_DIM)), st(mv.reshape(B, n_mem, N_XHEADS, XHEAD_DIM)),
            st(ks), st(vs), st(kis), st(ps))
```

```python
import functools
import math

import numpy as np
import jax
import jax.numpy as jnp
from jax import lax
from jax.experimental import pallas as pl
from jax.experimental.pallas import tpu as pltpu

F32 = jnp.float32
BF16 = jnp.bfloat16

CHUNK = 64
N_HEADS_A = 8
HEAD_DIM_A = 128
N_KV_A = 2
D_ATTN = N_HEADS_A * HEAD_DIM_A
D_KV = N_KV_A * HEAD_DIM_A
N_IDX_HEADS = 8
IDX_DIM = 64
TOPK_MAX = 256
POOL_WINDOWS = (2, 4, 8, 16)
POOL_GROUP_DIM = 256
D_POOL = len(POOL_WINDOWS) * POOL_GROUP_DIM
POOL_HIST = max(POOL_WINDOWS) - 1
N_BUCKETS = 32
MAX_DISTANCE = 128
N_XHEADS = 4
XHEAD_DIM = 128
D_X = N_XHEADS * XHEAD_DIM
N_GROUPS = 4
EXPERTS_PER_GROUP = 8
N_EXPERTS = N_GROUPS * EXPERTS_PER_GROUP
D_EXPERT = 256
EPS = 1e-6

LANES = 128
INT_MIN = -(2 ** 31)
NEG = -1e30
LOG2E = math.log2(math.e)
VMEM_LIMIT = 48 * 1024 * 1024


def _rms(x, g):
    ms = jnp.mean(x * x, axis=-1, keepdims=True)
    return x * lax.rsqrt(ms + EPS) * g


def _dot(a, b):
    return jnp.dot(a, b, preferred_element_type=F32)


def _dot_nt(a, b):
    return lax.dot_general(a, b, (((1,), (1,)), ((), ())), preferred_element_type=F32)


def _const_spec(shape):
    nd = len(shape)
    return pl.BlockSpec(shape, lambda *_: (0,) * nd, pipeline_mode=pl.Buffered(1))


def _norm_matmul_kernel(x_ref, g_ref, w_ref, *out_refs, groups):
    h = _rms(x_ref[...], g_ref[...]).astype(BF16)
    k = 0
    for c0, c1, scale, n_out in groups:
        acc = _dot(h, w_ref[:, c0:c1])
        if scale != 1.0:
            acc = acc * scale
        for _ in range(n_out):
            out_refs[k][...] = acc.astype(out_refs[k].dtype)
            k += 1


def _norm_matmul(x, g, w, groups, out_dtypes, tm):
    n, d = x.shape
    assert n % tm == 0
    widths = []
    for c0, c1, _, n_out in groups:
        widths += [c1 - c0] * n_out
    return pl.pallas_call(
        functools.partial(_norm_matmul_kernel, groups=tuple(groups)),
        grid=(n // tm,),
        in_specs=[pl.BlockSpec((tm, d), lambda i: (i, 0)),
                  _const_spec((1, d)),
                  _const_spec(w.shape)],
        out_specs=[pl.BlockSpec((tm, wd), lambda i: (i, 0)) for wd in widths],
        out_shape=[jax.ShapeDtypeStruct((n, wd), dt) for wd, dt in zip(widths, out_dtypes)],
        compiler_params=pltpu.CompilerParams(dimension_semantics=("arbitrary",),
                                             vmem_limit_bytes=VMEM_LIMIT),
        name="norm_matmul",
    )(x, g.reshape(1, d), w)


def _bias_table_kernel(relb_ref, bucket_ref, o_ref, *, shift_bucket):
    bucket = bucket_ref[...]
    for h in range(N_HEADS_A):
        acc = jnp.zeros(bucket.shape, F32)
        for b in range(N_BUCKETS):
            acc = jnp.where(bucket == b, relb_ref[b, h], acc)
        if shift_bucket is None:
            o_ref[h] = acc * LOG2E
        else:
            o_ref[h] = (acc - relb_ref[shift_bucket, h]) * (HEAD_DIM_A ** 0.5)


def _bias_table(rel_bias, bucket, shift_bucket=None):
    tq, w = bucket.shape
    return pl.pallas_call(
        functools.partial(_bias_table_kernel, shift_bucket=shift_bucket),
        in_specs=[pl.BlockSpec(memory_space=pltpu.SMEM),
                  pl.BlockSpec((tq, w), lambda: (0, 0))],
        out_specs=pl.BlockSpec((N_HEADS_A, tq, w), lambda: (0, 0, 0)),
        out_shape=jax.ShapeDtypeStruct((N_HEADS_A, tq, w), F32),
        name="bias_table",
    )(rel_bias, bucket)


def _t5_bucket(rel):
    nb = N_BUCKETS // 2
    max_exact = nb // 2
    bucket = (rel > 0).astype(jnp.int32) * nb
    n = jnp.abs(rel)
    nf = jnp.maximum(n, 1).astype(F32)
    large = max_exact + (jnp.log(nf / max_exact) / math.log(MAX_DISTANCE / max_exact)
                         * (nb - max_exact)).astype(jnp.int32)
    large = jnp.minimum(large, nb - 1)
    return bucket + jnp.where(n < max_exact, n, large)


def _far_bucket(min_dist, max_dist):
    nb = N_BUCKETS // 2
    max_exact = nb // 2
    n = np.arange(min_dist, max_dist + 1, dtype=np.float64)
    large = max_exact + np.floor(np.log(n / max_exact) / math.log(MAX_DISTANCE / max_exact)
                                 * (nb - max_exact) * (1 - 1e-6)).astype(np.int64)
    assert large.min() >= nb - 1, "far key blocks must sit in the saturated distance bucket"
    return nb - 1


def _dsa_kernel(relb_ref, q_ref, qi_ref, kiwi_ref, k_ref, v_ref, kie_ref, kio_ref, bnear_ref, vis_ref,
                o_ref, keys, madd, wb, qs, xcut, mrun, lrun, acc_s,
                *, TQ, KB, RB, kb0, kstep, topk, far_bucket, tie_bits):
    i = pl.program_id(1)
    nkb = kb0 + kstep * i
    n_far = jnp.maximum(nkb - 2, 0)
    n_lc = KB // LANES
    HPG = N_HEADS_A // N_KV_A

    kiwi = kiwi_ref[0]
    for h in range(N_IDX_HEADS):
        w = (kiwi[:, IDX_DIM + h:IDX_DIM + h + 1] * (N_IDX_HEADS ** -0.5)).astype(BF16).astype(F32)
        wb[h] = jnp.broadcast_to(w * (IDX_DIM ** -0.5), (TQ, LANES))

    def idx_block(j, near_col):
        c0 = pl.multiple_of(j * KB, KB)
        ke = kie_ref[0, pl.ds(c0, KB), :]
        ko = kio_ref[0, pl.ds(c0, KB), :]
        acc = [jnp.zeros((TQ, LANES), F32) for _ in range(n_lc)]
        for hp in range(N_IDX_HEADS // 2):
            qh = qi_ref[0, :, hp * LANES:(hp + 1) * LANES]
            se = _dot_nt(qh, ke)
            so = _dot_nt(qh, ko)
            we = wb[2 * hp]
            wo = wb[2 * hp + 1]
            for c in range(n_lc):
                cs = slice(c * LANES, (c + 1) * LANES)
                re = jnp.maximum(se[:, cs], 0.0).astype(BF16).astype(F32)
                ro = jnp.maximum(so[:, cs], 0.0).astype(BF16).astype(F32)
                acc[c] = acc[c] + we * re + wo * ro
        for c in range(n_lc):
            bits = lax.bitcast_convert_type(acc[c], jnp.int32)
            key = bits ^ ((bits >> 31) & jnp.int32(0x7FFFFFFF))
            if near_col is not None:
                vis = vis_ref[:, near_col + c * LANES:near_col + (c + 1) * LANES]
                key = jnp.where(vis > 0.0, key, jnp.int32(INT_MIN))
            keys[j, :, c * LANES:(c + 1) * LANES] = key

    def far_idx(j, carry):
        idx_block(j, None)
        return carry

    lax.fori_loop(0, n_far, far_idx, 0)

    @pl.when(nkb >= 2)
    def _():
        idx_block(nkb - 2, 0)

    idx_block(nkb - 1, KB)

    kf = float(topk)
    lane_f = lax.broadcasted_iota(jnp.int32, (RB, LANES), 1).astype(F32)
    NRB = TQ // RB
    rbs = [slice(rb * RB, (rb + 1) * RB) for rb in range(NRB)]

    def count(pred):
        def body(jj, accs):
            out = []
            for rb in range(NRB):
                blk = keys[jj, rbs[rb], :]
                acc = accs[rb]
                for c in range(n_lc):
                    acc = acc + jnp.where(pred(rb, blk[:, c * LANES:(c + 1) * LANES], jj, c), 1.0, 0.0)
                out.append(acc)
            return tuple(out)
        accs = lax.fori_loop(0, nkb, body, tuple(jnp.zeros((RB, LANES), F32) for _ in range(NRB)))
        return [jnp.sum(acc, axis=1, keepdims=True) for acc in accs]

    def count_ge(cands):
        cb = [jnp.broadcast_to(cand, (RB, LANES)) for cand in cands]
        return count(lambda rb, kblk, jj, c: kblk >= cb[rb])

    def bis_body(p, thrs):
        cands = [thr + lax.shift_left(jnp.int32(1), 31 - p) for thr in thrs]
        cnts = count_ge(cands)
        return tuple(jnp.where(cnt >= kf, cand, thr) for cnt, cand, thr in zip(cnts, cands, thrs))

    thrs = lax.fori_loop(0, 32, bis_body, tuple(jnp.full((RB, 1), INT_MIN, jnp.int32) for _ in range(NRB)))
    thrs = [jnp.maximum(thr, jnp.int32(INT_MIN + 1)) for thr in thrs]
    cnt_ge = count_ge(thrs)
    cnt_gt = count_ge([thr + 1 for thr in thrs])
    needs = [kf - c for c in cnt_gt]
    thrb = [jnp.broadcast_to(thr, (RB, LANES)) for thr in thrs]

    xcut[...] = jnp.full(xcut.shape, 2.0 ** 24, F32)
    excess = cnt_ge[0] - kf
    for c in cnt_ge[1:]:
        excess = jnp.maximum(excess, c - kf)

    @pl.when(jnp.max(excess) > 0.0)
    def _():
        def count_eq_lt(xs):
            xb = [jnp.broadcast_to(x, (RB, LANES)) for x in xs]

            def pred(rb, kblk, jj, c):
                col = lane_f + (jj * KB + c * LANES).astype(F32)
                return jnp.where(kblk == thrb[rb], col, 2.0 ** 25) < xb[rb]
            return count(pred)

        def tie_body(p, xs):
            cands = [x + lax.shift_left(jnp.int32(1), tie_bits - 1 - p).astype(F32) for x in xs]
            cnts = count_eq_lt(cands)
            return tuple(jnp.where(cnt < need, cand, x) for cnt, need, cand, x in zip(cnts, needs, cands, xs))

        xs = lax.fori_loop(0, tie_bits, tie_body, tuple(jnp.zeros((RB, 1), F32) for _ in range(NRB)))
        for rb in range(NRB):
            xcut[rbs[rb], :] = xs[rb]

    xb = [jnp.broadcast_to(xcut[rbs[rb], :], (RB, LANES)) for rb in range(NRB)]

    def mask_body(jj, carry):
        for rb in range(NRB):
            blk = keys[jj, rbs[rb], :]
            for c in range(n_lc):
                kblk = blk[:, c * LANES:(c + 1) * LANES]
                col = lane_f + (jj * KB + c * LANES).astype(F32)
                keep = (kblk > thrb[rb]) | ((kblk == thrb[rb]) & (col <= xb[rb]))
                madd[jj, rbs[rb], c * LANES:(c + 1) * LANES] = jnp.where(keep, 0.0, NEG)
        return carry

    lax.fori_loop(0, nkb, mask_body, 0)

    _attend(relb_ref, q_ref, k_ref, v_ref, bnear_ref, o_ref, madd, qs, mrun, lrun, acc_s,
            TQ=TQ, KB=KB, nkb=nkb, n_far=n_far, far_bucket=far_bucket)


def _attend(relb_ref, q_ref, k_ref, v_ref, bnear_ref, o_ref, madd, qs, mrun, lrun, acc_s,
            *, TQ, KB, nkb, n_far, far_bucket):
    n_lc = KB // LANES
    HPG = N_HEADS_A // N_KV_A
    for g in range(N_KV_A):
        for r in range(HPG):
            h = g * HPG + r
            qs[g, r * TQ:(r + 1) * TQ, :] = q_ref[0, :, h * HEAD_DIM_A:(h + 1) * HEAD_DIM_A]

    def logits(j, near_col, g):
        c0 = pl.multiple_of(j * KB, KB)
        kg = k_ref[0, pl.ds(c0, KB), g * HEAD_DIM_A:(g + 1) * HEAD_DIM_A]
        s = _dot_nt(qs[g], kg)
        mk = madd[j]
        out = []
        for r in range(HPG):
            h = g * HPG + r
            if near_col is None:
                bias = relb_ref[far_bucket, h] * LOG2E
            else:
                bias = bnear_ref[h, :, near_col:near_col + KB]
            out.append(s[r * TQ:(r + 1) * TQ] * (HEAD_DIM_A ** -0.5 * LOG2E) + bias + mk)
        return out

    def over_blocks(fn):
        def far(j, carry):
            fn(j, None)
            return carry

        lax.fori_loop(0, n_far, far, 0)

        @pl.when(nkb >= 2)
        def _():
            fn(nkb - 2, 0)

        fn(nkb - 1, KB)

    mrun[...] = jnp.full(mrun.shape, NEG, F32)

    def max_block(j, near_col):
        for g in range(N_KV_A):
            for r, sh in enumerate(logits(j, near_col, g)):
                rs = slice(r * TQ, (r + 1) * TQ)
                m = mrun[g, rs, :]
                for c in range(n_lc):
                    m = jnp.maximum(m, sh[:, c * LANES:(c + 1) * LANES])
                mrun[g, rs, :] = m

    over_blocks(max_block)

    for g in range(N_KV_A):
        mrun[g] = jnp.broadcast_to(jnp.max(mrun[g], axis=1, keepdims=True), mrun.shape[1:])
    lrun[...] = jnp.zeros(lrun.shape, F32)
    acc_s[...] = jnp.zeros(acc_s.shape, F32)

    def pv_block(j, near_col):
        c0 = pl.multiple_of(j * KB, KB)
        for g in range(N_KV_A):
            vg = v_ref[0, pl.ds(c0, KB), g * HEAD_DIM_A:(g + 1) * HEAD_DIM_A]
            parts = []
            for r, sh in enumerate(logits(j, near_col, g)):
                rs = slice(r * TQ, (r + 1) * TQ)
                mb = mrun[g, rs, :]
                l = lrun[g, rs, :]
                pcs = []
                for c in range(n_lc):
                    p = jnp.exp2(sh[:, c * LANES:(c + 1) * LANES] - mb)
                    l = l + p
                    pcs.append(p.astype(BF16))
                lrun[g, rs, :] = l
                parts.append(jnp.concatenate(pcs, axis=1))
            acc_s[g] += _dot(jnp.concatenate(parts, axis=0), vg)

    over_blocks(pv_block)

    for g in range(N_KV_A):
        out = acc_s[g] / jnp.sum(lrun[g], axis=1, keepdims=True)
        for r in range(HPG):
            h = g * HPG + r
            o_ref[0, :, h * HEAD_DIM_A:(h + 1) * HEAD_DIM_A] = out[r * TQ:(r + 1) * TQ].astype(o_ref.dtype)


DIGIT_BITS = 8
N_DIGITS = 32 // DIGIT_BITS
SUB_BF16 = 16


def _dsa_kernel_t(q_ref, qi_ref, kiwi_ref, k_ref, vt_ref, kie_ref, kio_ref, bnear_ref, vist_ref,
                  o_ref, keys, dig, madd, qs, xcut, acc_s, tbuf,
                  *, TQ, KB, kb0, kstep, topk, tie_bits):
    i = pl.program_id(1)
    nkb = kb0 + kstep * i
    n_far = jnp.maximum(nkb - 2, 0)
    n_slab = KB // SUB_BF16

    w_t = kiwi_ref[0].T
    wrow = [(w_t[IDX_DIM + h:IDX_DIM + h + 1, :] * (N_IDX_HEADS ** -0.5)).astype(BF16).astype(F32)
            * (IDX_DIM ** -0.5) for h in range(N_IDX_HEADS)]

    def idx_dots(j):
        c0 = pl.multiple_of(j * KB, KB)
        ke = kie_ref[0, pl.ds(c0, KB), :]
        ko = kio_ref[0, pl.ds(c0, KB), :]
        raw = []
        for hp in range(N_IDX_HEADS // 2):
            qh = qi_ref[0, :, hp * LANES:(hp + 1) * LANES]
            raw.append((_dot_nt(ke, qh), _dot_nt(ko, qh)))
        return raw

    def idx_finish(j, raw, near_row):
        acc = jnp.zeros((KB, TQ), F32)
        for hp, (se, so) in enumerate(raw):
            re = jnp.maximum(se.astype(BF16), 0.0).astype(F32)
            ro = jnp.maximum(so.astype(BF16), 0.0).astype(F32)
            acc = acc + wrow[2 * hp] * re + wrow[2 * hp + 1] * ro
        bits = lax.bitcast_convert_type(acc, jnp.int32)
        key = bits ^ ((bits >> 31) & jnp.int32(0x7FFFFFFF))
        if near_row is not None:
            key = jnp.where(vist_ref[near_row:near_row + KB, :] > 0.0, key, jnp.int32(INT_MIN))
        keys[j] = key
        for dk in range(N_DIGITS):
            sh = 32 - DIGIT_BITS * (dk + 1)
            d = (key >> sh) if dk == 0 else ((key >> sh) & jnp.int32(2 ** DIGIT_BITS - 1))
            dig[dk, j] = d.astype(F32).astype(BF16)

    def idx_blocks(js, near_rows):
        raws = [idx_dots(j) for j in js]
        for j, raw, near_row in zip(js, raws, near_rows):
            idx_finish(j, raw, near_row)

    def far_idx_pair(jp, carry):
        idx_blocks([2 * jp, 2 * jp + 1], [None, None])
        return carry

    lax.fori_loop(0, n_far // 2, far_idx_pair, 0)

    @pl.when(lax.rem(n_far, 2) == 1)
    def _():
        idx_blocks([n_far - 1], [None])

    @pl.when(nkb >= 2)
    def _():
        idx_blocks([nkb - 2, nkb - 1], [0, KB])

    @pl.when(nkb < 2)
    def _():
        idx_blocks([nkb - 1], [KB])

    one = jnp.ones((SUB_BF16, TQ), BF16)
    zero = jnp.zeros((SUB_BF16, TQ), BF16)

    def count_ge(dk, cand):
        cb = jnp.broadcast_to(cand.astype(BF16), (SUB_BF16, TQ))

        def body(jj, tot):
            blk = dig[dk, jj]
            parts = [jnp.where(blk[s * SUB_BF16:(s + 1) * SUB_BF16, :] >= cb, one, zero) for s in range(n_slab)]
            while len(parts) > 1:
                parts = [parts[n] + parts[n + 1] for n in range(0, len(parts), 2)]
            return tot + parts[0].astype(F32)

        tot = lax.fori_loop(0, nkb, body, jnp.zeros((SUB_BF16, TQ), F32))
        return jnp.sum(tot, axis=0, keepdims=True)

    kneed = jnp.full((1, TQ), float(topk), F32)
    thr = jnp.zeros((1, TQ), jnp.int32)
    cnt_eq = None
    for dk in range(N_DIGITS):
        lo = -float(2 ** (DIGIT_BITS - 1)) if dk == 0 else 0.0

        def bis_body(p, carry, dk=dk, kneed=kneed):
            t, c_rej = carry
            cand = t + lax.shift_left(jnp.int32(1), DIGIT_BITS - 1 - p).astype(F32)
            cnt = count_ge(dk, cand)
            ok = cnt >= kneed
            return jnp.where(ok, cand, t), jnp.where(ok, c_rej, cnt)

        t, n_gt = lax.fori_loop(0, DIGIT_BITS, bis_body,
                                (jnp.full((1, TQ), lo, F32), jnp.zeros((1, TQ), F32)))
        if dk == N_DIGITS - 1:
            cnt_eq = count_ge(dk, t) - n_gt
        kneed = kneed - n_gt
        thr = thr + lax.shift_left(t.astype(jnp.int32), 32 - DIGIT_BITS * (dk + 1))
        if dk + 1 < N_DIGITS:
            tb = t.astype(BF16)

            def prep(jj, carry, dk=dk, tb=tb):
                dig[dk + 1, jj] = jnp.where(dig[dk, jj] == tb, dig[dk + 1, jj], jnp.asarray(-1.0, BF16))
                return carry

            lax.fori_loop(0, nkb, prep, 0)

    thr = jnp.maximum(thr, jnp.int32(INT_MIN + 1))
    need = kneed
    sub_f = lax.broadcasted_iota(jnp.int32, (KB, TQ), 0).astype(F32)

    xcut[...] = jnp.full(xcut.shape, 2.0 ** 24, F32)

    @pl.when(jnp.max(cnt_eq - need) > 0.0)
    def _():
        def count_eq_lt(x):
            def body(jj, tot):
                col = sub_f + (jj * KB).astype(F32)
                hit = jnp.where(jnp.where(keys[jj] == thr, col, 2.0 ** 25) < x, 1.0, 0.0)
                for s in range(KB // 8):
                    tot = tot + hit[s * 8:(s + 1) * 8, :]
                return tot
            tot = lax.fori_loop(0, nkb, body, jnp.zeros((8, TQ), F32))
            return jnp.sum(tot, axis=0, keepdims=True)

        def tie_body(p, x):
            cand = x + lax.shift_left(jnp.int32(1), tie_bits - 1 - p).astype(F32)
            return jnp.where(count_eq_lt(cand) < need, cand, x)

        x = lax.fori_loop(0, tie_bits, tie_body, jnp.zeros((1, TQ), F32))
        xcut[...] = jnp.broadcast_to(x, xcut.shape)

    xc = xcut[0:1, :]

    def mask_body(jj, carry):
        kblk = keys[jj]
        col = sub_f + (jj * KB).astype(F32)
        at_thr = jnp.where(col <= xc, 0.0, NEG)
        madd[jj] = jnp.where(kblk > thr, 0.0, jnp.where(kblk == thr, at_thr, NEG))
        return carry

    lax.fori_loop(0, nkb, mask_body, 0)

    HPG = N_HEADS_A // N_KV_A
    c2 = HEAD_DIM_A ** -0.5 * LOG2E
    for g in range(N_KV_A):
        for r in range(HPG):
            h = g * HPG + r
            qs[g, r * TQ:(r + 1) * TQ, :] = q_ref[0, :, h * HEAD_DIM_A:(h + 1) * HEAD_DIM_A]
    acc_s[...] = jnp.zeros(acc_s.shape, F32)

    def attn_blocks(js, near_rows, carry):
        ms, ls = carry
        s_all = []
        for j in js:
            c0 = pl.multiple_of(j * KB, KB)
            s_all.append([_dot_nt(k_ref[0, pl.ds(c0, KB), g * HEAD_DIM_A:(g + 1) * HEAD_DIM_A], qs[g])
                          for g in range(N_KV_A)])
        for b, (j, near_row) in enumerate(zip(js, near_rows)):
            mk = madd[j]
            m_news = []
            for g in range(N_KV_A):
                for r in range(HPG):
                    h = g * HPG + r
                    t = s_all[b][g][:, r * TQ:(r + 1) * TQ] + mk
                    if near_row is not None:
                        t = t + bnear_ref[h, near_row:near_row + KB, :]
                    tbuf[b * N_HEADS_A + h] = t
                    m_news.append(jnp.maximum(ms[h], jnp.max(t, axis=0, keepdims=True)))
            ms_out, ls_out = [], []
            for g in range(N_KV_A):
                vt = vt_ref[0, j, g * HEAD_DIM_A:(g + 1) * HEAD_DIM_A, :]
                for r in range(HPG):
                    h = g * HPG + r
                    cs = slice(r * TQ, (r + 1) * TQ)
                    m_new = m_news[h]
                    alpha = jnp.exp2((ms[h] - m_new) * c2)
                    p = jnp.exp2((tbuf[b * N_HEADS_A + h] - m_new) * c2)
                    ls_out.append(alpha * ls[h] + jnp.sum(p, axis=0, keepdims=True))
                    ms_out.append(m_new)
                    acc_s[g, :, cs] = alpha * acc_s[g, :, cs] + _dot(vt, p.astype(BF16))
            ms, ls = tuple(ms_out), tuple(ls_out)
        return ms, ls

    carry = (tuple(jnp.full((1, TQ), NEG, F32) for _ in range(N_HEADS_A)),
             tuple(jnp.zeros((1, TQ), F32) for _ in range(N_HEADS_A)))
    carry = lax.fori_loop(0, n_far // 2, lambda jp, c: attn_blocks([2 * jp, 2 * jp + 1], [None, None], c), carry)
    carry = lax.cond(lax.rem(n_far, 2) == 1, lambda c: attn_blocks([n_far - 1], [None], c), lambda c: c, carry)
    ms, ls = lax.cond(nkb >= 2, lambda c: attn_blocks([nkb - 2, nkb - 1], [0, KB], c),
                      lambda c: attn_blocks([nkb - 1], [KB], c), carry)

    for g in range(N_KV_A):
        for r in range(HPG):
            h = g * HPG + r
            out_t = acc_s[g, :, r * TQ:(r + 1) * TQ] / ls[h]
            o_ref[0, :, h * HEAD_DIM_A:(h + 1) * HEAD_DIM_A] = out_t.T.astype(o_ref.dtype)


def _dsa(q, qi, kiwi, kb, vb, kie, kio, rel_bias, *, n_keys, pos0, tq, kblk):
    B, T, _ = q.shape
    Lp = kb.shape[1]
    assert T % tq == 0 and Lp % kblk == 0 and tq % CHUNK == 0 or T == tq
    n_qt = T // tq
    if n_qt == 1:
        kb0, kstep = Lp // kblk, 0
        assert ((pos0 + tq - 1) // CHUNK + 1) * CHUNK >= n_keys
    else:
        assert pos0 == 0 and tq == kblk and n_keys == Lp == T
        kb0, kstep = 1, 1
    topk = min(TOPK_MAX, n_keys // 4)
    rb = min(128, tq)

    t = jnp.arange(tq, dtype=jnp.int32)[:, None]
    c = jnp.arange(2 * kblk, dtype=jnp.int32)[None, :]
    if n_qt == 1:
        q_abs = pos0 + t
        s_abs = (kb0 - 2) * kblk + c
    else:
        q_abs = kblk + t
        s_abs = c
    vis = ((s_abs // CHUNK) <= (q_abs // CHUNK)) & (s_abs < (n_keys if n_qt == 1 else 2 * kblk))
    far_bucket = _far_bucket(kblk + 1, max(Lp, kblk + 2))

    n_kb = Lp // kblk
    hpg = N_HEADS_A // N_KV_A
    common = dict(TQ=tq, KB=kblk, kb0=kb0, kstep=kstep, topk=topk, far_bucket=far_bucket,
                  tie_bits=int(math.ceil(math.log2(Lp))) + 1)
    attend_scratch = [pltpu.VMEM((N_KV_A, hpg * tq, HEAD_DIM_A), BF16),
                      pltpu.VMEM((N_KV_A, hpg * tq, LANES), F32),
                      pltpu.VMEM((N_KV_A, hpg * tq, LANES), F32),
                      pltpu.VMEM((N_KV_A, hpg * tq, HEAD_DIM_A), F32)]
    if tq % LANES == 0:
        tab_t = _bias_table(rel_bias, _t5_bucket(s_abs - q_abs).T, shift_bucket=far_bucket)
        vt = jnp.swapaxes(vb.reshape(B, n_kb, kblk, D_KV), 2, 3)
        common.pop("far_bucket")
        return pl.pallas_call(
            functools.partial(_dsa_kernel_t, **common),
            grid=(B, n_qt),
            in_specs=[pl.BlockSpec((1, tq, D_ATTN), lambda b, i: (b, i, 0)),
                      pl.BlockSpec((1, tq, N_IDX_HEADS * IDX_DIM), lambda b, i: (b, i, 0)),
                      pl.BlockSpec((1, tq, LANES), lambda b, i: (b, i, 0)),
                      pl.BlockSpec((1, Lp, D_KV), lambda b, i: (b, 0, 0)),
                      pl.BlockSpec((1, n_kb, D_KV, kblk), lambda b, i: (b, 0, 0, 0)),
                      pl.BlockSpec((1, Lp, LANES), lambda b, i: (b, 0, 0)),
                      pl.BlockSpec((1, Lp, LANES), lambda b, i: (b, 0, 0)),
                      _const_spec((N_HEADS_A, 2 * kblk, tq)),
                      _const_spec((2 * kblk, tq))],
            out_specs=pl.BlockSpec((1, tq, D_ATTN), lambda b, i: (b, i, 0)),
            out_shape=jax.ShapeDtypeStruct((B, T, D_ATTN), BF16),
            scratch_shapes=[pltpu.VMEM((n_kb, kblk, tq), jnp.int32),
                            pltpu.VMEM((N_DIGITS, n_kb, kblk, tq), BF16),
                            pltpu.VMEM((n_kb, kblk, tq), F32),
                            pltpu.VMEM((N_KV_A, hpg * tq, HEAD_DIM_A), BF16),
                            pltpu.VMEM((8, tq), F32),
                            pltpu.VMEM((N_KV_A, HEAD_DIM_A, hpg * tq), F32),
                            pltpu.VMEM((2 * N_HEADS_A, kblk, tq), F32)],
            compiler_params=pltpu.CompilerParams(dimension_semantics=("arbitrary", "arbitrary"),
                                                 vmem_limit_bytes=VMEM_LIMIT),
            name="dsa_t",
        )(q, qi, kiwi, kb, vt, kie, kio, tab_t, vis.astype(F32).T)

    bias_near = _bias_table(rel_bias, _t5_bucket(s_abs - q_abs))
    kern = functools.partial(_dsa_kernel, RB=rb, **common)
    return pl.pallas_call(
        kern,
        grid=(B, n_qt),
        in_specs=[pl.BlockSpec(memory_space=pltpu.SMEM),
                  pl.BlockSpec((1, tq, D_ATTN), lambda b, i: (b, i, 0)),
                  pl.BlockSpec((1, tq, N_IDX_HEADS * IDX_DIM), lambda b, i: (b, i, 0)),
                  pl.BlockSpec((1, tq, LANES), lambda b, i: (b, i, 0)),
                  pl.BlockSpec((1, Lp, D_KV), lambda b, i: (b, 0, 0)),
                  pl.BlockSpec((1, Lp, D_KV), lambda b, i: (b, 0, 0)),
                  pl.BlockSpec((1, Lp, LANES), lambda b, i: (b, 0, 0)),
                  pl.BlockSpec((1, Lp, LANES), lambda b, i: (b, 0, 0)),
                  _const_spec((N_HEADS_A, tq, 2 * kblk)),
                  _const_spec((tq, 2 * kblk))],
        out_specs=pl.BlockSpec((1, tq, D_ATTN), lambda b, i: (b, i, 0)),
        out_shape=jax.ShapeDtypeStruct((B, T, D_ATTN), BF16),
        scratch_shapes=[pltpu.VMEM((n_kb, tq, kblk), jnp.int32),
                        pltpu.VMEM((n_kb, tq, kblk), F32),
                        pltpu.VMEM((N_IDX_HEADS, tq, LANES), F32),
                        pltpu.VMEM((N_KV_A, hpg * tq, HEAD_DIM_A), BF16),
                        pltpu.VMEM((tq, 1), F32),
                        pltpu.VMEM((N_KV_A, hpg * tq, LANES), F32),
                        pltpu.VMEM((N_KV_A, hpg * tq, LANES), F32),
                        pltpu.VMEM((N_KV_A, hpg * tq, HEAD_DIM_A), F32)],
        compiler_params=pltpu.CompilerParams(dimension_semantics=("arbitrary", "arbitrary"),
                                             vmem_limit_bytes=VMEM_LIMIT),
        name="dsa",
    )(rel_bias, q, qi, kiwi, kb, vb, kie, kio, bias_near, vis.astype(F32))


def _mix_kernel(x_ref, a_ref, u_ref, hist_ref, mk_ref, mv_ref, wpool_ref, pscale_ref, wout_ref, gx_ref,
                wxq_ref, wxo_ref, gffn_ref, wr_ref, br_ref, o_ref, ext, *, tm, pos0):
    t = pl.program_id(1)
    H = POOL_HIST + 1

    @pl.when(t == 0)
    def _():
        ext[0:H, :] = hist_ref[0]

    @pl.when(t > 0)
    def _():
        ext[0:H, :] = ext[tm:tm + H, :]

    ext[H:H + tm, :] = u_ref[0]

    da = _dot(a_ref[0], wout_ref[0:D_ATTN, :])

    pos = pos0 + t * tm + lax.broadcasted_iota(jnp.int32, (tm, 1), 0)
    pouts = []
    for gi, w in enumerate(POOL_WINDOWS):
        c0, c1 = gi * POOL_GROUP_DIM, (gi + 1) * POOL_GROUP_DIM
        wsum = ext[H:H + tm, c0:c1]
        for s in range(1, w):
            wsum = wsum + ext[H - s:H - s + tm, c0:c1]
        cnt = jnp.minimum(pos + 1, w).astype(F32)
        d = wsum / cnt - u_ref[0, :, c0:c1]
        y = _dot(d.astype(BF16), wpool_ref[gi]) * pscale_ref[:, c0:c1]
        pouts.append(y.astype(BF16))
    pcat = jnp.concatenate(pouts, axis=1)

    D = x_ref.shape[2]
    x1 = x_ref[0] + da + _dot(pcat, wout_ref[D_ATTN:D_ATTN + D_POOL, :])
    h = _rms(x1, gx_ref[...]).astype(BF16)
    qx = _dot(h, wxq_ref[...])
    heads = [slice(hh * XHEAD_DIM, (hh + 1) * XHEAD_DIM) for hh in range(N_XHEADS)]
    all_logits = [_dot_nt(qx[:, hs].astype(BF16), mk_ref[0, :, hs]) for hs in heads]
    outs = []
    for hs, logits in zip(heads, all_logits):
        logits = logits * (XHEAD_DIM ** -0.5)
        m = jnp.max(logits, axis=1, keepdims=True)
        p = jnp.exp(logits - m)
        l = jnp.sum(p, axis=1, keepdims=True)
        outs.append((_dot(p.astype(BF16), mv_ref[0, :, hs]) / l).astype(BF16))
    o = jnp.concatenate(outs, axis=1)
    x2 = x1 + _dot(o, wxo_ref[...])
    o_ref[0, :, 0:D] = x2
    hf = _rms(x2, gffn_ref[...]).astype(BF16)
    o_ref[0, :, D:D + LANES] = _route(_dot(hf, wr_ref[...]) + br_ref[...])


def _mix(x, a, u, hist, mk, mv, w_pool, pool_scale, w_out, g_x, w_xq, w_xo, g_ffn, w_r, b_r, *, pos0, tm):
    B, T, D = x.shape
    n_mem = mk.shape[1]
    H = POOL_HIST + 1
    assert T % tm == 0 and tm >= H
    hist16 = jnp.concatenate([jnp.zeros((B, 1, D_POOL), F32), hist], axis=1)
    return pl.pallas_call(
        functools.partial(_mix_kernel, tm=tm, pos0=pos0),
        grid=(B, T // tm),
        in_specs=[pl.BlockSpec((1, tm, D), lambda b, t: (b, t, 0)),
                  pl.BlockSpec((1, tm, D_ATTN), lambda b, t: (b, t, 0)),
                  pl.BlockSpec((1, tm, D_POOL), lambda b, t: (b, t, 0)),
                  pl.BlockSpec((1, H, D_POOL), lambda b, t: (b, 0, 0)),
                  pl.BlockSpec((1, n_mem, D_X), lambda b, t: (b, 0, 0)),
                  pl.BlockSpec((1, n_mem, D_X), lambda b, t: (b, 0, 0)),
                  _const_spec(w_pool.shape),
                  _const_spec((1, D_POOL)),
                  _const_spec(w_out.shape),
                  _const_spec((1, D)),
                  _const_spec(w_xq.shape),
                  _const_spec(w_xo.shape),
                  _const_spec((1, D)),
                  _const_spec(w_r.shape),
                  _const_spec((1, LANES))],
        out_specs=pl.BlockSpec((1, tm, D + LANES), lambda b, t: (b, t, 0)),
        out_shape=jax.ShapeDtypeStruct((B, T, D + LANES), F32),
        scratch_shapes=[pltpu.VMEM((H + tm, D_POOL), F32)],
        compiler_params=pltpu.CompilerParams(dimension_semantics=("arbitrary", "arbitrary"),
                                             vmem_limit_bytes=VMEM_LIMIT),
        name="mix",
    )(x, a, u, hist16, mk, mv, w_pool, pool_scale.reshape(1, D_POOL), w_out, g_x.reshape(1, D), w_xq, w_xo,
      g_ffn.reshape(1, D), w_r, b_r)


R_OFF = N_GROUPS


def _route(logits):
    tm = logits.shape[0]
    lane = lax.broadcasted_iota(jnp.int32, (tm, LANES), 1).astype(F32)
    ninf = -jnp.inf
    big = float(LANES)
    gl = jnp.where(lane < N_GROUPS, logits, ninf)
    gmax = jnp.max(gl, axis=1, keepdims=True)
    g_sel = jnp.min(jnp.where(gl == gmax, lane, big), axis=1, keepdims=True)
    g_prob = 1.0 / jnp.sum(jnp.exp(gl - gmax), axis=1, keepdims=True)
    lo = R_OFF + g_sel * EXPERTS_PER_GROUP
    el = jnp.where((lane >= lo) & (lane < lo + EXPERTS_PER_GROUP), logits, ninf)
    tv0 = jnp.max(el, axis=1, keepdims=True)
    ti0 = jnp.min(jnp.where(el == tv0, lane, big), axis=1, keepdims=True)
    el2 = jnp.where(lane == ti0, ninf, el)
    tv1 = jnp.max(el2, axis=1, keepdims=True)
    ti1 = jnp.min(jnp.where(el2 == tv1, lane, big), axis=1, keepdims=True)
    e1 = jnp.exp(tv1 - tv0)
    den = 1.0 + e1
    w0 = g_prob / den
    w1 = g_prob * e1 / den
    ids = jnp.where(lane == 0.0, jnp.minimum(ti0, ti1) - R_OFF,
                    jnp.where(lane == 1.0, jnp.maximum(ti0, ti1) - R_OFF, 0.0))
    return jnp.where(lane == ti0, w0, 0.0) + jnp.where(lane == ti1, w1, 0.0) + ids


def _expert(h, rec, e, wg, wu, wd):
    a = _dot(h, wg)
    b = _dot(h, wu)
    lane = lax.broadcasted_iota(jnp.int32, rec.shape, 1)
    cw = jnp.sum(jnp.where(lane == e + R_OFF, rec, 0.0), axis=1, keepdims=True)
    hid = a * jax.nn.sigmoid(a) * b * cw
    return _dot(hid.astype(BF16), wd)


def _moe_kernel(x_ref, gffn_ref, wg_ref, wu_ref, wd_ref, gfin_ref, o_ref, h_s, acc_s):
    e = pl.program_id(1)
    d = o_ref.shape[1]

    @pl.when(e == 0)
    def _():
        h_s[...] = _rms(x_ref[:, 0:d], gffn_ref[...]).astype(BF16)
        acc_s[...] = jnp.zeros(acc_s.shape, F32)

    acc_s[...] += _expert(h_s[...], x_ref[:, d:d + LANES], e, wg_ref[0], wu_ref[0], wd_ref[0])

    @pl.when(e == pl.num_programs(1) - 1)
    def _():
        o_ref[...] = _rms(x_ref[:, 0:d] + acc_s[...], gfin_ref[...])


def _moe(x, g_ffn, w_gate, w_up, w_down, g_final, *, tm):
    n, de = x.shape
    d = de - LANES
    assert n % tm == 0
    return pl.pallas_call(
        _moe_kernel,
        grid=(n // tm, N_EXPERTS),
        in_specs=[pl.BlockSpec((tm, de), lambda i, e: (i, 0)),
                  _const_spec((1, d)),
                  pl.BlockSpec((1, d, D_EXPERT), lambda i, e: (e, 0, 0)),
                  pl.BlockSpec((1, d, D_EXPERT), lambda i, e: (e, 0, 0)),
                  pl.BlockSpec((1, D_EXPERT, d), lambda i, e: (e, 0, 0)),
                  _const_spec((1, d))],
        out_specs=pl.BlockSpec((tm, d), lambda i, e: (i, 0)),
        out_shape=jax.ShapeDtypeStruct((n, d), F32),
        scratch_shapes=[pltpu.VMEM((tm, d), BF16),
                        pltpu.VMEM((tm, d), F32)],
        compiler_params=pltpu.CompilerParams(dimension_semantics=("arbitrary", "arbitrary"),
                                             vmem_limit_bytes=VMEM_LIMIT),
        name="moe",
    )(x, g_ffn.reshape(1, d), w_gate, w_up, w_down, g_final.reshape(1, d))


def _moe_sparse_kernel(ea_ref, eb_ref, nv_ref, idx_ref, idxn_ref, x_hbm, gffn_ref, gfin_ref,
                       wga_ref, wua_ref, wda_ref, wgb_ref, wub_ref, wdb_ref, o_hbm,
                       xbuf, ybuf, gsem, ssem, *, tm):
    t = pl.program_id(0)
    nt = pl.num_programs(0)
    d = o_hbm.shape[1]
    slot = lax.rem(t, 2)
    nslot = 1 - slot
    nv = nv_ref[t]
    WAIT_ROWS = 8

    def gather_row(ids, s, r):
        return pltpu.make_async_copy(x_hbm.at[pl.ds(ids[0, 0, r], 1), :], xbuf.at[s, pl.ds(r, 1), :], gsem.at[s])

    def gather_loop(ids, s):
        def body(r, carry):
            gather_row(ids, s, r).start()
            return carry
        lax.fori_loop(0, tm, body, 0)

    def gather_wait(s):
        pltpu.make_async_copy(x_hbm.at[pl.ds(0, tm), :], xbuf.at[s], gsem.at[s]).wait()

    def scatter_rows(s, r, row, n):
        return pltpu.make_async_copy(ybuf.at[s, pl.ds(r, n), :], o_hbm.at[pl.ds(row, n), :], ssem.at[s])

    def scatter_wait(s, n):
        def body_many(r, carry):
            scatter_rows(s, 0, 0, WAIT_ROWS).wait()
            return carry
        lax.fori_loop(0, n // WAIT_ROWS, body_many, 0)

        def body_one(r, carry):
            scatter_rows(s, 0, 0, 1).wait()
            return carry
        lax.fori_loop(0, lax.rem(n, WAIT_ROWS), body_one, 0)

    @pl.when(t == 0)
    def _():
        gather_loop(idx_ref, 0)

    @pl.when(t >= 2)
    def _():
        scatter_wait(slot, nv_ref[jnp.maximum(t - 2, 0)])

    gather_wait(slot)

    @pl.when(nv > 0)
    def _():
        x = xbuf[slot]
        xr = x[:, 0:d]
        rec = x[:, d:d + LANES]
        h = _rms(xr, gffn_ref[...]).astype(BF16)
        for r in range(tm):
            gather_row(idxn_ref, nslot, r).start()
        lane = lax.broadcasted_iota(jnp.int32, rec.shape, 1)
        pre = [(_dot(h, wg[0]), _dot(h, wu[0])) for wg, wu in ((wga_ref, wua_ref), (wgb_ref, wub_ref))]
        acc = None
        for (a, b), e, wd in zip(pre, (ea_ref[t], eb_ref[t]), (wda_ref, wdb_ref)):
            cw = jnp.sum(jnp.where(lane == e + R_OFF, rec, 0.0), axis=1, keepdims=True)
            hid = a * jax.nn.sigmoid(a) * b * cw
            out = _dot(hid.astype(BF16), wd[0])
            acc = out if acc is None else acc + out
        ybuf[slot] = _rms(xr + acc, gfin_ref[...])

        def body_many(i, carry):
            for k in range(WAIT_ROWS):
                r = i * WAIT_ROWS + k
                scatter_rows(slot, r, idx_ref[0, 0, r], 1).start()
            return carry
        lax.fori_loop(0, nv // WAIT_ROWS, body_many, 0)

        def body_one(r, carry):
            scatter_rows(slot, r, idx_ref[0, 0, r], 1).start()
            return carry
        lax.fori_loop((nv // WAIT_ROWS) * WAIT_ROWS, nv, body_one, 0)

    @pl.when(nv <= 0)
    def _():
        gather_loop(idxn_ref, nslot)

    @pl.when(t == nt - 1)
    def _():
        gather_wait(nslot)

        @pl.when(t >= 1)
        def _():
            scatter_wait(nslot, nv_ref[jnp.maximum(t - 1, 0)])
        scatter_wait(slot, nv)


def _moe_tile_rows(n):
    n_pairs = N_GROUPS * (EXPERTS_PER_GROUP * (EXPERTS_PER_GROUP - 1) // 2)
    mean = n / n_pairs
    rows = (mean + 3.0 * math.sqrt(mean)) / 2
    return int(min(256, max(SUB_BF16, -(-rows // SUB_BF16) * SUB_BF16)))


def _moe_sparse(x, g_ffn, w_gate, w_up, w_down, g_final, *, tm):
    n, de = x.shape
    d = de - LANES
    n_pairs = N_GROUPS * (EXPERTS_PER_GROUP * (EXPERTS_PER_GROUP - 1) // 2)
    nt = -(-n // tm) + n_pairs
    n_keys = N_EXPERTS * N_EXPERTS

    ids = x[:, d:d + 2].astype(jnp.int32)
    key = ids[:, 0] * N_EXPERTS + ids[:, 1]
    order = jnp.argsort(key).astype(jnp.int32)
    skey = key[order]
    count_below = lambda sorted_vals, q: jnp.sum((sorted_vals[None, :] < q[:, None]).astype(jnp.int32), axis=1)
    edges = count_below(skey, jnp.arange(n_keys + 1, dtype=jnp.int32))
    starts, ends = edges[:-1], edges[1:]
    tiles = (ends - starts + tm - 1) // tm
    cum = jnp.cumsum(tiles)
    total = cum[-1]
    tt = jnp.arange(nt, dtype=jnp.int32)
    tc = jnp.minimum(tt, total - 1)
    cls = count_below(cum, tc + 1)
    first = cum[cls] - tiles[cls]
    tstart = starts[cls] + (tc - first) * tm
    nv = jnp.where(tt < total, jnp.clip(ends[cls] - tstart, 0, tm), 0).astype(jnp.int32)
    ea = (cls // N_EXPERTS).astype(jnp.int32)
    eb = (cls % N_EXPERTS).astype(jnp.int32)
    rows = jnp.clip(tstart[:, None] + jnp.arange(tm, dtype=jnp.int32)[None, :], 0, n - 1)
    idx = order[rows].reshape(nt, 1, tm)

    wspec_a = lambda shape: pl.BlockSpec(shape, lambda t, ea, eb, nv: (ea[t], 0, 0))
    wspec_b = lambda shape: pl.BlockSpec(shape, lambda t, ea, eb, nv: (eb[t], 0, 0))
    cspec = lambda shape: pl.BlockSpec(shape, lambda t, ea, eb, nv: (0,) * len(shape),
                                       pipeline_mode=pl.Buffered(1))
    grid_spec = pltpu.PrefetchScalarGridSpec(
        num_scalar_prefetch=3,
        grid=(nt,),
        in_specs=[pl.BlockSpec((1, 1, tm), lambda t, ea, eb, nv: (t, 0, 0), memory_space=pltpu.SMEM),
                  pl.BlockSpec((1, 1, tm), lambda t, ea, eb, nv: (jnp.minimum(t + 1, nt - 1), 0, 0),
                               memory_space=pltpu.SMEM),
                  pl.BlockSpec(memory_space=pl.ANY),
                  cspec((1, d)), cspec((1, d)),
                  wspec_a((1, d, D_EXPERT)), wspec_a((1, d, D_EXPERT)), wspec_a((1, D_EXPERT, d)),
                  wspec_b((1, d, D_EXPERT)), wspec_b((1, d, D_EXPERT)), wspec_b((1, D_EXPERT, d))],
        out_specs=pl.BlockSpec(memory_space=pl.ANY),
        scratch_shapes=[pltpu.VMEM((2, tm, de), F32),
                        pltpu.VMEM((2, tm, d), F32),
                        pltpu.SemaphoreType.DMA((2,)),
                        pltpu.SemaphoreType.DMA((2,))])
    return pl.pallas_call(
        functools.partial(_moe_sparse_kernel, tm=tm),
        grid_spec=grid_spec,
        out_shape=jax.ShapeDtypeStruct((n, d), F32),
        compiler_params=pltpu.CompilerParams(dimension_semantics=("arbitrary",),
                                             vmem_limit_bytes=VMEM_LIMIT),
        name="moe_sparse",
    )(ea, eb, nv, idx, idx, x, g_ffn.reshape(1, d), g_final.reshape(1, d),
      w_gate, w_up, w_down, w_gate, w_up, w_down)


C_Q = 0
C_QI = C_Q + D_ATTN
C_K = C_QI + N_IDX_HEADS * IDX_DIM
C_V = C_K + D_KV
C_KIWI = C_V + D_KV
C_KIE = C_KIWI + LANES
C_KIO = C_KIE + LANES
C_U = C_KIO + LANES
C_END = C_U + D_POOL


def _layout_w_in(w_in):
    d = w_in.shape[0]
    offs = np.cumsum((0, D_ATTN, D_KV, D_KV, N_IDX_HEADS * IDX_DIM, IDX_DIM, N_IDX_HEADS, D_POOL))
    wq, wk, wv, wqi, wki, wwi, wu = (w_in[:, offs[n]:offs[n + 1]] for n in range(7))
    z = lambda n: jnp.zeros((d, n), w_in.dtype)
    cat = jnp.concatenate([wq, wqi, wk, wv,
                           wki, wwi, z(LANES - IDX_DIM - N_IDX_HEADS),
                           wki, z(LANES - IDX_DIM),
                           z(LANES - IDX_DIM), wki,
                           wu], axis=1)
    assert cat.shape[1] == C_END
    return cat.astype(BF16)


IN_GROUPS = ((C_Q, C_QI, 1.0, 1),
             (C_QI, C_K, 1.0, 1),
             (C_K, C_V, 1.0, 2),
             (C_V, C_KIWI, 1.0, 2),
             (C_KIWI, C_KIE, 1.0, 1),
             (C_KIE, C_KIO, 1.0, 1),
             (C_KIO, C_U, 1.0, 1),
             (C_U, C_END, 1.0, 1))
IN_DTYPES = (BF16, BF16, F32, BF16, F32, BF16, F32, BF16, BF16, F32)


def _pad_rows(a, n):
    return jnp.pad(a, ((0, 0), (0, n - a.shape[1]), (0, 0)))


def _layer(x, pos0, caches, pool_hist, mk, mv, rel_bias, W, *, tm_in, tq, tm_mix, tm_moe, sparse_moe):
    B, T, D = x.shape
    n = B * T
    q, qi, k, kb, v, vb, kiwi, kie, kio, u = _norm_matmul(
        x.reshape(n, D), W["g_mix"], W["w_in"], IN_GROUPS, IN_DTYPES, tm_in)
    r3 = lambda a: a.reshape(B, T, a.shape[-1])
    kblk = 256
    if caches is None:
        n_keys = T
        kb3, vb3, kie3, kio3 = r3(kb), r3(vb), r3(kie), r3(kio)
    else:
        k_hist, v_hist, ki_hist = caches
        past = k_hist.shape[1]
        n_keys = past + T
        lp = -(-n_keys // kblk) * kblk
        ki = r3(kiwi)[:, :, :IDX_DIM]
        ki_all = jnp.concatenate([ki_hist, ki], axis=1)
        zeros = jnp.zeros_like(ki_all)
        kb3 = _pad_rows(jnp.concatenate([k_hist.reshape(B, past, D_KV), r3(k)], axis=1), lp).astype(BF16)
        vb3 = _pad_rows(jnp.concatenate([v_hist.reshape(B, past, D_KV), r3(v)], axis=1), lp).astype(BF16)
        kie3 = _pad_rows(jnp.concatenate([ki_all, zeros], axis=2), lp).astype(BF16)
        kio3 = _pad_rows(jnp.concatenate([zeros, ki_all], axis=2), lp).astype(BF16)
    a_out = _dsa(r3(q), r3(qi), r3(kiwi), kb3, vb3, kie3, kio3, rel_bias,
                 n_keys=n_keys, pos0=pos0, tq=tq, kblk=kblk)
    x2 = _mix(x, a_out, r3(u), pool_hist, mk.astype(BF16), mv.astype(BF16), W["w_pool"], W["pool_scale"],
              W["w_out"], W["g_x"], W["w_xq"], W["w_xo"], W["g_ffn"], W["w_r"], W["b_r"], pos0=pos0, tm=tm_mix)
    moe = _moe_sparse if sparse_moe else _moe
    y = moe(x2.reshape(n, D + LANES), W["g_ffn"], W["w_gate"], W["w_up"], W["w_down"], W["g_final"], tm=tm_moe)
    new_pool = jnp.concatenate([pool_hist, r3(u)], axis=1)[:, -POOL_HIST:]
    return (y.reshape(B, T, D), k.reshape(B, T, N_KV_A, HEAD_DIM_A), v.reshape(B, T, N_KV_A, HEAD_DIM_A),
            r3(kiwi)[:, :, :IDX_DIM], new_pool)


def kernel(x_prompt, x_sample, mem_prompt, cache_k, cache_v, cache_k_idx, cache_pool, cache_mem_k, cache_mem_v, rel_bias, g_mix, w_in, w_pool, pool_scale, w_out, g_mem, w_mk, w_mv, g_x, w_xq, w_xo, g_ffn, w_rg, b_rg, w_re, b_re, w_gate, w_up, w_down, g_final):
    depth = g_mix.shape[0]
    assert depth == 1
    l = 0
    B, T, D = x_prompt.shape
    Bs, Ts, _ = x_sample.shape
    n_mem = mem_prompt.shape[1]
    past = cache_k.shape[2]

    w_r = jnp.concatenate([w_rg[l], w_re[l], jnp.zeros((D, LANES - N_GROUPS - N_EXPERTS), F32)], axis=1)
    W = dict(
        g_mix=g_mix[l], w_in=_layout_w_in(w_in[l]), w_pool=w_pool[l].astype(BF16), pool_scale=pool_scale[l],
        w_out=w_out[l].astype(BF16), g_x=g_x[l], w_xq=w_xq[l].astype(BF16), w_xo=w_xo[l].astype(BF16),
        g_ffn=g_ffn[l], w_r=w_r.astype(BF16),
        b_r=jnp.concatenate([b_rg[l], b_re[l], jnp.zeros((LANES - N_GROUPS - N_EXPERTS,), F32)]).reshape(1, LANES),
        w_gate=w_gate[l].astype(BF16), w_up=w_up[l].astype(BF16), w_down=w_down[l].astype(BF16),
        g_final=g_final)

    w_m = jnp.concatenate([w_mk[l], w_mv[l]], axis=1).astype(BF16)
    mk, mv = _norm_matmul(mem_prompt.reshape(B * n_mem, D), g_mem[l], w_m,
                          ((0, D_X, 1.0, 1), (D_X, 2 * D_X, 1.0, 1)), (F32, F32), 256)
    mk = mk.reshape(B, n_mem, D_X)
    mv = mv.reshape(B, n_mem, D_X)

    yp, kp, vp, kip, pp = _layer(x_prompt, 0, None, jnp.zeros((B, POOL_HIST, D_POOL), F32), mk, mv, rel_bias, W,
                                 tm_in=256, tq=256, tm_mix=256, tm_moe=_moe_tile_rows(B * T), sparse_moe=True)
    ys, ks, vs, kis, ps = _layer(x_sample, past,
                                 (cache_k[l], cache_v[l], cache_k_idx[l]), cache_pool[l],
                                 cache_mem_k[l].reshape(Bs, n_mem, D_X), cache_mem_v[l].reshape(Bs, n_mem, D_X),
                                 rel_bias, W, tm_in=Bs * Ts, tq=Ts, tm_mix=Ts, tm_moe=Bs * Ts, sparse_moe=False)
    st = lambda a: a[None]
    return (yp, ys, st(kp), st(vp), st(kip), st(pp),
            st(mk.reshape(B, n_mem, N_XHEADS, XHEAD_DIM)), st(mv.reshape(B, n_mem, N_XHEADS, XHEAD_DIM)),
            st(ks), st(vs), st(kis), st(ps))
```

```python
import functools
import math

import numpy as np
import jax
import jax.numpy as jnp
from jax import lax
from jax.experimental import pallas as pl
from jax.experimental.pallas import tpu as pltpu

F32 = jnp.float32
BF16 = jnp.bfloat16

CHUNK = 64
N_HEADS_A = 8
HEAD_DIM_A = 128
N_KV_A = 2
D_ATTN = N_HEADS_A * HEAD_DIM_A
D_KV = N_KV_A * HEAD_DIM_A
N_IDX_HEADS = 8
IDX_DIM = 64
TOPK_MAX = 256
POOL_WINDOWS = (2, 4, 8, 16)
POOL_GROUP_DIM = 256
D_POOL = len(POOL_WINDOWS) * POOL_GROUP_DIM
POOL_HIST = max(POOL_WINDOWS) - 1
N_BUCKETS = 32
MAX_DISTANCE = 128
N_XHEADS = 4
XHEAD_DIM = 128
D_X = N_XHEADS * XHEAD_DIM
N_GROUPS = 4
EXPERTS_PER_GROUP = 8
N_EXPERTS = N_GROUPS * EXPERTS_PER_GROUP
D_EXPERT = 256
EPS = 1e-6

LANES = 128
INT_MIN = -(2 ** 31)
NEG = -1e30
LOG2E = math.log2(math.e)
VMEM_LIMIT = 48 * 1024 * 1024


def _rms(x, g):
    ms = jnp.mean(x * x, axis=-1, keepdims=True)
    return x * lax.rsqrt(ms + EPS) * g


def _dot(a, b):
    return jnp.dot(a, b, preferred_element_type=F32)


def _dot_nt(a, b):
    return lax.dot_general(a, b, (((1,), (1,)), ((), ())), preferred_element_type=F32)


def _const_spec(shape):
    nd = len(shape)
    return pl.BlockSpec(shape, lambda *_: (0,) * nd, pipeline_mode=pl.Buffered(1))


def _norm_matmul_kernel(x_ref, g_ref, w_ref, *out_refs, groups):
    h = _rms(x_ref[...], g_ref[...]).astype(BF16)
    k = 0
    for c0, c1, scale, n_out in groups:
        acc = _dot(h, w_ref[:, c0:c1])
        if scale != 1.0:
            acc = acc * scale
        for _ in range(n_out):
            out_refs[k][...] = acc.astype(out_refs[k].dtype)
            k += 1


def _norm_matmul(x, g, w, groups, out_dtypes, tm):
    n, d = x.shape
    assert n % tm == 0
    widths = []
    for c0, c1, _, n_out in groups:
        widths += [c1 - c0] * n_out
    return pl.pallas_call(
        functools.partial(_norm_matmul_kernel, groups=tuple(groups)),
        grid=(n // tm,),
        in_specs=[pl.BlockSpec((tm, d), lambda i: (i, 0)),
                  _const_spec((1, d)),
                  _const_spec(w.shape)],
        out_specs=[pl.BlockSpec((tm, wd), lambda i: (i, 0)) for wd in widths],
        out_shape=[jax.ShapeDtypeStruct((n, wd), dt) for wd, dt in zip(widths, out_dtypes)],
        compiler_params=pltpu.CompilerParams(dimension_semantics=("arbitrary",),
                                             vmem_limit_bytes=VMEM_LIMIT),
        name="norm_matmul",
    )(x, g.reshape(1, d), w)


def _bias_table_kernel(relb_ref, bucket_ref, o_ref, *, shift_bucket):
    bucket = bucket_ref[...]
    for h in range(N_HEADS_A):
        acc = jnp.zeros(bucket.shape, F32)
        for b in range(N_BUCKETS):
            acc = jnp.where(bucket == b, relb_ref[b, h], acc)
        if shift_bucket is None:
            o_ref[h] = acc * LOG2E
        else:
            o_ref[h] = (acc - relb_ref[shift_bucket, h]) * (HEAD_DIM_A ** 0.5)


def _bias_table(rel_bias, bucket, shift_bucket=None):
    tq, w = bucket.shape
    return pl.pallas_call(
        functools.partial(_bias_table_kernel, shift_bucket=shift_bucket),
        in_specs=[pl.BlockSpec(memory_space=pltpu.SMEM),
                  pl.BlockSpec((tq, w), lambda: (0, 0))],
        out_specs=pl.BlockSpec((N_HEADS_A, tq, w), lambda: (0, 0, 0)),
        out_shape=jax.ShapeDtypeStruct((N_HEADS_A, tq, w), F32),
        name="bias_table",
    )(rel_bias, bucket)


def _t5_bucket(rel):
    nb = N_BUCKETS // 2
    max_exact = nb // 2
    bucket = (rel > 0).astype(jnp.int32) * nb
    n = jnp.abs(rel)
    nf = jnp.maximum(n, 1).astype(F32)
    large = max_exact + (jnp.log(nf / max_exact) / math.log(MAX_DISTANCE / max_exact)
                         * (nb - max_exact)).astype(jnp.int32)
    large = jnp.minimum(large, nb - 1)
    return bucket + jnp.where(n < max_exact, n, large)


def _far_bucket(min_dist, max_dist):
    nb = N_BUCKETS // 2
    max_exact = nb // 2
    n = np.arange(min_dist, max_dist + 1, dtype=np.float64)
    large = max_exact + np.floor(np.log(n / max_exact) / math.log(MAX_DISTANCE / max_exact)
                                 * (nb - max_exact) * (1 - 1e-6)).astype(np.int64)
    assert large.min() >= nb - 1, "far key blocks must sit in the saturated distance bucket"
    return nb - 1


def _dsa_kernel(relb_ref, q_ref, qi_ref, kiwi_ref, k_ref, v_ref, kie_ref, kio_ref, bnear_ref, vis_ref,
                o_ref, keys, madd, wb, qs, xcut, mrun, lrun, acc_s,
                *, TQ, KB, RB, kb0, kstep, topk, far_bucket, tie_bits):
    i = pl.program_id(1)
    nkb = kb0 + kstep * i
    n_far = jnp.maximum(nkb - 2, 0)
    n_lc = KB // LANES
    HPG = N_HEADS_A // N_KV_A

    kiwi = kiwi_ref[0]
    for h in range(N_IDX_HEADS):
        w = (kiwi[:, IDX_DIM + h:IDX_DIM + h + 1] * (N_IDX_HEADS ** -0.5)).astype(BF16).astype(F32)
        wb[h] = jnp.broadcast_to(w * (IDX_DIM ** -0.5), (TQ, LANES))

    def idx_block(j, near_col):
        c0 = pl.multiple_of(j * KB, KB)
        ke = kie_ref[0, pl.ds(c0, KB), :]
        ko = kio_ref[0, pl.ds(c0, KB), :]
        acc = [jnp.zeros((TQ, LANES), F32) for _ in range(n_lc)]
        for hp in range(N_IDX_HEADS // 2):
            qh = qi_ref[0, :, hp * LANES:(hp + 1) * LANES]
            se = _dot_nt(qh, ke)
            so = _dot_nt(qh, ko)
            we = wb[2 * hp]
            wo = wb[2 * hp + 1]
            for c in range(n_lc):
                cs = slice(c * LANES, (c + 1) * LANES)
                re = jnp.maximum(se[:, cs], 0.0).astype(BF16).astype(F32)
                ro = jnp.maximum(so[:, cs], 0.0).astype(BF16).astype(F32)
                acc[c] = acc[c] + we * re + wo * ro
        for c in range(n_lc):
            bits = lax.bitcast_convert_type(acc[c], jnp.int32)
            key = bits ^ ((bits >> 31) & jnp.int32(0x7FFFFFFF))
            if near_col is not None:
                vis = vis_ref[:, near_col + c * LANES:near_col + (c + 1) * LANES]
                key = jnp.where(vis > 0.0, key, jnp.int32(INT_MIN))
            keys[j, :, c * LANES:(c + 1) * LANES] = key

    def far_idx(j, carry):
        idx_block(j, None)
        return carry

    lax.fori_loop(0, n_far, far_idx, 0)

    @pl.when(nkb >= 2)
    def _():
        idx_block(nkb - 2, 0)

    idx_block(nkb - 1, KB)

    kf = float(topk)
    lane_f = lax.broadcasted_iota(jnp.int32, (RB, LANES), 1).astype(F32)
    NRB = TQ // RB
    rbs = [slice(rb * RB, (rb + 1) * RB) for rb in range(NRB)]

    def count(pred):
        def body(jj, accs):
            out = []
            for rb in range(NRB):
                blk = keys[jj, rbs[rb], :]
                acc = accs[rb]
                for c in range(n_lc):
                    acc = acc + jnp.where(pred(rb, blk[:, c * LANES:(c + 1) * LANES], jj, c), 1.0, 0.0)
                out.append(acc)
            return tuple(out)
        accs = lax.fori_loop(0, nkb, body, tuple(jnp.zeros((RB, LANES), F32) for _ in range(NRB)))
        return [jnp.sum(acc, axis=1, keepdims=True) for acc in accs]

    def count_ge(cands):
        cb = [jnp.broadcast_to(cand, (RB, LANES)) for cand in cands]
        return count(lambda rb, kblk, jj, c: kblk >= cb[rb])

    def bis_body(p, thrs):
        cands = [thr + lax.shift_left(jnp.int32(1), 31 - p) for thr in thrs]
        cnts = count_ge(cands)
        return tuple(jnp.where(cnt >= kf, cand, thr) for cnt, cand, thr in zip(cnts, cands, thrs))

    thrs = lax.fori_loop(0, 32, bis_body, tuple(jnp.full((RB, 1), INT_MIN, jnp.int32) for _ in range(NRB)))
    thrs = [jnp.maximum(thr, jnp.int32(INT_MIN + 1)) for thr in thrs]
    cnt_ge = count_ge(thrs)
    cnt_gt = count_ge([thr + 1 for thr in thrs])
    needs = [kf - c for c in cnt_gt]
    thrb = [jnp.broadcast_to(thr, (RB, LANES)) for thr in thrs]

    xcut[...] = jnp.full(xcut.shape, 2.0 ** 24, F32)
    excess = cnt_ge[0] - kf
    for c in cnt_ge[1:]:
        excess = jnp.maximum(excess, c - kf)

    @pl.when(jnp.max(excess) > 0.0)
    def _():
        def count_eq_lt(xs):
            xb = [jnp.broadcast_to(x, (RB, LANES)) for x in xs]

            def pred(rb, kblk, jj, c):
                col = lane_f + (jj * KB + c * LANES).astype(F32)
                return jnp.where(kblk == thrb[rb], col, 2.0 ** 25) < xb[rb]
            return count(pred)

        def tie_body(p, xs):
            cands = [x + lax.shift_left(jnp.int32(1), tie_bits - 1 - p).astype(F32) for x in xs]
            cnts = count_eq_lt(cands)
            return tuple(jnp.where(cnt < need, cand, x) for cnt, need, cand, x in zip(cnts, needs, cands, xs))

        xs = lax.fori_loop(0, tie_bits, tie_body, tuple(jnp.zeros((RB, 1), F32) for _ in range(NRB)))
        for rb in range(NRB):
            xcut[rbs[rb], :] = xs[rb]

    xb = [jnp.broadcast_to(xcut[rbs[rb], :], (RB, LANES)) for rb in range(NRB)]

    def mask_body(jj, carry):
        for rb in range(NRB):
            blk = keys[jj, rbs[rb], :]
            for c in range(n_lc):
                kblk = blk[:, c * LANES:(c + 1) * LANES]
                col = lane_f + (jj * KB + c * LANES).astype(F32)
                keep = (kblk > thrb[rb]) | ((kblk == thrb[rb]) & (col <= xb[rb]))
                madd[jj, rbs[rb], c * LANES:(c + 1) * LANES] = jnp.where(keep, 0.0, NEG)
        return carry

    lax.fori_loop(0, nkb, mask_body, 0)

    _attend(relb_ref, q_ref, k_ref, v_ref, bnear_ref, o_ref, madd, qs, mrun, lrun, acc_s,
            TQ=TQ, KB=KB, nkb=nkb, n_far=n_far, far_bucket=far_bucket)


def _attend(relb_ref, q_ref, k_ref, v_ref, bnear_ref, o_ref, madd, qs, mrun, lrun, acc_s,
            *, TQ, KB, nkb, n_far, far_bucket):
    n_lc = KB // LANES
    HPG = N_HEADS_A // N_KV_A
    for g in range(N_KV_A):
        for r in range(HPG):
            h = g * HPG + r
            qs[g, r * TQ:(r + 1) * TQ, :] = q_ref[0, :, h * HEAD_DIM_A:(h + 1) * HEAD_DIM_A]

    def logits(j, near_col, g):
        c0 = pl.multiple_of(j * KB, KB)
        kg = k_ref[0, pl.ds(c0, KB), g * HEAD_DIM_A:(g + 1) * HEAD_DIM_A]
        s = _dot_nt(qs[g], kg)
        mk = madd[j]
        out = []
        for r in range(HPG):
            h = g * HPG + r
            if near_col is None:
                bias = relb_ref[far_bucket, h] * LOG2E
            else:
                bias = bnear_ref[h, :, near_col:near_col + KB]
            out.append(s[r * TQ:(r + 1) * TQ] * (HEAD_DIM_A ** -0.5 * LOG2E) + bias + mk)
        return out

    def over_blocks(fn):
        def far(j, carry):
            fn(j, None)
            return carry

        lax.fori_loop(0, n_far, far, 0)

        @pl.when(nkb >= 2)
        def _():
            fn(nkb - 2, 0)

        fn(nkb - 1, KB)

    mrun[...] = jnp.full(mrun.shape, NEG, F32)

    def max_block(j, near_col):
        for g in range(N_KV_A):
            for r, sh in enumerate(logits(j, near_col, g)):
                rs = slice(r * TQ, (r + 1) * TQ)
                m = mrun[g, rs, :]
                for c in range(n_lc):
                    m = jnp.maximum(m, sh[:, c * LANES:(c + 1) * LANES])
                mrun[g, rs, :] = m

    over_blocks(max_block)

    for g in range(N_KV_A):
        mrun[g] = jnp.broadcast_to(jnp.max(mrun[g], axis=1, keepdims=True), mrun.shape[1:])
    lrun[...] = jnp.zeros(lrun.shape, F32)
    acc_s[...] = jnp.zeros(acc_s.shape, F32)

    def pv_block(j, near_col):
        c0 = pl.multiple_of(j * KB, KB)
        for g in range(N_KV_A):
            vg = v_ref[0, pl.ds(c0, KB), g * HEAD_DIM_A:(g + 1) * HEAD_DIM_A]
            parts = []
            for r, sh in enumerate(logits(j, near_col, g)):
                rs = slice(r * TQ, (r + 1) * TQ)
                mb = mrun[g, rs, :]
                l = lrun[g, rs, :]
                pcs = []
                for c in range(n_lc):
                    p = jnp.exp2(sh[:, c * LANES:(c + 1) * LANES] - mb)
                    l = l + p
                    pcs.append(p.astype(BF16))
                lrun[g, rs, :] = l
                parts.append(jnp.concatenate(pcs, axis=1))
            acc_s[g] += _dot(jnp.concatenate(parts, axis=0), vg)

    over_blocks(pv_block)

    for g in range(N_KV_A):
        out = acc_s[g] / jnp.sum(lrun[g], axis=1, keepdims=True)
        for r in range(HPG):
            h = g * HPG + r
            o_ref[0, :, h * HEAD_DIM_A:(h + 1) * HEAD_DIM_A] = out[r * TQ:(r + 1) * TQ].astype(o_ref.dtype)


DIGIT_BITS = 8
N_DIGITS = 32 // DIGIT_BITS
SUB_BF16 = 16


def _dsa_kernel_t(q_ref, qi_ref, kiwi_ref, k_ref, vt_ref, kie_ref, kio_ref, bnear_ref, vist_ref,
                  o_ref, keys, dig, madd, qs, xcut, acc_s, tbuf,
                  *, TQ, KB, kb0, kstep, topk, tie_bits):
    i = pl.program_id(1)
    nkb = kb0 + kstep * i
    n_far = jnp.maximum(nkb - 2, 0)
    n_slab = KB // SUB_BF16

    w_t = kiwi_ref[0].T
    wrow = [(w_t[IDX_DIM + h:IDX_DIM + h + 1, :] * (N_IDX_HEADS ** -0.5)).astype(BF16).astype(F32)
            * (IDX_DIM ** -0.5) for h in range(N_IDX_HEADS)]

    def idx_dots(j):
        c0 = pl.multiple_of(j * KB, KB)
        ke = kie_ref[0, pl.ds(c0, KB), :]
        ko = kio_ref[0, pl.ds(c0, KB), :]
        raw = []
        for hp in range(N_IDX_HEADS // 2):
            qh = qi_ref[0, :, hp * LANES:(hp + 1) * LANES]
            raw.append((_dot_nt(ke, qh), _dot_nt(ko, qh)))
        return raw

    def idx_finish(j, raw, near_row):
        acc = jnp.zeros((KB, TQ), F32)
        for hp, (se, so) in enumerate(raw):
            re = jnp.maximum(se.astype(BF16), 0.0).astype(F32)
            ro = jnp.maximum(so.astype(BF16), 0.0).astype(F32)
            acc = acc + wrow[2 * hp] * re + wrow[2 * hp + 1] * ro
        bits = lax.bitcast_convert_type(acc, jnp.int32)
        key = bits ^ ((bits >> 31) & jnp.int32(0x7FFFFFFF))
        if near_row is not None:
            key = jnp.where(vist_ref[near_row:near_row + KB, :] > 0.0, key, jnp.int32(INT_MIN))
        keys[j] = key
        for dk in range(N_DIGITS):
            sh = 32 - DIGIT_BITS * (dk + 1)
            d = (key >> sh) if dk == 0 else ((key >> sh) & jnp.int32(2 ** DIGIT_BITS - 1))
            dig[dk, j] = d.astype(F32).astype(BF16)

    def idx_blocks(js, near_rows):
        raws = [idx_dots(j) for j in js]
        for j, raw, near_row in zip(js, raws, near_rows):
            idx_finish(j, raw, near_row)

    def far_idx_pair(jp, carry):
        idx_blocks([2 * jp, 2 * jp + 1], [None, None])
        return carry

    lax.fori_loop(0, n_far // 2, far_idx_pair, 0)

    @pl.when(lax.rem(n_far, 2) == 1)
    def _():
        idx_blocks([n_far - 1], [None])

    @pl.when(nkb >= 2)
    def _():
        idx_blocks([nkb - 2, nkb - 1], [0, KB])

    @pl.when(nkb < 2)
    def _():
        idx_blocks([nkb - 1], [KB])

    one = jnp.ones((SUB_BF16, TQ), BF16)
    zero = jnp.zeros((SUB_BF16, TQ), BF16)

    def count_ge(dk, cand):
        cb = jnp.broadcast_to(cand.astype(BF16), (SUB_BF16, TQ))

        def block_count(jj):
            blk = dig[dk, jj]
            parts = [jnp.where(blk[s * SUB_BF16:(s + 1) * SUB_BF16, :] >= cb, one, zero) for s in range(n_slab)]
            while len(parts) > 1:
                parts = [parts[n] + parts[n + 1] for n in range(0, len(parts), 2)]
            return parts[0]

        def body_pair(jp, tot):
            return tot + (block_count(2 * jp) + block_count(2 * jp + 1)).astype(F32)

        def body_one(jj, tot):
            return tot + block_count(jj).astype(F32)

        tot = lax.fori_loop(0, nkb // 2, body_pair, jnp.zeros((SUB_BF16, TQ), F32))
        tot = lax.fori_loop((nkb // 2) * 2, nkb, body_one, tot)
        return jnp.sum(tot, axis=0, keepdims=True)

    kneed = jnp.full((1, TQ), float(topk), F32)
    thr = jnp.zeros((1, TQ), jnp.int32)
    cnt_eq = None
    for dk in range(N_DIGITS):
        lo = -float(2 ** (DIGIT_BITS - 1)) if dk == 0 else 0.0

        def bis_body(p, carry, dk=dk, kneed=kneed):
            t, c_rej = carry
            cand = t + lax.shift_left(jnp.int32(1), DIGIT_BITS - 1 - p).astype(F32)
            cnt = count_ge(dk, cand)
            ok = cnt >= kneed
            return jnp.where(ok, cand, t), jnp.where(ok, c_rej, cnt)

        t, n_gt = lax.fori_loop(0, DIGIT_BITS, bis_body,
                                (jnp.full((1, TQ), lo, F32), jnp.zeros((1, TQ), F32)))
        if dk == N_DIGITS - 1:
            cnt_eq = count_ge(dk, t) - n_gt
        kneed = kneed - n_gt
        thr = thr + lax.shift_left(t.astype(jnp.int32), 32 - DIGIT_BITS * (dk + 1))
        if dk + 1 < N_DIGITS:
            tb = t.astype(BF16)

            def prep(jj, carry, dk=dk, tb=tb):
                dig[dk + 1, jj] = jnp.where(dig[dk, jj] == tb, dig[dk + 1, jj], jnp.asarray(-1.0, BF16))
                return carry

            lax.fori_loop(0, nkb, prep, 0)

    thr = jnp.maximum(thr, jnp.int32(INT_MIN + 1))
    need = kneed
    sub_f = lax.broadcasted_iota(jnp.int32, (KB, TQ), 0).astype(F32)

    xcut[...] = jnp.full(xcut.shape, 2.0 ** 24, F32)

    @pl.when(jnp.max(cnt_eq - need) > 0.0)
    def _():
        def count_eq_lt(x):
            def body(jj, tot):
                col = sub_f + (jj * KB).astype(F32)
                hit = jnp.where(jnp.where(keys[jj] == thr, col, 2.0 ** 25) < x, 1.0, 0.0)
                for s in range(KB // 8):
                    tot = tot + hit[s * 8:(s + 1) * 8, :]
                return tot
            tot = lax.fori_loop(0, nkb, body, jnp.zeros((8, TQ), F32))
            return jnp.sum(tot, axis=0, keepdims=True)

        def tie_body(p, x):
            cand = x + lax.shift_left(jnp.int32(1), tie_bits - 1 - p).astype(F32)
            return jnp.where(count_eq_lt(cand) < need, cand, x)

        x = lax.fori_loop(0, tie_bits, tie_body, jnp.zeros((1, TQ), F32))
        xcut[...] = jnp.broadcast_to(x, xcut.shape)

    xc = xcut[0:1, :]

    def mask_body(jj, carry):
        kblk = keys[jj]
        col = sub_f + (jj * KB).astype(F32)
        at_thr = jnp.where(col <= xc, 0.0, NEG)
        madd[jj] = jnp.where(kblk > thr, 0.0, jnp.where(kblk == thr, at_thr, NEG))
        return carry

    lax.fori_loop(0, nkb, mask_body, 0)

    HPG = N_HEADS_A // N_KV_A
    c2 = HEAD_DIM_A ** -0.5 * LOG2E
    for g in range(N_KV_A):
        for r in range(HPG):
            h = g * HPG + r
            qs[g, r * TQ:(r + 1) * TQ, :] = q_ref[0, :, h * HEAD_DIM_A:(h + 1) * HEAD_DIM_A]
    acc_s[...] = jnp.zeros(acc_s.shape, F32)

    def attn_blocks(js, near_rows, carry):
        ms, ls = carry
        s_all = []
        for j in js:
            c0 = pl.multiple_of(j * KB, KB)
            s_all.append([_dot_nt(k_ref[0, pl.ds(c0, KB), g * HEAD_DIM_A:(g + 1) * HEAD_DIM_A], qs[g])
                          for g in range(N_KV_A)])
        for b, (j, near_row) in enumerate(zip(js, near_rows)):
            mk = madd[j]
            m_news = []
            for g in range(N_KV_A):
                for r in range(HPG):
                    h = g * HPG + r
                    t = s_all[b][g][:, r * TQ:(r + 1) * TQ] + mk
                    if near_row is not None:
                        t = t + bnear_ref[h, near_row:near_row + KB, :]
                    tbuf[b * N_HEADS_A + h] = t
                    m_news.append(jnp.maximum(ms[h], jnp.max(t, axis=0, keepdims=True)))
            ms_out, ls_out = [], []
            for g in range(N_KV_A):
                vt = vt_ref[0, j, g * HEAD_DIM_A:(g + 1) * HEAD_DIM_A, :]
                for r in range(HPG):
                    h = g * HPG + r
                    cs = slice(r * TQ, (r + 1) * TQ)
                    m_new = m_news[h]
                    alpha = jnp.exp2((ms[h] - m_new) * c2)
                    p = jnp.exp2((tbuf[b * N_HEADS_A + h] - m_new) * c2)
                    ls_out.append(alpha * ls[h] + jnp.sum(p, axis=0, keepdims=True))
                    ms_out.append(m_new)
                    acc_s[g, :, cs] = alpha * acc_s[g, :, cs] + _dot(vt, p.astype(BF16))
            ms, ls = tuple(ms_out), tuple(ls_out)
        return ms, ls

    carry = (tuple(jnp.full((1, TQ), NEG, F32) for _ in range(N_HEADS_A)),
             tuple(jnp.zeros((1, TQ), F32) for _ in range(N_HEADS_A)))
    carry = lax.fori_loop(0, n_far // 2, lambda jp, c: attn_blocks([2 * jp, 2 * jp + 1], [None, None], c), carry)
    carry = lax.cond(lax.rem(n_far, 2) == 1, lambda c: attn_blocks([n_far - 1], [None], c), lambda c: c, carry)
    ms, ls = lax.cond(nkb >= 2, lambda c: attn_blocks([nkb - 2, nkb - 1], [0, KB], c),
                      lambda c: attn_blocks([nkb - 1], [KB], c), carry)

    for g in range(N_KV_A):
        for r in range(HPG):
            h = g * HPG + r
            out_t = acc_s[g, :, r * TQ:(r + 1) * TQ] / ls[h]
            o_ref[0, :, h * HEAD_DIM_A:(h + 1) * HEAD_DIM_A] = out_t.T.astype(o_ref.dtype)


def _dsa(q, qi, kiwi, kb, vb, kie, kio, rel_bias, *, n_keys, pos0, tq, kblk):
    B, T, _ = q.shape
    Lp = kb.shape[1]
    assert T % tq == 0 and Lp % kblk == 0 and tq % CHUNK == 0 or T == tq
    n_qt = T // tq
    if n_qt == 1:
        kb0, kstep = Lp // kblk, 0
        assert ((pos0 + tq - 1) // CHUNK + 1) * CHUNK >= n_keys
    else:
        assert pos0 == 0 and tq == kblk and n_keys == Lp == T
        kb0, kstep = 1, 1
    topk = min(TOPK_MAX, n_keys // 4)
    rb = min(128, tq)

    t = jnp.arange(tq, dtype=jnp.int32)[:, None]
    c = jnp.arange(2 * kblk, dtype=jnp.int32)[None, :]
    if n_qt == 1:
        q_abs = pos0 + t
        s_abs = (kb0 - 2) * kblk + c
    else:
        q_abs = kblk + t
        s_abs = c
    vis = ((s_abs // CHUNK) <= (q_abs // CHUNK)) & (s_abs < (n_keys if n_qt == 1 else 2 * kblk))
    far_bucket = _far_bucket(kblk + 1, max(Lp, kblk + 2))

    n_kb = Lp // kblk
    hpg = N_HEADS_A // N_KV_A
    common = dict(TQ=tq, KB=kblk, kb0=kb0, kstep=kstep, topk=topk, far_bucket=far_bucket,
                  tie_bits=int(math.ceil(math.log2(Lp))) + 1)
    attend_scratch = [pltpu.VMEM((N_KV_A, hpg * tq, HEAD_DIM_A), BF16),
                      pltpu.VMEM((N_KV_A, hpg * tq, LANES), F32),
                      pltpu.VMEM((N_KV_A, hpg * tq, LANES), F32),
                      pltpu.VMEM((N_KV_A, hpg * tq, HEAD_DIM_A), F32)]
    if tq % LANES == 0:
        tab_t = _bias_table(rel_bias, _t5_bucket(s_abs - q_abs).T, shift_bucket=far_bucket)
        vt = jnp.swapaxes(vb.reshape(B, n_kb, kblk, D_KV), 2, 3)
        common.pop("far_bucket")
        return pl.pallas_call(
            functools.partial(_dsa_kernel_t, **common),
            grid=(B, n_qt),
            in_specs=[pl.BlockSpec((1, tq, D_ATTN), lambda b, i: (b, i, 0)),
                      pl.BlockSpec((1, tq, N_IDX_HEADS * IDX_DIM), lambda b, i: (b, i, 0)),
                      pl.BlockSpec((1, tq, LANES), lambda b, i: (b, i, 0)),
                      pl.BlockSpec((1, Lp, D_KV), lambda b, i: (b, 0, 0)),
                      pl.BlockSpec((1, n_kb, D_KV, kblk), lambda b, i: (b, 0, 0, 0)),
                      pl.BlockSpec((1, Lp, LANES), lambda b, i: (b, 0, 0)),
                      pl.BlockSpec((1, Lp, LANES), lambda b, i: (b, 0, 0)),
                      _const_spec((N_HEADS_A, 2 * kblk, tq)),
                      _const_spec((2 * kblk, tq))],
            out_specs=pl.BlockSpec((1, tq, D_ATTN), lambda b, i: (b, i, 0)),
            out_shape=jax.ShapeDtypeStruct((B, T, D_ATTN), BF16),
            scratch_shapes=[pltpu.VMEM((n_kb, kblk, tq), jnp.int32),
                            pltpu.VMEM((N_DIGITS, n_kb, kblk, tq), BF16),
                            pltpu.VMEM((n_kb, kblk, tq), F32),
                            pltpu.VMEM((N_KV_A, hpg * tq, HEAD_DIM_A), BF16),
                            pltpu.VMEM((8, tq), F32),
                            pltpu.VMEM((N_KV_A, HEAD_DIM_A, hpg * tq), F32),
                            pltpu.VMEM((2 * N_HEADS_A, kblk, tq), F32)],
            compiler_params=pltpu.CompilerParams(dimension_semantics=("arbitrary", "arbitrary"),
                                                 vmem_limit_bytes=VMEM_LIMIT),
            name="dsa_t",
        )(q, qi, kiwi, kb, vt, kie, kio, tab_t, vis.astype(F32).T)

    bias_near = _bias_table(rel_bias, _t5_bucket(s_abs - q_abs))
    kern = functools.partial(_dsa_kernel, RB=rb, **common)
    return pl.pallas_call(
        kern,
        grid=(B, n_qt),
        in_specs=[pl.BlockSpec(memory_space=pltpu.SMEM),
                  pl.BlockSpec((1, tq, D_ATTN), lambda b, i: (b, i, 0)),
                  pl.BlockSpec((1, tq, N_IDX_HEADS * IDX_DIM), lambda b, i: (b, i, 0)),
                  pl.BlockSpec((1, tq, LANES), lambda b, i: (b, i, 0)),
                  pl.BlockSpec((1, Lp, D_KV), lambda b, i: (b, 0, 0)),
                  pl.BlockSpec((1, Lp, D_KV), lambda b, i: (b, 0, 0)),
                  pl.BlockSpec((1, Lp, LANES), lambda b, i: (b, 0, 0)),
                  pl.BlockSpec((1, Lp, LANES), lambda b, i: (b, 0, 0)),
                  _const_spec((N_HEADS_A, tq, 2 * kblk)),
                  _const_spec((tq, 2 * kblk))],
        out_specs=pl.BlockSpec((1, tq, D_ATTN), lambda b, i: (b, i, 0)),
        out_shape=jax.ShapeDtypeStruct((B, T, D_ATTN), BF16),
        scratch_shapes=[pltpu.VMEM((n_kb, tq, kblk), jnp.int32),
                        pltpu.VMEM((n_kb, tq, kblk), F32),
                        pltpu.VMEM((N_IDX_HEADS, tq, LANES), F32),
                        pltpu.VMEM((N_KV_A, hpg * tq, HEAD_DIM_A), BF16),
                        pltpu.VMEM((tq, 1), F32),
                        pltpu.VMEM((N_KV_A, hpg * tq, LANES), F32),
                        pltpu.VMEM((N_KV_A, hpg * tq, LANES), F32),
                        pltpu.VMEM((N_KV_A, hpg * tq, HEAD_DIM_A), F32)],
        compiler_params=pltpu.CompilerParams(dimension_semantics=("arbitrary", "arbitrary"),
                                             vmem_limit_bytes=VMEM_LIMIT),
        name="dsa",
    )(rel_bias, q, qi, kiwi, kb, vb, kie, kio, bias_near, vis.astype(F32))


def _mix_kernel(x_ref, a_ref, u_ref, hist_ref, mk_ref, mv_ref, wpool_ref, pscale_ref, wout_ref, gx_ref,
                wxq_ref, wxo_ref, gffn_ref, wr_ref, br_ref, o_ref, ext, *, tm, pos0):
    t = pl.program_id(1)
    H = POOL_HIST + 1

    @pl.when(t == 0)
    def _():
        ext[0:H, :] = hist_ref[0]

    @pl.when(t > 0)
    def _():
        ext[0:H, :] = ext[tm:tm + H, :]

    ext[H:H + tm, :] = u_ref[0]

    da = _dot(a_ref[0], wout_ref[0:D_ATTN, :])

    pos = pos0 + t * tm + lax.broadcasted_iota(jnp.int32, (tm, 1), 0)
    pouts = []
    for gi, w in enumerate(POOL_WINDOWS):
        c0, c1 = gi * POOL_GROUP_DIM, (gi + 1) * POOL_GROUP_DIM
        wsum = ext[H:H + tm, c0:c1]
        for s in range(1, w):
            wsum = wsum + ext[H - s:H - s + tm, c0:c1]
        cnt = jnp.minimum(pos + 1, w).astype(F32)
        d = wsum / cnt - u_ref[0, :, c0:c1]
        y = _dot(d.astype(BF16), wpool_ref[gi]) * pscale_ref[:, c0:c1]
        pouts.append(y.astype(BF16))
    pcat = jnp.concatenate(pouts, axis=1)

    D = x_ref.shape[2]
    x1 = x_ref[0] + da + _dot(pcat, wout_ref[D_ATTN:D_ATTN + D_POOL, :])
    h = _rms(x1, gx_ref[...]).astype(BF16)
    qx = _dot(h, wxq_ref[...])
    heads = [slice(hh * XHEAD_DIM, (hh + 1) * XHEAD_DIM) for hh in range(N_XHEADS)]
    all_logits = [_dot_nt(qx[:, hs].astype(BF16), mk_ref[0, :, hs]) for hs in heads]
    outs = []
    for hs, logits in zip(heads, all_logits):
        logits = logits * (XHEAD_DIM ** -0.5)
        m = jnp.max(logits, axis=1, keepdims=True)
        p = jnp.exp(logits - m)
        l = jnp.sum(p, axis=1, keepdims=True)
        outs.append((_dot(p.astype(BF16), mv_ref[0, :, hs]) / l).astype(BF16))
    o = jnp.concatenate(outs, axis=1)
    x2 = x1 + _dot(o, wxo_ref[...])
    o_ref[0, :, 0:D] = x2
    hf = _rms(x2, gffn_ref[...]).astype(BF16)
    o_ref[0, :, D:D + LANES] = _route(_dot(hf, wr_ref[...]) + br_ref[...])


def _mix(x, a, u, hist, mk, mv, w_pool, pool_scale, w_out, g_x, w_xq, w_xo, g_ffn, w_r, b_r, *, pos0, tm):
    B, T, D = x.shape
    n_mem = mk.shape[1]
    H = POOL_HIST + 1
    assert T % tm == 0 and tm >= H
    hist16 = jnp.concatenate([jnp.zeros((B, 1, D_POOL), F32), hist], axis=1)
    return pl.pallas_call(
        functools.partial(_mix_kernel, tm=tm, pos0=pos0),
        grid=(B, T // tm),
        in_specs=[pl.BlockSpec((1, tm, D), lambda b, t: (b, t, 0)),
                  pl.BlockSpec((1, tm, D_ATTN), lambda b, t: (b, t, 0)),
                  pl.BlockSpec((1, tm, D_POOL), lambda b, t: (b, t, 0)),
                  pl.BlockSpec((1, H, D_POOL), lambda b, t: (b, 0, 0)),
                  pl.BlockSpec((1, n_mem, D_X), lambda b, t: (b, 0, 0)),
                  pl.BlockSpec((1, n_mem, D_X), lambda b, t: (b, 0, 0)),
                  _const_spec(w_pool.shape),
                  _const_spec((1, D_POOL)),
                  _const_spec(w_out.shape),
                  _const_spec((1, D)),
                  _const_spec(w_xq.shape),
                  _const_spec(w_xo.shape),
                  _const_spec((1, D)),
                  _const_spec(w_r.shape),
                  _const_spec((1, LANES))],
        out_specs=pl.BlockSpec((1, tm, D + LANES), lambda b, t: (b, t, 0)),
        out_shape=jax.ShapeDtypeStruct((B, T, D + LANES), F32),
        scratch_shapes=[pltpu.VMEM((H + tm, D_POOL), F32)],
        compiler_params=pltpu.CompilerParams(dimension_semantics=("arbitrary", "arbitrary"),
                                             vmem_limit_bytes=VMEM_LIMIT),
        name="mix",
    )(x, a, u, hist16, mk, mv, w_pool, pool_scale.reshape(1, D_POOL), w_out, g_x.reshape(1, D), w_xq, w_xo,
      g_ffn.reshape(1, D), w_r, b_r)


R_OFF = N_GROUPS


def _route(logits):
    tm = logits.shape[0]
    lane = lax.broadcasted_iota(jnp.int32, (tm, LANES), 1).astype(F32)
    ninf = -jnp.inf
    big = float(LANES)
    gl = jnp.where(lane < N_GROUPS, logits, ninf)
    gmax = jnp.max(gl, axis=1, keepdims=True)
    g_sel = jnp.min(jnp.where(gl == gmax, lane, big), axis=1, keepdims=True)
    g_prob = 1.0 / jnp.sum(jnp.exp(gl - gmax), axis=1, keepdims=True)
    lo = R_OFF + g_sel * EXPERTS_PER_GROUP
    el = jnp.where((lane >= lo) & (lane < lo + EXPERTS_PER_GROUP), logits, ninf)
    tv0 = jnp.max(el, axis=1, keepdims=True)
    ti0 = jnp.min(jnp.where(el == tv0, lane, big), axis=1, keepdims=True)
    el2 = jnp.where(lane == ti0, ninf, el)
    tv1 = jnp.max(el2, axis=1, keepdims=True)
    ti1 = jnp.min(jnp.where(el2 == tv1, lane, big), axis=1, keepdims=True)
    e1 = jnp.exp(tv1 - tv0)
    den = 1.0 + e1
    w0 = g_prob / den
    w1 = g_prob * e1 / den
    ids = jnp.where(lane == 0.0, jnp.minimum(ti0, ti1) - R_OFF,
                    jnp.where(lane == 1.0, jnp.maximum(ti0, ti1) - R_OFF, 0.0))
    return jnp.where(lane == ti0, w0, 0.0) + jnp.where(lane == ti1, w1, 0.0) + ids


def _expert(h, rec, e, wg, wu, wd):
    a = _dot(h, wg)
    b = _dot(h, wu)
    lane = lax.broadcasted_iota(jnp.int32, rec.shape, 1)
    cw = jnp.sum(jnp.where(lane == e + R_OFF, rec, 0.0), axis=1, keepdims=True)
    hid = a * jax.nn.sigmoid(a) * b * cw
    return _dot(hid.astype(BF16), wd)


def _moe_kernel(x_ref, gffn_ref, wg_ref, wu_ref, wd_ref, gfin_ref, o_ref, h_s, acc_s):
    e = pl.program_id(1)
    d = o_ref.shape[1]

    @pl.when(e == 0)
    def _():
        h_s[...] = _rms(x_ref[:, 0:d], gffn_ref[...]).astype(BF16)
        acc_s[...] = jnp.zeros(acc_s.shape, F32)

    acc_s[...] += _expert(h_s[...], x_ref[:, d:d + LANES], e, wg_ref[0], wu_ref[0], wd_ref[0])

    @pl.when(e == pl.num_programs(1) - 1)
    def _():
        o_ref[...] = _rms(x_ref[:, 0:d] + acc_s[...], gfin_ref[...])


def _moe(x, g_ffn, w_gate, w_up, w_down, g_final, *, tm):
    n, de = x.shape
    d = de - LANES
    assert n % tm == 0
    return pl.pallas_call(
        _moe_kernel,
        grid=(n // tm, N_EXPERTS),
        in_specs=[pl.BlockSpec((tm, de), lambda i, e: (i, 0)),
                  _const_spec((1, d)),
                  pl.BlockSpec((1, d, D_EXPERT), lambda i, e: (e, 0, 0)),
                  pl.BlockSpec((1, d, D_EXPERT), lambda i, e: (e, 0, 0)),
                  pl.BlockSpec((1, D_EXPERT, d), lambda i, e: (e, 0, 0)),
                  _const_spec((1, d))],
        out_specs=pl.BlockSpec((tm, d), lambda i, e: (i, 0)),
        out_shape=jax.ShapeDtypeStruct((n, d), F32),
        scratch_shapes=[pltpu.VMEM((tm, d), BF16),
                        pltpu.VMEM((tm, d), F32)],
        compiler_params=pltpu.CompilerParams(dimension_semantics=("arbitrary", "arbitrary"),
                                             vmem_limit_bytes=VMEM_LIMIT),
        name="moe",
    )(x, g_ffn.reshape(1, d), w_gate, w_up, w_down, g_final.reshape(1, d))


MOE_TILES_PER_STEP = 2


def _moe_sparse_kernel(ea_ref, eb_ref, nv_ref, idx_ref, idxn_ref, x_hbm, gffn_ref, gfin_ref, *rest, tm, tps):
    w_refs = [rest[6 * k:6 * k + 6] for k in range(tps)]
    o_hbm, xbuf, ybuf, gsem, ssem = rest[6 * tps:]
    t = pl.program_id(0)
    nt = pl.num_programs(0)
    d = o_hbm.shape[1]
    slot = lax.rem(t, 2)
    nslot = 1 - slot
    rows = tps * tm
    WAIT_ROWS = 8

    def step_rows(step):
        nvs = [nv_ref[tps * step + k] for k in range(tps)]
        return nvs, functools.reduce(lambda a, b: a + b, nvs)

    nvs, nv = step_rows(t)

    def gather_row(ids, s, r):
        return pltpu.make_async_copy(x_hbm.at[pl.ds(ids[0, 0, r], 1), :], xbuf.at[s, pl.ds(r, 1), :], gsem.at[s])

    def gather_loop(ids, s):
        def body(r, carry):
            gather_row(ids, s, r).start()
            return carry
        lax.fori_loop(0, rows, body, 0)

    def gather_wait(s):
        pltpu.make_async_copy(x_hbm.at[pl.ds(0, rows), :], xbuf.at[s], gsem.at[s]).wait()

    def scatter_rows(s, r, row, n):
        return pltpu.make_async_copy(ybuf.at[s, pl.ds(r, n), :], o_hbm.at[pl.ds(row, n), :], ssem.at[s])

    def scatter_wait(s, n):
        def body_many(r, carry):
            scatter_rows(s, 0, 0, WAIT_ROWS).wait()
            return carry
        lax.fori_loop(0, n // WAIT_ROWS, body_many, 0)

        def body_one(r, carry):
            scatter_rows(s, 0, 0, 1).wait()
            return carry
        lax.fori_loop(0, lax.rem(n, WAIT_ROWS), body_one, 0)

    @pl.when(t == 0)
    def _():
        gather_loop(idx_ref, 0)

    @pl.when(t >= 2)
    def _():
        scatter_wait(slot, step_rows(jnp.maximum(t - 2, 0))[1])

    gather_wait(slot)

    @pl.when(nv > 0)
    def _():
        tiles = []
        for k in range(tps):
            x = xbuf[slot, k * tm:(k + 1) * tm, :]
            xr = x[:, 0:d]
            tiles.append((xr, x[:, d:d + LANES], _rms(xr, gffn_ref[...]).astype(BF16)))
        for r in range(rows):
            gather_row(idxn_ref, nslot, r).start()
        pre = [[(_dot(h, wga[0]), _dot(h, wua[0])), (_dot(h, wgb[0]), _dot(h, wub[0]))]
               for (_, _, h), (wga, wua, _, wgb, wub, _) in zip(tiles, w_refs)]
        for k in range(tps):
            xr, rec, _ = tiles[k]
            lane = lax.broadcasted_iota(jnp.int32, rec.shape, 1)
            experts = (ea_ref[tps * t + k], eb_ref[tps * t + k])
            acc = None
            for (a, b), e, wd in zip(pre[k], experts, (w_refs[k][2], w_refs[k][5])):
                cw = jnp.sum(jnp.where(lane == e + R_OFF, rec, 0.0), axis=1, keepdims=True)
                hid = a * jax.nn.sigmoid(a) * b * cw
                out = _dot(hid.astype(BF16), wd[0])
                acc = out if acc is None else acc + out
            ybuf[slot, k * tm:(k + 1) * tm, :] = _rms(xr + acc, gfin_ref[...])

        for k in range(tps):
            def body_many(i, carry, k=k):
                for u in range(WAIT_ROWS):
                    r = k * tm + i * WAIT_ROWS + u
                    scatter_rows(slot, r, idx_ref[0, 0, r], 1).start()
                return carry
            lax.fori_loop(0, nvs[k] // WAIT_ROWS, body_many, 0)

            def body_one(i, carry, k=k):
                r = k * tm + i
                scatter_rows(slot, r, idx_ref[0, 0, r], 1).start()
                return carry
            lax.fori_loop((nvs[k] // WAIT_ROWS) * WAIT_ROWS, nvs[k], body_one, 0)

    @pl.when(nv <= 0)
    def _():
        gather_loop(idxn_ref, nslot)

    @pl.when(t == nt - 1)
    def _():
        gather_wait(nslot)

        @pl.when(t >= 1)
        def _():
            scatter_wait(nslot, step_rows(jnp.maximum(t - 1, 0))[1])
        scatter_wait(slot, nv)


def _moe_tile_rows(n):
    n_pairs = N_GROUPS * (EXPERTS_PER_GROUP * (EXPERTS_PER_GROUP - 1) // 2)
    mean = n / n_pairs
    rows = (mean + 3.0 * math.sqrt(mean)) / 2
    return int(min(256, max(SUB_BF16, -(-rows // SUB_BF16) * SUB_BF16)))


def _moe_sparse(x, g_ffn, w_gate, w_up, w_down, g_final, *, tm):
    n, de = x.shape
    d = de - LANES
    n_pairs = N_GROUPS * (EXPERTS_PER_GROUP * (EXPERTS_PER_GROUP - 1) // 2)
    tps = MOE_TILES_PER_STEP
    nt = -(-n // tm) + n_pairs
    nt = -(-nt // tps) * tps
    n_steps = nt // tps
    n_keys = N_EXPERTS * N_EXPERTS

    ids = x[:, d:d + 2].astype(jnp.int32)
    key = ids[:, 0] * N_EXPERTS + ids[:, 1]
    order = jnp.argsort(key).astype(jnp.int32)
    skey = key[order]
    count_below = lambda sorted_vals, q: jnp.sum((sorted_vals[None, :] < q[:, None]).astype(jnp.int32), axis=1)
    edges = count_below(skey, jnp.arange(n_keys + 1, dtype=jnp.int32))
    starts, ends = edges[:-1], edges[1:]
    tiles = (ends - starts + tm - 1) // tm
    cum = jnp.cumsum(tiles)
    total = cum[-1]
    tt = jnp.arange(nt, dtype=jnp.int32)
    tc = jnp.minimum(tt, total - 1)
    cls = count_below(cum, tc + 1)
    first = cum[cls] - tiles[cls]
    tstart = starts[cls] + (tc - first) * tm
    nv = jnp.where(tt < total, jnp.clip(ends[cls] - tstart, 0, tm), 0).astype(jnp.int32)
    ea = (cls // N_EXPERTS).astype(jnp.int32)
    eb = (cls % N_EXPERTS).astype(jnp.int32)
    rows = jnp.clip(tstart[:, None] + jnp.arange(tm, dtype=jnp.int32)[None, :], 0, n - 1)
    idx = order[rows].reshape(n_steps, 1, tps * tm)

    def wspec(shape, experts, k):
        return pl.BlockSpec(shape, lambda t, ea, eb, nv: ((ea, eb)[experts][tps * t + k], 0, 0))

    cspec = lambda shape: pl.BlockSpec(shape, lambda t, ea, eb, nv: (0,) * len(shape),
                                       pipeline_mode=pl.Buffered(1))
    w_specs, w_args = [], []
    for k in range(tps):
        for experts in (0, 1):
            w_specs += [wspec((1, d, D_EXPERT), experts, k), wspec((1, d, D_EXPERT), experts, k),
                        wspec((1, D_EXPERT, d), experts, k)]
            w_args += [w_gate, w_up, w_down]
    grid_spec = pltpu.PrefetchScalarGridSpec(
        num_scalar_prefetch=3,
        grid=(n_steps,),
        in_specs=[pl.BlockSpec((1, 1, tps * tm), lambda t, ea, eb, nv: (t, 0, 0), memory_space=pltpu.SMEM),
                  pl.BlockSpec((1, 1, tps * tm), lambda t, ea, eb, nv: (jnp.minimum(t + 1, n_steps - 1), 0, 0),
                               memory_space=pltpu.SMEM),
                  pl.BlockSpec(memory_space=pl.ANY),
                  cspec((1, d)), cspec((1, d))] + w_specs,
        out_specs=pl.BlockSpec(memory_space=pl.ANY),
        scratch_shapes=[pltpu.VMEM((2, tps * tm, de), F32),
                        pltpu.VMEM((2, tps * tm, d), F32),
                        pltpu.SemaphoreType.DMA((2,)),
                        pltpu.SemaphoreType.DMA((2,))])
    return pl.pallas_call(
        functools.partial(_moe_sparse_kernel, tm=tm, tps=tps),
        grid_spec=grid_spec,
        out_shape=jax.ShapeDtypeStruct((n, d), F32),
        compiler_params=pltpu.CompilerParams(dimension_semantics=("arbitrary",),
                                             vmem_limit_bytes=VMEM_LIMIT),
        name="moe_sparse",
    )(ea, eb, nv, idx, idx, x, g_ffn.reshape(1, d), g_final.reshape(1, d), *w_args)


C_Q = 0
C_QI = C_Q + D_ATTN
C_K = C_QI + N_IDX_HEADS * IDX_DIM
C_V = C_K + D_KV
C_KIWI = C_V + D_KV
C_KIE = C_KIWI + LANES
C_KIO = C_KIE + LANES
C_U = C_KIO + LANES
C_END = C_U + D_POOL


def _layout_w_in(w_in):
    d = w_in.shape[0]
    offs = np.cumsum((0, D_ATTN, D_KV, D_KV, N_IDX_HEADS * IDX_DIM, IDX_DIM, N_IDX_HEADS, D_POOL))
    wq, wk, wv, wqi, wki, wwi, wu = (w_in[:, offs[n]:offs[n + 1]] for n in range(7))
    z = lambda n: jnp.zeros((d, n), w_in.dtype)
    cat = jnp.concatenate([wq, wqi, wk, wv,
                           wki, wwi, z(LANES - IDX_DIM - N_IDX_HEADS),
                           wki, z(LANES - IDX_DIM),
                           z(LANES - IDX_DIM), wki,
                           wu], axis=1)
    assert cat.shape[1] == C_END
    return cat.astype(BF16)


IN_GROUPS = ((C_Q, C_QI, 1.0, 1),
             (C_QI, C_K, 1.0, 1),
             (C_K, C_V, 1.0, 2),
             (C_V, C_KIWI, 1.0, 2),
             (C_KIWI, C_KIE, 1.0, 1),
             (C_KIE, C_KIO, 1.0, 1),
             (C_KIO, C_U, 1.0, 1),
             (C_U, C_END, 1.0, 1))
IN_DTYPES = (BF16, BF16, F32, BF16, F32, BF16, F32, BF16, BF16, F32)


def _pad_rows(a, n):
    return jnp.pad(a, ((0, 0), (0, n - a.shape[1]), (0, 0)))


def _layer(x, pos0, caches, pool_hist, mk, mv, rel_bias, W, *, tm_in, tq, tm_mix, tm_moe, sparse_moe):
    B, T, D = x.shape
    n = B * T
    q, qi, k, kb, v, vb, kiwi, kie, kio, u = _norm_matmul(
        x.reshape(n, D), W["g_mix"], W["w_in"], IN_GROUPS, IN_DTYPES, tm_in)
    r3 = lambda a: a.reshape(B, T, a.shape[-1])
    kblk = 256
    if caches is None:
        n_keys = T
        kb3, vb3, kie3, kio3 = r3(kb), r3(vb), r3(kie), r3(kio)
    else:
        k_hist, v_hist, ki_hist = caches
        past = k_hist.shape[1]
        n_keys = past + T
        lp = -(-n_keys // kblk) * kblk
        ki = r3(kiwi)[:, :, :IDX_DIM]
        ki_all = jnp.concatenate([ki_hist, ki], axis=1)
        zeros = jnp.zeros_like(ki_all)
        kb3 = _pad_rows(jnp.concatenate([k_hist.reshape(B, past, D_KV), r3(k)], axis=1), lp).astype(BF16)
        vb3 = _pad_rows(jnp.concatenate([v_hist.reshape(B, past, D_KV), r3(v)], axis=1), lp).astype(BF16)
        kie3 = _pad_rows(jnp.concatenate([ki_all, zeros], axis=2), lp).astype(BF16)
        kio3 = _pad_rows(jnp.concatenate([zeros, ki_all], axis=2), lp).astype(BF16)
    a_out = _dsa(r3(q), r3(qi), r3(kiwi), kb3, vb3, kie3, kio3, rel_bias,
                 n_keys=n_keys, pos0=pos0, tq=tq, kblk=kblk)
    x2 = _mix(x, a_out, r3(u), pool_hist, mk.astype(BF16), mv.astype(BF16), W["w_pool"], W["pool_scale"],
              W["w_out"], W["g_x"], W["w_xq"], W["w_xo"], W["g_ffn"], W["w_r"], W["b_r"], pos0=pos0, tm=tm_mix)
    moe = _moe_sparse if sparse_moe else _moe
    y = moe(x2.reshape(n, D + LANES), W["g_ffn"], W["w_gate"], W["w_up"], W["w_down"], W["g_final"], tm=tm_moe)
    new_pool = jnp.concatenate([pool_hist, r3(u)], axis=1)[:, -POOL_HIST:]
    return (y.reshape(B, T, D), k.reshape(B, T, N_KV_A, HEAD_DIM_A), v.reshape(B, T, N_KV_A, HEAD_DIM_A),
            r3(kiwi)[:, :, :IDX_DIM], new_pool)


def kernel(x_prompt, x_sample, mem_prompt, cache_k, cache_v, cache_k_idx, cache_pool, cache_mem_k, cache_mem_v, rel_bias, g_mix, w_in, w_pool, pool_scale, w_out, g_mem, w_mk, w_mv, g_x, w_xq, w_xo, g_ffn, w_rg, b_rg, w_re, b_re, w_gate, w_up, w_down, g_final):
    depth = g_mix.shape[0]
    assert depth == 1
    l = 0
    B, T, D = x_prompt.shape
    Bs, Ts, _ = x_sample.shape
    n_mem = mem_prompt.shape[1]
    past = cache_k.shape[2]

    w_r = jnp.concatenate([w_rg[l], w_re[l], jnp.zeros((D, LANES - N_GROUPS - N_EXPERTS), F32)], axis=1)
    W = dict(
        g_mix=g_mix[l], w_in=_layout_w_in(w_in[l]), w_pool=w_pool[l].astype(BF16), pool_scale=pool_scale[l],
        w_out=w_out[l].astype(BF16), g_x=g_x[l], w_xq=w_xq[l].astype(BF16), w_xo=w_xo[l].astype(BF16),
        g_ffn=g_ffn[l], w_r=w_r.astype(BF16),
        b_r=jnp.concatenate([b_rg[l], b_re[l], jnp.zeros((LANES - N_GROUPS - N_EXPERTS,), F32)]).reshape(1, LANES),
        w_gate=w_gate[l].astype(BF16), w_up=w_up[l].astype(BF16), w_down=w_down[l].astype(BF16),
        g_final=g_final)

    w_m = jnp.concatenate([w_mk[l], w_mv[l]], axis=1).astype(BF16)
    mk, mv = _norm_matmul(mem_prompt.reshape(B * n_mem, D), g_mem[l], w_m,
                          ((0, D_X, 1.0, 1), (D_X, 2 * D_X, 1.0, 1)), (F32, F32), 256)
    mk = mk.reshape(B, n_mem, D_X)
    mv = mv.reshape(B, n_mem, D_X)

    yp, kp, vp, kip, pp = _layer(x_prompt, 0, None, jnp.zeros((B, POOL_HIST, D_POOL), F32), mk, mv, rel_bias, W,
                                 tm_in=256, tq=256, tm_mix=256, tm_moe=_moe_tile_rows(B * T), sparse_moe=True)
    ys, ks, vs, kis, ps = _layer(x_sample, past,
                                 (cache_k[l], cache_v[l], cache_k_idx[l]), cache_pool[l],
                                 cache_mem_k[l].reshape(Bs, n_mem, D_X), cache_mem_v[l].reshape(Bs, n_mem, D_X),
                                 rel_bias, W, tm_in=Bs * Ts, tq=Ts, tm_mix=Ts, tm_moe=Bs * Ts, sparse_moe=False)
    st = lambda a: a[None]
    return (yp, ys, st(kp), st(vp), st(kip), st(pp),
            st(mk.reshape(B, n_mem, N_XHEADS, XHEAD_DIM)), st(mv.reshape(B, n_mem, N_XHEADS, XHEAD_DIM)),
            st(ks), st(vs), st(kis), st(ps))
```

```python
import functools
import math

import numpy as np
import jax
import jax.numpy as jnp
from jax import lax
from jax.experimental import pallas as pl
from jax.experimental.pallas import tpu as pltpu

F32 = jnp.float32
BF16 = jnp.bfloat16

CHUNK = 64
N_HEADS_A = 8
HEAD_DIM_A = 128
N_KV_A = 2
D_ATTN = N_HEADS_A * HEAD_DIM_A
D_KV = N_KV_A * HEAD_DIM_A
N_IDX_HEADS = 8
IDX_DIM = 64
TOPK_MAX = 256
POOL_WINDOWS = (2, 4, 8, 16)
POOL_GROUP_DIM = 256
D_POOL = len(POOL_WINDOWS) * POOL_GROUP_DIM
POOL_HIST = max(POOL_WINDOWS) - 1
N_BUCKETS = 32
MAX_DISTANCE = 128
N_XHEADS = 4
XHEAD_DIM = 128
D_X = N_XHEADS * XHEAD_DIM
N_GROUPS = 4
EXPERTS_PER_GROUP = 8
N_EXPERTS = N_GROUPS * EXPERTS_PER_GROUP
D_EXPERT = 256
EPS = 1e-6

LANES = 128
INT_MIN = -(2 ** 31)
NEG = -1e30
LOG2E = math.log2(math.e)
VMEM_LIMIT = 48 * 1024 * 1024
TOKEN_TILE = 256
KEY_BLOCK = 256
QUERY_TILE = KEY_BLOCK


def _rms(x, g):
    ms = jnp.mean(x * x, axis=-1, keepdims=True)
    return x * lax.rsqrt(ms + EPS) * g


def _dot(a, b):
    return jnp.dot(a, b, preferred_element_type=F32)


def _dot_nt(a, b):
    return lax.dot_general(a, b, (((1,), (1,)), ((), ())), preferred_element_type=F32)


def _const_spec(shape):
    nd = len(shape)
    return pl.BlockSpec(shape, lambda *_: (0,) * nd, pipeline_mode=pl.Buffered(1))


def _norm_matmul_kernel(x_ref, g_ref, w_ref, *out_refs, groups):
    h = _rms(x_ref[...], g_ref[...]).astype(BF16)
    k = 0
    for c0, c1, n_out in groups:
        acc = _dot(h, w_ref[:, c0:c1])
        for _ in range(n_out):
            out_refs[k][...] = acc.astype(out_refs[k].dtype)
            k += 1


def _norm_matmul(x, g, w, groups, out_dtypes, tm):
    n, d = x.shape
    assert n % tm == 0
    widths = []
    for c0, c1, n_out in groups:
        widths += [c1 - c0] * n_out
    return pl.pallas_call(
        functools.partial(_norm_matmul_kernel, groups=tuple(groups)),
        grid=(n // tm,),
        in_specs=[pl.BlockSpec((tm, d), lambda i: (i, 0)),
                  _const_spec((1, d)),
                  _const_spec(w.shape)],
        out_specs=[pl.BlockSpec((tm, wd), lambda i: (i, 0)) for wd in widths],
        out_shape=[jax.ShapeDtypeStruct((n, wd), dt) for wd, dt in zip(widths, out_dtypes)],
        compiler_params=pltpu.CompilerParams(dimension_semantics=("arbitrary",),
                                             vmem_limit_bytes=VMEM_LIMIT),
        name="norm_matmul",
    )(x, g.reshape(1, d), w)


def _bias_table_kernel(relb_ref, bucket_ref, o_ref, *, shift_bucket):
    bucket = bucket_ref[...]
    for h in range(N_HEADS_A):
        acc = jnp.zeros(bucket.shape, F32)
        for b in range(N_BUCKETS):
            acc = jnp.where(bucket == b, relb_ref[b, h], acc)
        if shift_bucket is None:
            o_ref[h] = acc * LOG2E
        else:
            o_ref[h] = (acc - relb_ref[shift_bucket, h]) * (HEAD_DIM_A ** 0.5)


def _bias_table(rel_bias, bucket, shift_bucket=None):
    tq, w = bucket.shape
    return pl.pallas_call(
        functools.partial(_bias_table_kernel, shift_bucket=shift_bucket),
        in_specs=[pl.BlockSpec(memory_space=pltpu.SMEM),
                  pl.BlockSpec((tq, w), lambda: (0, 0))],
        out_specs=pl.BlockSpec((N_HEADS_A, tq, w), lambda: (0, 0, 0)),
        out_shape=jax.ShapeDtypeStruct((N_HEADS_A, tq, w), F32),
        name="bias_table",
    )(rel_bias, bucket)


def _t5_bucket(rel):
    nb = N_BUCKETS // 2
    max_exact = nb // 2
    bucket = (rel > 0).astype(jnp.int32) * nb
    n = jnp.abs(rel)
    nf = jnp.maximum(n, 1).astype(F32)
    large = max_exact + (jnp.log(nf / max_exact) / math.log(MAX_DISTANCE / max_exact)
                         * (nb - max_exact)).astype(jnp.int32)
    large = jnp.minimum(large, nb - 1)
    return bucket + jnp.where(n < max_exact, n, large)


def _far_bucket(min_dist, max_dist):
    nb = N_BUCKETS // 2
    max_exact = nb // 2
    n = np.arange(min_dist, max_dist + 1, dtype=np.float64)
    large = max_exact + np.floor(np.log(n / max_exact) / math.log(MAX_DISTANCE / max_exact)
                                 * (nb - max_exact) * (1 - 1e-6)).astype(np.int64)
    assert large.min() >= nb - 1, "far key blocks must sit in the saturated distance bucket"
    return nb - 1


def _dsa_kernel(relb_ref, q_ref, qi_ref, kiwi_ref, k_ref, v_ref, kie_ref, kio_ref, bnear_ref, vis_ref,
                o_ref, keys, madd, wb, qs, xcut, mrun, lrun, acc_s,
                *, TQ, KB, RB, kb0, kstep, topk, far_bucket, tie_bits):
    i = pl.program_id(1)
    nkb = kb0 + kstep * i
    n_far = jnp.maximum(nkb - 2, 0)
    n_lc = KB // LANES

    kiwi = kiwi_ref[0]
    for h in range(N_IDX_HEADS):
        w = (kiwi[:, IDX_DIM + h:IDX_DIM + h + 1] * (N_IDX_HEADS ** -0.5)).astype(BF16).astype(F32)
        wb[h] = jnp.broadcast_to(w * (IDX_DIM ** -0.5), (TQ, LANES))

    def idx_block(j, near_col):
        c0 = pl.multiple_of(j * KB, KB)
        ke = kie_ref[0, pl.ds(c0, KB), :]
        ko = kio_ref[0, pl.ds(c0, KB), :]
        acc = [jnp.zeros((TQ, LANES), F32) for _ in range(n_lc)]
        for hp in range(N_IDX_HEADS // 2):
            qh = qi_ref[0, :, hp * LANES:(hp + 1) * LANES]
            se = _dot_nt(qh, ke)
            so = _dot_nt(qh, ko)
            we = wb[2 * hp]
            wo = wb[2 * hp + 1]
            for c in range(n_lc):
                cs = slice(c * LANES, (c + 1) * LANES)
                re = jnp.maximum(se[:, cs], 0.0).astype(BF16).astype(F32)
                ro = jnp.maximum(so[:, cs], 0.0).astype(BF16).astype(F32)
                acc[c] = acc[c] + we * re + wo * ro
        for c in range(n_lc):
            bits = lax.bitcast_convert_type(acc[c], jnp.int32)
            key = bits ^ ((bits >> 31) & jnp.int32(0x7FFFFFFF))
            if near_col is not None:
                vis = vis_ref[:, near_col + c * LANES:near_col + (c + 1) * LANES]
                key = jnp.where(vis > 0.0, key, jnp.int32(INT_MIN))
            keys[j, :, c * LANES:(c + 1) * LANES] = key

    def far_idx(j, carry):
        idx_block(j, None)
        return carry

    lax.fori_loop(0, n_far, far_idx, 0)

    @pl.when(nkb >= 2)
    def _():
        idx_block(nkb - 2, 0)

    idx_block(nkb - 1, KB)

    kf = float(topk)
    lane_f = lax.broadcasted_iota(jnp.int32, (RB, LANES), 1).astype(F32)
    NRB = TQ // RB
    rbs = [slice(rb * RB, (rb + 1) * RB) for rb in range(NRB)]

    def count(pred):
        def body(jj, accs):
            out = []
            for rb in range(NRB):
                blk = keys[jj, rbs[rb], :]
                acc = accs[rb]
                for c in range(n_lc):
                    acc = acc + jnp.where(pred(rb, blk[:, c * LANES:(c + 1) * LANES], jj, c), 1.0, 0.0)
                out.append(acc)
            return tuple(out)
        accs = lax.fori_loop(0, nkb, body, tuple(jnp.zeros((RB, LANES), F32) for _ in range(NRB)))
        return [jnp.sum(acc, axis=1, keepdims=True) for acc in accs]

    def count_ge(cands):
        cb = [jnp.broadcast_to(cand, (RB, LANES)) for cand in cands]
        return count(lambda rb, kblk, jj, c: kblk >= cb[rb])

    def bis_body(p, thrs):
        cands = [thr + lax.shift_left(jnp.int32(1), 31 - p) for thr in thrs]
        cnts = count_ge(cands)
        return tuple(jnp.where(cnt >= kf, cand, thr) for cnt, cand, thr in zip(cnts, cands, thrs))

    thrs = lax.fori_loop(0, 32, bis_body, tuple(jnp.full((RB, 1), INT_MIN, jnp.int32) for _ in range(NRB)))
    thrs = [jnp.maximum(thr, jnp.int32(INT_MIN + 1)) for thr in thrs]
    cnt_ge = count_ge(thrs)
    cnt_gt = count_ge([thr + 1 for thr in thrs])
    needs = [kf - c for c in cnt_gt]
    thrb = [jnp.broadcast_to(thr, (RB, LANES)) for thr in thrs]

    xcut[...] = jnp.full(xcut.shape, 2.0 ** 24, F32)
    excess = cnt_ge[0] - kf
    for c in cnt_ge[1:]:
        excess = jnp.maximum(excess, c - kf)

    @pl.when(jnp.max(excess) > 0.0)
    def _():
        def count_eq_lt(xs):
            xb = [jnp.broadcast_to(x, (RB, LANES)) for x in xs]

            def pred(rb, kblk, jj, c):
                col = lane_f + (jj * KB + c * LANES).astype(F32)
                return jnp.where(kblk == thrb[rb], col, 2.0 ** 25) < xb[rb]
            return count(pred)

        def tie_body(p, xs):
            cands = [x + lax.shift_left(jnp.int32(1), tie_bits - 1 - p).astype(F32) for x in xs]
            cnts = count_eq_lt(cands)
            return tuple(jnp.where(cnt < need, cand, x) for cnt, need, cand, x in zip(cnts, needs, cands, xs))

        xs = lax.fori_loop(0, tie_bits, tie_body, tuple(jnp.zeros((RB, 1), F32) for _ in range(NRB)))
        for rb in range(NRB):
            xcut[rbs[rb], :] = xs[rb]

    xb = [jnp.broadcast_to(xcut[rbs[rb], :], (RB, LANES)) for rb in range(NRB)]

    def mask_body(jj, carry):
        for rb in range(NRB):
            blk = keys[jj, rbs[rb], :]
            for c in range(n_lc):
                kblk = blk[:, c * LANES:(c + 1) * LANES]
                col = lane_f + (jj * KB + c * LANES).astype(F32)
                keep = (kblk > thrb[rb]) | ((kblk == thrb[rb]) & (col <= xb[rb]))
                madd[jj, rbs[rb], c * LANES:(c + 1) * LANES] = jnp.where(keep, 0.0, NEG)
        return carry

    lax.fori_loop(0, nkb, mask_body, 0)

    _attend(relb_ref, q_ref, k_ref, v_ref, bnear_ref, o_ref, madd, qs, mrun, lrun, acc_s,
            TQ=TQ, KB=KB, nkb=nkb, n_far=n_far, far_bucket=far_bucket)


def _attend(relb_ref, q_ref, k_ref, v_ref, bnear_ref, o_ref, madd, qs, mrun, lrun, acc_s,
            *, TQ, KB, nkb, n_far, far_bucket):
    n_lc = KB // LANES
    HPG = N_HEADS_A // N_KV_A
    for g in range(N_KV_A):
        for r in range(HPG):
            h = g * HPG + r
            qs[g, r * TQ:(r + 1) * TQ, :] = q_ref[0, :, h * HEAD_DIM_A:(h + 1) * HEAD_DIM_A]

    def logits(j, near_col, g):
        c0 = pl.multiple_of(j * KB, KB)
        kg = k_ref[0, pl.ds(c0, KB), g * HEAD_DIM_A:(g + 1) * HEAD_DIM_A]
        s = _dot_nt(qs[g], kg)
        mk = madd[j]
        out = []
        for r in range(HPG):
            h = g * HPG + r
            if near_col is None:
                bias = relb_ref[far_bucket, h] * LOG2E
            else:
                bias = bnear_ref[h, :, near_col:near_col + KB]
            out.append(s[r * TQ:(r + 1) * TQ] * (HEAD_DIM_A ** -0.5 * LOG2E) + bias + mk)
        return out

    def over_blocks(fn):
        def far(j, carry):
            fn(j, None)
            return carry

        lax.fori_loop(0, n_far, far, 0)

        @pl.when(nkb >= 2)
        def _():
            fn(nkb - 2, 0)

        fn(nkb - 1, KB)

    mrun[...] = jnp.full(mrun.shape, NEG, F32)

    def max_block(j, near_col):
        for g in range(N_KV_A):
            for r, sh in enumerate(logits(j, near_col, g)):
                rs = slice(r * TQ, (r + 1) * TQ)
                m = mrun[g, rs, :]
                for c in range(n_lc):
                    m = jnp.maximum(m, sh[:, c * LANES:(c + 1) * LANES])
                mrun[g, rs, :] = m

    over_blocks(max_block)

    for g in range(N_KV_A):
        mrun[g] = jnp.broadcast_to(jnp.max(mrun[g], axis=1, keepdims=True), mrun.shape[1:])
    lrun[...] = jnp.zeros(lrun.shape, F32)
    acc_s[...] = jnp.zeros(acc_s.shape, F32)

    def pv_block(j, near_col):
        c0 = pl.multiple_of(j * KB, KB)
        for g in range(N_KV_A):
            vg = v_ref[0, pl.ds(c0, KB), g * HEAD_DIM_A:(g + 1) * HEAD_DIM_A]
            parts = []
            for r, sh in enumerate(logits(j, near_col, g)):
                rs = slice(r * TQ, (r + 1) * TQ)
                mb = mrun[g, rs, :]
                l = lrun[g, rs, :]
                pcs = []
                for c in range(n_lc):
                    p = jnp.exp2(sh[:, c * LANES:(c + 1) * LANES] - mb)
                    l = l + p
                    pcs.append(p.astype(BF16))
                lrun[g, rs, :] = l
                parts.append(jnp.concatenate(pcs, axis=1))
            acc_s[g] += _dot(jnp.concatenate(parts, axis=0), vg)

    over_blocks(pv_block)

    for g in range(N_KV_A):
        out = acc_s[g] / jnp.sum(lrun[g], axis=1, keepdims=True)
        for r in range(HPG):
            h = g * HPG + r
            o_ref[0, :, h * HEAD_DIM_A:(h + 1) * HEAD_DIM_A] = out[r * TQ:(r + 1) * TQ].astype(o_ref.dtype)


DIGIT_BITS = 8
N_DIGITS = 32 // DIGIT_BITS
SUB_BF16 = 16
COUNT_BLOCKS = 4


def _dsa_kernel_t(q_ref, qi_ref, kiwi_ref, k_ref, vt_ref, kie_ref, kio_ref, bnear_ref, vist_ref,
                  o_ref, keys, dig, madd, qs, xcut, acc_s, tbuf,
                  *, TQ, KB, kb0, kstep, topk, tie_bits):
    i = pl.program_id(1)
    nkb = kb0 + kstep * i
    n_far = jnp.maximum(nkb - 2, 0)
    n_slab = KB // SUB_BF16

    w_t = kiwi_ref[0].T
    wrow = [(w_t[IDX_DIM + h:IDX_DIM + h + 1, :] * (N_IDX_HEADS ** -0.5)).astype(BF16).astype(F32)
            * (IDX_DIM ** -0.5) for h in range(N_IDX_HEADS)]

    def idx_dots(j):
        c0 = pl.multiple_of(j * KB, KB)
        ke = kie_ref[0, pl.ds(c0, KB), :]
        ko = kio_ref[0, pl.ds(c0, KB), :]
        raw = []
        for hp in range(N_IDX_HEADS // 2):
            qh = qi_ref[0, :, hp * LANES:(hp + 1) * LANES]
            raw.append((_dot_nt(ke, qh), _dot_nt(ko, qh)))
        return raw

    def idx_finish(j, raw, near_row):
        acc = jnp.zeros((KB, TQ), F32)
        for hp, (se, so) in enumerate(raw):
            re = jnp.maximum(se.astype(BF16), 0.0).astype(F32)
            ro = jnp.maximum(so.astype(BF16), 0.0).astype(F32)
            acc = acc + wrow[2 * hp] * re + wrow[2 * hp + 1] * ro
        bits = lax.bitcast_convert_type(acc, jnp.int32)
        key = bits ^ ((bits >> 31) & jnp.int32(0x7FFFFFFF))
        if near_row is not None:
            key = jnp.where(vist_ref[near_row:near_row + KB, :] > 0.0, key, jnp.int32(INT_MIN))
        keys[j] = key
        for dk in range(N_DIGITS):
            sh = 32 - DIGIT_BITS * (dk + 1)
            d = (key >> sh) if dk == 0 else ((key >> sh) & jnp.int32(2 ** DIGIT_BITS - 1))
            dig[dk, j] = d.astype(F32).astype(BF16)

    def idx_blocks(js, near_rows):
        raws = [idx_dots(j) for j in js]
        for j, raw, near_row in zip(js, raws, near_rows):
            idx_finish(j, raw, near_row)

    def far_idx_pair(jp, carry):
        idx_blocks([2 * jp, 2 * jp + 1], [None, None])
        return carry

    lax.fori_loop(0, n_far // 2, far_idx_pair, 0)

    @pl.when(lax.rem(n_far, 2) == 1)
    def _():
        idx_blocks([n_far - 1], [None])

    @pl.when(nkb >= 2)
    def _():
        idx_blocks([nkb - 2, nkb - 1], [0, KB])

    @pl.when(nkb < 2)
    def _():
        idx_blocks([nkb - 1], [KB])

    one = jnp.ones((SUB_BF16, TQ), BF16)
    zero = jnp.zeros((SUB_BF16, TQ), BF16)

    def count_ge(dk, cand):
        cb = jnp.broadcast_to(cand.astype(BF16), (SUB_BF16, TQ))

        def block_count(jj):
            blk = dig[dk, jj]
            parts = [jnp.where(blk[s * SUB_BF16:(s + 1) * SUB_BF16, :] >= cb, one, zero) for s in range(n_slab)]
            while len(parts) > 1:
                parts = [parts[n] + parts[n + 1] for n in range(0, len(parts), 2)]
            return parts[0]

        def body_many(jq, tot):
            cnts = [block_count(COUNT_BLOCKS * jq + u) for u in range(COUNT_BLOCKS)]
            while len(cnts) > 1:
                cnts = [cnts[n] + cnts[n + 1] for n in range(0, len(cnts), 2)]
            return tot + cnts[0].astype(F32)

        def body_one(jj, tot):
            return tot + block_count(jj).astype(F32)

        tot = lax.fori_loop(0, nkb // COUNT_BLOCKS, body_many, jnp.zeros((SUB_BF16, TQ), F32))
        tot = lax.fori_loop((nkb // COUNT_BLOCKS) * COUNT_BLOCKS, nkb, body_one, tot)
        return jnp.sum(tot, axis=0, keepdims=True)

    kneed = jnp.full((1, TQ), float(topk), F32)
    thr = jnp.zeros((1, TQ), jnp.int32)
    cnt_eq = None
    for dk in range(N_DIGITS):
        lo = -float(2 ** (DIGIT_BITS - 1)) if dk == 0 else 0.0

        def bis_body(p, carry, dk=dk, kneed=kneed):
            t, c_rej = carry
            cand = t + lax.shift_left(jnp.int32(1), DIGIT_BITS - 1 - p).astype(F32)
            cnt = count_ge(dk, cand)
            ok = cnt >= kneed
            return jnp.where(ok, cand, t), jnp.where(ok, c_rej, cnt)

        t, n_gt = lax.fori_loop(0, DIGIT_BITS, bis_body,
                                (jnp.full((1, TQ), lo, F32), jnp.zeros((1, TQ), F32)))
        if dk == N_DIGITS - 1:
            cnt_eq = count_ge(dk, t) - n_gt
        kneed = kneed - n_gt
        thr = thr + lax.shift_left(t.astype(jnp.int32), 32 - DIGIT_BITS * (dk + 1))
        if dk + 1 < N_DIGITS:
            tb = t.astype(BF16)

            def prep(jj, carry, dk=dk, tb=tb):
                dig[dk + 1, jj] = jnp.where(dig[dk, jj] == tb, dig[dk + 1, jj], jnp.asarray(-1.0, BF16))
                return carry

            lax.fori_loop(0, nkb, prep, 0)

    thr = jnp.maximum(thr, jnp.int32(INT_MIN + 1))
    need = kneed
    sub_f = lax.broadcasted_iota(jnp.int32, (KB, TQ), 0).astype(F32)

    xcut[...] = jnp.full(xcut.shape, 2.0 ** 24, F32)

    @pl.when(jnp.max(cnt_eq - need) > 0.0)
    def _():
        def count_eq_lt(x):
            def body(jj, tot):
                col = sub_f + (jj * KB).astype(F32)
                hit = jnp.where(jnp.where(keys[jj] == thr, col, 2.0 ** 25) < x, 1.0, 0.0)
                for s in range(KB // 8):
                    tot = tot + hit[s * 8:(s + 1) * 8, :]
                return tot
            tot = lax.fori_loop(0, nkb, body, jnp.zeros((8, TQ), F32))
            return jnp.sum(tot, axis=0, keepdims=True)

        def tie_body(p, x):
            cand = x + lax.shift_left(jnp.int32(1), tie_bits - 1 - p).astype(F32)
            return jnp.where(count_eq_lt(cand) < need, cand, x)

        x = lax.fori_loop(0, tie_bits, tie_body, jnp.zeros((1, TQ), F32))
        xcut[...] = jnp.broadcast_to(x, xcut.shape)

    xc = xcut[0:1, :]

    def mask_body(jj, carry):
        kblk = keys[jj]
        col = sub_f + (jj * KB).astype(F32)
        at_thr = jnp.where(col <= xc, 0.0, NEG)
        madd[jj] = jnp.where(kblk > thr, 0.0, jnp.where(kblk == thr, at_thr, NEG))
        return carry

    lax.fori_loop(0, nkb, mask_body, 0)

    HPG = N_HEADS_A // N_KV_A
    c2 = HEAD_DIM_A ** -0.5 * LOG2E
    for g in range(N_KV_A):
        for r in range(HPG):
            h = g * HPG + r
            qs[g, r * TQ:(r + 1) * TQ, :] = q_ref[0, :, h * HEAD_DIM_A:(h + 1) * HEAD_DIM_A]
    acc_s[...] = jnp.zeros(acc_s.shape, F32)

    def attn_blocks(js, near_rows, carry):
        ms, ls = carry
        s_all = []
        for j in js:
            c0 = pl.multiple_of(j * KB, KB)
            s_all.append([_dot_nt(k_ref[0, pl.ds(c0, KB), g * HEAD_DIM_A:(g + 1) * HEAD_DIM_A], qs[g])
                          for g in range(N_KV_A)])
        for b, (j, near_row) in enumerate(zip(js, near_rows)):
            mk = madd[j]
            m_news = []
            for g in range(N_KV_A):
                for r in range(HPG):
                    h = g * HPG + r
                    t = s_all[b][g][:, r * TQ:(r + 1) * TQ] + mk
                    if near_row is not None:
                        t = t + bnear_ref[h, near_row:near_row + KB, :]
                    tbuf[b * N_HEADS_A + h] = t
                    m_news.append(jnp.maximum(ms[h], jnp.max(t, axis=0, keepdims=True)))
            ms_out, ls_out = [], []
            for g in range(N_KV_A):
                vt = vt_ref[0, j, g * HEAD_DIM_A:(g + 1) * HEAD_DIM_A, :]
                for r in range(HPG):
                    h = g * HPG + r
                    cs = slice(r * TQ, (r + 1) * TQ)
                    m_new = m_news[h]
                    alpha = jnp.exp2((ms[h] - m_new) * c2)
                    p = jnp.exp2((tbuf[b * N_HEADS_A + h] - m_new) * c2)
                    ls_out.append(alpha * ls[h] + jnp.sum(p, axis=0, keepdims=True))
                    ms_out.append(m_new)
                    acc_s[g, :, cs] = alpha * acc_s[g, :, cs] + _dot(vt, p.astype(BF16))
            ms, ls = tuple(ms_out), tuple(ls_out)
        return ms, ls

    carry = (tuple(jnp.full((1, TQ), NEG, F32) for _ in range(N_HEADS_A)),
             tuple(jnp.zeros((1, TQ), F32) for _ in range(N_HEADS_A)))
    carry = lax.fori_loop(0, n_far // 2, lambda jp, c: attn_blocks([2 * jp, 2 * jp + 1], [None, None], c), carry)
    carry = lax.cond(lax.rem(n_far, 2) == 1, lambda c: attn_blocks([n_far - 1], [None], c), lambda c: c, carry)
    ms, ls = lax.cond(nkb >= 2, lambda c: attn_blocks([nkb - 2, nkb - 1], [0, KB], c),
                      lambda c: attn_blocks([nkb - 1], [KB], c), carry)

    for g in range(N_KV_A):
        for r in range(HPG):
            h = g * HPG + r
            out_t = acc_s[g, :, r * TQ:(r + 1) * TQ] / ls[h]
            o_ref[0, :, h * HEAD_DIM_A:(h + 1) * HEAD_DIM_A] = out_t.T.astype(o_ref.dtype)


def _dsa(q, qi, kiwi, kb, vb, kie, kio, rel_bias, *, n_keys, pos0, tq, kblk):
    B, T, _ = q.shape
    Lp = kb.shape[1]
    assert T % tq == 0 and Lp % kblk == 0 and tq % CHUNK == 0 or T == tq
    n_qt = T // tq
    if n_qt == 1:
        kb0, kstep = Lp // kblk, 0
        assert ((pos0 + tq - 1) // CHUNK + 1) * CHUNK >= n_keys
    else:
        assert pos0 == 0 and tq == kblk and n_keys == Lp == T
        kb0, kstep = 1, 1
    topk = min(TOPK_MAX, n_keys // 4)
    rb = min(128, tq)

    t = jnp.arange(tq, dtype=jnp.int32)[:, None]
    c = jnp.arange(2 * kblk, dtype=jnp.int32)[None, :]
    if n_qt == 1:
        q_abs = pos0 + t
        s_abs = (kb0 - 2) * kblk + c
    else:
        q_abs = kblk + t
        s_abs = c
    vis = ((s_abs // CHUNK) <= (q_abs // CHUNK)) & (s_abs < (n_keys if n_qt == 1 else 2 * kblk))
    far_bucket = _far_bucket(kblk + 1, max(Lp, kblk + 2))

    n_kb = Lp // kblk
    hpg = N_HEADS_A // N_KV_A
    common = dict(TQ=tq, KB=kblk, kb0=kb0, kstep=kstep, topk=topk, far_bucket=far_bucket,
                  tie_bits=int(math.ceil(math.log2(Lp))) + 1)
    attend_scratch = [pltpu.VMEM((N_KV_A, hpg * tq, HEAD_DIM_A), BF16),
                      pltpu.VMEM((N_KV_A, hpg * tq, LANES), F32),
                      pltpu.VMEM((N_KV_A, hpg * tq, LANES), F32),
                      pltpu.VMEM((N_KV_A, hpg * tq, HEAD_DIM_A), F32)]
    if tq % LANES == 0:
        tab_t = _bias_table(rel_bias, _t5_bucket(s_abs - q_abs).T, shift_bucket=far_bucket)
        vt = jnp.swapaxes(vb.reshape(B, n_kb, kblk, D_KV), 2, 3)
        common.pop("far_bucket")
        return pl.pallas_call(
            functools.partial(_dsa_kernel_t, **common),
            grid=(B, n_qt),
            in_specs=[pl.BlockSpec((1, tq, D_ATTN), lambda b, i: (b, i, 0)),
                      pl.BlockSpec((1, tq, N_IDX_HEADS * IDX_DIM), lambda b, i: (b, i, 0)),
                      pl.BlockSpec((1, tq, LANES), lambda b, i: (b, i, 0)),
                      pl.BlockSpec((1, Lp, D_KV), lambda b, i: (b, 0, 0)),
                      pl.BlockSpec((1, n_kb, D_KV, kblk), lambda b, i: (b, 0, 0, 0)),
                      pl.BlockSpec((1, Lp, LANES), lambda b, i: (b, 0, 0)),
                      pl.BlockSpec((1, Lp, LANES), lambda b, i: (b, 0, 0)),
                      _const_spec((N_HEADS_A, 2 * kblk, tq)),
                      _const_spec((2 * kblk, tq))],
            out_specs=pl.BlockSpec((1, tq, D_ATTN), lambda b, i: (b, i, 0)),
            out_shape=jax.ShapeDtypeStruct((B, T, D_ATTN), BF16),
            scratch_shapes=[pltpu.VMEM((n_kb, kblk, tq), jnp.int32),
                            pltpu.VMEM((N_DIGITS, n_kb, kblk, tq), BF16),
                            pltpu.VMEM((n_kb, kblk, tq), F32),
                            pltpu.VMEM((N_KV_A, hpg * tq, HEAD_DIM_A), BF16),
                            pltpu.VMEM((8, tq), F32),
                            pltpu.VMEM((N_KV_A, HEAD_DIM_A, hpg * tq), F32),
                            pltpu.VMEM((2 * N_HEADS_A, kblk, tq), F32)],
            compiler_params=pltpu.CompilerParams(dimension_semantics=("arbitrary", "arbitrary"),
                                                 vmem_limit_bytes=VMEM_LIMIT),
            name="dsa_t",
        )(q, qi, kiwi, kb, vt, kie, kio, tab_t, vis.astype(F32).T)

    bias_near = _bias_table(rel_bias, _t5_bucket(s_abs - q_abs))
    kern = functools.partial(_dsa_kernel, RB=rb, **common)
    return pl.pallas_call(
        kern,
        grid=(B, n_qt),
        in_specs=[pl.BlockSpec(memory_space=pltpu.SMEM),
                  pl.BlockSpec((1, tq, D_ATTN), lambda b, i: (b, i, 0)),
                  pl.BlockSpec((1, tq, N_IDX_HEADS * IDX_DIM), lambda b, i: (b, i, 0)),
                  pl.BlockSpec((1, tq, LANES), lambda b, i: (b, i, 0)),
                  pl.BlockSpec((1, Lp, D_KV), lambda b, i: (b, 0, 0)),
                  pl.BlockSpec((1, Lp, D_KV), lambda b, i: (b, 0, 0)),
                  pl.BlockSpec((1, Lp, LANES), lambda b, i: (b, 0, 0)),
                  pl.BlockSpec((1, Lp, LANES), lambda b, i: (b, 0, 0)),
                  _const_spec((N_HEADS_A, tq, 2 * kblk)),
                  _const_spec((tq, 2 * kblk))],
        out_specs=pl.BlockSpec((1, tq, D_ATTN), lambda b, i: (b, i, 0)),
        out_shape=jax.ShapeDtypeStruct((B, T, D_ATTN), BF16),
        scratch_shapes=[pltpu.VMEM((n_kb, tq, kblk), jnp.int32),
                        pltpu.VMEM((n_kb, tq, kblk), F32),
                        pltpu.VMEM((N_IDX_HEADS, tq, LANES), F32),
                        pltpu.VMEM((N_KV_A, hpg * tq, HEAD_DIM_A), BF16),
                        pltpu.VMEM((tq, 1), F32),
                        pltpu.VMEM((N_KV_A, hpg * tq, LANES), F32),
                        pltpu.VMEM((N_KV_A, hpg * tq, LANES), F32),
                        pltpu.VMEM((N_KV_A, hpg * tq, HEAD_DIM_A), F32)],
        compiler_params=pltpu.CompilerParams(dimension_semantics=("arbitrary", "arbitrary"),
                                             vmem_limit_bytes=VMEM_LIMIT),
        name="dsa",
    )(rel_bias, q, qi, kiwi, kb, vb, kie, kio, bias_near, vis.astype(F32))


def _mix_kernel(x_ref, a_ref, u_ref, hist_ref, mk_ref, mv_ref, wpool_ref, pscale_ref, wout_ref, gx_ref,
                wxq_ref, wxo_ref, gffn_ref, wr_ref, br_ref, o_ref, ext, *, tm, pos0):
    t = pl.program_id(1)
    H = POOL_HIST + 1

    @pl.when(t == 0)
    def _():
        ext[0:H, :] = hist_ref[0]

    @pl.when(t > 0)
    def _():
        ext[0:H, :] = ext[tm:tm + H, :]

    ext[H:H + tm, :] = u_ref[0]

    da = _dot(a_ref[0], wout_ref[0:D_ATTN, :])

    pos = pos0 + t * tm + lax.broadcasted_iota(jnp.int32, (tm, 1), 0)
    pouts = []
    for gi, w in enumerate(POOL_WINDOWS):
        c0, c1 = gi * POOL_GROUP_DIM, (gi + 1) * POOL_GROUP_DIM
        wsum = ext[H:H + tm, c0:c1]
        for s in range(1, w):
            wsum = wsum + ext[H - s:H - s + tm, c0:c1]
        cnt = jnp.minimum(pos + 1, w).astype(F32)
        d = wsum / cnt - u_ref[0, :, c0:c1]
        y = _dot(d.astype(BF16), wpool_ref[gi]) * pscale_ref[:, c0:c1]
        pouts.append(y.astype(BF16))
    pcat = jnp.concatenate(pouts, axis=1)

    D = x_ref.shape[2]
    x1 = x_ref[0] + da + _dot(pcat, wout_ref[D_ATTN:D_ATTN + D_POOL, :])
    h = _rms(x1, gx_ref[...]).astype(BF16)
    qx = _dot(h, wxq_ref[...])
    heads = [slice(hh * XHEAD_DIM, (hh + 1) * XHEAD_DIM) for hh in range(N_XHEADS)]
    all_logits = [_dot_nt(qx[:, hs].astype(BF16), mk_ref[0, :, hs]) for hs in heads]
    outs = []
    for hs, logits in zip(heads, all_logits):
        logits = logits * (XHEAD_DIM ** -0.5)
        m = jnp.max(logits, axis=1, keepdims=True)
        p = jnp.exp(logits - m)
        l = jnp.sum(p, axis=1, keepdims=True)
        outs.append((_dot(p.astype(BF16), mv_ref[0, :, hs]) / l).astype(BF16))
    o = jnp.concatenate(outs, axis=1)
    x2 = x1 + _dot(o, wxo_ref[...])
    o_ref[0, :, 0:D] = x2
    hf = _rms(x2, gffn_ref[...]).astype(BF16)
    o_ref[0, :, D:D + LANES] = _route(_dot(hf, wr_ref[...]) + br_ref[...])


def _mix(x, a, u, hist, mk, mv, w_pool, pool_scale, w_out, g_x, w_xq, w_xo, g_ffn, w_r, b_r, *, pos0, tm):
    B, T, D = x.shape
    n_mem = mk.shape[1]
    H = POOL_HIST + 1
    assert T % tm == 0 and tm >= H
    hist16 = jnp.concatenate([jnp.zeros((B, 1, D_POOL), F32), hist], axis=1)
    return pl.pallas_call(
        functools.partial(_mix_kernel, tm=tm, pos0=pos0),
        grid=(B, T // tm),
        in_specs=[pl.BlockSpec((1, tm, D), lambda b, t: (b, t, 0)),
                  pl.BlockSpec((1, tm, D_ATTN), lambda b, t: (b, t, 0)),
                  pl.BlockSpec((1, tm, D_POOL), lambda b, t: (b, t, 0)),
                  pl.BlockSpec((1, H, D_POOL), lambda b, t: (b, 0, 0)),
                  pl.BlockSpec((1, n_mem, D_X), lambda b, t: (b, 0, 0)),
                  pl.BlockSpec((1, n_mem, D_X), lambda b, t: (b, 0, 0)),
                  _const_spec(w_pool.shape),
                  _const_spec((1, D_POOL)),
                  _const_spec(w_out.shape),
                  _const_spec((1, D)),
                  _const_spec(w_xq.shape),
                  _const_spec(w_xo.shape),
                  _const_spec((1, D)),
                  _const_spec(w_r.shape),
                  _const_spec((1, LANES))],
        out_specs=pl.BlockSpec((1, tm, D + LANES), lambda b, t: (b, t, 0)),
        out_shape=jax.ShapeDtypeStruct((B, T, D + LANES), F32),
        scratch_shapes=[pltpu.VMEM((H + tm, D_POOL), F32)],
        compiler_params=pltpu.CompilerParams(dimension_semantics=("arbitrary", "arbitrary"),
                                             vmem_limit_bytes=VMEM_LIMIT),
        name="mix",
    )(x, a, u, hist16, mk, mv, w_pool, pool_scale.reshape(1, D_POOL), w_out, g_x.reshape(1, D), w_xq, w_xo,
      g_ffn.reshape(1, D), w_r, b_r)


R_OFF = N_GROUPS


def _route(logits):
    tm = logits.shape[0]
    lane = lax.broadcasted_iota(jnp.int32, (tm, LANES), 1).astype(F32)
    ninf = -jnp.inf
    big = float(LANES)
    gl = jnp.where(lane < N_GROUPS, logits, ninf)
    gmax = jnp.max(gl, axis=1, keepdims=True)
    g_sel = jnp.min(jnp.where(gl == gmax, lane, big), axis=1, keepdims=True)
    g_prob = 1.0 / jnp.sum(jnp.exp(gl - gmax), axis=1, keepdims=True)
    lo = R_OFF + g_sel * EXPERTS_PER_GROUP
    el = jnp.where((lane >= lo) & (lane < lo + EXPERTS_PER_GROUP), logits, ninf)
    tv0 = jnp.max(el, axis=1, keepdims=True)
    ti0 = jnp.min(jnp.where(el == tv0, lane, big), axis=1, keepdims=True)
    el2 = jnp.where(lane == ti0, ninf, el)
    tv1 = jnp.max(el2, axis=1, keepdims=True)
    ti1 = jnp.min(jnp.where(el2 == tv1, lane, big), axis=1, keepdims=True)
    e1 = jnp.exp(tv1 - tv0)
    den = 1.0 + e1
    w0 = g_prob / den
    w1 = g_prob * e1 / den
    ids = jnp.where(lane == 0.0, jnp.minimum(ti0, ti1) - R_OFF,
                    jnp.where(lane == 1.0, jnp.maximum(ti0, ti1) - R_OFF, 0.0))
    return jnp.where(lane == ti0, w0, 0.0) + jnp.where(lane == ti1, w1, 0.0) + ids


def _expert(h, rec, e, wg, wu, wd):
    a = _dot(h, wg)
    b = _dot(h, wu)
    lane = lax.broadcasted_iota(jnp.int32, rec.shape, 1)
    cw = jnp.sum(jnp.where(lane == e + R_OFF, rec, 0.0), axis=1, keepdims=True)
    hid = a * jax.nn.sigmoid(a) * b * cw
    return _dot(hid.astype(BF16), wd)


def _moe_kernel(x_ref, gffn_ref, wg_ref, wu_ref, wd_ref, gfin_ref, o_ref, h_s, acc_s):
    e = pl.program_id(1)
    d = o_ref.shape[1]

    @pl.when(e == 0)
    def _():
        h_s[...] = _rms(x_ref[:, 0:d], gffn_ref[...]).astype(BF16)
        acc_s[...] = jnp.zeros(acc_s.shape, F32)

    acc_s[...] += _expert(h_s[...], x_ref[:, d:d + LANES], e, wg_ref[0], wu_ref[0], wd_ref[0])

    @pl.when(e == pl.num_programs(1) - 1)
    def _():
        o_ref[...] = _rms(x_ref[:, 0:d] + acc_s[...], gfin_ref[...])


def _moe(x, g_ffn, w_gate, w_up, w_down, g_final, *, tm):
    n, de = x.shape
    d = de - LANES
    assert n % tm == 0
    return pl.pallas_call(
        _moe_kernel,
        grid=(n // tm, N_EXPERTS),
        in_specs=[pl.BlockSpec((tm, de), lambda i, e: (i, 0)),
                  _const_spec((1, d)),
                  pl.BlockSpec((1, d, D_EXPERT), lambda i, e: (e, 0, 0)),
                  pl.BlockSpec((1, d, D_EXPERT), lambda i, e: (e, 0, 0)),
                  pl.BlockSpec((1, D_EXPERT, d), lambda i, e: (e, 0, 0)),
                  _const_spec((1, d))],
        out_specs=pl.BlockSpec((tm, d), lambda i, e: (i, 0)),
        out_shape=jax.ShapeDtypeStruct((n, d), F32),
        scratch_shapes=[pltpu.VMEM((tm, d), BF16),
                        pltpu.VMEM((tm, d), F32)],
        compiler_params=pltpu.CompilerParams(dimension_semantics=("arbitrary", "arbitrary"),
                                             vmem_limit_bytes=VMEM_LIMIT),
        name="moe",
    )(x, g_ffn.reshape(1, d), w_gate, w_up, w_down, g_final.reshape(1, d))


MOE_TILES_PER_STEP = 2


def _moe_sparse_kernel(ea_ref, eb_ref, nv_ref, idx_ref, idxn_ref, x_hbm, gffn_ref, gfin_ref, *rest, tm, tps):
    w_refs = [rest[6 * k:6 * k + 6] for k in range(tps)]
    o_hbm, xbuf, ybuf, gsem, ssem = rest[6 * tps:]
    t = pl.program_id(0)
    nt = pl.num_programs(0)
    d = o_hbm.shape[1]
    slot = lax.rem(t, 2)
    nslot = 1 - slot
    rows = tps * tm
    WAIT_ROWS = 8

    def step_rows(step):
        nvs = [nv_ref[tps * step + k] for k in range(tps)]
        return nvs, functools.reduce(lambda a, b: a + b, nvs)

    nvs, nv = step_rows(t)

    def gather_row(ids, s, r):
        return pltpu.make_async_copy(x_hbm.at[pl.ds(ids[0, 0, r], 1), :], xbuf.at[s, pl.ds(r, 1), :], gsem.at[s])

    def gather_loop(ids, s):
        def body(r, carry):
            gather_row(ids, s, r).start()
            return carry
        lax.fori_loop(0, rows, body, 0)

    def gather_wait(s):
        pltpu.make_async_copy(x_hbm.at[pl.ds(0, rows), :], xbuf.at[s], gsem.at[s]).wait()

    def scatter_rows(s, r, row, n):
        return pltpu.make_async_copy(ybuf.at[s, pl.ds(r, n), :], o_hbm.at[pl.ds(row, n), :], ssem.at[s])

    def scatter_wait(s, n):
        def body_many(r, carry):
            scatter_rows(s, 0, 0, WAIT_ROWS).wait()
            return carry
        lax.fori_loop(0, n // WAIT_ROWS, body_many, 0)

        def body_one(r, carry):
            scatter_rows(s, 0, 0, 1).wait()
            return carry
        lax.fori_loop(0, lax.rem(n, WAIT_ROWS), body_one, 0)

    @pl.when(t == 0)
    def _():
        gather_loop(idx_ref, 0)

    @pl.when(t >= 2)
    def _():
        scatter_wait(slot, step_rows(jnp.maximum(t - 2, 0))[1])

    gather_wait(slot)

    @pl.when(nv > 0)
    def _():
        tiles = []
        for k in range(tps):
            x = xbuf[slot, k * tm:(k + 1) * tm, :]
            xr = x[:, 0:d]
            tiles.append((xr, x[:, d:d + LANES], _rms(xr, gffn_ref[...]).astype(BF16)))
        for r in range(rows):
            gather_row(idxn_ref, nslot, r).start()
        pre = [[(_dot(h, wga[0]), _dot(h, wua[0])), (_dot(h, wgb[0]), _dot(h, wub[0]))]
               for (_, _, h), (wga, wua, _, wgb, wub, _) in zip(tiles, w_refs)]
        for k in range(tps):
            xr, rec, _ = tiles[k]
            lane = lax.broadcasted_iota(jnp.int32, rec.shape, 1)
            experts = (ea_ref[tps * t + k], eb_ref[tps * t + k])
            acc = None
            for (a, b), e, wd in zip(pre[k], experts, (w_refs[k][2], w_refs[k][5])):
                cw = jnp.sum(jnp.where(lane == e + R_OFF, rec, 0.0), axis=1, keepdims=True)
                hid = a * jax.nn.sigmoid(a) * b * cw
                out = _dot(hid.astype(BF16), wd[0])
                acc = out if acc is None else acc + out
            ybuf[slot, k * tm:(k + 1) * tm, :] = _rms(xr + acc, gfin_ref[...])

        for k in range(tps):
            def body_many(i, carry, k=k):
                for u in range(WAIT_ROWS):
                    r = k * tm + i * WAIT_ROWS + u
                    scatter_rows(slot, r, idx_ref[0, 0, r], 1).start()
                return carry
            lax.fori_loop(0, nvs[k] // WAIT_ROWS, body_many, 0)

            def body_one(i, carry, k=k):
                r = k * tm + i
                scatter_rows(slot, r, idx_ref[0, 0, r], 1).start()
                return carry
            lax.fori_loop((nvs[k] // WAIT_ROWS) * WAIT_ROWS, nvs[k], body_one, 0)

    @pl.when(nv <= 0)
    def _():
        gather_loop(idxn_ref, nslot)

    @pl.when(t == nt - 1)
    def _():
        gather_wait(nslot)

        @pl.when(t >= 1)
        def _():
            scatter_wait(nslot, step_rows(jnp.maximum(t - 1, 0))[1])
        scatter_wait(slot, nv)


def _moe_tile_rows(n):
    n_pairs = N_GROUPS * (EXPERTS_PER_GROUP * (EXPERTS_PER_GROUP - 1) // 2)
    mean = n / n_pairs
    rows = (mean + 3.0 * math.sqrt(mean)) / 2
    return int(min(256, max(SUB_BF16, -(-rows // SUB_BF16) * SUB_BF16)))


def _moe_sparse(x, g_ffn, w_gate, w_up, w_down, g_final, *, tm):
    n, de = x.shape
    d = de - LANES
    n_pairs = N_GROUPS * (EXPERTS_PER_GROUP * (EXPERTS_PER_GROUP - 1) // 2)
    tps = MOE_TILES_PER_STEP
    nt = -(-n // tm) + n_pairs
    nt = -(-nt // tps) * tps
    n_steps = nt // tps
    n_keys = N_EXPERTS * N_EXPERTS

    ids = x[:, d:d + 2].astype(jnp.int32)
    key = ids[:, 0] * N_EXPERTS + ids[:, 1]
    order = jnp.argsort(key).astype(jnp.int32)
    skey = key[order]
    count_below = lambda sorted_vals, q: jnp.sum((sorted_vals[None, :] < q[:, None]).astype(jnp.int32), axis=1)
    edges = count_below(skey, jnp.arange(n_keys + 1, dtype=jnp.int32))
    starts, ends = edges[:-1], edges[1:]
    tiles = (ends - starts + tm - 1) // tm
    cum = jnp.cumsum(tiles)
    total = cum[-1]
    tt = jnp.arange(nt, dtype=jnp.int32)
    tc = jnp.minimum(tt, total - 1)
    cls = count_below(cum, tc + 1)
    first = cum[cls] - tiles[cls]
    tstart = starts[cls] + (tc - first) * tm
    nv = jnp.where(tt < total, jnp.clip(ends[cls] - tstart, 0, tm), 0).astype(jnp.int32)
    ea = (cls // N_EXPERTS).astype(jnp.int32)
    eb = (cls % N_EXPERTS).astype(jnp.int32)
    rows = jnp.clip(tstart[:, None] + jnp.arange(tm, dtype=jnp.int32)[None, :], 0, n - 1)
    idx = order[rows].reshape(n_steps, 1, tps * tm)

    def wspec(shape, experts, k):
        return pl.BlockSpec(shape, lambda t, ea, eb, nv: ((ea, eb)[experts][tps * t + k], 0, 0))

    cspec = lambda shape: pl.BlockSpec(shape, lambda t, ea, eb, nv: (0,) * len(shape),
                                       pipeline_mode=pl.Buffered(1))
    w_specs, w_args = [], []
    for k in range(tps):
        for experts in (0, 1):
            w_specs += [wspec((1, d, D_EXPERT), experts, k), wspec((1, d, D_EXPERT), experts, k),
                        wspec((1, D_EXPERT, d), experts, k)]
            w_args += [w_gate, w_up, w_down]
    grid_spec = pltpu.PrefetchScalarGridSpec(
        num_scalar_prefetch=3,
        grid=(n_steps,),
        in_specs=[pl.BlockSpec((1, 1, tps * tm), lambda t, ea, eb, nv: (t, 0, 0), memory_space=pltpu.SMEM),
                  pl.BlockSpec((1, 1, tps * tm), lambda t, ea, eb, nv: (jnp.minimum(t + 1, n_steps - 1), 0, 0),
                               memory_space=pltpu.SMEM),
                  pl.BlockSpec(memory_space=pl.ANY),
                  cspec((1, d)), cspec((1, d))] + w_specs,
        out_specs=pl.BlockSpec(memory_space=pl.ANY),
        scratch_shapes=[pltpu.VMEM((2, tps * tm, de), F32),
                        pltpu.VMEM((2, tps * tm, d), F32),
                        pltpu.SemaphoreType.DMA((2,)),
                        pltpu.SemaphoreType.DMA((2,))])
    return pl.pallas_call(
        functools.partial(_moe_sparse_kernel, tm=tm, tps=tps),
        grid_spec=grid_spec,
        out_shape=jax.ShapeDtypeStruct((n, d), F32),
        compiler_params=pltpu.CompilerParams(dimension_semantics=("arbitrary",),
                                             vmem_limit_bytes=VMEM_LIMIT),
        name="moe_sparse",
    )(ea, eb, nv, idx, idx, x, g_ffn.reshape(1, d), g_final.reshape(1, d), *w_args)


C_Q = 0
C_QI = C_Q + D_ATTN
C_K = C_QI + N_IDX_HEADS * IDX_DIM
C_V = C_K + D_KV
C_KIWI = C_V + D_KV
C_KIE = C_KIWI + LANES
C_KIO = C_KIE + LANES
C_U = C_KIO + LANES
C_END = C_U + D_POOL


def _layout_w_in(w_in):
    d = w_in.shape[0]
    offs = np.cumsum((0, D_ATTN, D_KV, D_KV, N_IDX_HEADS * IDX_DIM, IDX_DIM, N_IDX_HEADS, D_POOL))
    wq, wk, wv, wqi, wki, wwi, wu = (w_in[:, offs[n]:offs[n + 1]] for n in range(7))
    z = lambda n: jnp.zeros((d, n), w_in.dtype)
    cat = jnp.concatenate([wq, wqi, wk, wv,
                           wki, wwi, z(LANES - IDX_DIM - N_IDX_HEADS),
                           wki, z(LANES - IDX_DIM),
                           z(LANES - IDX_DIM), wki,
                           wu], axis=1)
    assert cat.shape[1] == C_END
    return cat.astype(BF16)


IN_GROUPS = ((C_Q, C_QI, 1),
             (C_QI, C_K, 1),
             (C_K, C_V, 2),
             (C_V, C_KIWI, 2),
             (C_KIWI, C_KIE, 1),
             (C_KIE, C_KIO, 1),
             (C_KIO, C_U, 1),
             (C_U, C_END, 1))
IN_DTYPES = (BF16, BF16, F32, BF16, F32, BF16, F32, BF16, BF16, F32)


def _pad_rows(a, n):
    return jnp.pad(a, ((0, 0), (0, n - a.shape[1]), (0, 0)))


def _layer(x, pos0, caches, pool_hist, mk, mv, rel_bias, W, *, tm_in, tq, tm_mix, tm_moe, sparse_moe):
    B, T, D = x.shape
    n = B * T
    q, qi, k, kb, v, vb, kiwi, kie, kio, u = _norm_matmul(
        x.reshape(n, D), W["g_mix"], W["w_in"], IN_GROUPS, IN_DTYPES, tm_in)
    r3 = lambda a: a.reshape(B, T, a.shape[-1])
    kblk = KEY_BLOCK
    if caches is None:
        n_keys = T
        kb3, vb3, kie3, kio3 = r3(kb), r3(vb), r3(kie), r3(kio)
    else:
        k_hist, v_hist, ki_hist = caches
        past = k_hist.shape[1]
        n_keys = past + T
        lp = -(-n_keys // kblk) * kblk
        ki = r3(kiwi)[:, :, :IDX_DIM]
        ki_all = jnp.concatenate([ki_hist, ki], axis=1)
        zeros = jnp.zeros_like(ki_all)
        kb3 = _pad_rows(jnp.concatenate([k_hist.reshape(B, past, D_KV), r3(k)], axis=1), lp).astype(BF16)
        vb3 = _pad_rows(jnp.concatenate([v_hist.reshape(B, past, D_KV), r3(v)], axis=1), lp).astype(BF16)
        kie3 = _pad_rows(jnp.concatenate([ki_all, zeros], axis=2), lp).astype(BF16)
        kio3 = _pad_rows(jnp.concatenate([zeros, ki_all], axis=2), lp).astype(BF16)
    a_out = _dsa(r3(q), r3(qi), r3(kiwi), kb3, vb3, kie3, kio3, rel_bias,
                 n_keys=n_keys, pos0=pos0, tq=tq, kblk=kblk)
    x2 = _mix(x, a_out, r3(u), pool_hist, mk.astype(BF16), mv.astype(BF16), W["w_pool"], W["pool_scale"],
              W["w_out"], W["g_x"], W["w_xq"], W["w_xo"], W["g_ffn"], W["w_r"], W["b_r"], pos0=pos0, tm=tm_mix)
    moe = _moe_sparse if sparse_moe else _moe
    y = moe(x2.reshape(n, D + LANES), W["g_ffn"], W["w_gate"], W["w_up"], W["w_down"], W["g_final"], tm=tm_moe)
    new_pool = jnp.concatenate([pool_hist, r3(u)], axis=1)[:, -POOL_HIST:]
    return (y.reshape(B, T, D), k.reshape(B, T, N_KV_A, HEAD_DIM_A), v.reshape(B, T, N_KV_A, HEAD_DIM_A),
            r3(kiwi)[:, :, :IDX_DIM], new_pool)


def kernel(x_prompt, x_sample, mem_prompt, cache_k, cache_v, cache_k_idx, cache_pool, cache_mem_k, cache_mem_v, rel_bias, g_mix, w_in, w_pool, pool_scale, w_out, g_mem, w_mk, w_mv, g_x, w_xq, w_xo, g_ffn, w_rg, b_rg, w_re, b_re, w_gate, w_up, w_down, g_final):
    depth = g_mix.shape[0]
    assert depth == 1
    l = 0
    B, T, D = x_prompt.shape
    Bs, Ts, _ = x_sample.shape
    n_mem = mem_prompt.shape[1]
    past = cache_k.shape[2]

    w_r = jnp.concatenate([w_rg[l], w_re[l], jnp.zeros((D, LANES - N_GROUPS - N_EXPERTS), F32)], axis=1)
    W = dict(
        g_mix=g_mix[l], w_in=_layout_w_in(w_in[l]), w_pool=w_pool[l].astype(BF16), pool_scale=pool_scale[l],
        w_out=w_out[l].astype(BF16), g_x=g_x[l], w_xq=w_xq[l].astype(BF16), w_xo=w_xo[l].astype(BF16),
        g_ffn=g_ffn[l], w_r=w_r.astype(BF16),
        b_r=jnp.concatenate([b_rg[l], b_re[l], jnp.zeros((LANES - N_GROUPS - N_EXPERTS,), F32)]).reshape(1, LANES),
        w_gate=w_gate[l].astype(BF16), w_up=w_up[l].astype(BF16), w_down=w_down[l].astype(BF16),
        g_final=g_final)

    w_m = jnp.concatenate([w_mk[l], w_mv[l]], axis=1).astype(BF16)
    mk, mv = _norm_matmul(mem_prompt.reshape(B * n_mem, D), g_mem[l], w_m,
                          ((0, D_X, 1), (D_X, 2 * D_X, 1)), (F32, F32), TOKEN_TILE)
    mk = mk.reshape(B, n_mem, D_X)
    mv = mv.reshape(B, n_mem, D_X)

    yp, kp, vp, kip, pp = _layer(x_prompt, 0, None, jnp.zeros((B, POOL_HIST, D_POOL), F32), mk, mv, rel_bias, W,
                                 tm_in=TOKEN_TILE, tq=QUERY_TILE, tm_mix=TOKEN_TILE, tm_moe=_moe_tile_rows(B * T),
                                 sparse_moe=True)
    ys, ks, vs, kis, ps = _layer(x_sample, past,
                                 (cache_k[l], cache_v[l], cache_k_idx[l]), cache_pool[l],
                                 cache_mem_k[l].reshape(Bs, n_mem, D_X), cache_mem_v[l].reshape(Bs, n_mem, D_X),
                                 rel_bias, W, tm_in=Bs * Ts, tq=Ts, tm_mix=Ts, tm_moe=Bs * Ts, sparse_moe=False)
    st = lambda a: a[None]
    return (yp, ys, st(kp), st(vp), st(kip), st(pp),
            st(mk.reshape(B, n_mem, N_XHEADS, XHEAD_DIM)), st(mv.reshape(B, n_mem, N_XHEADS, XHEAD_DIM)),
            st(ks), st(vs), st(kis), st(ps))
```

```python
import functools
import math

import numpy as np
import jax
import jax.numpy as jnp
from jax import lax
from jax.experimental import pallas as pl
from jax.experimental.pallas import tpu as pltpu

F32 = jnp.float32
BF16 = jnp.bfloat16

CHUNK = 64
N_HEADS_A = 8
HEAD_DIM_A = 128
N_KV_A = 2
D_ATTN = N_HEADS_A * HEAD_DIM_A
D_KV = N_KV_A * HEAD_DIM_A
N_IDX_HEADS = 8
IDX_DIM = 64
TOPK_MAX = 256
POOL_WINDOWS = (2, 4, 8, 16)
POOL_GROUP_DIM = 256
D_POOL = len(POOL_WINDOWS) * POOL_GROUP_DIM
POOL_HIST = max(POOL_WINDOWS) - 1
N_BUCKETS = 32
MAX_DISTANCE = 128
N_XHEADS = 4
XHEAD_DIM = 128
D_X = N_XHEADS * XHEAD_DIM
N_GROUPS = 4
EXPERTS_PER_GROUP = 8
N_EXPERTS = N_GROUPS * EXPERTS_PER_GROUP
D_EXPERT = 256
EPS = 1e-6

LANES = 128
INT_MIN = -(2 ** 31)
NEG = -1e30
LOG2E = math.log2(math.e)
VMEM_LIMIT = 48 * 1024 * 1024
TOKEN_TILE = 256
KEY_BLOCK = 256
QUERY_TILE = KEY_BLOCK


def _rms(x, g):
    ms = jnp.mean(x * x, axis=-1, keepdims=True)
    return x * lax.rsqrt(ms + EPS) * g


def _dot(a, b):
    return jnp.dot(a, b, preferred_element_type=F32)


def _dot_nt(a, b):
    return lax.dot_general(a, b, (((1,), (1,)), ((), ())), preferred_element_type=F32)


def _const_spec(shape):
    nd = len(shape)
    return pl.BlockSpec(shape, lambda *_: (0,) * nd, pipeline_mode=pl.Buffered(1))


def _norm_matmul_kernel(x_ref, g_ref, w_ref, *out_refs, groups):
    h = _rms(x_ref[...], g_ref[...]).astype(BF16)
    k = 0
    for c0, c1, n_out in groups:
        acc = _dot(h, w_ref[:, c0:c1])
        for _ in range(n_out):
            out_refs[k][...] = acc.astype(out_refs[k].dtype)
            k += 1


def _norm_matmul(x, g, w, groups, out_dtypes, tm):
    n, d = x.shape
    assert n % tm == 0
    widths = []
    for c0, c1, n_out in groups:
        widths += [c1 - c0] * n_out
    return pl.pallas_call(
        functools.partial(_norm_matmul_kernel, groups=tuple(groups)),
        grid=(n // tm,),
        in_specs=[pl.BlockSpec((tm, d), lambda i: (i, 0)),
                  _const_spec((1, d)),
                  _const_spec(w.shape)],
        out_specs=[pl.BlockSpec((tm, wd), lambda i: (i, 0)) for wd in widths],
        out_shape=[jax.ShapeDtypeStruct((n, wd), dt) for wd, dt in zip(widths, out_dtypes)],
        compiler_params=pltpu.CompilerParams(dimension_semantics=("arbitrary",),
                                             vmem_limit_bytes=VMEM_LIMIT),
        name="norm_matmul",
    )(x, g.reshape(1, d), w)


def _bias_table_kernel(relb_ref, bucket_ref, o_ref, *, shift_bucket):
    bucket = bucket_ref[...]
    for h in range(N_HEADS_A):
        acc = jnp.zeros(bucket.shape, F32)
        for b in range(N_BUCKETS):
            acc = jnp.where(bucket == b, relb_ref[b, h], acc)
        if shift_bucket is None:
            o_ref[h] = acc * LOG2E
        else:
            o_ref[h] = (acc - relb_ref[shift_bucket, h]) * (HEAD_DIM_A ** 0.5)


def _bias_table(rel_bias, bucket, shift_bucket=None):
    tq, w = bucket.shape
    return pl.pallas_call(
        functools.partial(_bias_table_kernel, shift_bucket=shift_bucket),
        in_specs=[pl.BlockSpec(memory_space=pltpu.SMEM),
                  pl.BlockSpec((tq, w), lambda: (0, 0))],
        out_specs=pl.BlockSpec((N_HEADS_A, tq, w), lambda: (0, 0, 0)),
        out_shape=jax.ShapeDtypeStruct((N_HEADS_A, tq, w), F32),
        name="bias_table",
    )(rel_bias, bucket)


def _t5_bucket(rel):
    nb = N_BUCKETS // 2
    max_exact = nb // 2
    bucket = (rel > 0).astype(jnp.int32) * nb
    n = jnp.abs(rel)
    nf = jnp.maximum(n, 1).astype(F32)
    large = max_exact + (jnp.log(nf / max_exact) / math.log(MAX_DISTANCE / max_exact)
                         * (nb - max_exact)).astype(jnp.int32)
    large = jnp.minimum(large, nb - 1)
    return bucket + jnp.where(n < max_exact, n, large)


def _far_bucket(min_dist, max_dist):
    nb = N_BUCKETS // 2
    max_exact = nb // 2
    n = np.arange(min_dist, max_dist + 1, dtype=np.float64)
    large = max_exact + np.floor(np.log(n / max_exact) / math.log(MAX_DISTANCE / max_exact)
                                 * (nb - max_exact) * (1 - 1e-6)).astype(np.int64)
    assert large.min() >= nb - 1, "far key blocks must sit in the saturated distance bucket"
    return nb - 1


def _dsa_kernel(relb_ref, q_ref, qi_ref, kiwi_ref, k_ref, v_ref, kie_ref, kio_ref, bnear_ref, vis_ref,
                o_ref, keys, madd, wb, qs, xcut, mrun, lrun, acc_s,
                *, TQ, KB, RB, kb0, kstep, topk, far_bucket, tie_bits):
    i = pl.program_id(1)
    nkb = kb0 + kstep * i
    n_far = jnp.maximum(nkb - 2, 0)
    n_lc = KB // LANES

    kiwi = kiwi_ref[0]
    for h in range(N_IDX_HEADS):
        w = (kiwi[:, IDX_DIM + h:IDX_DIM + h + 1] * (N_IDX_HEADS ** -0.5)).astype(BF16).astype(F32)
        wb[h] = jnp.broadcast_to(w * (IDX_DIM ** -0.5), (TQ, LANES))

    def idx_block(j, near_col):
        c0 = pl.multiple_of(j * KB, KB)
        ke = kie_ref[0, pl.ds(c0, KB), :]
        ko = kio_ref[0, pl.ds(c0, KB), :]
        acc = [jnp.zeros((TQ, LANES), F32) for _ in range(n_lc)]
        for hp in range(N_IDX_HEADS // 2):
            qh = qi_ref[0, :, hp * LANES:(hp + 1) * LANES]
            se = _dot_nt(qh, ke)
            so = _dot_nt(qh, ko)
            we = wb[2 * hp]
            wo = wb[2 * hp + 1]
            for c in range(n_lc):
                cs = slice(c * LANES, (c + 1) * LANES)
                re = jnp.maximum(se[:, cs], 0.0).astype(BF16).astype(F32)
                ro = jnp.maximum(so[:, cs], 0.0).astype(BF16).astype(F32)
                acc[c] = acc[c] + we * re + wo * ro
        for c in range(n_lc):
            bits = lax.bitcast_convert_type(acc[c], jnp.int32)
            key = bits ^ ((bits >> 31) & jnp.int32(0x7FFFFFFF))
            if near_col is not None:
                vis = vis_ref[:, near_col + c * LANES:near_col + (c + 1) * LANES]
                key = jnp.where(vis > 0.0, key, jnp.int32(INT_MIN))
            keys[j, :, c * LANES:(c + 1) * LANES] = key

    def far_idx(j, carry):
        idx_block(j, None)
        return carry

    lax.fori_loop(0, n_far, far_idx, 0)

    @pl.when(nkb >= 2)
    def _():
        idx_block(nkb - 2, 0)

    idx_block(nkb - 1, KB)

    kf = float(topk)
    lane_f = lax.broadcasted_iota(jnp.int32, (RB, LANES), 1).astype(F32)
    NRB = TQ // RB
    rbs = [slice(rb * RB, (rb + 1) * RB) for rb in range(NRB)]

    def count(pred):
        def body(jj, accs):
            out = []
            for rb in range(NRB):
                blk = keys[jj, rbs[rb], :]
                acc = accs[rb]
                for c in range(n_lc):
                    acc = acc + jnp.where(pred(rb, blk[:, c * LANES:(c + 1) * LANES], jj, c), 1.0, 0.0)
                out.append(acc)
            return tuple(out)
        accs = lax.fori_loop(0, nkb, body, tuple(jnp.zeros((RB, LANES), F32) for _ in range(NRB)))
        return [jnp.sum(acc, axis=1, keepdims=True) for acc in accs]

    def count_ge(cands):
        cb = [jnp.broadcast_to(cand, (RB, LANES)) for cand in cands]
        return count(lambda rb, kblk, jj, c: kblk >= cb[rb])

    def bis_body(p, thrs):
        cands = [thr + lax.shift_left(jnp.int32(1), 31 - p) for thr in thrs]
        cnts = count_ge(cands)
        return tuple(jnp.where(cnt >= kf, cand, thr) for cnt, cand, thr in zip(cnts, cands, thrs))

    thrs = lax.fori_loop(0, 32, bis_body, tuple(jnp.full((RB, 1), INT_MIN, jnp.int32) for _ in range(NRB)))
    thrs = [jnp.maximum(thr, jnp.int32(INT_MIN + 1)) for thr in thrs]
    cnt_ge = count_ge(thrs)
    cnt_gt = count_ge([thr + 1 for thr in thrs])
    needs = [kf - c for c in cnt_gt]
    thrb = [jnp.broadcast_to(thr, (RB, LANES)) for thr in thrs]

    xcut[...] = jnp.full(xcut.shape, 2.0 ** 24, F32)
    excess = cnt_ge[0] - kf
    for c in cnt_ge[1:]:
        excess = jnp.maximum(excess, c - kf)

    @pl.when(jnp.max(excess) > 0.0)
    def _():
        def count_eq_lt(xs):
            xb = [jnp.broadcast_to(x, (RB, LANES)) for x in xs]

            def pred(rb, kblk, jj, c):
                col = lane_f + (jj * KB + c * LANES).astype(F32)
                return jnp.where(kblk == thrb[rb], col, 2.0 ** 25) < xb[rb]
            return count(pred)

        def tie_body(p, xs):
            cands = [x + lax.shift_left(jnp.int32(1), tie_bits - 1 - p).astype(F32) for x in xs]
            cnts = count_eq_lt(cands)
            return tuple(jnp.where(cnt < need, cand, x) for cnt, need, cand, x in zip(cnts, needs, cands, xs))

        xs = lax.fori_loop(0, tie_bits, tie_body, tuple(jnp.zeros((RB, 1), F32) for _ in range(NRB)))
        for rb in range(NRB):
            xcut[rbs[rb], :] = xs[rb]

    xb = [jnp.broadcast_to(xcut[rbs[rb], :], (RB, LANES)) for rb in range(NRB)]

    def mask_body(jj, carry):
        for rb in range(NRB):
            blk = keys[jj, rbs[rb], :]
            for c in range(n_lc):
                kblk = blk[:, c * LANES:(c + 1) * LANES]
                col = lane_f + (jj * KB + c * LANES).astype(F32)
                keep = (kblk > thrb[rb]) | ((kblk == thrb[rb]) & (col <= xb[rb]))
                madd[jj, rbs[rb], c * LANES:(c + 1) * LANES] = jnp.where(keep, 0.0, NEG)
        return carry

    lax.fori_loop(0, nkb, mask_body, 0)

    _attend(relb_ref, q_ref, k_ref, v_ref, bnear_ref, o_ref, madd, qs, mrun, lrun, acc_s,
            TQ=TQ, KB=KB, nkb=nkb, n_far=n_far, far_bucket=far_bucket)


def _attend(relb_ref, q_ref, k_ref, v_ref, bnear_ref, o_ref, madd, qs, mrun, lrun, acc_s,
            *, TQ, KB, nkb, n_far, far_bucket):
    n_lc = KB // LANES
    HPG = N_HEADS_A // N_KV_A
    for g in range(N_KV_A):
        for r in range(HPG):
            h = g * HPG + r
            qs[g, r * TQ:(r + 1) * TQ, :] = q_ref[0, :, h * HEAD_DIM_A:(h + 1) * HEAD_DIM_A]

    def logits(j, near_col, g):
        c0 = pl.multiple_of(j * KB, KB)
        kg = k_ref[0, pl.ds(c0, KB), g * HEAD_DIM_A:(g + 1) * HEAD_DIM_A]
        s = _dot_nt(qs[g], kg)
        mk = madd[j]
        out = []
        for r in range(HPG):
            h = g * HPG + r
            if near_col is None:
                bias = relb_ref[far_bucket, h] * LOG2E
            else:
                bias = bnear_ref[h, :, near_col:near_col + KB]
            out.append(s[r * TQ:(r + 1) * TQ] * (HEAD_DIM_A ** -0.5 * LOG2E) + bias + mk)
        return out

    def over_blocks(fn):
        def far(j, carry):
            fn(j, None)
            return carry

        lax.fori_loop(0, n_far, far, 0)

        @pl.when(nkb >= 2)
        def _():
            fn(nkb - 2, 0)

        fn(nkb - 1, KB)

    mrun[...] = jnp.full(mrun.shape, NEG, F32)

    def max_block(j, near_col):
        for g in range(N_KV_A):
            for r, sh in enumerate(logits(j, near_col, g)):
                rs = slice(r * TQ, (r + 1) * TQ)
                m = mrun[g, rs, :]
                for c in range(n_lc):
                    m = jnp.maximum(m, sh[:, c * LANES:(c + 1) * LANES])
                mrun[g, rs, :] = m

    over_blocks(max_block)

    for g in range(N_KV_A):
        mrun[g] = jnp.broadcast_to(jnp.max(mrun[g], axis=1, keepdims=True), mrun.shape[1:])
    lrun[...] = jnp.zeros(lrun.shape, F32)
    acc_s[...] = jnp.zeros(acc_s.shape, F32)

    def pv_block(j, near_col):
        c0 = pl.multiple_of(j * KB, KB)
        for g in range(N_KV_A):
            vg = v_ref[0, pl.ds(c0, KB), g * HEAD_DIM_A:(g + 1) * HEAD_DIM_A]
            parts = []
            for r, sh in enumerate(logits(j, near_col, g)):
                rs = slice(r * TQ, (r + 1) * TQ)
                mb = mrun[g, rs, :]
                l = lrun[g, rs, :]
                pcs = []
                for c in range(n_lc):
                    p = jnp.exp2(sh[:, c * LANES:(c + 1) * LANES] - mb)
                    l = l + p
                    pcs.append(p.astype(BF16))
                lrun[g, rs, :] = l
                parts.append(jnp.concatenate(pcs, axis=1))
            acc_s[g] += _dot(jnp.concatenate(parts, axis=0), vg)

    over_blocks(pv_block)

    for g in range(N_KV_A):
        out = acc_s[g] / jnp.sum(lrun[g], axis=1, keepdims=True)
        for r in range(HPG):
            h = g * HPG + r
            o_ref[0, :, h * HEAD_DIM_A:(h + 1) * HEAD_DIM_A] = out[r * TQ:(r + 1) * TQ].astype(o_ref.dtype)


DIGIT_BITS = 8
N_DIGITS = 32 // DIGIT_BITS
SUB_BF16 = 16
COUNT_BLOCKS = 4
FAR_BLOCKS = 3


def _dsa_kernel_t(q_ref, qi_ref, kiwi_ref, k_ref, vt_ref, kie_ref, kio_ref, bnear_ref, vist_ref,
                  o_ref, keys, dig, madd, qs, xcut, acc_s, tbuf,
                  *, TQ, KB, kb0, kstep, topk, tie_bits):
    i = pl.program_id(1)
    nkb = kb0 + kstep * i
    n_far = jnp.maximum(nkb - 2, 0)
    n_slab = KB // SUB_BF16

    w_t = kiwi_ref[0].T
    wrow = [(w_t[IDX_DIM + h:IDX_DIM + h + 1, :] * (N_IDX_HEADS ** -0.5)).astype(BF16).astype(F32)
            * (IDX_DIM ** -0.5) for h in range(N_IDX_HEADS)]

    def idx_dots(j):
        c0 = pl.multiple_of(j * KB, KB)
        ke = kie_ref[0, pl.ds(c0, KB), :]
        ko = kio_ref[0, pl.ds(c0, KB), :]
        raw = []
        for hp in range(N_IDX_HEADS // 2):
            qh = qi_ref[0, :, hp * LANES:(hp + 1) * LANES]
            raw.append((_dot_nt(ke, qh), _dot_nt(ko, qh)))
        return raw

    def idx_finish(j, raw, near_row):
        acc = jnp.zeros((KB, TQ), F32)
        for hp, (se, so) in enumerate(raw):
            re = jnp.maximum(se.astype(BF16), 0.0).astype(F32)
            ro = jnp.maximum(so.astype(BF16), 0.0).astype(F32)
            acc = acc + wrow[2 * hp] * re + wrow[2 * hp + 1] * ro
        bits = lax.bitcast_convert_type(acc, jnp.int32)
        key = bits ^ ((bits >> 31) & jnp.int32(0x7FFFFFFF))
        if near_row is not None:
            key = jnp.where(vist_ref[near_row:near_row + KB, :] > 0.0, key, jnp.int32(INT_MIN))
        keys[j] = key
        for dk in range(N_DIGITS):
            sh = 32 - DIGIT_BITS * (dk + 1)
            d = (key >> sh) if dk == 0 else ((key >> sh) & jnp.int32(2 ** DIGIT_BITS - 1))
            dig[dk, j] = d.astype(F32).astype(BF16)

    def idx_blocks(js, near_rows):
        raws = [idx_dots(j) for j in js]
        for j, raw, near_row in zip(js, raws, near_rows):
            idx_finish(j, raw, near_row)

    n_trips = n_far // FAR_BLOCKS
    n_left = n_far - n_trips * FAR_BLOCKS

    def far_idx_trip(jt, carry):
        idx_blocks([FAR_BLOCKS * jt + u for u in range(FAR_BLOCKS)], [None] * FAR_BLOCKS)
        return carry

    lax.fori_loop(0, n_trips, far_idx_trip, 0)

    for left in range(1, FAR_BLOCKS):
        @pl.when(n_left == left)
        def _(left=left):
            idx_blocks([n_far - left + u for u in range(left)], [None] * left)

    @pl.when(nkb >= 2)
    def _():
        idx_blocks([nkb - 2, nkb - 1], [0, KB])

    @pl.when(nkb < 2)
    def _():
        idx_blocks([nkb - 1], [KB])

    one = jnp.ones((SUB_BF16, TQ), BF16)
    zero = jnp.zeros((SUB_BF16, TQ), BF16)

    def count_ge(dk, cand):
        cb = jnp.broadcast_to(cand.astype(BF16), (SUB_BF16, TQ))

        def block_count(jj):
            blk = dig[dk, jj]
            parts = [jnp.where(blk[s * SUB_BF16:(s + 1) * SUB_BF16, :] >= cb, one, zero) for s in range(n_slab)]
            while len(parts) > 1:
                parts = [parts[n] + parts[n + 1] for n in range(0, len(parts), 2)]
            return parts[0]

        def body_many(jq, tot):
            cnts = [block_count(COUNT_BLOCKS * jq + u) for u in range(COUNT_BLOCKS)]
            while len(cnts) > 1:
                cnts = [cnts[n] + cnts[n + 1] for n in range(0, len(cnts), 2)]
            return tot + cnts[0].astype(F32)

        def body_one(jj, tot):
            return tot + block_count(jj).astype(F32)

        tot = lax.fori_loop(0, nkb // COUNT_BLOCKS, body_many, jnp.zeros((SUB_BF16, TQ), F32))
        tot = lax.fori_loop((nkb // COUNT_BLOCKS) * COUNT_BLOCKS, nkb, body_one, tot)
        return jnp.sum(tot, axis=0, keepdims=True)

    kneed = jnp.full((1, TQ), float(topk), F32)
    thr = jnp.zeros((1, TQ), jnp.int32)
    cnt_eq = None
    for dk in range(N_DIGITS):
        lo = -float(2 ** (DIGIT_BITS - 1)) if dk == 0 else 0.0

        def bis_body(p, carry, dk=dk, kneed=kneed):
            t, c_rej = carry
            cand = t + lax.shift_left(jnp.int32(1), DIGIT_BITS - 1 - p).astype(F32)
            cnt = count_ge(dk, cand)
            ok = cnt >= kneed
            return jnp.where(ok, cand, t), jnp.where(ok, c_rej, cnt)

        t, n_gt = lax.fori_loop(0, DIGIT_BITS, bis_body,
                                (jnp.full((1, TQ), lo, F32), jnp.zeros((1, TQ), F32)))
        if dk == N_DIGITS - 1:
            cnt_eq = count_ge(dk, t) - n_gt
        kneed = kneed - n_gt
        thr = thr + lax.shift_left(t.astype(jnp.int32), 32 - DIGIT_BITS * (dk + 1))
        if dk + 1 < N_DIGITS:
            tb = t.astype(BF16)

            def prep(jj, carry, dk=dk, tb=tb):
                dig[dk + 1, jj] = jnp.where(dig[dk, jj] == tb, dig[dk + 1, jj], jnp.asarray(-1.0, BF16))
                return carry

            lax.fori_loop(0, nkb, prep, 0)

    thr = jnp.maximum(thr, jnp.int32(INT_MIN + 1))
    need = kneed
    sub_f = lax.broadcasted_iota(jnp.int32, (KB, TQ), 0).astype(F32)

    xcut[...] = jnp.full(xcut.shape, 2.0 ** 24, F32)

    @pl.when(jnp.max(cnt_eq - need) > 0.0)
    def _():
        def count_eq_lt(x):
            def body(jj, tot):
                col = sub_f + (jj * KB).astype(F32)
                hit = jnp.where(jnp.where(keys[jj] == thr, col, 2.0 ** 25) < x, 1.0, 0.0)
                for s in range(KB // 8):
                    tot = tot + hit[s * 8:(s + 1) * 8, :]
                return tot
            tot = lax.fori_loop(0, nkb, body, jnp.zeros((8, TQ), F32))
            return jnp.sum(tot, axis=0, keepdims=True)

        def tie_body(p, x):
            cand = x + lax.shift_left(jnp.int32(1), tie_bits - 1 - p).astype(F32)
            return jnp.where(count_eq_lt(cand) < need, cand, x)

        x = lax.fori_loop(0, tie_bits, tie_body, jnp.zeros((1, TQ), F32))
        xcut[...] = jnp.broadcast_to(x, xcut.shape)

    xc = xcut[0:1, :]

    def mask_body(jj, carry):
        kblk = keys[jj]
        col = sub_f + (jj * KB).astype(F32)
        at_thr = jnp.where(col <= xc, 0.0, NEG)
        madd[jj] = jnp.where(kblk > thr, 0.0, jnp.where(kblk == thr, at_thr, NEG))
        return carry

    lax.fori_loop(0, nkb, mask_body, 0)

    HPG = N_HEADS_A // N_KV_A
    c2 = HEAD_DIM_A ** -0.5 * LOG2E
    for g in range(N_KV_A):
        for r in range(HPG):
            h = g * HPG + r
            qs[g, r * TQ:(r + 1) * TQ, :] = q_ref[0, :, h * HEAD_DIM_A:(h + 1) * HEAD_DIM_A]
    acc_s[...] = jnp.zeros(acc_s.shape, F32)

    def attn_blocks(js, near_rows, carry):
        ms, ls = carry
        s_all = []
        for j in js:
            c0 = pl.multiple_of(j * KB, KB)
            s_all.append([_dot_nt(k_ref[0, pl.ds(c0, KB), g * HEAD_DIM_A:(g + 1) * HEAD_DIM_A], qs[g])
                          for g in range(N_KV_A)])
        for b, (j, near_row) in enumerate(zip(js, near_rows)):
            mk = madd[j]
            m_news = []
            for g in range(N_KV_A):
                for r in range(HPG):
                    h = g * HPG + r
                    t = s_all[b][g][:, r * TQ:(r + 1) * TQ] + mk
                    if near_row is not None:
                        t = t + bnear_ref[h, near_row:near_row + KB, :]
                    tbuf[b * N_HEADS_A + h] = t
                    m_news.append(jnp.maximum(ms[h], jnp.max(t, axis=0, keepdims=True)))
            ms_out, ls_out = [], []
            for g in range(N_KV_A):
                vt = vt_ref[0, j, g * HEAD_DIM_A:(g + 1) * HEAD_DIM_A, :]
                for r in range(HPG):
                    h = g * HPG + r
                    cs = slice(r * TQ, (r + 1) * TQ)
                    m_new = m_news[h]
                    alpha = jnp.exp2((ms[h] - m_new) * c2)
                    p = jnp.exp2((tbuf[b * N_HEADS_A + h] - m_new) * c2)
                    ls_out.append(alpha * ls[h] + jnp.sum(p, axis=0, keepdims=True))
                    ms_out.append(m_new)
                    acc_s[g, :, cs] = alpha * acc_s[g, :, cs] + _dot(vt, p.astype(BF16))
            ms, ls = tuple(ms_out), tuple(ls_out)
        return ms, ls

    carry = (tuple(jnp.full((1, TQ), NEG, F32) for _ in range(N_HEADS_A)),
             tuple(jnp.zeros((1, TQ), F32) for _ in range(N_HEADS_A)))
    carry = lax.fori_loop(
        0, n_trips,
        lambda jt, c: attn_blocks([FAR_BLOCKS * jt + u for u in range(FAR_BLOCKS)], [None] * FAR_BLOCKS, c), carry)
    for left in range(1, FAR_BLOCKS):
        carry = lax.cond(n_left == left,
                         lambda c, left=left: attn_blocks([n_far - left + u for u in range(left)], [None] * left, c),
                         lambda c: c, carry)
    ms, ls = lax.cond(nkb >= 2, lambda c: attn_blocks([nkb - 2, nkb - 1], [0, KB], c),
                      lambda c: attn_blocks([nkb - 1], [KB], c), carry)

    for g in range(N_KV_A):
        for r in range(HPG):
            h = g * HPG + r
            out_t = acc_s[g, :, r * TQ:(r + 1) * TQ] / ls[h]
            o_ref[0, :, h * HEAD_DIM_A:(h + 1) * HEAD_DIM_A] = out_t.T.astype(o_ref.dtype)


def _dsa(q, qi, kiwi, kb, vb, kie, kio, rel_bias, *, n_keys, pos0, tq, kblk):
    B, T, _ = q.shape
    Lp = kb.shape[1]
    assert T % tq == 0 and Lp % kblk == 0 and tq % CHUNK == 0 or T == tq
    n_qt = T // tq
    if n_qt == 1:
        kb0, kstep = Lp // kblk, 0
        assert ((pos0 + tq - 1) // CHUNK + 1) * CHUNK >= n_keys
    else:
        assert pos0 == 0 and tq == kblk and n_keys == Lp == T
        kb0, kstep = 1, 1
    topk = min(TOPK_MAX, n_keys // 4)
    rb = min(128, tq)

    t = jnp.arange(tq, dtype=jnp.int32)[:, None]
    c = jnp.arange(2 * kblk, dtype=jnp.int32)[None, :]
    if n_qt == 1:
        q_abs = pos0 + t
        s_abs = (kb0 - 2) * kblk + c
    else:
        q_abs = kblk + t
        s_abs = c
    vis = ((s_abs // CHUNK) <= (q_abs // CHUNK)) & (s_abs < (n_keys if n_qt == 1 else 2 * kblk))
    far_bucket = _far_bucket(kblk + 1, max(Lp, kblk + 2))

    n_kb = Lp // kblk
    hpg = N_HEADS_A // N_KV_A
    common = dict(TQ=tq, KB=kblk, kb0=kb0, kstep=kstep, topk=topk, far_bucket=far_bucket,
                  tie_bits=int(math.ceil(math.log2(Lp))) + 1)
    attend_scratch = [pltpu.VMEM((N_KV_A, hpg * tq, HEAD_DIM_A), BF16),
                      pltpu.VMEM((N_KV_A, hpg * tq, LANES), F32),
                      pltpu.VMEM((N_KV_A, hpg * tq, LANES), F32),
                      pltpu.VMEM((N_KV_A, hpg * tq, HEAD_DIM_A), F32)]
    if tq % LANES == 0:
        tab_t = _bias_table(rel_bias, _t5_bucket(s_abs - q_abs).T, shift_bucket=far_bucket)
        vt = jnp.swapaxes(vb.reshape(B, n_kb, kblk, D_KV), 2, 3)
        common.pop("far_bucket")
        return pl.pallas_call(
            functools.partial(_dsa_kernel_t, **common),
            grid=(B, n_qt),
            in_specs=[pl.BlockSpec((1, tq, D_ATTN), lambda b, i: (b, i, 0)),
                      pl.BlockSpec((1, tq, N_IDX_HEADS * IDX_DIM), lambda b, i: (b, i, 0)),
                      pl.BlockSpec((1, tq, LANES), lambda b, i: (b, i, 0)),
                      pl.BlockSpec((1, Lp, D_KV), lambda b, i: (b, 0, 0)),
                      pl.BlockSpec((1, n_kb, D_KV, kblk), lambda b, i: (b, 0, 0, 0)),
                      pl.BlockSpec((1, Lp, LANES), lambda b, i: (b, 0, 0)),
                      pl.BlockSpec((1, Lp, LANES), lambda b, i: (b, 0, 0)),
                      _const_spec((N_HEADS_A, 2 * kblk, tq)),
                      _const_spec((2 * kblk, tq))],
            out_specs=pl.BlockSpec((1, tq, D_ATTN), lambda b, i: (b, i, 0)),
            out_shape=jax.ShapeDtypeStruct((B, T, D_ATTN), BF16),
            scratch_shapes=[pltpu.VMEM((n_kb, kblk, tq), jnp.int32),
                            pltpu.VMEM((N_DIGITS, n_kb, kblk, tq), BF16),
                            pltpu.VMEM((n_kb, kblk, tq), F32),
                            pltpu.VMEM((N_KV_A, hpg * tq, HEAD_DIM_A), BF16),
                            pltpu.VMEM((8, tq), F32),
                            pltpu.VMEM((N_KV_A, HEAD_DIM_A, hpg * tq), F32),
                            pltpu.VMEM((FAR_BLOCKS * N_HEADS_A, kblk, tq), F32)],
            compiler_params=pltpu.CompilerParams(dimension_semantics=("arbitrary", "arbitrary"),
                                                 vmem_limit_bytes=VMEM_LIMIT),
            name="dsa_t",
        )(q, qi, kiwi, kb, vt, kie, kio, tab_t, vis.astype(F32).T)

    bias_near = _bias_table(rel_bias, _t5_bucket(s_abs - q_abs))
    kern = functools.partial(_dsa_kernel, RB=rb, **common)
    return pl.pallas_call(
        kern,
        grid=(B, n_qt),
        in_specs=[pl.BlockSpec(memory_space=pltpu.SMEM),
                  pl.BlockSpec((1, tq, D_ATTN), lambda b, i: (b, i, 0)),
                  pl.BlockSpec((1, tq, N_IDX_HEADS * IDX_DIM), lambda b, i: (b, i, 0)),
                  pl.BlockSpec((1, tq, LANES), lambda b, i: (b, i, 0)),
                  pl.BlockSpec((1, Lp, D_KV), lambda b, i: (b, 0, 0)),
                  pl.BlockSpec((1, Lp, D_KV), lambda b, i: (b, 0, 0)),
                  pl.BlockSpec((1, Lp, LANES), lambda b, i: (b, 0, 0)),
                  pl.BlockSpec((1, Lp, LANES), lambda b, i: (b, 0, 0)),
                  _const_spec((N_HEADS_A, tq, 2 * kblk)),
                  _const_spec((tq, 2 * kblk))],
        out_specs=pl.BlockSpec((1, tq, D_ATTN), lambda b, i: (b, i, 0)),
        out_shape=jax.ShapeDtypeStruct((B, T, D_ATTN), BF16),
        scratch_shapes=[pltpu.VMEM((n_kb, tq, kblk), jnp.int32),
                        pltpu.VMEM((n_kb, tq, kblk), F32),
                        pltpu.VMEM((N_IDX_HEADS, tq, LANES), F32),
                        pltpu.VMEM((N_KV_A, hpg * tq, HEAD_DIM_A), BF16),
                        pltpu.VMEM((tq, 1), F32),
                        pltpu.VMEM((N_KV_A, hpg * tq, LANES), F32),
                        pltpu.VMEM((N_KV_A, hpg * tq, LANES), F32),
                        pltpu.VMEM((N_KV_A, hpg * tq, HEAD_DIM_A), F32)],
        compiler_params=pltpu.CompilerParams(dimension_semantics=("arbitrary", "arbitrary"),
                                             vmem_limit_bytes=VMEM_LIMIT),
        name="dsa",
    )(rel_bias, q, qi, kiwi, kb, vb, kie, kio, bias_near, vis.astype(F32))


def _mix_kernel(x_ref, a_ref, u_ref, hist_ref, mk_ref, mv_ref, wpool_ref, pscale_ref, wout_ref, gx_ref,
                wxq_ref, wxo_ref, gffn_ref, wr_ref, br_ref, o_ref, ext, *, tm, pos0):
    t = pl.program_id(1)
    H = POOL_HIST + 1

    @pl.when(t == 0)
    def _():
        ext[0:H, :] = hist_ref[0]

    @pl.when(t > 0)
    def _():
        ext[0:H, :] = ext[tm:tm + H, :]

    ext[H:H + tm, :] = u_ref[0]

    da = _dot(a_ref[0], wout_ref[0:D_ATTN, :])

    pos = pos0 + t * tm + lax.broadcasted_iota(jnp.int32, (tm, 1), 0)
    pouts = []
    for gi, w in enumerate(POOL_WINDOWS):
        c0, c1 = gi * POOL_GROUP_DIM, (gi + 1) * POOL_GROUP_DIM
        wsum = ext[H:H + tm, c0:c1]
        for s in range(1, w):
            wsum = wsum + ext[H - s:H - s + tm, c0:c1]
        cnt = jnp.minimum(pos + 1, w).astype(F32)
        d = wsum / cnt - u_ref[0, :, c0:c1]
        y = _dot(d.astype(BF16), wpool_ref[gi]) * pscale_ref[:, c0:c1]
        pouts.append(y.astype(BF16))
    pcat = jnp.concatenate(pouts, axis=1)

    D = x_ref.shape[2]
    x1 = x_ref[0] + da + _dot(pcat, wout_ref[D_ATTN:D_ATTN + D_POOL, :])
    h = _rms(x1, gx_ref[...]).astype(BF16)
    qx = _dot(h, wxq_ref[...])
    heads = [slice(hh * XHEAD_DIM, (hh + 1) * XHEAD_DIM) for hh in range(N_XHEADS)]
    all_logits = [_dot_nt(qx[:, hs].astype(BF16), mk_ref[0, :, hs]) for hs in heads]
    outs = []
    for hs, logits in zip(heads, all_logits):
        logits = logits * (XHEAD_DIM ** -0.5)
        m = jnp.max(logits, axis=1, keepdims=True)
        p = jnp.exp(logits - m)
        l = jnp.sum(p, axis=1, keepdims=True)
        outs.append((_dot(p.astype(BF16), mv_ref[0, :, hs]) / l).astype(BF16))
    o = jnp.concatenate(outs, axis=1)
    x2 = x1 + _dot(o, wxo_ref[...])
    o_ref[0, :, 0:D] = x2
    hf = _rms(x2, gffn_ref[...]).astype(BF16)
    o_ref[0, :, D:D + LANES] = _route(_dot(hf, wr_ref[...]) + br_ref[...])


def _mix(x, a, u, hist, mk, mv, w_pool, pool_scale, w_out, g_x, w_xq, w_xo, g_ffn, w_r, b_r, *, pos0, tm):
    B, T, D = x.shape
    n_mem = mk.shape[1]
    H = POOL_HIST + 1
    assert T % tm == 0 and tm >= H
    hist16 = jnp.concatenate([jnp.zeros((B, 1, D_POOL), F32), hist], axis=1)
    return pl.pallas_call(
        functools.partial(_mix_kernel, tm=tm, pos0=pos0),
        grid=(B, T // tm),
        in_specs=[pl.BlockSpec((1, tm, D), lambda b, t: (b, t, 0)),
                  pl.BlockSpec((1, tm, D_ATTN), lambda b, t: (b, t, 0)),
                  pl.BlockSpec((1, tm, D_POOL), lambda b, t: (b, t, 0)),
                  pl.BlockSpec((1, H, D_POOL), lambda b, t: (b, 0, 0)),
                  pl.BlockSpec((1, n_mem, D_X), lambda b, t: (b, 0, 0)),
                  pl.BlockSpec((1, n_mem, D_X), lambda b, t: (b, 0, 0)),
                  _const_spec(w_pool.shape),
                  _const_spec((1, D_POOL)),
                  _const_spec(w_out.shape),
                  _const_spec((1, D)),
                  _const_spec(w_xq.shape),
                  _const_spec(w_xo.shape),
                  _const_spec((1, D)),
                  _const_spec(w_r.shape),
                  _const_spec((1, LANES))],
        out_specs=pl.BlockSpec((1, tm, D + LANES), lambda b, t: (b, t, 0)),
        out_shape=jax.ShapeDtypeStruct((B, T, D + LANES), F32),
        scratch_shapes=[pltpu.VMEM((H + tm, D_POOL), F32)],
        compiler_params=pltpu.CompilerParams(dimension_semantics=("arbitrary", "arbitrary"),
                                             vmem_limit_bytes=VMEM_LIMIT),
        name="mix",
    )(x, a, u, hist16, mk, mv, w_pool, pool_scale.reshape(1, D_POOL), w_out, g_x.reshape(1, D), w_xq, w_xo,
      g_ffn.reshape(1, D), w_r, b_r)


R_OFF = N_GROUPS


def _route(logits):
    tm = logits.shape[0]
    lane = lax.broadcasted_iota(jnp.int32, (tm, LANES), 1).astype(F32)
    ninf = -jnp.inf
    big = float(LANES)
    gl = jnp.where(lane < N_GROUPS, logits, ninf)
    gmax = jnp.max(gl, axis=1, keepdims=True)
    g_sel = jnp.min(jnp.where(gl == gmax, lane, big), axis=1, keepdims=True)
    g_prob = 1.0 / jnp.sum(jnp.exp(gl - gmax), axis=1, keepdims=True)
    lo = R_OFF + g_sel * EXPERTS_PER_GROUP
    el = jnp.where((lane >= lo) & (lane < lo + EXPERTS_PER_GROUP), logits, ninf)
    tv0 = jnp.max(el, axis=1, keepdims=True)
    ti0 = jnp.min(jnp.where(el == tv0, lane, big), axis=1, keepdims=True)
    el2 = jnp.where(lane == ti0, ninf, el)
    tv1 = jnp.max(el2, axis=1, keepdims=True)
    ti1 = jnp.min(jnp.where(el2 == tv1, lane, big), axis=1, keepdims=True)
    e1 = jnp.exp(tv1 - tv0)
    den = 1.0 + e1
    w0 = g_prob / den
    w1 = g_prob * e1 / den
    ids = jnp.where(lane == 0.0, jnp.minimum(ti0, ti1) - R_OFF,
                    jnp.where(lane == 1.0, jnp.maximum(ti0, ti1) - R_OFF, 0.0))
    return jnp.where(lane == ti0, w0, 0.0) + jnp.where(lane == ti1, w1, 0.0) + ids


def _expert(h, rec, e, wg, wu, wd):
    a = _dot(h, wg)
    b = _dot(h, wu)
    lane = lax.broadcasted_iota(jnp.int32, rec.shape, 1)
    cw = jnp.sum(jnp.where(lane == e + R_OFF, rec, 0.0), axis=1, keepdims=True)
    hid = a * jax.nn.sigmoid(a) * b * cw
    return _dot(hid.astype(BF16), wd)


def _moe_kernel(x_ref, gffn_ref, wg_ref, wu_ref, wd_ref, gfin_ref, o_ref, h_s, acc_s):
    e = pl.program_id(1)
    d = o_ref.shape[1]

    @pl.when(e == 0)
    def _():
        h_s[...] = _rms(x_ref[:, 0:d], gffn_ref[...]).astype(BF16)
        acc_s[...] = jnp.zeros(acc_s.shape, F32)

    acc_s[...] += _expert(h_s[...], x_ref[:, d:d + LANES], e, wg_ref[0], wu_ref[0], wd_ref[0])

    @pl.when(e == pl.num_programs(1) - 1)
    def _():
        o_ref[...] = _rms(x_ref[:, 0:d] + acc_s[...], gfin_ref[...])


def _moe(x, g_ffn, w_gate, w_up, w_down, g_final, *, tm):
    n, de = x.shape
    d = de - LANES
    assert n % tm == 0
    return pl.pallas_call(
        _moe_kernel,
        grid=(n // tm, N_EXPERTS),
        in_specs=[pl.BlockSpec((tm, de), lambda i, e: (i, 0)),
                  _const_spec((1, d)),
                  pl.BlockSpec((1, d, D_EXPERT), lambda i, e: (e, 0, 0)),
                  pl.BlockSpec((1, d, D_EXPERT), lambda i, e: (e, 0, 0)),
                  pl.BlockSpec((1, D_EXPERT, d), lambda i, e: (e, 0, 0)),
                  _const_spec((1, d))],
        out_specs=pl.BlockSpec((tm, d), lambda i, e: (i, 0)),
        out_shape=jax.ShapeDtypeStruct((n, d), F32),
        scratch_shapes=[pltpu.VMEM((tm, d), BF16),
                        pltpu.VMEM((tm, d), F32)],
        compiler_params=pltpu.CompilerParams(dimension_semantics=("arbitrary", "arbitrary"),
                                             vmem_limit_bytes=VMEM_LIMIT),
        name="moe",
    )(x, g_ffn.reshape(1, d), w_gate, w_up, w_down, g_final.reshape(1, d))


MOE_TILES_PER_STEP = 2


def _moe_sparse_kernel(ea_ref, eb_ref, nv_ref, idx_ref, idxn_ref, x_hbm, gffn_ref, gfin_ref, *rest, tm, tps):
    w_refs = [rest[6 * k:6 * k + 6] for k in range(tps)]
    o_hbm, xbuf, ybuf, gsem, ssem = rest[6 * tps:]
    t = pl.program_id(0)
    nt = pl.num_programs(0)
    d = o_hbm.shape[1]
    slot = lax.rem(t, 2)
    nslot = 1 - slot
    rows = tps * tm
    WAIT_ROWS = 8

    def step_rows(step):
        nvs = [nv_ref[tps * step + k] for k in range(tps)]
        return nvs, functools.reduce(lambda a, b: a + b, nvs)

    nvs, nv = step_rows(t)

    def gather_row(ids, s, r):
        return pltpu.make_async_copy(x_hbm.at[pl.ds(ids[0, 0, r], 1), :], xbuf.at[s, pl.ds(r, 1), :], gsem.at[s])

    def gather_loop(ids, s):
        def body(r, carry):
            gather_row(ids, s, r).start()
            return carry
        lax.fori_loop(0, rows, body, 0)

    def gather_wait(s):
        pltpu.make_async_copy(x_hbm.at[pl.ds(0, rows), :], xbuf.at[s], gsem.at[s]).wait()

    def scatter_rows(s, r, row, n):
        return pltpu.make_async_copy(ybuf.at[s, pl.ds(r, n), :], o_hbm.at[pl.ds(row, n), :], ssem.at[s])

    def scatter_wait(s, n):
        def body_many(r, carry):
            scatter_rows(s, 0, 0, WAIT_ROWS).wait()
            return carry
        lax.fori_loop(0, n // WAIT_ROWS, body_many, 0)

        def body_one(r, carry):
            scatter_rows(s, 0, 0, 1).wait()
            return carry
        lax.fori_loop(0, lax.rem(n, WAIT_ROWS), body_one, 0)

    @pl.when(t == 0)
    def _():
        gather_loop(idx_ref, 0)

    @pl.when(t >= 2)
    def _():
        scatter_wait(slot, step_rows(jnp.maximum(t - 2, 0))[1])

    gather_wait(slot)

    @pl.when(nv > 0)
    def _():
        tiles = []
        for k in range(tps):
            x = xbuf[slot, k * tm:(k + 1) * tm, :]
            xr = x[:, 0:d]
            tiles.append((xr, x[:, d:d + LANES], _rms(xr, gffn_ref[...]).astype(BF16)))
        for r in range(rows):
            gather_row(idxn_ref, nslot, r).start()
        pre = [[(_dot(h, wga[0]), _dot(h, wua[0])), (_dot(h, wgb[0]), _dot(h, wub[0]))]
               for (_, _, h), (wga, wua, _, wgb, wub, _) in zip(tiles, w_refs)]
        for k in range(tps):
            xr, rec, _ = tiles[k]
            lane = lax.broadcasted_iota(jnp.int32, rec.shape, 1)
            experts = (ea_ref[tps * t + k], eb_ref[tps * t + k])
            acc = None
            for (a, b), e, wd in zip(pre[k], experts, (w_refs[k][2], w_refs[k][5])):
                cw = jnp.sum(jnp.where(lane == e + R_OFF, rec, 0.0), axis=1, keepdims=True)
                hid = a * jax.nn.sigmoid(a) * b * cw
                out = _dot(hid.astype(BF16), wd[0])
                acc = out if acc is None else acc + out
            ybuf[slot, k * tm:(k + 1) * tm, :] = _rms(xr + acc, gfin_ref[...])

        for k in range(tps):
            def body_many(i, carry, k=k):
                for u in range(WAIT_ROWS):
                    r = k * tm + i * WAIT_ROWS + u
                    scatter_rows(slot, r, idx_ref[0, 0, r], 1).start()
                return carry
            lax.fori_loop(0, nvs[k] // WAIT_ROWS, body_many, 0)

            def body_one(i, carry, k=k):
                r = k * tm + i
                scatter_rows(slot, r, idx_ref[0, 0, r], 1).start()
                return carry
            lax.fori_loop((nvs[k] // WAIT_ROWS) * WAIT_ROWS, nvs[k], body_one, 0)

    @pl.when(nv <= 0)
    def _():
        gather_loop(idxn_ref, nslot)

    @pl.when(t == nt - 1)
    def _():
        gather_wait(nslot)

        @pl.when(t >= 1)
        def _():
            scatter_wait(nslot, step_rows(jnp.maximum(t - 1, 0))[1])
        scatter_wait(slot, nv)


def _moe_tile_rows(n):
    n_pairs = N_GROUPS * (EXPERTS_PER_GROUP * (EXPERTS_PER_GROUP - 1) // 2)
    mean = n / n_pairs
    rows = (mean + 3.0 * math.sqrt(mean)) / 2
    return int(min(256, max(SUB_BF16, -(-rows // SUB_BF16) * SUB_BF16)))


def _moe_sparse(x, g_ffn, w_gate, w_up, w_down, g_final, *, tm):
    n, de = x.shape
    d = de - LANES
    n_pairs = N_GROUPS * (EXPERTS_PER_GROUP * (EXPERTS_PER_GROUP - 1) // 2)
    tps = MOE_TILES_PER_STEP
    nt = -(-n // tm) + n_pairs
    nt = -(-nt // tps) * tps
    n_steps = nt // tps
    n_keys = N_EXPERTS * N_EXPERTS

    ids = x[:, d:d + 2].astype(jnp.int32)
    key = ids[:, 0] * N_EXPERTS + ids[:, 1]
    order = jnp.argsort(key).astype(jnp.int32)
    skey = key[order]
    count_below = lambda sorted_vals, q: jnp.sum((sorted_vals[None, :] < q[:, None]).astype(jnp.int32), axis=1)
    edges = count_below(skey, jnp.arange(n_keys + 1, dtype=jnp.int32))
    starts, ends = edges[:-1], edges[1:]
    tiles = (ends - starts + tm - 1) // tm
    cum = jnp.cumsum(tiles)
    total = cum[-1]
    tt = jnp.arange(nt, dtype=jnp.int32)
    tc = jnp.minimum(tt, total - 1)
    cls = count_below(cum, tc + 1)
    first = cum[cls] - tiles[cls]
    tstart = starts[cls] + (tc - first) * tm
    nv = jnp.where(tt < total, jnp.clip(ends[cls] - tstart, 0, tm), 0).astype(jnp.int32)
    ea = (cls // N_EXPERTS).astype(jnp.int32)
    eb = (cls % N_EXPERTS).astype(jnp.int32)
    rows = jnp.clip(tstart[:, None] + jnp.arange(tm, dtype=jnp.int32)[None, :], 0, n - 1)
    idx = order[rows].reshape(n_steps, 1, tps * tm)

    def wspec(shape, experts, k):
        return pl.BlockSpec(shape, lambda t, ea, eb, nv: ((ea, eb)[experts][tps * t + k], 0, 0))

    cspec = lambda shape: pl.BlockSpec(shape, lambda t, ea, eb, nv: (0,) * len(shape),
                                       pipeline_mode=pl.Buffered(1))
    w_specs, w_args = [], []
    for k in range(tps):
        for experts in (0, 1):
            w_specs += [wspec((1, d, D_EXPERT), experts, k), wspec((1, d, D_EXPERT), experts, k),
                        wspec((1, D_EXPERT, d), experts, k)]
            w_args += [w_gate, w_up, w_down]
    grid_spec = pltpu.PrefetchScalarGridSpec(
        num_scalar_prefetch=3,
        grid=(n_steps,),
        in_specs=[pl.BlockSpec((1, 1, tps * tm), lambda t, ea, eb, nv: (t, 0, 0), memory_space=pltpu.SMEM),
                  pl.BlockSpec((1, 1, tps * tm), lambda t, ea, eb, nv: (jnp.minimum(t + 1, n_steps - 1), 0, 0),
                               memory_space=pltpu.SMEM),
                  pl.BlockSpec(memory_space=pl.ANY),
                  cspec((1, d)), cspec((1, d))] + w_specs,
        out_specs=pl.BlockSpec(memory_space=pl.ANY),
        scratch_shapes=[pltpu.VMEM((2, tps * tm, de), F32),
                        pltpu.VMEM((2, tps * tm, d), F32),
                        pltpu.SemaphoreType.DMA((2,)),
                        pltpu.SemaphoreType.DMA((2,))])
    return pl.pallas_call(
        functools.partial(_moe_sparse_kernel, tm=tm, tps=tps),
        grid_spec=grid_spec,
        out_shape=jax.ShapeDtypeStruct((n, d), F32),
        compiler_params=pltpu.CompilerParams(dimension_semantics=("arbitrary",),
                                             vmem_limit_bytes=VMEM_LIMIT),
        name="moe_sparse",
    )(ea, eb, nv, idx, idx, x, g_ffn.reshape(1, d), g_final.reshape(1, d), *w_args)


C_Q = 0
C_QI = C_Q + D_ATTN
C_K = C_QI + N_IDX_HEADS * IDX_DIM
C_V = C_K + D_KV
C_KIWI = C_V + D_KV
C_KIE = C_KIWI + LANES
C_KIO = C_KIE + LANES
C_U = C_KIO + LANES
C_END = C_U + D_POOL


def _layout_w_in(w_in):
    d = w_in.shape[0]
    offs = np.cumsum((0, D_ATTN, D_KV, D_KV, N_IDX_HEADS * IDX_DIM, IDX_DIM, N_IDX_HEADS, D_POOL))
    wq, wk, wv, wqi, wki, wwi, wu = (w_in[:, offs[n]:offs[n + 1]] for n in range(7))
    z = lambda n: jnp.zeros((d, n), w_in.dtype)
    cat = jnp.concatenate([wq, wqi, wk, wv,
                           wki, wwi, z(LANES - IDX_DIM - N_IDX_HEADS),
                           wki, z(LANES - IDX_DIM),
                           z(LANES - IDX_DIM), wki,
                           wu], axis=1)
    assert cat.shape[1] == C_END
    return cat.astype(BF16)


IN_GROUPS = ((C_Q, C_QI, 1),
             (C_QI, C_K, 1),
             (C_K, C_V, 2),
             (C_V, C_KIWI, 2),
             (C_KIWI, C_KIE, 1),
             (C_KIE, C_KIO, 1),
             (C_KIO, C_U, 1),
             (C_U, C_END, 1))
IN_DTYPES = (BF16, BF16, F32, BF16, F32, BF16, F32, BF16, BF16, F32)


def _pad_rows(a, n):
    return jnp.pad(a, ((0, 0), (0, n - a.shape[1]), (0, 0)))


def _layer(x, pos0, caches, pool_hist, mk, mv, rel_bias, W, *, tm_in, tq, tm_mix, tm_moe, sparse_moe):
    B, T, D = x.shape
    n = B * T
    q, qi, k, kb, v, vb, kiwi, kie, kio, u = _norm_matmul(
        x.reshape(n, D), W["g_mix"], W["w_in"], IN_GROUPS, IN_DTYPES, tm_in)
    r3 = lambda a: a.reshape(B, T, a.shape[-1])
    kblk = KEY_BLOCK
    if caches is None:
        n_keys = T
        kb3, vb3, kie3, kio3 = r3(kb), r3(vb), r3(kie), r3(kio)
    else:
        k_hist, v_hist, ki_hist = caches
        past = k_hist.shape[1]
        n_keys = past + T
        lp = -(-n_keys // kblk) * kblk
        ki = r3(kiwi)[:, :, :IDX_DIM]
        ki_all = jnp.concatenate([ki_hist, ki], axis=1)
        zeros = jnp.zeros_like(ki_all)
        kb3 = _pad_rows(jnp.concatenate([k_hist.reshape(B, past, D_KV), r3(k)], axis=1), lp).astype(BF16)
        vb3 = _pad_rows(jnp.concatenate([v_hist.reshape(B, past, D_KV), r3(v)], axis=1), lp).astype(BF16)
        kie3 = _pad_rows(jnp.concatenate([ki_all, zeros], axis=2), lp).astype(BF16)
        kio3 = _pad_rows(jnp.concatenate([zeros, ki_all], axis=2), lp).astype(BF16)
    a_out = _dsa(r3(q), r3(qi), r3(kiwi), kb3, vb3, kie3, kio3, rel_bias,
                 n_keys=n_keys, pos0=pos0, tq=tq, kblk=kblk)
    x2 = _mix(x, a_out, r3(u), pool_hist, mk.astype(BF16), mv.astype(BF16), W["w_pool"], W["pool_scale"],
              W["w_out"], W["g_x"], W["w_xq"], W["w_xo"], W["g_ffn"], W["w_r"], W["b_r"], pos0=pos0, tm=tm_mix)
    moe = _moe_sparse if sparse_moe else _moe
    y = moe(x2.reshape(n, D + LANES), W["g_ffn"], W["w_gate"], W["w_up"], W["w_down"], W["g_final"], tm=tm_moe)
    new_pool = jnp.concatenate([pool_hist, r3(u)], axis=1)[:, -POOL_HIST:]
    return (y.reshape(B, T, D), k.reshape(B, T, N_KV_A, HEAD_DIM_A), v.reshape(B, T, N_KV_A, HEAD_DIM_A),
            r3(kiwi)[:, :, :IDX_DIM], new_pool)


def kernel(x_prompt, x_sample, mem_prompt, cache_k, cache_v, cache_k_idx, cache_pool, cache_mem_k, cache_mem_v, rel_bias, g_mix, w_in, w_pool, pool_scale, w_out, g_mem, w_mk, w_mv, g_x, w_xq, w_xo, g_ffn, w_rg, b_rg, w_re, b_re, w_gate, w_up, w_down, g_final):
    depth = g_mix.shape[0]
    assert depth == 1
    l = 0
    B, T, D = x_prompt.shape
    Bs, Ts, _ = x_sample.shape
    n_mem = mem_prompt.shape[1]
    past = cache_k.shape[2]

    w_r = jnp.concatenate([w_rg[l], w_re[l], jnp.zeros((D, LANES - N_GROUPS - N_EXPERTS), F32)], axis=1)
    W = dict(
        g_mix=g_mix[l], w_in=_layout_w_in(w_in[l]), w_pool=w_pool[l].astype(BF16), pool_scale=pool_scale[l],
        w_out=w_out[l].astype(BF16), g_x=g_x[l], w_xq=w_xq[l].astype(BF16), w_xo=w_xo[l].astype(BF16),
        g_ffn=g_ffn[l], w_r=w_r.astype(BF16),
        b_r=jnp.concatenate([b_rg[l], b_re[l], jnp.zeros((LANES - N_GROUPS - N_EXPERTS,), F32)]).reshape(1, LANES),
        w_gate=w_gate[l].astype(BF16), w_up=w_up[l].astype(BF16), w_down=w_down[l].astype(BF16),
        g_final=g_final)

    w_m = jnp.concatenate([w_mk[l], w_mv[l]], axis=1).astype(BF16)
    mk, mv = _norm_matmul(mem_prompt.reshape(B * n_mem, D), g_mem[l], w_m,
                          ((0, D_X, 1), (D_X, 2 * D_X, 1)), (F32, F32), TOKEN_TILE)
    mk = mk.reshape(B, n_mem, D_X)
    mv = mv.reshape(B, n_mem, D_X)

    yp, kp, vp, kip, pp = _layer(x_prompt, 0, None, jnp.zeros((B, POOL_HIST, D_POOL), F32), mk, mv, rel_bias, W,
                                 tm_in=TOKEN_TILE, tq=QUERY_TILE, tm_mix=TOKEN_TILE, tm_moe=_moe_tile_rows(B * T),
                                 sparse_moe=True)
    ys, ks, vs, kis, ps = _layer(x_sample, past,
                                 (cache_k[l], cache_v[l], cache_k_idx[l]), cache_pool[l],
                                 cache_mem_k[l].reshape(Bs, n_mem, D_X), cache_mem_v[l].reshape(Bs, n_mem, D_X),
                                 rel_bias, W, tm_in=Bs * Ts, tq=Ts, tm_mix=Ts, tm_moe=Bs * Ts, sparse_moe=False)
    st = lambda a: a[None]
    return (yp, ys, st(kp), st(vp), st(kip), st(pp),
            st(mk.reshape(B, n_mem, N_XHEADS, XHEAD_DIM)), st(mv.reshape(B, n_mem, N_XHEADS, XHEAD_DIM)),
            st(ks), st(vs), st(kis), st(ps))
```

```python
import functools
import math

import numpy as np
import jax
import jax.numpy as jnp
from jax import lax
from jax.experimental import pallas as pl
from jax.experimental.pallas import tpu as pltpu

F32 = jnp.float32
BF16 = jnp.bfloat16

CHUNK = 64
N_HEADS_A = 8
HEAD_DIM_A = 128
N_KV_A = 2
D_ATTN = N_HEADS_A * HEAD_DIM_A
D_KV = N_KV_A * HEAD_DIM_A
N_IDX_HEADS = 8
IDX_DIM = 64
TOPK_MAX = 256
POOL_WINDOWS = (2, 4, 8, 16)
POOL_GROUP_DIM = 256
D_POOL = len(POOL_WINDOWS) * POOL_GROUP_DIM
POOL_HIST = max(POOL_WINDOWS) - 1
N_BUCKETS = 32
MAX_DISTANCE = 128
N_XHEADS = 4
XHEAD_DIM = 128
D_X = N_XHEADS * XHEAD_DIM
N_GROUPS = 4
EXPERTS_PER_GROUP = 8
N_EXPERTS = N_GROUPS * EXPERTS_PER_GROUP
D_EXPERT = 256
EPS = 1e-6

LANES = 128
INT_MIN = -(2 ** 31)
NEG = -1e30
LOG2E = math.log2(math.e)
VMEM_LIMIT = 48 * 1024 * 1024
TOKEN_TILE = 512
KEY_BLOCK = 256
QUERY_TILE = KEY_BLOCK


def _rms(x, g):
    ms = jnp.mean(x * x, axis=-1, keepdims=True)
    return x * lax.rsqrt(ms + EPS) * g


def _dot(a, b):
    return jnp.dot(a, b, preferred_element_type=F32)


def _dot_nt(a, b):
    return lax.dot_general(a, b, (((1,), (1,)), ((), ())), preferred_element_type=F32)


def _const_spec(shape):
    nd = len(shape)
    return pl.BlockSpec(shape, lambda *_: (0,) * nd, pipeline_mode=pl.Buffered(1))


def _norm_matmul_kernel(x_ref, g_ref, w_ref, *out_refs, groups):
    h = _rms(x_ref[...], g_ref[...]).astype(BF16)
    k = 0
    for c0, c1, n_out in groups:
        acc = _dot(h, w_ref[:, c0:c1])
        for _ in range(n_out):
            out_refs[k][...] = acc.astype(out_refs[k].dtype)
            k += 1


def _norm_matmul(x, g, w, groups, out_dtypes, tm):
    n, d = x.shape
    assert n % tm == 0
    widths = []
    for c0, c1, n_out in groups:
        widths += [c1 - c0] * n_out
    return pl.pallas_call(
        functools.partial(_norm_matmul_kernel, groups=tuple(groups)),
        grid=(n // tm,),
        in_specs=[pl.BlockSpec((tm, d), lambda i: (i, 0)),
                  _const_spec((1, d)),
                  _const_spec(w.shape)],
        out_specs=[pl.BlockSpec((tm, wd), lambda i: (i, 0)) for wd in widths],
        out_shape=[jax.ShapeDtypeStruct((n, wd), dt) for wd, dt in zip(widths, out_dtypes)],
        compiler_params=pltpu.CompilerParams(dimension_semantics=("arbitrary",),
                                             vmem_limit_bytes=VMEM_LIMIT),
        name="norm_matmul",
    )(x, g.reshape(1, d), w)


def _bias_table_kernel(relb_ref, bucket_ref, o_ref, *, shift_bucket):
    bucket = bucket_ref[...]
    for h in range(N_HEADS_A):
        acc = jnp.zeros(bucket.shape, F32)
        for b in range(N_BUCKETS):
            acc = jnp.where(bucket == b, relb_ref[b, h], acc)
        if shift_bucket is None:
            o_ref[h] = acc * LOG2E
        else:
            o_ref[h] = (acc - relb_ref[shift_bucket, h]) * (HEAD_DIM_A ** 0.5)


def _bias_table(rel_bias, bucket, shift_bucket=None):
    tq, w = bucket.shape
    return pl.pallas_call(
        functools.partial(_bias_table_kernel, shift_bucket=shift_bucket),
        in_specs=[pl.BlockSpec(memory_space=pltpu.SMEM),
                  pl.BlockSpec((tq, w), lambda: (0, 0))],
        out_specs=pl.BlockSpec((N_HEADS_A, tq, w), lambda: (0, 0, 0)),
        out_shape=jax.ShapeDtypeStruct((N_HEADS_A, tq, w), F32),
        name="bias_table",
    )(rel_bias, bucket)


def _t5_bucket(rel):
    nb = N_BUCKETS // 2
    max_exact = nb // 2
    bucket = (rel > 0).astype(jnp.int32) * nb
    n = jnp.abs(rel)
    nf = jnp.maximum(n, 1).astype(F32)
    large = max_exact + (jnp.log(nf / max_exact) / math.log(MAX_DISTANCE / max_exact)
                         * (nb - max_exact)).astype(jnp.int32)
    large = jnp.minimum(large, nb - 1)
    return bucket + jnp.where(n < max_exact, n, large)


def _far_bucket(min_dist, max_dist):
    nb = N_BUCKETS // 2
    max_exact = nb // 2
    n = np.arange(min_dist, max_dist + 1, dtype=np.float64)
    large = max_exact + np.floor(np.log(n / max_exact) / math.log(MAX_DISTANCE / max_exact)
                                 * (nb - max_exact) * (1 - 1e-6)).astype(np.int64)
    assert large.min() >= nb - 1, "far key blocks must sit in the saturated distance bucket"
    return nb - 1


def _dsa_kernel(relb_ref, q_ref, qi_ref, kiwi_ref, k_ref, v_ref, kie_ref, kio_ref, bnear_ref, vis_ref,
                o_ref, keys, madd, wb, qs, xcut, mrun, lrun, acc_s,
                *, TQ, KB, RB, kb0, kstep, topk, far_bucket, tie_bits):
    i = pl.program_id(1)
    nkb = kb0 + kstep * i
    n_far = jnp.maximum(nkb - 2, 0)
    n_lc = KB // LANES

    kiwi = kiwi_ref[0]
    for h in range(N_IDX_HEADS):
        w = (kiwi[:, IDX_DIM + h:IDX_DIM + h + 1] * (N_IDX_HEADS ** -0.5)).astype(BF16).astype(F32)
        wb[h] = jnp.broadcast_to(w * (IDX_DIM ** -0.5), (TQ, LANES))

    def idx_block(j, near_col):
        c0 = pl.multiple_of(j * KB, KB)
        ke = kie_ref[0, pl.ds(c0, KB), :]
        ko = kio_ref[0, pl.ds(c0, KB), :]
        acc = [jnp.zeros((TQ, LANES), F32) for _ in range(n_lc)]
        for hp in range(N_IDX_HEADS // 2):
            qh = qi_ref[0, :, hp * LANES:(hp + 1) * LANES]
            se = _dot_nt(qh, ke)
            so = _dot_nt(qh, ko)
            we = wb[2 * hp]
            wo = wb[2 * hp + 1]
            for c in range(n_lc):
                cs = slice(c * LANES, (c + 1) * LANES)
                re = jnp.maximum(se[:, cs], 0.0).astype(BF16).astype(F32)
                ro = jnp.maximum(so[:, cs], 0.0).astype(BF16).astype(F32)
                acc[c] = acc[c] + we * re + wo * ro
        for c in range(n_lc):
            bits = lax.bitcast_convert_type(acc[c], jnp.int32)
            key = bits ^ ((bits >> 31) & jnp.int32(0x7FFFFFFF))
            if near_col is not None:
                vis = vis_ref[:, near_col + c * LANES:near_col + (c + 1) * LANES]
                key = jnp.where(vis > 0.0, key, jnp.int32(INT_MIN))
            keys[j, :, c * LANES:(c + 1) * LANES] = key

    def far_idx(j, carry):
        idx_block(j, None)
        return carry

    lax.fori_loop(0, n_far, far_idx, 0)

    @pl.when(nkb >= 2)
    def _():
        idx_block(nkb - 2, 0)

    idx_block(nkb - 1, KB)

    kf = float(topk)
    lane_f = lax.broadcasted_iota(jnp.int32, (RB, LANES), 1).astype(F32)
    NRB = TQ // RB
    rbs = [slice(rb * RB, (rb + 1) * RB) for rb in range(NRB)]

    def count(pred):
        def body(jj, accs):
            out = []
            for rb in range(NRB):
                blk = keys[jj, rbs[rb], :]
                acc = accs[rb]
                for c in range(n_lc):
                    acc = acc + jnp.where(pred(rb, blk[:, c * LANES:(c + 1) * LANES], jj, c), 1.0, 0.0)
                out.append(acc)
            return tuple(out)
        accs = lax.fori_loop(0, nkb, body, tuple(jnp.zeros((RB, LANES), F32) for _ in range(NRB)))
        return [jnp.sum(acc, axis=1, keepdims=True) for acc in accs]

    def count_ge(cands):
        cb = [jnp.broadcast_to(cand, (RB, LANES)) for cand in cands]
        return count(lambda rb, kblk, jj, c: kblk >= cb[rb])

    def bis_body(p, thrs):
        cands = [thr + lax.shift_left(jnp.int32(1), 31 - p) for thr in thrs]
        cnts = count_ge(cands)
        return tuple(jnp.where(cnt >= kf, cand, thr) for cnt, cand, thr in zip(cnts, cands, thrs))

    thrs = lax.fori_loop(0, 32, bis_body, tuple(jnp.full((RB, 1), INT_MIN, jnp.int32) for _ in range(NRB)))
    thrs = [jnp.maximum(thr, jnp.int32(INT_MIN + 1)) for thr in thrs]
    cnt_ge = count_ge(thrs)
    cnt_gt = count_ge([thr + 1 for thr in thrs])
    needs = [kf - c for c in cnt_gt]
    thrb = [jnp.broadcast_to(thr, (RB, LANES)) for thr in thrs]

    xcut[...] = jnp.full(xcut.shape, 2.0 ** 24, F32)
    excess = cnt_ge[0] - kf
    for c in cnt_ge[1:]:
        excess = jnp.maximum(excess, c - kf)

    @pl.when(jnp.max(excess) > 0.0)
    def _():
        def count_eq_lt(xs):
            xb = [jnp.broadcast_to(x, (RB, LANES)) for x in xs]

            def pred(rb, kblk, jj, c):
                col = lane_f + (jj * KB + c * LANES).astype(F32)
                return jnp.where(kblk == thrb[rb], col, 2.0 ** 25) < xb[rb]
            return count(pred)

        def tie_body(p, xs):
            cands = [x + lax.shift_left(jnp.int32(1), tie_bits - 1 - p).astype(F32) for x in xs]
            cnts = count_eq_lt(cands)
            return tuple(jnp.where(cnt < need, cand, x) for cnt, need, cand, x in zip(cnts, needs, cands, xs))

        xs = lax.fori_loop(0, tie_bits, tie_body, tuple(jnp.zeros((RB, 1), F32) for _ in range(NRB)))
        for rb in range(NRB):
            xcut[rbs[rb], :] = xs[rb]

    xb = [jnp.broadcast_to(xcut[rbs[rb], :], (RB, LANES)) for rb in range(NRB)]

    def mask_body(jj, carry):
        for rb in range(NRB):
            blk = keys[jj, rbs[rb], :]
            for c in range(n_lc):
                kblk = blk[:, c * LANES:(c + 1) * LANES]
                col = lane_f + (jj * KB + c * LANES).astype(F32)
                keep = (kblk > thrb[rb]) | ((kblk == thrb[rb]) & (col <= xb[rb]))
                madd[jj, rbs[rb], c * LANES:(c + 1) * LANES] = jnp.where(keep, 0.0, NEG)
        return carry

    lax.fori_loop(0, nkb, mask_body, 0)

    _attend(relb_ref, q_ref, k_ref, v_ref, bnear_ref, o_ref, madd, qs, mrun, lrun, acc_s,
            TQ=TQ, KB=KB, nkb=nkb, n_far=n_far, far_bucket=far_bucket)


def _attend(relb_ref, q_ref, k_ref, v_ref, bnear_ref, o_ref, madd, qs, mrun, lrun, acc_s,
            *, TQ, KB, nkb, n_far, far_bucket):
    n_lc = KB // LANES
    HPG = N_HEADS_A // N_KV_A
    for g in range(N_KV_A):
        for r in range(HPG):
            h = g * HPG + r
            qs[g, r * TQ:(r + 1) * TQ, :] = q_ref[0, :, h * HEAD_DIM_A:(h + 1) * HEAD_DIM_A]

    def logits(j, near_col, g):
        c0 = pl.multiple_of(j * KB, KB)
        kg = k_ref[0, pl.ds(c0, KB), g * HEAD_DIM_A:(g + 1) * HEAD_DIM_A]
        s = _dot_nt(qs[g], kg)
        mk = madd[j]
        out = []
        for r in range(HPG):
            h = g * HPG + r
            if near_col is None:
                bias = relb_ref[far_bucket, h] * LOG2E
            else:
                bias = bnear_ref[h, :, near_col:near_col + KB]
            out.append(s[r * TQ:(r + 1) * TQ] * (HEAD_DIM_A ** -0.5 * LOG2E) + bias + mk)
        return out

    def over_blocks(fn):
        def far(j, carry):
            fn(j, None)
            return carry

        lax.fori_loop(0, n_far, far, 0)

        @pl.when(nkb >= 2)
        def _():
            fn(nkb - 2, 0)

        fn(nkb - 1, KB)

    mrun[...] = jnp.full(mrun.shape, NEG, F32)

    def max_block(j, near_col):
        for g in range(N_KV_A):
            for r, sh in enumerate(logits(j, near_col, g)):
                rs = slice(r * TQ, (r + 1) * TQ)
                m = mrun[g, rs, :]
                for c in range(n_lc):
                    m = jnp.maximum(m, sh[:, c * LANES:(c + 1) * LANES])
                mrun[g, rs, :] = m

    over_blocks(max_block)

    for g in range(N_KV_A):
        mrun[g] = jnp.broadcast_to(jnp.max(mrun[g], axis=1, keepdims=True), mrun.shape[1:])
    lrun[...] = jnp.zeros(lrun.shape, F32)
    acc_s[...] = jnp.zeros(acc_s.shape, F32)

    def pv_block(j, near_col):
        c0 = pl.multiple_of(j * KB, KB)
        for g in range(N_KV_A):
            vg = v_ref[0, pl.ds(c0, KB), g * HEAD_DIM_A:(g + 1) * HEAD_DIM_A]
            parts = []
            for r, sh in enumerate(logits(j, near_col, g)):
                rs = slice(r * TQ, (r + 1) * TQ)
                mb = mrun[g, rs, :]
                l = lrun[g, rs, :]
                pcs = []
                for c in range(n_lc):
                    p = jnp.exp2(sh[:, c * LANES:(c + 1) * LANES] - mb)
                    l = l + p
                    pcs.append(p.astype(BF16))
                lrun[g, rs, :] = l
                parts.append(jnp.concatenate(pcs, axis=1))
            acc_s[g] += _dot(jnp.concatenate(parts, axis=0), vg)

    over_blocks(pv_block)

    for g in range(N_KV_A):
        out = acc_s[g] / jnp.sum(lrun[g], axis=1, keepdims=True)
        for r in range(HPG):
            h = g * HPG + r
            o_ref[0, :, h * HEAD_DIM_A:(h + 1) * HEAD_DIM_A] = out[r * TQ:(r + 1) * TQ].astype(o_ref.dtype)


DIGIT_BITS = 8
N_DIGITS = 32 // DIGIT_BITS
SUB_BF16 = 16
COUNT_BLOCKS = 4
FAR_BLOCKS = 3


def _dsa_kernel_t(q_ref, qi_ref, kiwi_ref, k_ref, vt_ref, kie_ref, kio_ref, bnear_ref, vist_ref,
                  o_ref, keys, dig, madd, qs, xcut, acc_s, tbuf,
                  *, TQ, KB, kb0, kstep, topk, tie_bits):
    i = pl.program_id(1)
    nkb = kb0 + kstep * i
    n_far = jnp.maximum(nkb - 2, 0)
    n_slab = KB // SUB_BF16

    w_t = kiwi_ref[0].T
    wrow = [(w_t[IDX_DIM + h:IDX_DIM + h + 1, :] * (N_IDX_HEADS ** -0.5)).astype(BF16).astype(F32)
            * (IDX_DIM ** -0.5) for h in range(N_IDX_HEADS)]

    def idx_dots(j):
        c0 = pl.multiple_of(j * KB, KB)
        ke = kie_ref[0, pl.ds(c0, KB), :]
        ko = kio_ref[0, pl.ds(c0, KB), :]
        raw = []
        for hp in range(N_IDX_HEADS // 2):
            qh = qi_ref[0, :, hp * LANES:(hp + 1) * LANES]
            raw.append((_dot_nt(ke, qh), _dot_nt(ko, qh)))
        return raw

    def idx_finish(j, raw, near_row):
        acc = jnp.zeros((KB, TQ), F32)
        for hp, (se, so) in enumerate(raw):
            re = jnp.maximum(se.astype(BF16), 0.0).astype(F32)
            ro = jnp.maximum(so.astype(BF16), 0.0).astype(F32)
            acc = acc + wrow[2 * hp] * re + wrow[2 * hp + 1] * ro
        bits = lax.bitcast_convert_type(acc, jnp.int32)
        key = bits ^ ((bits >> 31) & jnp.int32(0x7FFFFFFF))
        if near_row is not None:
            key = jnp.where(vist_ref[near_row:near_row + KB, :] > 0.0, key, jnp.int32(INT_MIN))
        keys[j] = key
        for dk in range(N_DIGITS):
            sh = 32 - DIGIT_BITS * (dk + 1)
            d = (key >> sh) if dk == 0 else ((key >> sh) & jnp.int32(2 ** DIGIT_BITS - 1))
            dig[dk, j] = d.astype(F32).astype(BF16)

    def idx_blocks(js, near_rows):
        raws = [idx_dots(j) for j in js]
        for j, raw, near_row in zip(js, raws, near_rows):
            idx_finish(j, raw, near_row)

    n_trips = n_far // FAR_BLOCKS
    n_left = n_far - n_trips * FAR_BLOCKS

    def far_idx_trip(jt, carry):
        idx_blocks([FAR_BLOCKS * jt + u for u in range(FAR_BLOCKS)], [None] * FAR_BLOCKS)
        return carry

    lax.fori_loop(0, n_trips, far_idx_trip, 0)

    for left in range(1, FAR_BLOCKS):
        @pl.when(n_left == left)
        def _(left=left):
            idx_blocks([n_far - left + u for u in range(left)], [None] * left)

    @pl.when(nkb >= 2)
    def _():
        idx_blocks([nkb - 2, nkb - 1], [0, KB])

    @pl.when(nkb < 2)
    def _():
        idx_blocks([nkb - 1], [KB])

    one = jnp.ones((SUB_BF16, TQ), BF16)
    zero = jnp.zeros((SUB_BF16, TQ), BF16)

    def count_ge(dk, cand):
        cb = jnp.broadcast_to(cand.astype(BF16), (SUB_BF16, TQ))

        def block_count(jj):
            blk = dig[dk, jj]
            parts = [jnp.where(blk[s * SUB_BF16:(s + 1) * SUB_BF16, :] >= cb, one, zero) for s in range(n_slab)]
            while len(parts) > 1:
                parts = [parts[n] + parts[n + 1] for n in range(0, len(parts), 2)]
            return parts[0]

        def body_many(jq, tot):
            cnts = [block_count(COUNT_BLOCKS * jq + u) for u in range(COUNT_BLOCKS)]
            while len(cnts) > 1:
                cnts = [cnts[n] + cnts[n + 1] for n in range(0, len(cnts), 2)]
            return tot + cnts[0].astype(F32)

        def body_one(jj, tot):
            return tot + block_count(jj).astype(F32)

        tot = lax.fori_loop(0, nkb // COUNT_BLOCKS, body_many, jnp.zeros((SUB_BF16, TQ), F32))
        tot = lax.fori_loop((nkb // COUNT_BLOCKS) * COUNT_BLOCKS, nkb, body_one, tot)
        return jnp.sum(tot, axis=0, keepdims=True)

    kneed = jnp.full((1, TQ), float(topk), F32)
    thr = jnp.zeros((1, TQ), jnp.int32)
    cnt_eq = None
    for dk in range(N_DIGITS):
        lo = -float(2 ** (DIGIT_BITS - 1)) if dk == 0 else 0.0

        def bis_body(p, carry, dk=dk, kneed=kneed):
            t, c_rej = carry
            cand = t + lax.shift_left(jnp.int32(1), DIGIT_BITS - 1 - p).astype(F32)
            cnt = count_ge(dk, cand)
            ok = cnt >= kneed
            return jnp.where(ok, cand, t), jnp.where(ok, c_rej, cnt)

        t, n_gt = lax.fori_loop(0, DIGIT_BITS, bis_body,
                                (jnp.full((1, TQ), lo, F32), jnp.zeros((1, TQ), F32)))
        if dk == N_DIGITS - 1:
            cnt_eq = count_ge(dk, t) - n_gt
        kneed = kneed - n_gt
        thr = thr + lax.shift_left(t.astype(jnp.int32), 32 - DIGIT_BITS * (dk + 1))
        if dk + 1 < N_DIGITS:
            tb = t.astype(BF16)

            def prep(jj, carry, dk=dk, tb=tb):
                dig[dk + 1, jj] = jnp.where(dig[dk, jj] == tb, dig[dk + 1, jj], jnp.asarray(-1.0, BF16))
                return carry

            lax.fori_loop(0, nkb, prep, 0)

    thr = jnp.maximum(thr, jnp.int32(INT_MIN + 1))
    need = kneed
    sub_f = lax.broadcasted_iota(jnp.int32, (KB, TQ), 0).astype(F32)

    xcut[...] = jnp.full(xcut.shape, 2.0 ** 24, F32)

    @pl.when(jnp.max(cnt_eq - need) > 0.0)
    def _():
        def count_eq_lt(x):
            def body(jj, tot):
                col = sub_f + (jj * KB).astype(F32)
                hit = jnp.where(jnp.where(keys[jj] == thr, col, 2.0 ** 25) < x, 1.0, 0.0)
                for s in range(KB // 8):
                    tot = tot + hit[s * 8:(s + 1) * 8, :]
                return tot
            tot = lax.fori_loop(0, nkb, body, jnp.zeros((8, TQ), F32))
            return jnp.sum(tot, axis=0, keepdims=True)

        def tie_body(p, x):
            cand = x + lax.shift_left(jnp.int32(1), tie_bits - 1 - p).astype(F32)
            return jnp.where(count_eq_lt(cand) < need, cand, x)

        x = lax.fori_loop(0, tie_bits, tie_body, jnp.zeros((1, TQ), F32))
        xcut[...] = jnp.broadcast_to(x, xcut.shape)

    xc = xcut[0:1, :]

    def mask_body(jj, carry):
        kblk = keys[jj]
        col = sub_f + (jj * KB).astype(F32)
        at_thr = jnp.where(col <= xc, 0.0, NEG)
        madd[jj] = jnp.where(kblk > thr, 0.0, jnp.where(kblk == thr, at_thr, NEG))
        return carry

    lax.fori_loop(0, nkb, mask_body, 0)

    HPG = N_HEADS_A // N_KV_A
    c2 = HEAD_DIM_A ** -0.5 * LOG2E
    for g in range(N_KV_A):
        for r in range(HPG):
            h = g * HPG + r
            qs[g, r * TQ:(r + 1) * TQ, :] = q_ref[0, :, h * HEAD_DIM_A:(h + 1) * HEAD_DIM_A]
    acc_s[...] = jnp.zeros(acc_s.shape, F32)

    def attn_blocks(js, near_rows, carry):
        ms, ls = carry
        s_all = []
        for j in js:
            c0 = pl.multiple_of(j * KB, KB)
            s_all.append([_dot_nt(k_ref[0, pl.ds(c0, KB), g * HEAD_DIM_A:(g + 1) * HEAD_DIM_A], qs[g])
                          for g in range(N_KV_A)])
        for b, (j, near_row) in enumerate(zip(js, near_rows)):
            mk = madd[j]
            m_news = []
            for g in range(N_KV_A):
                for r in range(HPG):
                    h = g * HPG + r
                    t = s_all[b][g][:, r * TQ:(r + 1) * TQ] + mk
                    if near_row is not None:
                        t = t + bnear_ref[h, near_row:near_row + KB, :]
                    tbuf[b * N_HEADS_A + h] = t
                    m_news.append(jnp.maximum(ms[h], jnp.max(t, axis=0, keepdims=True)))
            ms_out, ls_out = [], []
            for g in range(N_KV_A):
                vt = vt_ref[0, j, g * HEAD_DIM_A:(g + 1) * HEAD_DIM_A, :]
                for r in range(HPG):
                    h = g * HPG + r
                    cs = slice(r * TQ, (r + 1) * TQ)
                    m_new = m_news[h]
                    alpha = jnp.exp2((ms[h] - m_new) * c2)
                    p = jnp.exp2((tbuf[b * N_HEADS_A + h] - m_new) * c2)
                    ls_out.append(alpha * ls[h] + jnp.sum(p, axis=0, keepdims=True))
                    ms_out.append(m_new)
                    acc_s[g, :, cs] = alpha * acc_s[g, :, cs] + _dot(vt, p.astype(BF16))
            ms, ls = tuple(ms_out), tuple(ls_out)
        return ms, ls

    carry = (tuple(jnp.full((1, TQ), NEG, F32) for _ in range(N_HEADS_A)),
             tuple(jnp.zeros((1, TQ), F32) for _ in range(N_HEADS_A)))
    carry = lax.fori_loop(
        0, n_trips,
        lambda jt, c: attn_blocks([FAR_BLOCKS * jt + u for u in range(FAR_BLOCKS)], [None] * FAR_BLOCKS, c), carry)
    for left in range(1, FAR_BLOCKS):
        carry = lax.cond(n_left == left,
                         lambda c, left=left: attn_blocks([n_far - left + u for u in range(left)], [None] * left, c),
                         lambda c: c, carry)
    ms, ls = lax.cond(nkb >= 2, lambda c: attn_blocks([nkb - 2, nkb - 1], [0, KB], c),
                      lambda c: attn_blocks([nkb - 1], [KB], c), carry)

    for g in range(N_KV_A):
        for r in range(HPG):
            h = g * HPG + r
            out_t = acc_s[g, :, r * TQ:(r + 1) * TQ] / ls[h]
            o_ref[0, :, h * HEAD_DIM_A:(h + 1) * HEAD_DIM_A] = out_t.T.astype(o_ref.dtype)


def _dsa(q, qi, kiwi, kb, vb, kie, kio, rel_bias, *, n_keys, pos0, tq, kblk):
    B, T, _ = q.shape
    Lp = kb.shape[1]
    assert T % tq == 0 and Lp % kblk == 0 and tq % CHUNK == 0 or T == tq
    n_qt = T // tq
    if n_qt == 1:
        kb0, kstep = Lp // kblk, 0
        assert ((pos0 + tq - 1) // CHUNK + 1) * CHUNK >= n_keys
    else:
        assert pos0 == 0 and tq == kblk and n_keys == Lp == T
        kb0, kstep = 1, 1
    topk = min(TOPK_MAX, n_keys // 4)
    rb = min(128, tq)

    t = jnp.arange(tq, dtype=jnp.int32)[:, None]
    c = jnp.arange(2 * kblk, dtype=jnp.int32)[None, :]
    if n_qt == 1:
        q_abs = pos0 + t
        s_abs = (kb0 - 2) * kblk + c
    else:
        q_abs = kblk + t
        s_abs = c
    vis = ((s_abs // CHUNK) <= (q_abs // CHUNK)) & (s_abs < (n_keys if n_qt == 1 else 2 * kblk))
    far_bucket = _far_bucket(kblk + 1, max(Lp, kblk + 2))

    n_kb = Lp // kblk
    hpg = N_HEADS_A // N_KV_A
    common = dict(TQ=tq, KB=kblk, kb0=kb0, kstep=kstep, topk=topk, far_bucket=far_bucket,
                  tie_bits=int(math.ceil(math.log2(Lp))) + 1)
    attend_scratch = [pltpu.VMEM((N_KV_A, hpg * tq, HEAD_DIM_A), BF16),
                      pltpu.VMEM((N_KV_A, hpg * tq, LANES), F32),
                      pltpu.VMEM((N_KV_A, hpg * tq, LANES), F32),
                      pltpu.VMEM((N_KV_A, hpg * tq, HEAD_DIM_A), F32)]
    if tq % LANES == 0:
        tab_t = _bias_table(rel_bias, _t5_bucket(s_abs - q_abs).T, shift_bucket=far_bucket)
        vt = jnp.swapaxes(vb.reshape(B, n_kb, kblk, D_KV), 2, 3)
        common.pop("far_bucket")
        return pl.pallas_call(
            functools.partial(_dsa_kernel_t, **common),
            grid=(B, n_qt),
            in_specs=[pl.BlockSpec((1, tq, D_ATTN), lambda b, i: (b, i, 0)),
                      pl.BlockSpec((1, tq, N_IDX_HEADS * IDX_DIM), lambda b, i: (b, i, 0)),
                      pl.BlockSpec((1, tq, LANES), lambda b, i: (b, i, 0)),
                      pl.BlockSpec((1, Lp, D_KV), lambda b, i: (b, 0, 0)),
                      pl.BlockSpec((1, n_kb, D_KV, kblk), lambda b, i: (b, 0, 0, 0)),
                      pl.BlockSpec((1, Lp, LANES), lambda b, i: (b, 0, 0)),
                      pl.BlockSpec((1, Lp, LANES), lambda b, i: (b, 0, 0)),
                      _const_spec((N_HEADS_A, 2 * kblk, tq)),
                      _const_spec((2 * kblk, tq))],
            out_specs=pl.BlockSpec((1, tq, D_ATTN), lambda b, i: (b, i, 0)),
            out_shape=jax.ShapeDtypeStruct((B, T, D_ATTN), BF16),
            scratch_shapes=[pltpu.VMEM((n_kb, kblk, tq), jnp.int32),
                            pltpu.VMEM((N_DIGITS, n_kb, kblk, tq), BF16),
                            pltpu.VMEM((n_kb, kblk, tq), F32),
                            pltpu.VMEM((N_KV_A, hpg * tq, HEAD_DIM_A), BF16),
                            pltpu.VMEM((8, tq), F32),
                            pltpu.VMEM((N_KV_A, HEAD_DIM_A, hpg * tq), F32),
                            pltpu.VMEM((FAR_BLOCKS * N_HEADS_A, kblk, tq), F32)],
            compiler_params=pltpu.CompilerParams(dimension_semantics=("arbitrary", "arbitrary"),
                                                 vmem_limit_bytes=VMEM_LIMIT),
            name="dsa_t",
        )(q, qi, kiwi, kb, vt, kie, kio, tab_t, vis.astype(F32).T)

    bias_near = _bias_table(rel_bias, _t5_bucket(s_abs - q_abs))
    kern = functools.partial(_dsa_kernel, RB=rb, **common)
    return pl.pallas_call(
        kern,
        grid=(B, n_qt),
        in_specs=[pl.BlockSpec(memory_space=pltpu.SMEM),
                  pl.BlockSpec((1, tq, D_ATTN), lambda b, i: (b, i, 0)),
                  pl.BlockSpec((1, tq, N_IDX_HEADS * IDX_DIM), lambda b, i: (b, i, 0)),
                  pl.BlockSpec((1, tq, LANES), lambda b, i: (b, i, 0)),
                  pl.BlockSpec((1, Lp, D_KV), lambda b, i: (b, 0, 0)),
                  pl.BlockSpec((1, Lp, D_KV), lambda b, i: (b, 0, 0)),
                  pl.BlockSpec((1, Lp, LANES), lambda b, i: (b, 0, 0)),
                  pl.BlockSpec((1, Lp, LANES), lambda b, i: (b, 0, 0)),
                  _const_spec((N_HEADS_A, tq, 2 * kblk)),
                  _const_spec((tq, 2 * kblk))],
        out_specs=pl.BlockSpec((1, tq, D_ATTN), lambda b, i: (b, i, 0)),
        out_shape=jax.ShapeDtypeStruct((B, T, D_ATTN), BF16),
        scratch_shapes=[pltpu.VMEM((n_kb, tq, kblk), jnp.int32),
                        pltpu.VMEM((n_kb, tq, kblk), F32),
                        pltpu.VMEM((N_IDX_HEADS, tq, LANES), F32),
                        pltpu.VMEM((N_KV_A, hpg * tq, HEAD_DIM_A), BF16),
                        pltpu.VMEM((tq, 1), F32),
                        pltpu.VMEM((N_KV_A, hpg * tq, LANES), F32),
                        pltpu.VMEM((N_KV_A, hpg * tq, LANES), F32),
                        pltpu.VMEM((N_KV_A, hpg * tq, HEAD_DIM_A), F32)],
        compiler_params=pltpu.CompilerParams(dimension_semantics=("arbitrary", "arbitrary"),
                                             vmem_limit_bytes=VMEM_LIMIT),
        name="dsa",
    )(rel_bias, q, qi, kiwi, kb, vb, kie, kio, bias_near, vis.astype(F32))


def _mix_kernel(x_ref, a_ref, u_ref, hist_ref, mk_ref, mv_ref, wpool_ref, pscale_ref, wout_ref, gx_ref,
                wxq_ref, wxo_ref, gffn_ref, wr_ref, br_ref, o_ref, ext, *, tm, pos0):
    t = pl.program_id(1)
    H = POOL_HIST + 1

    @pl.when(t == 0)
    def _():
        ext[0:H, :] = hist_ref[0]

    @pl.when(t > 0)
    def _():
        ext[0:H, :] = ext[tm:tm + H, :]

    ext[H:H + tm, :] = u_ref[0]

    da = _dot(a_ref[0], wout_ref[0:D_ATTN, :])

    pos = pos0 + t * tm + lax.broadcasted_iota(jnp.int32, (tm, 1), 0)
    pouts = []
    for gi, w in enumerate(POOL_WINDOWS):
        c0, c1 = gi * POOL_GROUP_DIM, (gi + 1) * POOL_GROUP_DIM
        wsum = ext[H:H + tm, c0:c1]
        for s in range(1, w):
            wsum = wsum + ext[H - s:H - s + tm, c0:c1]
        cnt = jnp.minimum(pos + 1, w).astype(F32)
        d = wsum / cnt - u_ref[0, :, c0:c1]
        y = _dot(d.astype(BF16), wpool_ref[gi]) * pscale_ref[:, c0:c1]
        pouts.append(y.astype(BF16))
    pcat = jnp.concatenate(pouts, axis=1)

    D = x_ref.shape[2]
    x1 = x_ref[0] + da + _dot(pcat, wout_ref[D_ATTN:D_ATTN + D_POOL, :])
    h = _rms(x1, gx_ref[...]).astype(BF16)
    qx = _dot(h, wxq_ref[...])
    heads = [slice(hh * XHEAD_DIM, (hh + 1) * XHEAD_DIM) for hh in range(N_XHEADS)]
    all_logits = [_dot_nt(qx[:, hs].astype(BF16), mk_ref[0, :, hs]) for hs in heads]
    outs = []
    for hs, logits in zip(heads, all_logits):
        logits = logits * (XHEAD_DIM ** -0.5)
        m = jnp.max(logits, axis=1, keepdims=True)
        p = jnp.exp(logits - m)
        l = jnp.sum(p, axis=1, keepdims=True)
        outs.append((_dot(p.astype(BF16), mv_ref[0, :, hs]) / l).astype(BF16))
    o = jnp.concatenate(outs, axis=1)
    x2 = x1 + _dot(o, wxo_ref[...])
    o_ref[0, :, 0:D] = x2
    hf = _rms(x2, gffn_ref[...]).astype(BF16)
    o_ref[0, :, D:D + LANES] = _route(_dot(hf, wr_ref[...]) + br_ref[...])


def _mix(x, a, u, hist, mk, mv, w_pool, pool_scale, w_out, g_x, w_xq, w_xo, g_ffn, w_r, b_r, *, pos0, tm):
    B, T, D = x.shape
    n_mem = mk.shape[1]
    H = POOL_HIST + 1
    assert T % tm == 0 and tm >= H
    hist16 = jnp.concatenate([jnp.zeros((B, 1, D_POOL), F32), hist], axis=1)
    return pl.pallas_call(
        functools.partial(_mix_kernel, tm=tm, pos0=pos0),
        grid=(B, T // tm),
        in_specs=[pl.BlockSpec((1, tm, D), lambda b, t: (b, t, 0)),
                  pl.BlockSpec((1, tm, D_ATTN), lambda b, t: (b, t, 0)),
                  pl.BlockSpec((1, tm, D_POOL), lambda b, t: (b, t, 0)),
                  pl.BlockSpec((1, H, D_POOL), lambda b, t: (b, 0, 0)),
                  pl.BlockSpec((1, n_mem, D_X), lambda b, t: (b, 0, 0)),
                  pl.BlockSpec((1, n_mem, D_X), lambda b, t: (b, 0, 0)),
                  _const_spec(w_pool.shape),
                  _const_spec((1, D_POOL)),
                  _const_spec(w_out.shape),
                  _const_spec((1, D)),
                  _const_spec(w_xq.shape),
                  _const_spec(w_xo.shape),
                  _const_spec((1, D)),
                  _const_spec(w_r.shape),
                  _const_spec((1, LANES))],
        out_specs=pl.BlockSpec((1, tm, D + LANES), lambda b, t: (b, t, 0)),
        out_shape=jax.ShapeDtypeStruct((B, T, D + LANES), F32),
        scratch_shapes=[pltpu.VMEM((H + tm, D_POOL), F32)],
        compiler_params=pltpu.CompilerParams(dimension_semantics=("arbitrary", "arbitrary"),
                                             vmem_limit_bytes=VMEM_LIMIT),
        name="mix",
    )(x, a, u, hist16, mk, mv, w_pool, pool_scale.reshape(1, D_POOL), w_out, g_x.reshape(1, D), w_xq, w_xo,
      g_ffn.reshape(1, D), w_r, b_r)


R_OFF = N_GROUPS


def _route(logits):
    tm = logits.shape[0]
    lane = lax.broadcasted_iota(jnp.int32, (tm, LANES), 1).astype(F32)
    ninf = -jnp.inf
    big = float(LANES)
    gl = jnp.where(lane < N_GROUPS, logits, ninf)
    gmax = jnp.max(gl, axis=1, keepdims=True)
    g_sel = jnp.min(jnp.where(gl == gmax, lane, big), axis=1, keepdims=True)
    g_prob = 1.0 / jnp.sum(jnp.exp(gl - gmax), axis=1, keepdims=True)
    lo = R_OFF + g_sel * EXPERTS_PER_GROUP
    el = jnp.where((lane >= lo) & (lane < lo + EXPERTS_PER_GROUP), logits, ninf)
    tv0 = jnp.max(el, axis=1, keepdims=True)
    ti0 = jnp.min(jnp.where(el == tv0, lane, big), axis=1, keepdims=True)
    el2 = jnp.where(lane == ti0, ninf, el)
    tv1 = jnp.max(el2, axis=1, keepdims=True)
    ti1 = jnp.min(jnp.where(el2 == tv1, lane, big), axis=1, keepdims=True)
    e1 = jnp.exp(tv1 - tv0)
    den = 1.0 + e1
    w0 = g_prob / den
    w1 = g_prob * e1 / den
    ids = jnp.where(lane == 0.0, jnp.minimum(ti0, ti1) - R_OFF,
                    jnp.where(lane == 1.0, jnp.maximum(ti0, ti1) - R_OFF, 0.0))
    return jnp.where(lane == ti0, w0, 0.0) + jnp.where(lane == ti1, w1, 0.0) + ids


def _expert(h, rec, e, wg, wu, wd):
    a = _dot(h, wg)
    b = _dot(h, wu)
    lane = lax.broadcasted_iota(jnp.int32, rec.shape, 1)
    cw = jnp.sum(jnp.where(lane == e + R_OFF, rec, 0.0), axis=1, keepdims=True)
    hid = a * jax.nn.sigmoid(a) * b * cw
    return _dot(hid.astype(BF16), wd)


def _moe_kernel(x_ref, gffn_ref, wg_ref, wu_ref, wd_ref, gfin_ref, o_ref, h_s, acc_s):
    e = pl.program_id(1)
    d = o_ref.shape[1]

    @pl.when(e == 0)
    def _():
        h_s[...] = _rms(x_ref[:, 0:d], gffn_ref[...]).astype(BF16)
        acc_s[...] = jnp.zeros(acc_s.shape, F32)

    acc_s[...] += _expert(h_s[...], x_ref[:, d:d + LANES], e, wg_ref[0], wu_ref[0], wd_ref[0])

    @pl.when(e == pl.num_programs(1) - 1)
    def _():
        o_ref[...] = _rms(x_ref[:, 0:d] + acc_s[...], gfin_ref[...])


def _moe(x, g_ffn, w_gate, w_up, w_down, g_final, *, tm):
    n, de = x.shape
    d = de - LANES
    assert n % tm == 0
    return pl.pallas_call(
        _moe_kernel,
        grid=(n // tm, N_EXPERTS),
        in_specs=[pl.BlockSpec((tm, de), lambda i, e: (i, 0)),
                  _const_spec((1, d)),
                  pl.BlockSpec((1, d, D_EXPERT), lambda i, e: (e, 0, 0)),
                  pl.BlockSpec((1, d, D_EXPERT), lambda i, e: (e, 0, 0)),
                  pl.BlockSpec((1, D_EXPERT, d), lambda i, e: (e, 0, 0)),
                  _const_spec((1, d))],
        out_specs=pl.BlockSpec((tm, d), lambda i, e: (i, 0)),
        out_shape=jax.ShapeDtypeStruct((n, d), F32),
        scratch_shapes=[pltpu.VMEM((tm, d), BF16),
                        pltpu.VMEM((tm, d), F32)],
        compiler_params=pltpu.CompilerParams(dimension_semantics=("arbitrary", "arbitrary"),
                                             vmem_limit_bytes=VMEM_LIMIT),
        name="moe",
    )(x, g_ffn.reshape(1, d), w_gate, w_up, w_down, g_final.reshape(1, d))


MOE_TILES_PER_STEP = 2


def _moe_sparse_kernel(ea_ref, eb_ref, nv_ref, idx_ref, idxn_ref, x_hbm, gffn_ref, gfin_ref, *rest, tm, tps):
    w_refs = [rest[6 * k:6 * k + 6] for k in range(tps)]
    o_hbm, xbuf, ybuf, gsem, ssem = rest[6 * tps:]
    t = pl.program_id(0)
    nt = pl.num_programs(0)
    d = o_hbm.shape[1]
    slot = lax.rem(t, 2)
    nslot = 1 - slot
    rows = tps * tm
    WAIT_ROWS = 8

    def step_rows(step):
        nvs = [nv_ref[tps * step + k] for k in range(tps)]
        return nvs, functools.reduce(lambda a, b: a + b, nvs)

    nvs, nv = step_rows(t)

    def gather_row(ids, s, r):
        return pltpu.make_async_copy(x_hbm.at[pl.ds(ids[0, 0, r], 1), :], xbuf.at[s, pl.ds(r, 1), :], gsem.at[s])

    def gather_loop(ids, s):
        def body(r, carry):
            gather_row(ids, s, r).start()
            return carry
        lax.fori_loop(0, rows, body, 0)

    def gather_wait(s):
        pltpu.make_async_copy(x_hbm.at[pl.ds(0, rows), :], xbuf.at[s], gsem.at[s]).wait()

    def scatter_rows(s, r, row, n):
        return pltpu.make_async_copy(ybuf.at[s, pl.ds(r, n), :], o_hbm.at[pl.ds(row, n), :], ssem.at[s])

    def scatter_wait(s, n):
        def body_many(r, carry):
            scatter_rows(s, 0, 0, WAIT_ROWS).wait()
            return carry
        lax.fori_loop(0, n // WAIT_ROWS, body_many, 0)

        def body_one(r, carry):
            scatter_rows(s, 0, 0, 1).wait()
            return carry
        lax.fori_loop(0, lax.rem(n, WAIT_ROWS), body_one, 0)

    @pl.when(t == 0)
    def _():
        gather_loop(idx_ref, 0)

    @pl.when(t >= 2)
    def _():
        scatter_wait(slot, step_rows(jnp.maximum(t - 2, 0))[1])

    gather_wait(slot)

    @pl.when(nv > 0)
    def _():
        tiles = []
        for k in range(tps):
            x = xbuf[slot, k * tm:(k + 1) * tm, :]
            xr = x[:, 0:d]
            tiles.append((xr, x[:, d:d + LANES], _rms(xr, gffn_ref[...]).astype(BF16)))
        for r in range(rows):
            gather_row(idxn_ref, nslot, r).start()
        pre = [[(_dot(h, wga[0]), _dot(h, wua[0])), (_dot(h, wgb[0]), _dot(h, wub[0]))]
               for (_, _, h), (wga, wua, _, wgb, wub, _) in zip(tiles, w_refs)]
        for k in range(tps):
            xr, rec, _ = tiles[k]
            lane = lax.broadcasted_iota(jnp.int32, rec.shape, 1)
            experts = (ea_ref[tps * t + k], eb_ref[tps * t + k])
            acc = None
            for (a, b), e, wd in zip(pre[k], experts, (w_refs[k][2], w_refs[k][5])):
                cw = jnp.sum(jnp.where(lane == e + R_OFF, rec, 0.0), axis=1, keepdims=True)
                hid = a * jax.nn.sigmoid(a) * b * cw
                out = _dot(hid.astype(BF16), wd[0])
                acc = out if acc is None else acc + out
            ybuf[slot, k * tm:(k + 1) * tm, :] = _rms(xr + acc, gfin_ref[...])

        for k in range(tps):
            def body_many(i, carry, k=k):
                for u in range(WAIT_ROWS):
                    r = k * tm + i * WAIT_ROWS + u
                    scatter_rows(slot, r, idx_ref[0, 0, r], 1).start()
                return carry
            lax.fori_loop(0, nvs[k] // WAIT_ROWS, body_many, 0)

            def body_one(i, carry, k=k):
                r = k * tm + i
                scatter_rows(slot, r, idx_ref[0, 0, r], 1).start()
                return carry
            lax.fori_loop((nvs[k] // WAIT_ROWS) * WAIT_ROWS, nvs[k], body_one, 0)

    @pl.when(nv <= 0)
    def _():
        gather_loop(idxn_ref, nslot)

    @pl.when(t == nt - 1)
    def _():
        gather_wait(nslot)

        @pl.when(t >= 1)
        def _():
            scatter_wait(nslot, step_rows(jnp.maximum(t - 1, 0))[1])
        scatter_wait(slot, nv)


def _moe_tile_rows(n):
    n_pairs = N_GROUPS * (EXPERTS_PER_GROUP * (EXPERTS_PER_GROUP - 1) // 2)
    mean = n / n_pairs
    rows = (mean + 3.0 * math.sqrt(mean)) / 2
    return int(min(256, max(SUB_BF16, -(-rows // SUB_BF16) * SUB_BF16)))


def _moe_sparse(x, g_ffn, w_gate, w_up, w_down, g_final, *, tm):
    n, de = x.shape
    d = de - LANES
    n_pairs = N_GROUPS * (EXPERTS_PER_GROUP * (EXPERTS_PER_GROUP - 1) // 2)
    tps = MOE_TILES_PER_STEP
    nt = -(-n // tm) + n_pairs
    nt = -(-nt // tps) * tps
    n_steps = nt // tps
    n_keys = N_EXPERTS * N_EXPERTS

    ids = x[:, d:d + 2].astype(jnp.int32)
    key = ids[:, 0] * N_EXPERTS + ids[:, 1]
    order = jnp.argsort(key).astype(jnp.int32)
    skey = key[order]
    count_below = lambda sorted_vals, q: jnp.sum((sorted_vals[None, :] < q[:, None]).astype(jnp.int32), axis=1)
    edges = count_below(skey, jnp.arange(n_keys + 1, dtype=jnp.int32))
    starts, ends = edges[:-1], edges[1:]
    tiles = (ends - starts + tm - 1) // tm
    cum = jnp.cumsum(tiles)
    total = cum[-1]
    tt = jnp.arange(nt, dtype=jnp.int32)
    tc = jnp.minimum(tt, total - 1)
    cls = count_below(cum, tc + 1)
    first = cum[cls] - tiles[cls]
    tstart = starts[cls] + (tc - first) * tm
    nv = jnp.where(tt < total, jnp.clip(ends[cls] - tstart, 0, tm), 0).astype(jnp.int32)
    ea = (cls // N_EXPERTS).astype(jnp.int32)
    eb = (cls % N_EXPERTS).astype(jnp.int32)
    rows = jnp.clip(tstart[:, None] + jnp.arange(tm, dtype=jnp.int32)[None, :], 0, n - 1)
    idx = order[rows].reshape(n_steps, 1, tps * tm)

    def wspec(shape, experts, k):
        return pl.BlockSpec(shape, lambda t, ea, eb, nv: ((ea, eb)[experts][tps * t + k], 0, 0))

    cspec = lambda shape: pl.BlockSpec(shape, lambda t, ea, eb, nv: (0,) * len(shape),
                                       pipeline_mode=pl.Buffered(1))
    w_specs, w_args = [], []
    for k in range(tps):
        for experts in (0, 1):
            w_specs += [wspec((1, d, D_EXPERT), experts, k), wspec((1, d, D_EXPERT), experts, k),
                        wspec((1, D_EXPERT, d), experts, k)]
            w_args += [w_gate, w_up, w_down]
    grid_spec = pltpu.PrefetchScalarGridSpec(
        num_scalar_prefetch=3,
        grid=(n_steps,),
        in_specs=[pl.BlockSpec((1, 1, tps * tm), lambda t, ea, eb, nv: (t, 0, 0), memory_space=pltpu.SMEM),
                  pl.BlockSpec((1, 1, tps * tm), lambda t, ea, eb, nv: (jnp.minimum(t + 1, n_steps - 1), 0, 0),
                               memory_space=pltpu.SMEM),
                  pl.BlockSpec(memory_space=pl.ANY),
                  cspec((1, d)), cspec((1, d))] + w_specs,
        out_specs=pl.BlockSpec(memory_space=pl.ANY),
        scratch_shapes=[pltpu.VMEM((2, tps * tm, de), F32),
                        pltpu.VMEM((2, tps * tm, d), F32),
                        pltpu.SemaphoreType.DMA((2,)),
                        pltpu.SemaphoreType.DMA((2,))])
    return pl.pallas_call(
        functools.partial(_moe_sparse_kernel, tm=tm, tps=tps),
        grid_spec=grid_spec,
        out_shape=jax.ShapeDtypeStruct((n, d), F32),
        compiler_params=pltpu.CompilerParams(dimension_semantics=("arbitrary",),
                                             vmem_limit_bytes=VMEM_LIMIT),
        name="moe_sparse",
    )(ea, eb, nv, idx, idx, x, g_ffn.reshape(1, d), g_final.reshape(1, d), *w_args)


C_Q = 0
C_QI = C_Q + D_ATTN
C_K = C_QI + N_IDX_HEADS * IDX_DIM
C_V = C_K + D_KV
C_KIWI = C_V + D_KV
C_KIE = C_KIWI + LANES
C_KIO = C_KIE + LANES
C_U = C_KIO + LANES
C_END = C_U + D_POOL


def _layout_w_in(w_in):
    d = w_in.shape[0]
    offs = np.cumsum((0, D_ATTN, D_KV, D_KV, N_IDX_HEADS * IDX_DIM, IDX_DIM, N_IDX_HEADS, D_POOL))
    wq, wk, wv, wqi, wki, wwi, wu = (w_in[:, offs[n]:offs[n + 1]] for n in range(7))
    z = lambda n: jnp.zeros((d, n), w_in.dtype)
    cat = jnp.concatenate([wq, wqi, wk, wv,
                           wki, wwi, z(LANES - IDX_DIM - N_IDX_HEADS),
                           wki, z(LANES - IDX_DIM),
                           z(LANES - IDX_DIM), wki,
                           wu], axis=1)
    assert cat.shape[1] == C_END
    return cat.astype(BF16)


IN_GROUPS = ((C_Q, C_QI, 1),
             (C_QI, C_K, 1),
             (C_K, C_V, 2),
             (C_V, C_KIWI, 2),
             (C_KIWI, C_KIE, 1),
             (C_KIE, C_KIO, 1),
             (C_KIO, C_U, 1),
             (C_U, C_END, 1))
IN_DTYPES = (BF16, BF16, F32, BF16, F32, BF16, F32, BF16, BF16, F32)


def _pad_rows(a, n):
    return jnp.pad(a, ((0, 0), (0, n - a.shape[1]), (0, 0)))


def _layer(x, pos0, caches, pool_hist, mk, mv, rel_bias, W, *, tm_in, tq, tm_mix, tm_moe, sparse_moe):
    B, T, D = x.shape
    n = B * T
    q, qi, k, kb, v, vb, kiwi, kie, kio, u = _norm_matmul(
        x.reshape(n, D), W["g_mix"], W["w_in"], IN_GROUPS, IN_DTYPES, tm_in)
    r3 = lambda a: a.reshape(B, T, a.shape[-1])
    kblk = KEY_BLOCK
    if caches is None:
        n_keys = T
        kb3, vb3, kie3, kio3 = r3(kb), r3(vb), r3(kie), r3(kio)
    else:
        k_hist, v_hist, ki_hist = caches
        past = k_hist.shape[1]
        n_keys = past + T
        lp = -(-n_keys // kblk) * kblk
        ki = r3(kiwi)[:, :, :IDX_DIM]
        ki_all = jnp.concatenate([ki_hist, ki], axis=1)
        zeros = jnp.zeros_like(ki_all)
        kb3 = _pad_rows(jnp.concatenate([k_hist.reshape(B, past, D_KV), r3(k)], axis=1), lp).astype(BF16)
        vb3 = _pad_rows(jnp.concatenate([v_hist.reshape(B, past, D_KV), r3(v)], axis=1), lp).astype(BF16)
        kie3 = _pad_rows(jnp.concatenate([ki_all, zeros], axis=2), lp).astype(BF16)
        kio3 = _pad_rows(jnp.concatenate([zeros, ki_all], axis=2), lp).astype(BF16)
    a_out = _dsa(r3(q), r3(qi), r3(kiwi), kb3, vb3, kie3, kio3, rel_bias,
                 n_keys=n_keys, pos0=pos0, tq=tq, kblk=kblk)
    x2 = _mix(x, a_out, r3(u), pool_hist, mk.astype(BF16), mv.astype(BF16), W["w_pool"], W["pool_scale"],
              W["w_out"], W["g_x"], W["w_xq"], W["w_xo"], W["g_ffn"], W["w_r"], W["b_r"], pos0=pos0, tm=tm_mix)
    moe = _moe_sparse if sparse_moe else _moe
    y = moe(x2.reshape(n, D + LANES), W["g_ffn"], W["w_gate"], W["w_up"], W["w_down"], W["g_final"], tm=tm_moe)
    new_pool = jnp.concatenate([pool_hist, r3(u)], axis=1)[:, -POOL_HIST:]
    return (y.reshape(B, T, D), k.reshape(B, T, N_KV_A, HEAD_DIM_A), v.reshape(B, T, N_KV_A, HEAD_DIM_A),
            r3(kiwi)[:, :, :IDX_DIM], new_pool)


def kernel(x_prompt, x_sample, mem_prompt, cache_k, cache_v, cache_k_idx, cache_pool, cache_mem_k, cache_mem_v, rel_bias, g_mix, w_in, w_pool, pool_scale, w_out, g_mem, w_mk, w_mv, g_x, w_xq, w_xo, g_ffn, w_rg, b_rg, w_re, b_re, w_gate, w_up, w_down, g_final):
    depth = g_mix.shape[0]
    assert depth == 1
    l = 0
    B, T, D = x_prompt.shape
    Bs, Ts, _ = x_sample.shape
    n_mem = mem_prompt.shape[1]
    past = cache_k.shape[2]

    w_r = jnp.concatenate([w_rg[l], w_re[l], jnp.zeros((D, LANES - N_GROUPS - N_EXPERTS), F32)], axis=1)
    W = dict(
        g_mix=g_mix[l], w_in=_layout_w_in(w_in[l]), w_pool=w_pool[l].astype(BF16), pool_scale=pool_scale[l],
        w_out=w_out[l].astype(BF16), g_x=g_x[l], w_xq=w_xq[l].astype(BF16), w_xo=w_xo[l].astype(BF16),
        g_ffn=g_ffn[l], w_r=w_r.astype(BF16),
        b_r=jnp.concatenate([b_rg[l], b_re[l], jnp.zeros((LANES - N_GROUPS - N_EXPERTS,), F32)]).reshape(1, LANES),
        w_gate=w_gate[l].astype(BF16), w_up=w_up[l].astype(BF16), w_down=w_down[l].astype(BF16),
        g_final=g_final)

    w_m = jnp.concatenate([w_mk[l], w_mv[l]], axis=1).astype(BF16)
    mk, mv = _norm_matmul(mem_prompt.reshape(B * n_mem, D), g_mem[l], w_m,
                          ((0, D_X, 1), (D_X, 2 * D_X, 1)), (F32, F32), TOKEN_TILE)
    mk = mk.reshape(B, n_mem, D_X)
    mv = mv.reshape(B, n_mem, D_X)

    yp, kp, vp, kip, pp = _layer(x_prompt, 0, None, jnp.zeros((B, POOL_HIST, D_POOL), F32), mk, mv, rel_bias, W,
                                 tm_in=TOKEN_TILE, tq=QUERY_TILE, tm_mix=TOKEN_TILE, tm_moe=_moe_tile_rows(B * T),
                                 sparse_moe=True)
    ys, ks, vs, kis, ps = _layer(x_sample, past,
                                 (cache_k[l], cache_v[l], cache_k_idx[l]), cache_pool[l],
                                 cache_mem_k[l].reshape(Bs, n_mem, D_X), cache_mem_v[l].reshape(Bs, n_mem, D_X),
                                 rel_bias, W, tm_in=Bs * Ts, tq=Ts, tm_mix=Ts, tm_moe=Bs * Ts, sparse_moe=False)
    st = lambda a: a[None]
    return (yp, ys, st(kp), st(vp), st(kip), st(pp),
            st(mk.reshape(B, n_mem, N_XHEADS, XHEAD_DIM)), st(mv.reshape(B, n_mem, N_XHEADS, XHEAD_DIM)),
            st(ks), st(vs), st(kis), st(ps))
```

```python
import functools
import math

import numpy as np
import jax
import jax.numpy as jnp
from jax import lax
from jax.experimental import pallas as pl
from jax.experimental.pallas import tpu as pltpu

F32 = jnp.float32
BF16 = jnp.bfloat16

CHUNK = 64
N_HEADS_A = 8
HEAD_DIM_A = 128
N_KV_A = 2
D_ATTN = N_HEADS_A * HEAD_DIM_A
D_KV = N_KV_A * HEAD_DIM_A
N_IDX_HEADS = 8
IDX_DIM = 64
TOPK_MAX = 256
POOL_WINDOWS = (2, 4, 8, 16)
POOL_GROUP_DIM = 256
D_POOL = len(POOL_WINDOWS) * POOL_GROUP_DIM
POOL_HIST = max(POOL_WINDOWS) - 1
N_BUCKETS = 32
MAX_DISTANCE = 128
N_XHEADS = 4
XHEAD_DIM = 128
D_X = N_XHEADS * XHEAD_DIM
N_GROUPS = 4
EXPERTS_PER_GROUP = 8
N_EXPERTS = N_GROUPS * EXPERTS_PER_GROUP
D_EXPERT = 256
EPS = 1e-6

LANES = 128
INT_MIN = -(2 ** 31)
NEG = -1e30
LOG2E = math.log2(math.e)
VMEM_LIMIT = 48 * 1024 * 1024
TOKEN_TILE = 512
KEY_BLOCK = 256
QUERY_TILE = KEY_BLOCK


def _rms(x, g):
    ms = jnp.mean(x * x, axis=-1, keepdims=True)
    return x * lax.rsqrt(ms + EPS) * g


def _dot(a, b):
    return jnp.dot(a, b, preferred_element_type=F32)


def _dot_nt(a, b):
    return lax.dot_general(a, b, (((1,), (1,)), ((), ())), preferred_element_type=F32)


def _const_spec(shape):
    nd = len(shape)
    return pl.BlockSpec(shape, lambda *_: (0,) * nd, pipeline_mode=pl.Buffered(1))


def _norm_matmul_kernel(x_ref, g_ref, w_ref, *out_refs, groups):
    h = _rms(x_ref[...], g_ref[...]).astype(BF16)
    k = 0
    for c0, c1, n_out in groups:
        acc = _dot(h, w_ref[:, c0:c1])
        for _ in range(n_out):
            out_refs[k][...] = acc.astype(out_refs[k].dtype)
            k += 1


def _norm_matmul(x, g, w, groups, out_dtypes, tm):
    n, d = x.shape
    assert n % tm == 0
    widths = []
    for c0, c1, n_out in groups:
        widths += [c1 - c0] * n_out
    return pl.pallas_call(
        functools.partial(_norm_matmul_kernel, groups=tuple(groups)),
        grid=(n // tm,),
        in_specs=[pl.BlockSpec((tm, d), lambda i: (i, 0)),
                  _const_spec((1, d)),
                  _const_spec(w.shape)],
        out_specs=[pl.BlockSpec((tm, wd), lambda i: (i, 0)) for wd in widths],
        out_shape=[jax.ShapeDtypeStruct((n, wd), dt) for wd, dt in zip(widths, out_dtypes)],
        compiler_params=pltpu.CompilerParams(dimension_semantics=("arbitrary",),
                                             vmem_limit_bytes=VMEM_LIMIT),
        name="norm_matmul",
    )(x, g.reshape(1, d), w)


def _bias_table_kernel(relb_ref, bucket_ref, o_ref, *, shift_bucket):
    bucket = bucket_ref[...]
    for h in range(N_HEADS_A):
        acc = jnp.zeros(bucket.shape, F32)
        for b in range(N_BUCKETS):
            acc = jnp.where(bucket == b, relb_ref[b, h], acc)
        if shift_bucket is None:
            o_ref[h] = acc * LOG2E
        else:
            o_ref[h] = (acc - relb_ref[shift_bucket, h]) * (HEAD_DIM_A ** 0.5)


def _bias_table(rel_bias, bucket, shift_bucket=None):
    tq, w = bucket.shape
    return pl.pallas_call(
        functools.partial(_bias_table_kernel, shift_bucket=shift_bucket),
        in_specs=[pl.BlockSpec(memory_space=pltpu.SMEM),
                  pl.BlockSpec((tq, w), lambda: (0, 0))],
        out_specs=pl.BlockSpec((N_HEADS_A, tq, w), lambda: (0, 0, 0)),
        out_shape=jax.ShapeDtypeStruct((N_HEADS_A, tq, w), F32),
        name="bias_table",
    )(rel_bias, bucket)


def _t5_bucket(rel):
    nb = N_BUCKETS // 2
    max_exact = nb // 2
    bucket = (rel > 0).astype(jnp.int32) * nb
    n = jnp.abs(rel)
    nf = jnp.maximum(n, 1).astype(F32)
    large = max_exact + (jnp.log(nf / max_exact) / math.log(MAX_DISTANCE / max_exact)
                         * (nb - max_exact)).astype(jnp.int32)
    large = jnp.minimum(large, nb - 1)
    return bucket + jnp.where(n < max_exact, n, large)


def _far_bucket(min_dist, max_dist):
    nb = N_BUCKETS // 2
    max_exact = nb // 2
    n = np.arange(min_dist, max_dist + 1, dtype=np.float64)
    large = max_exact + np.floor(np.log(n / max_exact) / math.log(MAX_DISTANCE / max_exact)
                                 * (nb - max_exact) * (1 - 1e-6)).astype(np.int64)
    assert large.min() >= nb - 1, "far key blocks must sit in the saturated distance bucket"
    return nb - 1


def _dsa_kernel(relb_ref, q_ref, qi_ref, kiwi_ref, k_ref, v_ref, kie_ref, kio_ref, bnear_ref, vis_ref,
                o_ref, keys, madd, wb, qs, xcut, mrun, lrun, acc_s,
                *, TQ, KB, RB, kb0, kstep, topk, far_bucket, tie_bits):
    i = pl.program_id(1)
    nkb = kb0 + kstep * i
    n_far = jnp.maximum(nkb - 2, 0)
    n_lc = KB // LANES

    kiwi = kiwi_ref[0]
    for h in range(N_IDX_HEADS):
        w = (kiwi[:, IDX_DIM + h:IDX_DIM + h + 1] * (N_IDX_HEADS ** -0.5)).astype(BF16).astype(F32)
        wb[h] = jnp.broadcast_to(w * (IDX_DIM ** -0.5), (TQ, LANES))

    def idx_block(j, near_col):
        c0 = pl.multiple_of(j * KB, KB)
        ke = kie_ref[0, pl.ds(c0, KB), :]
        ko = kio_ref[0, pl.ds(c0, KB), :]
        acc = [jnp.zeros((TQ, LANES), F32) for _ in range(n_lc)]
        for hp in range(N_IDX_HEADS // 2):
            qh = qi_ref[0, :, hp * LANES:(hp + 1) * LANES]
            se = _dot_nt(qh, ke)
            so = _dot_nt(qh, ko)
            we = wb[2 * hp]
            wo = wb[2 * hp + 1]
            for c in range(n_lc):
                cs = slice(c * LANES, (c + 1) * LANES)
                re = jnp.maximum(se[:, cs], 0.0).astype(BF16).astype(F32)
                ro = jnp.maximum(so[:, cs], 0.0).astype(BF16).astype(F32)
                acc[c] = acc[c] + we * re + wo * ro
        for c in range(n_lc):
            bits = lax.bitcast_convert_type(acc[c], jnp.int32)
            key = bits ^ ((bits >> 31) & jnp.int32(0x7FFFFFFF))
            if near_col is not None:
                vis = vis_ref[:, near_col + c * LANES:near_col + (c + 1) * LANES]
                key = jnp.where(vis > 0.0, key, jnp.int32(INT_MIN))
            keys[j, :, c * LANES:(c + 1) * LANES] = key

    def far_idx(j, carry):
        idx_block(j, None)
        return carry

    lax.fori_loop(0, n_far, far_idx, 0)

    @pl.when(nkb >= 2)
    def _():
        idx_block(nkb - 2, 0)

    idx_block(nkb - 1, KB)

    kf = float(topk)
    lane_f = lax.broadcasted_iota(jnp.int32, (RB, LANES), 1).astype(F32)
    NRB = TQ // RB
    rbs = [slice(rb * RB, (rb + 1) * RB) for rb in range(NRB)]

    def count(pred):
        def body(jj, accs):
            out = []
            for rb in range(NRB):
                blk = keys[jj, rbs[rb], :]
                acc = accs[rb]
                for c in range(n_lc):
                    acc = acc + jnp.where(pred(rb, blk[:, c * LANES:(c + 1) * LANES], jj, c), 1.0, 0.0)
                out.append(acc)
            return tuple(out)
        accs = lax.fori_loop(0, nkb, body, tuple(jnp.zeros((RB, LANES), F32) for _ in range(NRB)))
        return [jnp.sum(acc, axis=1, keepdims=True) for acc in accs]

    def count_ge(cands):
        cb = [jnp.broadcast_to(cand, (RB, LANES)) for cand in cands]
        return count(lambda rb, kblk, jj, c: kblk >= cb[rb])

    def bis_body(p, thrs):
        cands = [thr + lax.shift_left(jnp.int32(1), 31 - p) for thr in thrs]
        cnts = count_ge(cands)
        return tuple(jnp.where(cnt >= kf, cand, thr) for cnt, cand, thr in zip(cnts, cands, thrs))

    thrs = lax.fori_loop(0, 32, bis_body, tuple(jnp.full((RB, 1), INT_MIN, jnp.int32) for _ in range(NRB)))
    thrs = [jnp.maximum(thr, jnp.int32(INT_MIN + 1)) for thr in thrs]
    cnt_ge = count_ge(thrs)
    cnt_gt = count_ge([thr + 1 for thr in thrs])
    needs = [kf - c for c in cnt_gt]
    thrb = [jnp.broadcast_to(thr, (RB, LANES)) for thr in thrs]

    xcut[...] = jnp.full(xcut.shape, 2.0 ** 24, F32)
    excess = cnt_ge[0] - kf
    for c in cnt_ge[1:]:
        excess = jnp.maximum(excess, c - kf)

    @pl.when(jnp.max(excess) > 0.0)
    def _():
        def count_eq_lt(xs):
            xb = [jnp.broadcast_to(x, (RB, LANES)) for x in xs]

            def pred(rb, kblk, jj, c):
                col = lane_f + (jj * KB + c * LANES).astype(F32)
                return jnp.where(kblk == thrb[rb], col, 2.0 ** 25) < xb[rb]
            return count(pred)

        def tie_body(p, xs):
            cands = [x + lax.shift_left(jnp.int32(1), tie_bits - 1 - p).astype(F32) for x in xs]
            cnts = count_eq_lt(cands)
            return tuple(jnp.where(cnt < need, cand, x) for cnt, need, cand, x in zip(cnts, needs, cands, xs))

        xs = lax.fori_loop(0, tie_bits, tie_body, tuple(jnp.zeros((RB, 1), F32) for _ in range(NRB)))
        for rb in range(NRB):
            xcut[rbs[rb], :] = xs[rb]

    xb = [jnp.broadcast_to(xcut[rbs[rb], :], (RB, LANES)) for rb in range(NRB)]

    def mask_body(jj, carry):
        for rb in range(NRB):
            blk = keys[jj, rbs[rb], :]
            for c in range(n_lc):
                kblk = blk[:, c * LANES:(c + 1) * LANES]
                col = lane_f + (jj * KB + c * LANES).astype(F32)
                keep = (kblk > thrb[rb]) | ((kblk == thrb[rb]) & (col <= xb[rb]))
                madd[jj, rbs[rb], c * LANES:(c + 1) * LANES] = jnp.where(keep, 0.0, NEG)
        return carry

    lax.fori_loop(0, nkb, mask_body, 0)

    _attend(relb_ref, q_ref, k_ref, v_ref, bnear_ref, o_ref, madd, qs, mrun, lrun, acc_s,
            TQ=TQ, KB=KB, nkb=nkb, n_far=n_far, far_bucket=far_bucket)


def _attend(relb_ref, q_ref, k_ref, v_ref, bnear_ref, o_ref, madd, qs, mrun, lrun, acc_s,
            *, TQ, KB, nkb, n_far, far_bucket):
    n_lc = KB // LANES
    HPG = N_HEADS_A // N_KV_A
    for g in range(N_KV_A):
        for r in range(HPG):
            h = g * HPG + r
            qs[g, r * TQ:(r + 1) * TQ, :] = q_ref[0, :, h * HEAD_DIM_A:(h + 1) * HEAD_DIM_A]

    def logits(j, near_col, g):
        c0 = pl.multiple_of(j * KB, KB)
        kg = k_ref[0, pl.ds(c0, KB), g * HEAD_DIM_A:(g + 1) * HEAD_DIM_A]
        s = _dot_nt(qs[g], kg)
        mk = madd[j]
        out = []
        for r in range(HPG):
            h = g * HPG + r
            if near_col is None:
                bias = relb_ref[far_bucket, h] * LOG2E
            else:
                bias = bnear_ref[h, :, near_col:near_col + KB]
            out.append(s[r * TQ:(r + 1) * TQ] * (HEAD_DIM_A ** -0.5 * LOG2E) + bias + mk)
        return out

    def over_blocks(fn):
        def far(j, carry):
            fn(j, None)
            return carry

        lax.fori_loop(0, n_far, far, 0)

        @pl.when(nkb >= 2)
        def _():
            fn(nkb - 2, 0)

        fn(nkb - 1, KB)

    mrun[...] = jnp.full(mrun.shape, NEG, F32)

    def max_block(j, near_col):
        for g in range(N_KV_A):
            for r, sh in enumerate(logits(j, near_col, g)):
                rs = slice(r * TQ, (r + 1) * TQ)
                m = mrun[g, rs, :]
                for c in range(n_lc):
                    m = jnp.maximum(m, sh[:, c * LANES:(c + 1) * LANES])
                mrun[g, rs, :] = m

    over_blocks(max_block)

    for g in range(N_KV_A):
        mrun[g] = jnp.broadcast_to(jnp.max(mrun[g], axis=1, keepdims=True), mrun.shape[1:])
    lrun[...] = jnp.zeros(lrun.shape, F32)
    acc_s[...] = jnp.zeros(acc_s.shape, F32)

    def pv_block(j, near_col):
        c0 = pl.multiple_of(j * KB, KB)
        for g in range(N_KV_A):
            vg = v_ref[0, pl.ds(c0, KB), g * HEAD_DIM_A:(g + 1) * HEAD_DIM_A]
            parts = []
            for r, sh in enumerate(logits(j, near_col, g)):
                rs = slice(r * TQ, (r + 1) * TQ)
                mb = mrun[g, rs, :]
                l = lrun[g, rs, :]
                pcs = []
                for c in range(n_lc):
                    p = jnp.exp2(sh[:, c * LANES:(c + 1) * LANES] - mb)
                    l = l + p
                    pcs.append(p.astype(BF16))
                lrun[g, rs, :] = l
                parts.append(jnp.concatenate(pcs, axis=1))
            acc_s[g] += _dot(jnp.concatenate(parts, axis=0), vg)

    over_blocks(pv_block)

    for g in range(N_KV_A):
        out = acc_s[g] / jnp.sum(lrun[g], axis=1, keepdims=True)
        for r in range(HPG):
            h = g * HPG + r
            o_ref[0, :, h * HEAD_DIM_A:(h + 1) * HEAD_DIM_A] = out[r * TQ:(r + 1) * TQ].astype(o_ref.dtype)


DIGIT_BITS = 8
N_DIGITS = 32 // DIGIT_BITS
SUB_BF16 = 16
COUNT_BLOCKS = 4
FAR_BLOCKS = 3


def _dsa_kernel_t(q_ref, qi_ref, kiwi_ref, k_ref, vt_ref, kie_ref, kio_ref, bnear_ref, vist_ref,
                  o_ref, keys, dig, madd, qs, xcut, acc_s, tbuf,
                  *, TQ, KB, kb0, kstep, topk, tie_bits):
    i = pl.program_id(1)
    nkb = kb0 + kstep * i
    n_far = jnp.maximum(nkb - 2, 0)
    n_slab = KB // SUB_BF16

    w_t = kiwi_ref[0].T
    wrow = [(w_t[IDX_DIM + h:IDX_DIM + h + 1, :] * (N_IDX_HEADS ** -0.5)).astype(BF16).astype(F32)
            * (IDX_DIM ** -0.5) for h in range(N_IDX_HEADS)]

    def idx_dots(j):
        c0 = pl.multiple_of(j * KB, KB)
        ke = kie_ref[0, pl.ds(c0, KB), :]
        ko = kio_ref[0, pl.ds(c0, KB), :]
        raw = []
        for hp in range(N_IDX_HEADS // 2):
            qh = qi_ref[0, :, hp * LANES:(hp + 1) * LANES]
            raw.append((_dot_nt(ke, qh), _dot_nt(ko, qh)))
        return raw

    def idx_finish(j, raw, near_row):
        acc = jnp.zeros((KB, TQ), F32)
        for hp, (se, so) in enumerate(raw):
            re = jnp.maximum(se.astype(BF16), 0.0).astype(F32)
            ro = jnp.maximum(so.astype(BF16), 0.0).astype(F32)
            acc = acc + wrow[2 * hp] * re + wrow[2 * hp + 1] * ro
        bits = lax.bitcast_convert_type(acc, jnp.int32)
        key = bits ^ ((bits >> 31) & jnp.int32(0x7FFFFFFF))
        if near_row is not None:
            key = jnp.where(vist_ref[near_row:near_row + KB, :] > 0.0, key, jnp.int32(INT_MIN))
        keys[j] = key
        for dk in range(N_DIGITS):
            sh = 32 - DIGIT_BITS * (dk + 1)
            d = (key >> sh) if dk == 0 else ((key >> sh) & jnp.int32(2 ** DIGIT_BITS - 1))
            dig[dk, j] = d.astype(F32).astype(BF16)

    def idx_blocks(js, near_rows):
        raws = [idx_dots(j) for j in js]
        for j, raw, near_row in zip(js, raws, near_rows):
            idx_finish(j, raw, near_row)

    n_trips = n_far // FAR_BLOCKS
    n_left = n_far - n_trips * FAR_BLOCKS

    def far_idx_trip(jt, carry):
        idx_blocks([FAR_BLOCKS * jt + u for u in range(FAR_BLOCKS)], [None] * FAR_BLOCKS)
        return carry

    lax.fori_loop(0, n_trips, far_idx_trip, 0)

    for left in range(1, FAR_BLOCKS):
        @pl.when(n_left == left)
        def _(left=left):
            idx_blocks([n_far - left + u for u in range(left)], [None] * left)

    @pl.when(nkb >= 2)
    def _():
        idx_blocks([nkb - 2, nkb - 1], [0, KB])

    @pl.when(nkb < 2)
    def _():
        idx_blocks([nkb - 1], [KB])

    one = jnp.ones((SUB_BF16, TQ), BF16)
    zero = jnp.zeros((SUB_BF16, TQ), BF16)

    def count_ge(dk, cand):
        cb = jnp.broadcast_to(cand.astype(BF16), (SUB_BF16, TQ))

        def block_count(jj):
            blk = dig[dk, jj]
            parts = [jnp.where(blk[s * SUB_BF16:(s + 1) * SUB_BF16, :] >= cb, one, zero) for s in range(n_slab)]
            while len(parts) > 1:
                parts = [parts[n] + parts[n + 1] for n in range(0, len(parts), 2)]
            return parts[0]

        def body_many(jq, tot):
            cnts = [block_count(COUNT_BLOCKS * jq + u) for u in range(COUNT_BLOCKS)]
            while len(cnts) > 1:
                cnts = [cnts[n] + cnts[n + 1] for n in range(0, len(cnts), 2)]
            return tot + cnts[0].astype(F32)

        def body_one(jj, tot):
            return tot + block_count(jj).astype(F32)

        tot = lax.fori_loop(0, nkb // COUNT_BLOCKS, body_many, jnp.zeros((SUB_BF16, TQ), F32))
        tot = lax.fori_loop((nkb // COUNT_BLOCKS) * COUNT_BLOCKS, nkb, body_one, tot)
        return jnp.sum(tot, axis=0, keepdims=True)

    kneed = jnp.full((1, TQ), float(topk), F32)
    thr = jnp.zeros((1, TQ), jnp.int32)
    cnt_eq = None
    for dk in range(N_DIGITS):
        lo = -float(2 ** (DIGIT_BITS - 1)) if dk == 0 else 0.0

        def bis_body(p, carry, dk=dk, kneed=kneed):
            t, c_rej = carry
            cand = t + lax.shift_left(jnp.int32(1), DIGIT_BITS - 1 - p).astype(F32)
            cnt = count_ge(dk, cand)
            ok = cnt >= kneed
            return jnp.where(ok, cand, t), jnp.where(ok, c_rej, cnt)

        t, n_gt = lax.fori_loop(0, DIGIT_BITS, bis_body,
                                (jnp.full((1, TQ), lo, F32), jnp.zeros((1, TQ), F32)))
        if dk == N_DIGITS - 1:
            cnt_eq = count_ge(dk, t) - n_gt
        kneed = kneed - n_gt
        thr = thr + lax.shift_left(t.astype(jnp.int32), 32 - DIGIT_BITS * (dk + 1))
        if dk + 1 < N_DIGITS:
            tb = t.astype(BF16)

            def prep(jj, carry, dk=dk, tb=tb):
                dig[dk + 1, jj] = jnp.where(dig[dk, jj] == tb, dig[dk + 1, jj], jnp.asarray(-1.0, BF16))
                return carry

            lax.fori_loop(0, nkb, prep, 0)

    thr = jnp.maximum(thr, jnp.int32(INT_MIN + 1))
    need = kneed
    sub_f = lax.broadcasted_iota(jnp.int32, (KB, TQ), 0).astype(F32)

    xcut[...] = jnp.full(xcut.shape, 2.0 ** 24, F32)

    @pl.when(jnp.max(cnt_eq - need) > 0.0)
    def _():
        def count_eq_lt(x):
            def body(jj, tot):
                col = sub_f + (jj * KB).astype(F32)
                hit = jnp.where(jnp.where(keys[jj] == thr, col, 2.0 ** 25) < x, 1.0, 0.0)
                for s in range(KB // 8):
                    tot = tot + hit[s * 8:(s + 1) * 8, :]
                return tot
            tot = lax.fori_loop(0, nkb, body, jnp.zeros((8, TQ), F32))
            return jnp.sum(tot, axis=0, keepdims=True)

        def tie_body(p, x):
            cand = x + lax.shift_left(jnp.int32(1), tie_bits - 1 - p).astype(F32)
            return jnp.where(count_eq_lt(cand) < need, cand, x)

        x = lax.fori_loop(0, tie_bits, tie_body, jnp.zeros((1, TQ), F32))
        xcut[...] = jnp.broadcast_to(x, xcut.shape)

    xc = xcut[0:1, :]

    def mask_body(jj, carry):
        kblk = keys[jj]
        col = sub_f + (jj * KB).astype(F32)
        at_thr = jnp.where(col <= xc, 0.0, NEG)
        madd[jj] = jnp.where(kblk > thr, 0.0, jnp.where(kblk == thr, at_thr, NEG))
        return carry

    lax.fori_loop(0, nkb, mask_body, 0)

    HPG = N_HEADS_A // N_KV_A
    c2 = HEAD_DIM_A ** -0.5 * LOG2E
    for g in range(N_KV_A):
        for r in range(HPG):
            h = g * HPG + r
            qs[g, r * TQ:(r + 1) * TQ, :] = q_ref[0, :, h * HEAD_DIM_A:(h + 1) * HEAD_DIM_A]
    acc_s[...] = jnp.zeros(acc_s.shape, F32)

    def attn_blocks(js, near_rows, carry):
        ms, ls = carry
        s_all = []
        for j in js:
            c0 = pl.multiple_of(j * KB, KB)
            s_all.append([_dot_nt(k_ref[0, pl.ds(c0, KB), g * HEAD_DIM_A:(g + 1) * HEAD_DIM_A], qs[g])
                          for g in range(N_KV_A)])
        for b, (j, near_row) in enumerate(zip(js, near_rows)):
            mk = madd[j]
            m_news = []
            for g in range(N_KV_A):
                for r in range(HPG):
                    h = g * HPG + r
                    t = s_all[b][g][:, r * TQ:(r + 1) * TQ] + mk
                    if near_row is not None:
                        t = t + bnear_ref[h, near_row:near_row + KB, :]
                    tbuf[b * N_HEADS_A + h] = t
                    m_news.append(jnp.maximum(ms[h], jnp.max(t, axis=0, keepdims=True)))
            ms_out, ls_out = [], []
            for g in range(N_KV_A):
                vt = vt_ref[0, j, g * HEAD_DIM_A:(g + 1) * HEAD_DIM_A, :]
                for r in range(HPG):
                    h = g * HPG + r
                    cs = slice(r * TQ, (r + 1) * TQ)
                    m_new = m_news[h]
                    alpha = jnp.exp2((ms[h] - m_new) * c2)
                    p = jnp.exp2((tbuf[b * N_HEADS_A + h] - m_new) * c2)
                    ls_out.append(alpha * ls[h] + jnp.sum(p, axis=0, keepdims=True))
                    ms_out.append(m_new)
                    acc_s[g, :, cs] = alpha * acc_s[g, :, cs] + _dot(vt, p.astype(BF16))
            ms, ls = tuple(ms_out), tuple(ls_out)
        return ms, ls

    carry = (tuple(jnp.full((1, TQ), NEG, F32) for _ in range(N_HEADS_A)),
             tuple(jnp.zeros((1, TQ), F32) for _ in range(N_HEADS_A)))
    carry = lax.fori_loop(
        0, n_trips,
        lambda jt, c: attn_blocks([FAR_BLOCKS * jt + u for u in range(FAR_BLOCKS)], [None] * FAR_BLOCKS, c), carry)
    for left in range(1, FAR_BLOCKS):
        carry = lax.cond(n_left == left,
                         lambda c, left=left: attn_blocks([n_far - left + u for u in range(left)], [None] * left, c),
                         lambda c: c, carry)
    ms, ls = lax.cond(nkb >= 2, lambda c: attn_blocks([nkb - 2, nkb - 1], [0, KB], c),
                      lambda c: attn_blocks([nkb - 1], [KB], c), carry)

    for g in range(N_KV_A):
        for r in range(HPG):
            h = g * HPG + r
            out_t = acc_s[g, :, r * TQ:(r + 1) * TQ] / ls[h]
            o_ref[0, :, h * HEAD_DIM_A:(h + 1) * HEAD_DIM_A] = out_t.T.astype(o_ref.dtype)


def _dsa(q, qi, kiwi, kb, vb, kie, kio, rel_bias, *, n_keys, pos0, tq, kblk):
    B, T, _ = q.shape
    Lp = kb.shape[1]
    assert T % tq == 0 and Lp % kblk == 0 and tq % CHUNK == 0 or T == tq
    n_qt = T // tq
    if n_qt == 1:
        kb0, kstep = Lp // kblk, 0
        assert ((pos0 + tq - 1) // CHUNK + 1) * CHUNK >= n_keys
    else:
        assert pos0 == 0 and tq == kblk and n_keys == Lp == T
        kb0, kstep = 1, 1
    topk = min(TOPK_MAX, n_keys // 4)
    rb = min(128, tq)

    t = jnp.arange(tq, dtype=jnp.int32)[:, None]
    c = jnp.arange(2 * kblk, dtype=jnp.int32)[None, :]
    if n_qt == 1:
        q_abs = pos0 + t
        s_abs = (kb0 - 2) * kblk + c
    else:
        q_abs = kblk + t
        s_abs = c
    vis = ((s_abs // CHUNK) <= (q_abs // CHUNK)) & (s_abs < (n_keys if n_qt == 1 else 2 * kblk))
    far_bucket = _far_bucket(kblk + 1, max(Lp, kblk + 2))

    n_kb = Lp // kblk
    hpg = N_HEADS_A // N_KV_A
    common = dict(TQ=tq, KB=kblk, kb0=kb0, kstep=kstep, topk=topk, far_bucket=far_bucket,
                  tie_bits=int(math.ceil(math.log2(Lp))) + 1)
    attend_scratch = [pltpu.VMEM((N_KV_A, hpg * tq, HEAD_DIM_A), BF16),
                      pltpu.VMEM((N_KV_A, hpg * tq, LANES), F32),
                      pltpu.VMEM((N_KV_A, hpg * tq, LANES), F32),
                      pltpu.VMEM((N_KV_A, hpg * tq, HEAD_DIM_A), F32)]
    if tq % LANES == 0:
        tab_t = _bias_table(rel_bias, _t5_bucket(s_abs - q_abs).T, shift_bucket=far_bucket)
        vt = jnp.swapaxes(vb.reshape(B, n_kb, kblk, D_KV), 2, 3)
        common.pop("far_bucket")
        return pl.pallas_call(
            functools.partial(_dsa_kernel_t, **common),
            grid=(B, n_qt),
            in_specs=[pl.BlockSpec((1, tq, D_ATTN), lambda b, i: (b, i, 0)),
                      pl.BlockSpec((1, tq, N_IDX_HEADS * IDX_DIM), lambda b, i: (b, i, 0)),
                      pl.BlockSpec((1, tq, LANES), lambda b, i: (b, i, 0)),
                      pl.BlockSpec((1, Lp, D_KV), lambda b, i: (b, 0, 0)),
                      pl.BlockSpec((1, n_kb, D_KV, kblk), lambda b, i: (b, 0, 0, 0)),
                      pl.BlockSpec((1, Lp, LANES), lambda b, i: (b, 0, 0)),
                      pl.BlockSpec((1, Lp, LANES), lambda b, i: (b, 0, 0)),
                      _const_spec((N_HEADS_A, 2 * kblk, tq)),
                      _const_spec((2 * kblk, tq))],
            out_specs=pl.BlockSpec((1, tq, D_ATTN), lambda b, i: (b, i, 0)),
            out_shape=jax.ShapeDtypeStruct((B, T, D_ATTN), BF16),
            scratch_shapes=[pltpu.VMEM((n_kb, kblk, tq), jnp.int32),
                            pltpu.VMEM((N_DIGITS, n_kb, kblk, tq), BF16),
                            pltpu.VMEM((n_kb, kblk, tq), F32),
                            pltpu.VMEM((N_KV_A, hpg * tq, HEAD_DIM_A), BF16),
                            pltpu.VMEM((8, tq), F32),
                            pltpu.VMEM((N_KV_A, HEAD_DIM_A, hpg * tq), F32),
                            pltpu.VMEM((FAR_BLOCKS * N_HEADS_A, kblk, tq), F32)],
            compiler_params=pltpu.CompilerParams(dimension_semantics=("arbitrary", "arbitrary"),
                                                 vmem_limit_bytes=VMEM_LIMIT),
            name="dsa_t",
        )(q, qi, kiwi, kb, vt, kie, kio, tab_t, vis.astype(F32).T)

    bias_near = _bias_table(rel_bias, _t5_bucket(s_abs - q_abs))
    kern = functools.partial(_dsa_kernel, RB=rb, **common)
    return pl.pallas_call(
        kern,
        grid=(B, n_qt),
        in_specs=[pl.BlockSpec(memory_space=pltpu.SMEM),
                  pl.BlockSpec((1, tq, D_ATTN), lambda b, i: (b, i, 0)),
                  pl.BlockSpec((1, tq, N_IDX_HEADS * IDX_DIM), lambda b, i: (b, i, 0)),
                  pl.BlockSpec((1, tq, LANES), lambda b, i: (b, i, 0)),
                  pl.BlockSpec((1, Lp, D_KV), lambda b, i: (b, 0, 0)),
                  pl.BlockSpec((1, Lp, D_KV), lambda b, i: (b, 0, 0)),
                  pl.BlockSpec((1, Lp, LANES), lambda b, i: (b, 0, 0)),
                  pl.BlockSpec((1, Lp, LANES), lambda b, i: (b, 0, 0)),
                  _const_spec((N_HEADS_A, tq, 2 * kblk)),
                  _const_spec((tq, 2 * kblk))],
        out_specs=pl.BlockSpec((1, tq, D_ATTN), lambda b, i: (b, i, 0)),
        out_shape=jax.ShapeDtypeStruct((B, T, D_ATTN), BF16),
        scratch_shapes=[pltpu.VMEM((n_kb, tq, kblk), jnp.int32),
                        pltpu.VMEM((n_kb, tq, kblk), F32),
                        pltpu.VMEM((N_IDX_HEADS, tq, LANES), F32),
                        pltpu.VMEM((N_KV_A, hpg * tq, HEAD_DIM_A), BF16),
                        pltpu.VMEM((tq, 1), F32),
                        pltpu.VMEM((N_KV_A, hpg * tq, LANES), F32),
                        pltpu.VMEM((N_KV_A, hpg * tq, LANES), F32),
                        pltpu.VMEM((N_KV_A, hpg * tq, HEAD_DIM_A), F32)],
        compiler_params=pltpu.CompilerParams(dimension_semantics=("arbitrary", "arbitrary"),
                                             vmem_limit_bytes=VMEM_LIMIT),
        name="dsa",
    )(rel_bias, q, qi, kiwi, kb, vb, kie, kio, bias_near, vis.astype(F32))


def _mix_kernel(x_ref, a_ref, u_ref, hist_ref, mk_ref, mv_ref, wpool_ref, pscale_ref, wout_ref, gx_ref,
                wxq_ref, wxo_ref, gffn_ref, wr_ref, br_ref, o_ref, ext, *, tm, pos0):
    t = pl.program_id(1)
    H = POOL_HIST + 1

    @pl.when(t == 0)
    def _():
        ext[0:H, :] = hist_ref[0]

    @pl.when(t > 0)
    def _():
        ext[0:H, :] = ext[tm:tm + H, :]

    ext[H:H + tm, :] = u_ref[0]

    da = _dot(a_ref[0], wout_ref[0:D_ATTN, :])

    pos = pos0 + t * tm + lax.broadcasted_iota(jnp.int32, (tm, 1), 0)
    pouts = []
    for gi, w in enumerate(POOL_WINDOWS):
        c0, c1 = gi * POOL_GROUP_DIM, (gi + 1) * POOL_GROUP_DIM
        wsum = ext[H:H + tm, c0:c1]
        for s in range(1, w):
            wsum = wsum + ext[H - s:H - s + tm, c0:c1]
        cnt = jnp.minimum(pos + 1, w).astype(F32)
        d = wsum / cnt - u_ref[0, :, c0:c1]
        y = _dot(d.astype(BF16), wpool_ref[gi]) * pscale_ref[:, c0:c1]
        pouts.append(y.astype(BF16))
    pcat = jnp.concatenate(pouts, axis=1)

    D = x_ref.shape[2]
    x1 = x_ref[0] + da + _dot(pcat, wout_ref[D_ATTN:D_ATTN + D_POOL, :])
    h = _rms(x1, gx_ref[...]).astype(BF16)
    qx = _dot(h, wxq_ref[...])
    heads = [slice(hh * XHEAD_DIM, (hh + 1) * XHEAD_DIM) for hh in range(N_XHEADS)]
    all_logits = [_dot_nt(qx[:, hs].astype(BF16), mk_ref[0, :, hs]) for hs in heads]
    outs = []
    for hs, logits in zip(heads, all_logits):
        logits = logits * (XHEAD_DIM ** -0.5)
        m = jnp.max(logits, axis=1, keepdims=True)
        p = jnp.exp(logits - m)
        l = jnp.sum(p, axis=1, keepdims=True)
        outs.append((_dot(p.astype(BF16), mv_ref[0, :, hs]) / l).astype(BF16))
    o = jnp.concatenate(outs, axis=1)
    x2 = x1 + _dot(o, wxo_ref[...])
    o_ref[0, :, 0:D] = x2
    hf = _rms(x2, gffn_ref[...]).astype(BF16)
    o_ref[0, :, D:D + LANES] = _route(_dot(hf, wr_ref[...]) + br_ref[...])


def _mix(x, a, u, hist, mk, mv, w_pool, pool_scale, w_out, g_x, w_xq, w_xo, g_ffn, w_r, b_r, *, pos0, tm):
    B, T, D = x.shape
    n_mem = mk.shape[1]
    H = POOL_HIST + 1
    assert T % tm == 0 and tm >= H
    hist16 = jnp.concatenate([jnp.zeros((B, 1, D_POOL), F32), hist], axis=1)
    return pl.pallas_call(
        functools.partial(_mix_kernel, tm=tm, pos0=pos0),
        grid=(B, T // tm),
        in_specs=[pl.BlockSpec((1, tm, D), lambda b, t: (b, t, 0)),
                  pl.BlockSpec((1, tm, D_ATTN), lambda b, t: (b, t, 0)),
                  pl.BlockSpec((1, tm, D_POOL), lambda b, t: (b, t, 0)),
                  pl.BlockSpec((1, H, D_POOL), lambda b, t: (b, 0, 0)),
                  pl.BlockSpec((1, n_mem, D_X), lambda b, t: (b, 0, 0)),
                  pl.BlockSpec((1, n_mem, D_X), lambda b, t: (b, 0, 0)),
                  _const_spec(w_pool.shape),
                  _const_spec((1, D_POOL)),
                  _const_spec(w_out.shape),
                  _const_spec((1, D)),
                  _const_spec(w_xq.shape),
                  _const_spec(w_xo.shape),
                  _const_spec((1, D)),
                  _const_spec(w_r.shape),
                  _const_spec((1, LANES))],
        out_specs=pl.BlockSpec((1, tm, D + LANES), lambda b, t: (b, t, 0)),
        out_shape=jax.ShapeDtypeStruct((B, T, D + LANES), F32),
        scratch_shapes=[pltpu.VMEM((H + tm, D_POOL), F32)],
        compiler_params=pltpu.CompilerParams(dimension_semantics=("arbitrary", "arbitrary"),
                                             vmem_limit_bytes=VMEM_LIMIT),
        name="mix",
    )(x, a, u, hist16, mk, mv, w_pool, pool_scale.reshape(1, D_POOL), w_out, g_x.reshape(1, D), w_xq, w_xo,
      g_ffn.reshape(1, D), w_r, b_r)


R_OFF = N_GROUPS


def _route(logits):
    tm = logits.shape[0]
    lane = lax.broadcasted_iota(jnp.int32, (tm, LANES), 1).astype(F32)
    ninf = -jnp.inf
    big = float(LANES)
    gl = jnp.where(lane < N_GROUPS, logits, ninf)
    gmax = jnp.max(gl, axis=1, keepdims=True)
    g_sel = jnp.min(jnp.where(gl == gmax, lane, big), axis=1, keepdims=True)
    g_prob = 1.0 / jnp.sum(jnp.exp(gl - gmax), axis=1, keepdims=True)
    lo = R_OFF + g_sel * EXPERTS_PER_GROUP
    el = jnp.where((lane >= lo) & (lane < lo + EXPERTS_PER_GROUP), logits, ninf)
    tv0 = jnp.max(el, axis=1, keepdims=True)
    ti0 = jnp.min(jnp.where(el == tv0, lane, big), axis=1, keepdims=True)
    el2 = jnp.where(lane == ti0, ninf, el)
    tv1 = jnp.max(el2, axis=1, keepdims=True)
    ti1 = jnp.min(jnp.where(el2 == tv1, lane, big), axis=1, keepdims=True)
    e1 = jnp.exp(tv1 - tv0)
    den = 1.0 + e1
    w0 = g_prob / den
    w1 = g_prob * e1 / den
    ids = jnp.where(lane == 0.0, jnp.minimum(ti0, ti1) - R_OFF,
                    jnp.where(lane == 1.0, jnp.maximum(ti0, ti1) - R_OFF, 0.0))
    return jnp.where(lane == ti0, w0, 0.0) + jnp.where(lane == ti1, w1, 0.0) + ids


def _expert(h, rec, e, wg, wu, wd):
    a = _dot(h, wg)
    b = _dot(h, wu)
    lane = lax.broadcasted_iota(jnp.int32, rec.shape, 1)
    cw = jnp.sum(jnp.where(lane == e + R_OFF, rec, 0.0), axis=1, keepdims=True)
    hid = a * jax.nn.sigmoid(a) * b * cw
    return _dot(hid.astype(BF16), wd)


def _moe_kernel(x_ref, gffn_ref, wg_ref, wu_ref, wd_ref, gfin_ref, o_ref, h_s, acc_s):
    e = pl.program_id(1)
    d = o_ref.shape[1]

    @pl.when(e == 0)
    def _():
        h_s[...] = _rms(x_ref[:, 0:d], gffn_ref[...]).astype(BF16)
        acc_s[...] = jnp.zeros(acc_s.shape, F32)

    acc_s[...] += _expert(h_s[...], x_ref[:, d:d + LANES], e, wg_ref[0], wu_ref[0], wd_ref[0])

    @pl.when(e == pl.num_programs(1) - 1)
    def _():
        o_ref[...] = _rms(x_ref[:, 0:d] + acc_s[...], gfin_ref[...])


def _moe(x, g_ffn, w_gate, w_up, w_down, g_final, *, tm):
    n, de = x.shape
    d = de - LANES
    assert n % tm == 0
    return pl.pallas_call(
        _moe_kernel,
        grid=(n // tm, N_EXPERTS),
        in_specs=[pl.BlockSpec((tm, de), lambda i, e: (i, 0)),
                  _const_spec((1, d)),
                  pl.BlockSpec((1, d, D_EXPERT), lambda i, e: (e, 0, 0)),
                  pl.BlockSpec((1, d, D_EXPERT), lambda i, e: (e, 0, 0)),
                  pl.BlockSpec((1, D_EXPERT, d), lambda i, e: (e, 0, 0)),
                  _const_spec((1, d))],
        out_specs=pl.BlockSpec((tm, d), lambda i, e: (i, 0)),
        out_shape=jax.ShapeDtypeStruct((n, d), F32),
        scratch_shapes=[pltpu.VMEM((tm, d), BF16),
                        pltpu.VMEM((tm, d), F32)],
        compiler_params=pltpu.CompilerParams(dimension_semantics=("arbitrary", "arbitrary"),
                                             vmem_limit_bytes=VMEM_LIMIT),
        name="moe",
    )(x, g_ffn.reshape(1, d), w_gate, w_up, w_down, g_final.reshape(1, d))


MOE_TILES_PER_STEP = 2


def _moe_sparse_kernel(ea_ref, eb_ref, nv_ref, idx_ref, idxn_ref, x_hbm, gffn_ref, gfin_ref, *rest, tm, tps):
    w_refs = [rest[6 * k:6 * k + 6] for k in range(tps)]
    o_hbm, xbuf, ybuf, gsem, ssem = rest[6 * tps:]
    t = pl.program_id(0)
    nt = pl.num_programs(0)
    d = o_hbm.shape[1]
    slot = lax.rem(t, 2)
    nslot = 1 - slot
    rows = tps * tm
    WAIT_ROWS = 8

    def step_rows(step):
        nvs = [nv_ref[tps * step + k] for k in range(tps)]
        return nvs, functools.reduce(lambda a, b: a + b, nvs)

    nvs, nv = step_rows(t)

    def gather_row(ids, s, r):
        return pltpu.make_async_copy(x_hbm.at[pl.ds(ids[0, 0, r], 1), :], xbuf.at[s, pl.ds(r, 1), :], gsem.at[s])

    def gather_loop(ids, s):
        def body(r, carry):
            gather_row(ids, s, r).start()
            return carry
        lax.fori_loop(0, rows, body, 0)

    def gather_wait(s):
        pltpu.make_async_copy(x_hbm.at[pl.ds(0, rows), :], xbuf.at[s], gsem.at[s]).wait()

    def scatter_rows(s, r, row, n):
        return pltpu.make_async_copy(ybuf.at[s, pl.ds(r, n), :], o_hbm.at[pl.ds(row, n), :], ssem.at[s])

    def scatter_wait(s, n):
        def body_many(r, carry):
            scatter_rows(s, 0, 0, WAIT_ROWS).wait()
            return carry
        lax.fori_loop(0, n // WAIT_ROWS, body_many, 0)

        def body_one(r, carry):
            scatter_rows(s, 0, 0, 1).wait()
            return carry
        lax.fori_loop(0, lax.rem(n, WAIT_ROWS), body_one, 0)

    @pl.when(t == 0)
    def _():
        gather_loop(idx_ref, 0)

    @pl.when(t >= 2)
    def _():
        scatter_wait(slot, step_rows(jnp.maximum(t - 2, 0))[1])

    @pl.when((t == 0) | (step_rows(jnp.maximum(t - 1, 0))[1] > 0))
    def _():
        gather_wait(slot)

    @pl.when(nv > 0)
    def _():
        tiles = []
        for k in range(tps):
            x = xbuf[slot, k * tm:(k + 1) * tm, :]
            xr = x[:, 0:d]
            tiles.append((xr, x[:, d:d + LANES], _rms(xr, gffn_ref[...]).astype(BF16)))
        for r in range(rows):
            gather_row(idxn_ref, nslot, r).start()
        pre = [[(_dot(h, wga[0]), _dot(h, wua[0])), (_dot(h, wgb[0]), _dot(h, wub[0]))]
               for (_, _, h), (wga, wua, _, wgb, wub, _) in zip(tiles, w_refs)]
        for k in range(tps):
            xr, rec, _ = tiles[k]
            lane = lax.broadcasted_iota(jnp.int32, rec.shape, 1)
            experts = (ea_ref[tps * t + k], eb_ref[tps * t + k])
            acc = None
            for (a, b), e, wd in zip(pre[k], experts, (w_refs[k][2], w_refs[k][5])):
                cw = jnp.sum(jnp.where(lane == e + R_OFF, rec, 0.0), axis=1, keepdims=True)
                hid = a * jax.nn.sigmoid(a) * b * cw
                out = _dot(hid.astype(BF16), wd[0])
                acc = out if acc is None else acc + out
            ybuf[slot, k * tm:(k + 1) * tm, :] = _rms(xr + acc, gfin_ref[...])

        for k in range(tps):
            def body_many(i, carry, k=k):
                for u in range(WAIT_ROWS):
                    r = k * tm + i * WAIT_ROWS + u
                    scatter_rows(slot, r, idx_ref[0, 0, r], 1).start()
                return carry
            lax.fori_loop(0, nvs[k] // WAIT_ROWS, body_many, 0)

            def body_one(i, carry, k=k):
                r = k * tm + i
                scatter_rows(slot, r, idx_ref[0, 0, r], 1).start()
                return carry
            lax.fori_loop((nvs[k] // WAIT_ROWS) * WAIT_ROWS, nvs[k], body_one, 0)

    @pl.when(t == nt - 1)
    def _():
        @pl.when(nv > 0)
        def _():
            gather_wait(nslot)

        @pl.when(t >= 1)
        def _():
            scatter_wait(nslot, step_rows(jnp.maximum(t - 1, 0))[1])
        scatter_wait(slot, nv)


def _moe_tile_rows(n):
    n_pairs = N_GROUPS * (EXPERTS_PER_GROUP * (EXPERTS_PER_GROUP - 1) // 2)
    mean = n / n_pairs
    rows = (mean + 3.0 * math.sqrt(mean)) / 2
    return int(min(256, max(SUB_BF16, -(-rows // SUB_BF16) * SUB_BF16)))


def _moe_sparse(x, g_ffn, w_gate, w_up, w_down, g_final, *, tm):
    n, de = x.shape
    d = de - LANES
    n_pairs = N_GROUPS * (EXPERTS_PER_GROUP * (EXPERTS_PER_GROUP - 1) // 2)
    tps = MOE_TILES_PER_STEP
    nt = -(-n // tm) + n_pairs
    nt = -(-nt // tps) * tps
    n_steps = nt // tps
    n_keys = N_EXPERTS * N_EXPERTS

    ids = x[:, d:d + 2].astype(jnp.int32)
    key = ids[:, 0] * N_EXPERTS + ids[:, 1]
    order = jnp.argsort(key).astype(jnp.int32)
    skey = key[order]
    count_below = lambda sorted_vals, q: jnp.sum((sorted_vals[None, :] < q[:, None]).astype(jnp.int32), axis=1)
    edges = count_below(skey, jnp.arange(n_keys + 1, dtype=jnp.int32))
    starts, ends = edges[:-1], edges[1:]
    tiles = (ends - starts + tm - 1) // tm
    cum = jnp.cumsum(tiles)
    total = cum[-1]
    tt = jnp.arange(nt, dtype=jnp.int32)
    tc = jnp.minimum(tt, total - 1)
    cls = count_below(cum, tc + 1)
    first = cum[cls] - tiles[cls]
    tstart = starts[cls] + (tc - first) * tm
    nv = jnp.where(tt < total, jnp.clip(ends[cls] - tstart, 0, tm), 0).astype(jnp.int32)
    ea = (cls // N_EXPERTS).astype(jnp.int32)
    eb = (cls % N_EXPERTS).astype(jnp.int32)
    rows = jnp.clip(tstart[:, None] + jnp.arange(tm, dtype=jnp.int32)[None, :], 0, n - 1)
    idx = order[rows].reshape(n_steps, 1, tps * tm)

    def wspec(shape, experts, k):
        return pl.BlockSpec(shape, lambda t, ea, eb, nv: ((ea, eb)[experts][tps * t + k], 0, 0))

    cspec = lambda shape: pl.BlockSpec(shape, lambda t, ea, eb, nv: (0,) * len(shape),
                                       pipeline_mode=pl.Buffered(1))
    w_specs, w_args = [], []
    for k in range(tps):
        for experts in (0, 1):
            w_specs += [wspec((1, d, D_EXPERT), experts, k), wspec((1, d, D_EXPERT), experts, k),
                        wspec((1, D_EXPERT, d), experts, k)]
            w_args += [w_gate, w_up, w_down]
    grid_spec = pltpu.PrefetchScalarGridSpec(
        num_scalar_prefetch=3,
        grid=(n_steps,),
        in_specs=[pl.BlockSpec((1, 1, tps * tm), lambda t, ea, eb, nv: (t, 0, 0), memory_space=pltpu.SMEM),
                  pl.BlockSpec((1, 1, tps * tm), lambda t, ea, eb, nv: (jnp.minimum(t + 1, n_steps - 1), 0, 0),
                               memory_space=pltpu.SMEM),
                  pl.BlockSpec(memory_space=pl.ANY),
                  cspec((1, d)), cspec((1, d))] + w_specs,
        out_specs=pl.BlockSpec(memory_space=pl.ANY),
        scratch_shapes=[pltpu.VMEM((2, tps * tm, de), F32),
                        pltpu.VMEM((2, tps * tm, d), F32),
                        pltpu.SemaphoreType.DMA((2,)),
                        pltpu.SemaphoreType.DMA((2,))])
    return pl.pallas_call(
        functools.partial(_moe_sparse_kernel, tm=tm, tps=tps),
        grid_spec=grid_spec,
        out_shape=jax.ShapeDtypeStruct((n, d), F32),
        compiler_params=pltpu.CompilerParams(dimension_semantics=("arbitrary",),
                                             vmem_limit_bytes=VMEM_LIMIT),
        name="moe_sparse",
    )(ea, eb, nv, idx, idx, x, g_ffn.reshape(1, d), g_final.reshape(1, d), *w_args)


C_Q = 0
C_QI = C_Q + D_ATTN
C_K = C_QI + N_IDX_HEADS * IDX_DIM
C_V = C_K + D_KV
C_KIWI = C_V + D_KV
C_KIE = C_KIWI + LANES
C_KIO = C_KIE + LANES
C_U = C_KIO + LANES
C_END = C_U + D_POOL


def _layout_w_in(w_in):
    d = w_in.shape[0]
    offs = np.cumsum((0, D_ATTN, D_KV, D_KV, N_IDX_HEADS * IDX_DIM, IDX_DIM, N_IDX_HEADS, D_POOL))
    wq, wk, wv, wqi, wki, wwi, wu = (w_in[:, offs[n]:offs[n + 1]] for n in range(7))
    z = lambda n: jnp.zeros((d, n), w_in.dtype)
    cat = jnp.concatenate([wq, wqi, wk, wv,
                           wki, wwi, z(LANES - IDX_DIM - N_IDX_HEADS),
                           wki, z(LANES - IDX_DIM),
                           z(LANES - IDX_DIM), wki,
                           wu], axis=1)
    assert cat.shape[1] == C_END
    return cat.astype(BF16)


IN_GROUPS = ((C_Q, C_QI, 1),
             (C_QI, C_K, 1),
             (C_K, C_V, 2),
             (C_V, C_KIWI, 2),
             (C_KIWI, C_KIE, 1),
             (C_KIE, C_KIO, 1),
             (C_KIO, C_U, 1),
             (C_U, C_END, 1))
IN_DTYPES = (BF16, BF16, F32, BF16, F32, BF16, F32, BF16, BF16, F32)


def _pad_rows(a, n):
    return jnp.pad(a, ((0, 0), (0, n - a.shape[1]), (0, 0)))


def _layer(x, pos0, caches, pool_hist, mk, mv, rel_bias, W, *, tm_in, tq, tm_mix, tm_moe, sparse_moe):
    B, T, D = x.shape
    n = B * T
    q, qi, k, kb, v, vb, kiwi, kie, kio, u = _norm_matmul(
        x.reshape(n, D), W["g_mix"], W["w_in"], IN_GROUPS, IN_DTYPES, tm_in)
    r3 = lambda a: a.reshape(B, T, a.shape[-1])
    kblk = KEY_BLOCK
    if caches is None:
        n_keys = T
        kb3, vb3, kie3, kio3 = r3(kb), r3(vb), r3(kie), r3(kio)
    else:
        k_hist, v_hist, ki_hist = caches
        past = k_hist.shape[1]
        n_keys = past + T
        lp = -(-n_keys // kblk) * kblk
        ki = r3(kiwi)[:, :, :IDX_DIM]
        ki_all = jnp.concatenate([ki_hist, ki], axis=1)
        zeros = jnp.zeros_like(ki_all)
        kb3 = _pad_rows(jnp.concatenate([k_hist.reshape(B, past, D_KV), r3(k)], axis=1), lp).astype(BF16)
        vb3 = _pad_rows(jnp.concatenate([v_hist.reshape(B, past, D_KV), r3(v)], axis=1), lp).astype(BF16)
        kie3 = _pad_rows(jnp.concatenate([ki_all, zeros], axis=2), lp).astype(BF16)
        kio3 = _pad_rows(jnp.concatenate([zeros, ki_all], axis=2), lp).astype(BF16)
    a_out = _dsa(r3(q), r3(qi), r3(kiwi), kb3, vb3, kie3, kio3, rel_bias,
                 n_keys=n_keys, pos0=pos0, tq=tq, kblk=kblk)
    x2 = _mix(x, a_out, r3(u), pool_hist, mk.astype(BF16), mv.astype(BF16), W["w_pool"], W["pool_scale"],
              W["w_out"], W["g_x"], W["w_xq"], W["w_xo"], W["g_ffn"], W["w_r"], W["b_r"], pos0=pos0, tm=tm_mix)
    moe = _moe_sparse if sparse_moe else _moe
    y = moe(x2.reshape(n, D + LANES), W["g_ffn"], W["w_gate"], W["w_up"], W["w_down"], W["g_final"], tm=tm_moe)
    new_pool = jnp.concatenate([pool_hist, r3(u)], axis=1)[:, -POOL_HIST:]
    return (y.reshape(B, T, D), k.reshape(B, T, N_KV_A, HEAD_DIM_A), v.reshape(B, T, N_KV_A, HEAD_DIM_A),
            r3(kiwi)[:, :, :IDX_DIM], new_pool)


def kernel(x_prompt, x_sample, mem_prompt, cache_k, cache_v, cache_k_idx, cache_pool, cache_mem_k, cache_mem_v, rel_bias, g_mix, w_in, w_pool, pool_scale, w_out, g_mem, w_mk, w_mv, g_x, w_xq, w_xo, g_ffn, w_rg, b_rg, w_re, b_re, w_gate, w_up, w_down, g_final):
    depth = g_mix.shape[0]
    assert depth == 1
    l = 0
    B, T, D = x_prompt.shape
    Bs, Ts, _ = x_sample.shape
    n_mem = mem_prompt.shape[1]
    past = cache_k.shape[2]

    w_r = jnp.concatenate([w_rg[l], w_re[l], jnp.zeros((D, LANES - N_GROUPS - N_EXPERTS), F32)], axis=1)
    W = dict(
        g_mix=g_mix[l], w_in=_layout_w_in(w_in[l]), w_pool=w_pool[l].astype(BF16), pool_scale=pool_scale[l],
        w_out=w_out[l].astype(BF16), g_x=g_x[l], w_xq=w_xq[l].astype(BF16), w_xo=w_xo[l].astype(BF16),
        g_ffn=g_ffn[l], w_r=w_r.astype(BF16),
        b_r=jnp.concatenate([b_rg[l], b_re[l], jnp.zeros((LANES - N_GROUPS - N_EXPERTS,), F32)]).reshape(1, LANES),
        w_gate=w_gate[l].astype(BF16), w_up=w_up[l].astype(BF16), w_down=w_down[l].astype(BF16),
        g_final=g_final)

    w_m = jnp.concatenate([w_mk[l], w_mv[l]], axis=1).astype(BF16)
    mk, mv = _norm_matmul(mem_prompt.reshape(B * n_mem, D), g_mem[l], w_m,
                          ((0, D_X, 1), (D_X, 2 * D_X, 1)), (F32, F32), TOKEN_TILE)
    mk = mk.reshape(B, n_mem, D_X)
    mv = mv.reshape(B, n_mem, D_X)

    yp, kp, vp, kip, pp = _layer(x_prompt, 0, None, jnp.zeros((B, POOL_HIST, D_POOL), F32), mk, mv, rel_bias, W,
                                 tm_in=TOKEN_TILE, tq=QUERY_TILE, tm_mix=TOKEN_TILE, tm_moe=_moe_tile_rows(B * T),
                                 sparse_moe=True)
    ys, ks, vs, kis, ps = _layer(x_sample, past,
                                 (cache_k[l], cache_v[l], cache_k_idx[l]), cache_pool[l],
                                 cache_mem_k[l].reshape(Bs, n_mem, D_X), cache_mem_v[l].reshape(Bs, n_mem, D_X),
                                 rel_bias, W, tm_in=Bs * Ts, tq=Ts, tm_mix=Ts, tm_moe=Bs * Ts, sparse_moe=False)
    st = lambda a: a[None]
    return (yp, ys, st(kp), st(vp), st(kip), st(pp),
            st(mk.reshape(B, n_mem, N_XHEADS, XHEAD_DIM)), st(mv.reshape(B, n_mem, N_XHEADS, XHEAD_DIM)),
            st(ks), st(vs), st(kis), st(ps))
```

```python
import functools
import math

import numpy as np
import jax
import jax.numpy as jnp
from jax import lax
from jax.experimental import pallas as pl
from jax.experimental.pallas import tpu as pltpu

F32 = jnp.float32
BF16 = jnp.bfloat16

CHUNK = 64
N_HEADS_A = 8
HEAD_DIM_A = 128
N_KV_A = 2
D_ATTN = N_HEADS_A * HEAD_DIM_A
D_KV = N_KV_A * HEAD_DIM_A
N_IDX_HEADS = 8
IDX_DIM = 64
TOPK_MAX = 256
POOL_WINDOWS = (2, 4, 8, 16)
POOL_GROUP_DIM = 256
D_POOL = len(POOL_WINDOWS) * POOL_GROUP_DIM
POOL_HIST = max(POOL_WINDOWS) - 1
N_BUCKETS = 32
MAX_DISTANCE = 128
N_XHEADS = 4
XHEAD_DIM = 128
D_X = N_XHEADS * XHEAD_DIM
N_GROUPS = 4
EXPERTS_PER_GROUP = 8
N_EXPERTS = N_GROUPS * EXPERTS_PER_GROUP
D_EXPERT = 256
EPS = 1e-6

LANES = 128
INT_MIN = -(2 ** 31)
NEG = -1e30
LOG2E = math.log2(math.e)
VMEM_LIMIT = 48 * 1024 * 1024
TOKEN_TILE = 512
KEY_BLOCK = 256
QUERY_TILE = KEY_BLOCK


def _rms(x, g):
    ms = jnp.mean(x * x, axis=-1, keepdims=True)
    return x * lax.rsqrt(ms + EPS) * g


def _dot(a, b):
    return jnp.dot(a, b, preferred_element_type=F32)


def _dot_nt(a, b):
    return lax.dot_general(a, b, (((1,), (1,)), ((), ())), preferred_element_type=F32)


def _const_spec(shape):
    nd = len(shape)
    return pl.BlockSpec(shape, lambda *_: (0,) * nd, pipeline_mode=pl.Buffered(1))


def _norm_matmul_kernel(x_ref, g_ref, w_ref, *out_refs, groups):
    h = _rms(x_ref[...], g_ref[...]).astype(BF16)
    k = 0
    for c0, c1, n_out in groups:
        acc = _dot(h, w_ref[:, c0:c1])
        for _ in range(n_out):
            out_refs[k][...] = acc.astype(out_refs[k].dtype)
            k += 1


def _norm_matmul(x, g, w, groups, out_dtypes, tm):
    n, d = x.shape
    assert n % tm == 0
    widths = []
    for c0, c1, n_out in groups:
        widths += [c1 - c0] * n_out
    return pl.pallas_call(
        functools.partial(_norm_matmul_kernel, groups=tuple(groups)),
        grid=(n // tm,),
        in_specs=[pl.BlockSpec((tm, d), lambda i: (i, 0)),
                  _const_spec((1, d)),
                  _const_spec(w.shape)],
        out_specs=[pl.BlockSpec((tm, wd), lambda i: (i, 0)) for wd in widths],
        out_shape=[jax.ShapeDtypeStruct((n, wd), dt) for wd, dt in zip(widths, out_dtypes)],
        compiler_params=pltpu.CompilerParams(dimension_semantics=("arbitrary",),
                                             vmem_limit_bytes=VMEM_LIMIT),
        name="norm_matmul",
    )(x, g.reshape(1, d), w)


def _bias_table_kernel(relb_ref, bucket_ref, o_ref, *, shift_bucket):
    bucket = bucket_ref[...]
    for h in range(N_HEADS_A):
        acc = jnp.zeros(bucket.shape, F32)
        for b in range(N_BUCKETS):
            acc = jnp.where(bucket == b, relb_ref[b, h], acc)
        if shift_bucket is None:
            o_ref[h] = acc * LOG2E
        else:
            o_ref[h] = (acc - relb_ref[shift_bucket, h]) * LOG2E


def _bias_table(rel_bias, bucket, shift_bucket=None):
    tq, w = bucket.shape
    return pl.pallas_call(
        functools.partial(_bias_table_kernel, shift_bucket=shift_bucket),
        in_specs=[pl.BlockSpec(memory_space=pltpu.SMEM),
                  pl.BlockSpec((tq, w), lambda: (0, 0))],
        out_specs=pl.BlockSpec((N_HEADS_A, tq, w), lambda: (0, 0, 0)),
        out_shape=jax.ShapeDtypeStruct((N_HEADS_A, tq, w), F32),
        name="bias_table",
    )(rel_bias, bucket)


def _t5_bucket(rel):
    nb = N_BUCKETS // 2
    max_exact = nb // 2
    bucket = (rel > 0).astype(jnp.int32) * nb
    n = jnp.abs(rel)
    nf = jnp.maximum(n, 1).astype(F32)
    large = max_exact + (jnp.log(nf / max_exact) / math.log(MAX_DISTANCE / max_exact)
                         * (nb - max_exact)).astype(jnp.int32)
    large = jnp.minimum(large, nb - 1)
    return bucket + jnp.where(n < max_exact, n, large)


def _far_bucket(min_dist, max_dist):
    nb = N_BUCKETS // 2
    max_exact = nb // 2
    n = np.arange(min_dist, max_dist + 1, dtype=np.float64)
    large = max_exact + np.floor(np.log(n / max_exact) / math.log(MAX_DISTANCE / max_exact)
                                 * (nb - max_exact) * (1 - 1e-6)).astype(np.int64)
    assert large.min() >= nb - 1, "far key blocks must sit in the saturated distance bucket"
    return nb - 1


def _dsa_kernel(relb_ref, q_ref, qi_ref, kiwi_ref, k_ref, v_ref, kie_ref, kio_ref, bnear_ref, vis_ref,
                o_ref, keys, madd, wb, qs, xcut, mrun, lrun, acc_s,
                *, TQ, KB, RB, kb0, kstep, topk, far_bucket, tie_bits):
    i = pl.program_id(1)
    nkb = kb0 + kstep * i
    n_far = jnp.maximum(nkb - 2, 0)
    n_lc = KB // LANES

    kiwi = kiwi_ref[0]
    for h in range(N_IDX_HEADS):
        w = (kiwi[:, IDX_DIM + h:IDX_DIM + h + 1] * (N_IDX_HEADS ** -0.5)).astype(BF16).astype(F32)
        wb[h] = jnp.broadcast_to(w * (IDX_DIM ** -0.5), (TQ, LANES))

    def idx_block(j, near_col):
        c0 = pl.multiple_of(j * KB, KB)
        ke = kie_ref[0, pl.ds(c0, KB), :]
        ko = kio_ref[0, pl.ds(c0, KB), :]
        acc = [jnp.zeros((TQ, LANES), F32) for _ in range(n_lc)]
        for hp in range(N_IDX_HEADS // 2):
            qh = qi_ref[0, :, hp * LANES:(hp + 1) * LANES]
            se = _dot_nt(qh, ke)
            so = _dot_nt(qh, ko)
            we = wb[2 * hp]
            wo = wb[2 * hp + 1]
            for c in range(n_lc):
                cs = slice(c * LANES, (c + 1) * LANES)
                re = jnp.maximum(se[:, cs], 0.0).astype(BF16).astype(F32)
                ro = jnp.maximum(so[:, cs], 0.0).astype(BF16).astype(F32)
                acc[c] = acc[c] + we * re + wo * ro
        for c in range(n_lc):
            bits = lax.bitcast_convert_type(acc[c], jnp.int32)
            key = bits ^ ((bits >> 31) & jnp.int32(0x7FFFFFFF))
            if near_col is not None:
                vis = vis_ref[:, near_col + c * LANES:near_col + (c + 1) * LANES]
                key = jnp.where(vis > 0.0, key, jnp.int32(INT_MIN))
            keys[j, :, c * LANES:(c + 1) * LANES] = key

    def far_idx(j, carry):
        idx_block(j, None)
        return carry

    lax.fori_loop(0, n_far, far_idx, 0)

    @pl.when(nkb >= 2)
    def _():
        idx_block(nkb - 2, 0)

    idx_block(nkb - 1, KB)

    kf = float(topk)
    lane_f = lax.broadcasted_iota(jnp.int32, (RB, LANES), 1).astype(F32)
    NRB = TQ // RB
    rbs = [slice(rb * RB, (rb + 1) * RB) for rb in range(NRB)]

    def count(pred):
        def body(jj, accs):
            out = []
            for rb in range(NRB):
                blk = keys[jj, rbs[rb], :]
                acc = accs[rb]
                for c in range(n_lc):
                    acc = acc + jnp.where(pred(rb, blk[:, c * LANES:(c + 1) * LANES], jj, c), 1.0, 0.0)
                out.append(acc)
            return tuple(out)
        accs = lax.fori_loop(0, nkb, body, tuple(jnp.zeros((RB, LANES), F32) for _ in range(NRB)))
        return [jnp.sum(acc, axis=1, keepdims=True) for acc in accs]

    def count_ge(cands):
        cb = [jnp.broadcast_to(cand, (RB, LANES)) for cand in cands]
        return count(lambda rb, kblk, jj, c: kblk >= cb[rb])

    def bis_body(p, thrs):
        cands = [thr + lax.shift_left(jnp.int32(1), 31 - p) for thr in thrs]
        cnts = count_ge(cands)
        return tuple(jnp.where(cnt >= kf, cand, thr) for cnt, cand, thr in zip(cnts, cands, thrs))

    thrs = lax.fori_loop(0, 32, bis_body, tuple(jnp.full((RB, 1), INT_MIN, jnp.int32) for _ in range(NRB)))
    thrs = [jnp.maximum(thr, jnp.int32(INT_MIN + 1)) for thr in thrs]
    cnt_ge = count_ge(thrs)
    cnt_gt = count_ge([thr + 1 for thr in thrs])
    needs = [kf - c for c in cnt_gt]
    thrb = [jnp.broadcast_to(thr, (RB, LANES)) for thr in thrs]

    xcut[...] = jnp.full(xcut.shape, 2.0 ** 24, F32)
    excess = cnt_ge[0] - kf
    for c in cnt_ge[1:]:
        excess = jnp.maximum(excess, c - kf)

    @pl.when(jnp.max(excess) > 0.0)
    def _():
        def count_eq_lt(xs):
            xb = [jnp.broadcast_to(x, (RB, LANES)) for x in xs]

            def pred(rb, kblk, jj, c):
                col = lane_f + (jj * KB + c * LANES).astype(F32)
                return jnp.where(kblk == thrb[rb], col, 2.0 ** 25) < xb[rb]
            return count(pred)

        def tie_body(p, xs):
            cands = [x + lax.shift_left(jnp.int32(1), tie_bits - 1 - p).astype(F32) for x in xs]
            cnts = count_eq_lt(cands)
            return tuple(jnp.where(cnt < need, cand, x) for cnt, need, cand, x in zip(cnts, needs, cands, xs))

        xs = lax.fori_loop(0, tie_bits, tie_body, tuple(jnp.zeros((RB, 1), F32) for _ in range(NRB)))
        for rb in range(NRB):
            xcut[rbs[rb], :] = xs[rb]

    xb = [jnp.broadcast_to(xcut[rbs[rb], :], (RB, LANES)) for rb in range(NRB)]

    def mask_body(jj, carry):
        for rb in range(NRB):
            blk = keys[jj, rbs[rb], :]
            for c in range(n_lc):
                kblk = blk[:, c * LANES:(c + 1) * LANES]
                col = lane_f + (jj * KB + c * LANES).astype(F32)
                keep = (kblk > thrb[rb]) | ((kblk == thrb[rb]) & (col <= xb[rb]))
                madd[jj, rbs[rb], c * LANES:(c + 1) * LANES] = jnp.where(keep, 0.0, NEG)
        return carry

    lax.fori_loop(0, nkb, mask_body, 0)

    _attend(relb_ref, q_ref, k_ref, v_ref, bnear_ref, o_ref, madd, qs, mrun, lrun, acc_s,
            TQ=TQ, KB=KB, nkb=nkb, n_far=n_far, far_bucket=far_bucket)


def _attend(relb_ref, q_ref, k_ref, v_ref, bnear_ref, o_ref, madd, qs, mrun, lrun, acc_s,
            *, TQ, KB, nkb, n_far, far_bucket):
    n_lc = KB // LANES
    HPG = N_HEADS_A // N_KV_A
    for g in range(N_KV_A):
        for r in range(HPG):
            h = g * HPG + r
            qs[g, r * TQ:(r + 1) * TQ, :] = q_ref[0, :, h * HEAD_DIM_A:(h + 1) * HEAD_DIM_A]

    def logits(j, near_col, g):
        c0 = pl.multiple_of(j * KB, KB)
        kg = k_ref[0, pl.ds(c0, KB), g * HEAD_DIM_A:(g + 1) * HEAD_DIM_A]
        s = _dot_nt(qs[g], kg)
        mk = madd[j]
        out = []
        for r in range(HPG):
            h = g * HPG + r
            if near_col is None:
                bias = relb_ref[far_bucket, h] * LOG2E
            else:
                bias = bnear_ref[h, :, near_col:near_col + KB]
            out.append(s[r * TQ:(r + 1) * TQ] + bias + mk)
        return out

    def over_blocks(fn):
        def far(j, carry):
            fn(j, None)
            return carry

        lax.fori_loop(0, n_far, far, 0)

        @pl.when(nkb >= 2)
        def _():
            fn(nkb - 2, 0)

        fn(nkb - 1, KB)

    mrun[...] = jnp.full(mrun.shape, NEG, F32)

    def max_block(j, near_col):
        for g in range(N_KV_A):
            for r, sh in enumerate(logits(j, near_col, g)):
                rs = slice(r * TQ, (r + 1) * TQ)
                m = mrun[g, rs, :]
                for c in range(n_lc):
                    m = jnp.maximum(m, sh[:, c * LANES:(c + 1) * LANES])
                mrun[g, rs, :] = m

    over_blocks(max_block)

    for g in range(N_KV_A):
        mrun[g] = jnp.broadcast_to(jnp.max(mrun[g], axis=1, keepdims=True), mrun.shape[1:])
    lrun[...] = jnp.zeros(lrun.shape, F32)
    acc_s[...] = jnp.zeros(acc_s.shape, F32)

    def pv_block(j, near_col):
        c0 = pl.multiple_of(j * KB, KB)
        for g in range(N_KV_A):
            vg = v_ref[0, pl.ds(c0, KB), g * HEAD_DIM_A:(g + 1) * HEAD_DIM_A]
            parts = []
            for r, sh in enumerate(logits(j, near_col, g)):
                rs = slice(r * TQ, (r + 1) * TQ)
                mb = mrun[g, rs, :]
                l = lrun[g, rs, :]
                pcs = []
                for c in range(n_lc):
                    p = jnp.exp2(sh[:, c * LANES:(c + 1) * LANES] - mb)
                    l = l + p
                    pcs.append(p.astype(BF16))
                lrun[g, rs, :] = l
                parts.append(jnp.concatenate(pcs, axis=1))
            acc_s[g] += _dot(jnp.concatenate(parts, axis=0), vg)

    over_blocks(pv_block)

    for g in range(N_KV_A):
        out = acc_s[g] / jnp.sum(lrun[g], axis=1, keepdims=True)
        for r in range(HPG):
            h = g * HPG + r
            o_ref[0, :, h * HEAD_DIM_A:(h + 1) * HEAD_DIM_A] = out[r * TQ:(r + 1) * TQ].astype(o_ref.dtype)


DIGIT_BITS = 8
N_DIGITS = 32 // DIGIT_BITS
SUB_BF16 = 16
COUNT_BLOCKS = 4
FAR_BLOCKS = 3


def _dsa_kernel_t(q_ref, qi_ref, kiwi_ref, k_ref, vt_ref, kie_ref, kio_ref, bnear_ref, vist_ref,
                  o_ref, keys, dig, madd, qs, xcut, acc_s, tbuf,
                  *, TQ, KB, kb0, kstep, topk, tie_bits):
    i = pl.program_id(1)
    nkb = kb0 + kstep * i
    n_far = jnp.maximum(nkb - 2, 0)
    n_slab = KB // SUB_BF16

    w_t = kiwi_ref[0].T
    wrow = [(w_t[IDX_DIM + h:IDX_DIM + h + 1, :] * (N_IDX_HEADS ** -0.5)).astype(BF16).astype(F32)
            * (IDX_DIM ** -0.5) for h in range(N_IDX_HEADS)]

    def idx_dots(j):
        c0 = pl.multiple_of(j * KB, KB)
        ke = kie_ref[0, pl.ds(c0, KB), :]
        ko = kio_ref[0, pl.ds(c0, KB), :]
        raw = []
        for hp in range(N_IDX_HEADS // 2):
            qh = qi_ref[0, :, hp * LANES:(hp + 1) * LANES]
            raw.append((_dot_nt(ke, qh), _dot_nt(ko, qh)))
        return raw

    def idx_finish(j, raw, near_row):
        acc = jnp.zeros((KB, TQ), F32)
        for hp, (se, so) in enumerate(raw):
            re = jnp.maximum(se.astype(BF16), 0.0).astype(F32)
            ro = jnp.maximum(so.astype(BF16), 0.0).astype(F32)
            acc = acc + wrow[2 * hp] * re + wrow[2 * hp + 1] * ro
        bits = lax.bitcast_convert_type(acc, jnp.int32)
        key = bits ^ ((bits >> 31) & jnp.int32(0x7FFFFFFF))
        if near_row is not None:
            key = jnp.where(vist_ref[near_row:near_row + KB, :] > 0.0, key, jnp.int32(INT_MIN))
        keys[j] = key
        for dk in range(N_DIGITS):
            sh = 32 - DIGIT_BITS * (dk + 1)
            d = (key >> sh) if dk == 0 else ((key >> sh) & jnp.int32(2 ** DIGIT_BITS - 1))
            dig[dk, j] = d.astype(F32).astype(BF16)

    def idx_blocks(js, near_rows):
        raws = [idx_dots(j) for j in js]
        for j, raw, near_row in zip(js, raws, near_rows):
            idx_finish(j, raw, near_row)

    n_trips = n_far // FAR_BLOCKS
    n_left = n_far - n_trips * FAR_BLOCKS

    def far_idx_trip(jt, carry):
        idx_blocks([FAR_BLOCKS * jt + u for u in range(FAR_BLOCKS)], [None] * FAR_BLOCKS)
        return carry

    lax.fori_loop(0, n_trips, far_idx_trip, 0)

    for left in range(1, FAR_BLOCKS):
        @pl.when(n_left == left)
        def _(left=left):
            idx_blocks([n_far - left + u for u in range(left)], [None] * left)

    @pl.when(nkb >= 2)
    def _():
        idx_blocks([nkb - 2, nkb - 1], [0, KB])

    @pl.when(nkb < 2)
    def _():
        idx_blocks([nkb - 1], [KB])

    one = jnp.ones((SUB_BF16, TQ), BF16)
    zero = jnp.zeros((SUB_BF16, TQ), BF16)

    def count_ge(dk, cand):
        cb = jnp.broadcast_to(cand.astype(BF16), (SUB_BF16, TQ))

        def block_count(jj):
            blk = dig[dk, jj]
            parts = [jnp.where(blk[s * SUB_BF16:(s + 1) * SUB_BF16, :] >= cb, one, zero) for s in range(n_slab)]
            while len(parts) > 1:
                parts = [parts[n] + parts[n + 1] for n in range(0, len(parts), 2)]
            return parts[0]

        def body_many(jq, tot):
            cnts = [block_count(COUNT_BLOCKS * jq + u) for u in range(COUNT_BLOCKS)]
            while len(cnts) > 1:
                cnts = [cnts[n] + cnts[n + 1] for n in range(0, len(cnts), 2)]
            return tot + cnts[0].astype(F32)

        def body_one(jj, tot):
            return tot + block_count(jj).astype(F32)

        tot = lax.fori_loop(0, nkb // COUNT_BLOCKS, body_many, jnp.zeros((SUB_BF16, TQ), F32))
        tot = lax.fori_loop((nkb // COUNT_BLOCKS) * COUNT_BLOCKS, nkb, body_one, tot)
        return jnp.sum(tot, axis=0, keepdims=True)

    kneed = jnp.full((1, TQ), float(topk), F32)
    thr = jnp.zeros((1, TQ), jnp.int32)
    cnt_eq = None
    for dk in range(N_DIGITS):
        lo = -float(2 ** (DIGIT_BITS - 1)) if dk == 0 else 0.0

        def bis_body(p, carry, dk=dk, kneed=kneed):
            t, c_rej = carry
            cand = t + lax.shift_left(jnp.int32(1), DIGIT_BITS - 1 - p).astype(F32)
            cnt = count_ge(dk, cand)
            ok = cnt >= kneed
            return jnp.where(ok, cand, t), jnp.where(ok, c_rej, cnt)

        t, n_gt = lax.fori_loop(0, DIGIT_BITS, bis_body,
                                (jnp.full((1, TQ), lo, F32), jnp.zeros((1, TQ), F32)))
        if dk == N_DIGITS - 1:
            cnt_eq = count_ge(dk, t) - n_gt
        kneed = kneed - n_gt
        thr = thr + lax.shift_left(t.astype(jnp.int32), 32 - DIGIT_BITS * (dk + 1))
        if dk + 1 < N_DIGITS:
            tb = t.astype(BF16)

            def prep(jj, carry, dk=dk, tb=tb):
                dig[dk + 1, jj] = jnp.where(dig[dk, jj] == tb, dig[dk + 1, jj], jnp.asarray(-1.0, BF16))
                return carry

            lax.fori_loop(0, nkb, prep, 0)

    thr = jnp.maximum(thr, jnp.int32(INT_MIN + 1))
    need = kneed
    sub_f = lax.broadcasted_iota(jnp.int32, (KB, TQ), 0).astype(F32)

    xcut[...] = jnp.full(xcut.shape, 2.0 ** 24, F32)

    @pl.when(jnp.max(cnt_eq - need) > 0.0)
    def _():
        def count_eq_lt(x):
            def body(jj, tot):
                col = sub_f + (jj * KB).astype(F32)
                hit = jnp.where(jnp.where(keys[jj] == thr, col, 2.0 ** 25) < x, 1.0, 0.0)
                for s in range(KB // 8):
                    tot = tot + hit[s * 8:(s + 1) * 8, :]
                return tot
            tot = lax.fori_loop(0, nkb, body, jnp.zeros((8, TQ), F32))
            return jnp.sum(tot, axis=0, keepdims=True)

        def tie_body(p, x):
            cand = x + lax.shift_left(jnp.int32(1), tie_bits - 1 - p).astype(F32)
            return jnp.where(count_eq_lt(cand) < need, cand, x)

        x = lax.fori_loop(0, tie_bits, tie_body, jnp.zeros((1, TQ), F32))
        xcut[...] = jnp.broadcast_to(x, xcut.shape)

    xc = xcut[0:1, :]

    def mask_body(jj, carry):
        kblk = keys[jj]
        col = sub_f + (jj * KB).astype(F32)
        at_thr = jnp.where(col <= xc, 0.0, NEG)
        madd[jj] = jnp.where(kblk > thr, 0.0, jnp.where(kblk == thr, at_thr, NEG))
        return carry

    lax.fori_loop(0, nkb, mask_body, 0)

    HPG = N_HEADS_A // N_KV_A
    for g in range(N_KV_A):
        for r in range(HPG):
            h = g * HPG + r
            qs[g, r * TQ:(r + 1) * TQ, :] = q_ref[0, :, h * HEAD_DIM_A:(h + 1) * HEAD_DIM_A]
    acc_s[...] = jnp.zeros(acc_s.shape, F32)

    def attn_blocks(js, near_rows, carry):
        ms, ls = carry
        s_all = []
        for j in js:
            c0 = pl.multiple_of(j * KB, KB)
            s_all.append([_dot_nt(k_ref[0, pl.ds(c0, KB), g * HEAD_DIM_A:(g + 1) * HEAD_DIM_A], qs[g])
                          for g in range(N_KV_A)])
        for b, (j, near_row) in enumerate(zip(js, near_rows)):
            mk = madd[j]
            m_news = []
            for g in range(N_KV_A):
                for r in range(HPG):
                    h = g * HPG + r
                    t = s_all[b][g][:, r * TQ:(r + 1) * TQ] + mk
                    if near_row is not None:
                        t = t + bnear_ref[h, near_row:near_row + KB, :]
                    tbuf[b * N_HEADS_A + h] = t
                    m_news.append(jnp.maximum(ms[h], jnp.max(t, axis=0, keepdims=True)))
            ms_out, ls_out = [], []
            for g in range(N_KV_A):
                vt = vt_ref[0, j, g * HEAD_DIM_A:(g + 1) * HEAD_DIM_A, :]
                for r in range(HPG):
                    h = g * HPG + r
                    cs = slice(r * TQ, (r + 1) * TQ)
                    m_new = m_news[h]
                    alpha = jnp.exp2(ms[h] - m_new)
                    p = jnp.exp2(tbuf[b * N_HEADS_A + h] - m_new)
                    ls_out.append(alpha * ls[h] + jnp.sum(p, axis=0, keepdims=True))
                    ms_out.append(m_new)
                    acc_s[g, :, cs] = alpha * acc_s[g, :, cs] + _dot(vt, p.astype(BF16))
            ms, ls = tuple(ms_out), tuple(ls_out)
        return ms, ls

    carry = (tuple(jnp.full((1, TQ), NEG, F32) for _ in range(N_HEADS_A)),
             tuple(jnp.zeros((1, TQ), F32) for _ in range(N_HEADS_A)))
    carry = lax.fori_loop(
        0, n_trips,
        lambda jt, c: attn_blocks([FAR_BLOCKS * jt + u for u in range(FAR_BLOCKS)], [None] * FAR_BLOCKS, c), carry)
    for left in range(1, FAR_BLOCKS):
        carry = lax.cond(n_left == left,
                         lambda c, left=left: attn_blocks([n_far - left + u for u in range(left)], [None] * left, c),
                         lambda c: c, carry)
    ms, ls = lax.cond(nkb >= 2, lambda c: attn_blocks([nkb - 2, nkb - 1], [0, KB], c),
                      lambda c: attn_blocks([nkb - 1], [KB], c), carry)

    for g in range(N_KV_A):
        for r in range(HPG):
            h = g * HPG + r
            out_t = acc_s[g, :, r * TQ:(r + 1) * TQ] / ls[h]
            o_ref[0, :, h * HEAD_DIM_A:(h + 1) * HEAD_DIM_A] = out_t.T.astype(o_ref.dtype)


def _dsa(q, qi, kiwi, kb, vb, kie, kio, rel_bias, *, n_keys, pos0, tq, kblk):
    B, T, _ = q.shape
    Lp = kb.shape[1]
    assert T % tq == 0 and Lp % kblk == 0 and tq % CHUNK == 0 or T == tq
    n_qt = T // tq
    if n_qt == 1:
        kb0, kstep = Lp // kblk, 0
        assert ((pos0 + tq - 1) // CHUNK + 1) * CHUNK >= n_keys
    else:
        assert pos0 == 0 and tq == kblk and n_keys == Lp == T
        kb0, kstep = 1, 1
    topk = min(TOPK_MAX, n_keys // 4)
    rb = min(128, tq)

    t = jnp.arange(tq, dtype=jnp.int32)[:, None]
    c = jnp.arange(2 * kblk, dtype=jnp.int32)[None, :]
    if n_qt == 1:
        q_abs = pos0 + t
        s_abs = (kb0 - 2) * kblk + c
    else:
        q_abs = kblk + t
        s_abs = c
    vis = ((s_abs // CHUNK) <= (q_abs // CHUNK)) & (s_abs < (n_keys if n_qt == 1 else 2 * kblk))
    far_bucket = _far_bucket(kblk + 1, max(Lp, kblk + 2))

    n_kb = Lp // kblk
    hpg = N_HEADS_A // N_KV_A
    common = dict(TQ=tq, KB=kblk, kb0=kb0, kstep=kstep, topk=topk, far_bucket=far_bucket,
                  tie_bits=int(math.ceil(math.log2(Lp))) + 1)
    attend_scratch = [pltpu.VMEM((N_KV_A, hpg * tq, HEAD_DIM_A), BF16),
                      pltpu.VMEM((N_KV_A, hpg * tq, LANES), F32),
                      pltpu.VMEM((N_KV_A, hpg * tq, LANES), F32),
                      pltpu.VMEM((N_KV_A, hpg * tq, HEAD_DIM_A), F32)]
    if tq % LANES == 0:
        tab_t = _bias_table(rel_bias, _t5_bucket(s_abs - q_abs).T, shift_bucket=far_bucket)
        vt = jnp.swapaxes(vb.reshape(B, n_kb, kblk, D_KV), 2, 3)
        common.pop("far_bucket")
        return pl.pallas_call(
            functools.partial(_dsa_kernel_t, **common),
            grid=(B, n_qt),
            in_specs=[pl.BlockSpec((1, tq, D_ATTN), lambda b, i: (b, i, 0)),
                      pl.BlockSpec((1, tq, N_IDX_HEADS * IDX_DIM), lambda b, i: (b, i, 0)),
                      pl.BlockSpec((1, tq, LANES), lambda b, i: (b, i, 0)),
                      pl.BlockSpec((1, Lp, D_KV), lambda b, i: (b, 0, 0)),
                      pl.BlockSpec((1, n_kb, D_KV, kblk), lambda b, i: (b, 0, 0, 0)),
                      pl.BlockSpec((1, Lp, LANES), lambda b, i: (b, 0, 0)),
                      pl.BlockSpec((1, Lp, LANES), lambda b, i: (b, 0, 0)),
                      _const_spec((N_HEADS_A, 2 * kblk, tq)),
                      _const_spec((2 * kblk, tq))],
            out_specs=pl.BlockSpec((1, tq, D_ATTN), lambda b, i: (b, i, 0)),
            out_shape=jax.ShapeDtypeStruct((B, T, D_ATTN), BF16),
            scratch_shapes=[pltpu.VMEM((n_kb, kblk, tq), jnp.int32),
                            pltpu.VMEM((N_DIGITS, n_kb, kblk, tq), BF16),
                            pltpu.VMEM((n_kb, kblk, tq), F32),
                            pltpu.VMEM((N_KV_A, hpg * tq, HEAD_DIM_A), BF16),
                            pltpu.VMEM((8, tq), F32),
                            pltpu.VMEM((N_KV_A, HEAD_DIM_A, hpg * tq), F32),
                            pltpu.VMEM((FAR_BLOCKS * N_HEADS_A, kblk, tq), F32)],
            compiler_params=pltpu.CompilerParams(dimension_semantics=("arbitrary", "arbitrary"),
                                                 vmem_limit_bytes=VMEM_LIMIT),
            name="dsa_t",
        )(q, qi, kiwi, kb, vt, kie, kio, tab_t, vis.astype(F32).T)

    bias_near = _bias_table(rel_bias, _t5_bucket(s_abs - q_abs))
    kern = functools.partial(_dsa_kernel, RB=rb, **common)
    return pl.pallas_call(
        kern,
        grid=(B, n_qt),
        in_specs=[pl.BlockSpec(memory_space=pltpu.SMEM),
                  pl.BlockSpec((1, tq, D_ATTN), lambda b, i: (b, i, 0)),
                  pl.BlockSpec((1, tq, N_IDX_HEADS * IDX_DIM), lambda b, i: (b, i, 0)),
                  pl.BlockSpec((1, tq, LANES), lambda b, i: (b, i, 0)),
                  pl.BlockSpec((1, Lp, D_KV), lambda b, i: (b, 0, 0)),
                  pl.BlockSpec((1, Lp, D_KV), lambda b, i: (b, 0, 0)),
                  pl.BlockSpec((1, Lp, LANES), lambda b, i: (b, 0, 0)),
                  pl.BlockSpec((1, Lp, LANES), lambda b, i: (b, 0, 0)),
                  _const_spec((N_HEADS_A, tq, 2 * kblk)),
                  _const_spec((tq, 2 * kblk))],
        out_specs=pl.BlockSpec((1, tq, D_ATTN), lambda b, i: (b, i, 0)),
        out_shape=jax.ShapeDtypeStruct((B, T, D_ATTN), BF16),
        scratch_shapes=[pltpu.VMEM((n_kb, tq, kblk), jnp.int32),
                        pltpu.VMEM((n_kb, tq, kblk), F32),
                        pltpu.VMEM((N_IDX_HEADS, tq, LANES), F32),
                        pltpu.VMEM((N_KV_A, hpg * tq, HEAD_DIM_A), BF16),
                        pltpu.VMEM((tq, 1), F32),
                        pltpu.VMEM((N_KV_A, hpg * tq, LANES), F32),
                        pltpu.VMEM((N_KV_A, hpg * tq, LANES), F32),
                        pltpu.VMEM((N_KV_A, hpg * tq, HEAD_DIM_A), F32)],
        compiler_params=pltpu.CompilerParams(dimension_semantics=("arbitrary", "arbitrary"),
                                             vmem_limit_bytes=VMEM_LIMIT),
        name="dsa",
    )(rel_bias, q, qi, kiwi, kb, vb, kie, kio, bias_near, vis.astype(F32))


def _mix_kernel(x_ref, a_ref, u_ref, hist_ref, mk_ref, mv_ref, wpool_ref, pscale_ref, wout_ref, gx_ref,
                wxq_ref, wxo_ref, gffn_ref, wr_ref, br_ref, o_ref, ext, *, tm, pos0):
    t = pl.program_id(1)
    H = POOL_HIST + 1

    @pl.when(t == 0)
    def _():
        ext[0:H, :] = hist_ref[0]

    @pl.when(t > 0)
    def _():
        ext[0:H, :] = ext[tm:tm + H, :]

    ext[H:H + tm, :] = u_ref[0]

    da = _dot(a_ref[0], wout_ref[0:D_ATTN, :])

    pos = pos0 + t * tm + lax.broadcasted_iota(jnp.int32, (tm, 1), 0)
    pouts = []
    for gi, w in enumerate(POOL_WINDOWS):
        c0, c1 = gi * POOL_GROUP_DIM, (gi + 1) * POOL_GROUP_DIM
        wsum = ext[H:H + tm, c0:c1]
        for s in range(1, w):
            wsum = wsum + ext[H - s:H - s + tm, c0:c1]
        cnt = jnp.minimum(pos + 1, w).astype(F32)
        d = wsum / cnt - u_ref[0, :, c0:c1]
        y = _dot(d.astype(BF16), wpool_ref[gi]) * pscale_ref[:, c0:c1]
        pouts.append(y.astype(BF16))
    pcat = jnp.concatenate(pouts, axis=1)

    D = x_ref.shape[2]
    x1 = x_ref[0] + da + _dot(pcat, wout_ref[D_ATTN:D_ATTN + D_POOL, :])
    h = _rms(x1, gx_ref[...]).astype(BF16)
    qx = _dot(h, wxq_ref[...])
    heads = [slice(hh * XHEAD_DIM, (hh + 1) * XHEAD_DIM) for hh in range(N_XHEADS)]
    all_logits = [_dot_nt(qx[:, hs].astype(BF16), mk_ref[0, :, hs]) for hs in heads]
    outs = []
    for hs, logits in zip(heads, all_logits):
        logits = logits * (XHEAD_DIM ** -0.5)
        m = jnp.max(logits, axis=1, keepdims=True)
        p = jnp.exp(logits - m)
        l = jnp.sum(p, axis=1, keepdims=True)
        outs.append((_dot(p.astype(BF16), mv_ref[0, :, hs]) / l).astype(BF16))
    o = jnp.concatenate(outs, axis=1)
    x2 = x1 + _dot(o, wxo_ref[...])
    o_ref[0, :, 0:D] = x2
    hf = _rms(x2, gffn_ref[...]).astype(BF16)
    o_ref[0, :, D:D + LANES] = _route(_dot(hf, wr_ref[...]) + br_ref[...])


def _mix(x, a, u, hist, mk, mv, w_pool, pool_scale, w_out, g_x, w_xq, w_xo, g_ffn, w_r, b_r, *, pos0, tm):
    B, T, D = x.shape
    n_mem = mk.shape[1]
    H = POOL_HIST + 1
    assert T % tm == 0 and tm >= H
    hist16 = jnp.concatenate([jnp.zeros((B, 1, D_POOL), F32), hist], axis=1)
    return pl.pallas_call(
        functools.partial(_mix_kernel, tm=tm, pos0=pos0),
        grid=(B, T // tm),
        in_specs=[pl.BlockSpec((1, tm, D), lambda b, t: (b, t, 0)),
                  pl.BlockSpec((1, tm, D_ATTN), lambda b, t: (b, t, 0)),
                  pl.BlockSpec((1, tm, D_POOL), lambda b, t: (b, t, 0)),
                  pl.BlockSpec((1, H, D_POOL), lambda b, t: (b, 0, 0)),
                  pl.BlockSpec((1, n_mem, D_X), lambda b, t: (b, 0, 0)),
                  pl.BlockSpec((1, n_mem, D_X), lambda b, t: (b, 0, 0)),
                  _const_spec(w_pool.shape),
                  _const_spec((1, D_POOL)),
                  _const_spec(w_out.shape),
                  _const_spec((1, D)),
                  _const_spec(w_xq.shape),
                  _const_spec(w_xo.shape),
                  _const_spec((1, D)),
                  _const_spec(w_r.shape),
                  _const_spec((1, LANES))],
        out_specs=pl.BlockSpec((1, tm, D + LANES), lambda b, t: (b, t, 0)),
        out_shape=jax.ShapeDtypeStruct((B, T, D + LANES), F32),
        scratch_shapes=[pltpu.VMEM((H + tm, D_POOL), F32)],
        compiler_params=pltpu.CompilerParams(dimension_semantics=("arbitrary", "arbitrary"),
                                             vmem_limit_bytes=VMEM_LIMIT),
        name="mix",
    )(x, a, u, hist16, mk, mv, w_pool, pool_scale.reshape(1, D_POOL), w_out, g_x.reshape(1, D), w_xq, w_xo,
      g_ffn.reshape(1, D), w_r, b_r)


R_OFF = N_GROUPS


def _route(logits):
    tm = logits.shape[0]
    lane = lax.broadcasted_iota(jnp.int32, (tm, LANES), 1).astype(F32)
    ninf = -jnp.inf
    big = float(LANES)
    gl = jnp.where(lane < N_GROUPS, logits, ninf)
    gmax = jnp.max(gl, axis=1, keepdims=True)
    g_sel = jnp.min(jnp.where(gl == gmax, lane, big), axis=1, keepdims=True)
    g_prob = 1.0 / jnp.sum(jnp.exp(gl - gmax), axis=1, keepdims=True)
    lo = R_OFF + g_sel * EXPERTS_PER_GROUP
    el = jnp.where((lane >= lo) & (lane < lo + EXPERTS_PER_GROUP), logits, ninf)
    tv0 = jnp.max(el, axis=1, keepdims=True)
    ti0 = jnp.min(jnp.where(el == tv0, lane, big), axis=1, keepdims=True)
    el2 = jnp.where(lane == ti0, ninf, el)
    tv1 = jnp.max(el2, axis=1, keepdims=True)
    ti1 = jnp.min(jnp.where(el2 == tv1, lane, big), axis=1, keepdims=True)
    e1 = jnp.exp(tv1 - tv0)
    den = 1.0 + e1
    w0 = g_prob / den
    w1 = g_prob * e1 / den
    ids = jnp.where(lane == 0.0, jnp.minimum(ti0, ti1) - R_OFF,
                    jnp.where(lane == 1.0, jnp.maximum(ti0, ti1) - R_OFF, 0.0))
    return jnp.where(lane == ti0, w0, 0.0) + jnp.where(lane == ti1, w1, 0.0) + ids


def _expert(h, rec, e, wg, wu, wd):
    a = _dot(h, wg)
    b = _dot(h, wu)
    lane = lax.broadcasted_iota(jnp.int32, rec.shape, 1)
    cw = jnp.sum(jnp.where(lane == e + R_OFF, rec, 0.0), axis=1, keepdims=True)
    hid = a * jax.nn.sigmoid(a) * b * cw
    return _dot(hid.astype(BF16), wd)


def _moe_kernel(x_ref, gffn_ref, wg_ref, wu_ref, wd_ref, gfin_ref, o_ref, h_s, acc_s):
    e = pl.program_id(1)
    d = o_ref.shape[1]

    @pl.when(e == 0)
    def _():
        h_s[...] = _rms(x_ref[:, 0:d], gffn_ref[...]).astype(BF16)
        acc_s[...] = jnp.zeros(acc_s.shape, F32)

    acc_s[...] += _expert(h_s[...], x_ref[:, d:d + LANES], e, wg_ref[0], wu_ref[0], wd_ref[0])

    @pl.when(e == pl.num_programs(1) - 1)
    def _():
        o_ref[...] = _rms(x_ref[:, 0:d] + acc_s[...], gfin_ref[...])


def _moe(x, g_ffn, w_gate, w_up, w_down, g_final, *, tm):
    n, de = x.shape
    d = de - LANES
    assert n % tm == 0
    return pl.pallas_call(
        _moe_kernel,
        grid=(n // tm, N_EXPERTS),
        in_specs=[pl.BlockSpec((tm, de), lambda i, e: (i, 0)),
                  _const_spec((1, d)),
                  pl.BlockSpec((1, d, D_EXPERT), lambda i, e: (e, 0, 0)),
                  pl.BlockSpec((1, d, D_EXPERT), lambda i, e: (e, 0, 0)),
                  pl.BlockSpec((1, D_EXPERT, d), lambda i, e: (e, 0, 0)),
                  _const_spec((1, d))],
        out_specs=pl.BlockSpec((tm, d), lambda i, e: (i, 0)),
        out_shape=jax.ShapeDtypeStruct((n, d), F32),
        scratch_shapes=[pltpu.VMEM((tm, d), BF16),
                        pltpu.VMEM((tm, d), F32)],
        compiler_params=pltpu.CompilerParams(dimension_semantics=("arbitrary", "arbitrary"),
                                             vmem_limit_bytes=VMEM_LIMIT),
        name="moe",
    )(x, g_ffn.reshape(1, d), w_gate, w_up, w_down, g_final.reshape(1, d))


MOE_TILES_PER_STEP = 2


def _moe_sparse_kernel(ea_ref, eb_ref, nv_ref, idx_ref, idxn_ref, x_hbm, gffn_ref, gfin_ref, *rest, tm, tps):
    w_refs = [rest[6 * k:6 * k + 6] for k in range(tps)]
    o_hbm, xbuf, ybuf, gsem, ssem = rest[6 * tps:]
    t = pl.program_id(0)
    nt = pl.num_programs(0)
    d = o_hbm.shape[1]
    slot = lax.rem(t, 2)
    nslot = 1 - slot
    rows = tps * tm
    WAIT_ROWS = 8

    def step_rows(step):
        nvs = [nv_ref[tps * step + k] for k in range(tps)]
        return nvs, functools.reduce(lambda a, b: a + b, nvs)

    nvs, nv = step_rows(t)

    def gather_row(ids, s, r):
        return pltpu.make_async_copy(x_hbm.at[pl.ds(ids[0, 0, r], 1), :], xbuf.at[s, pl.ds(r, 1), :], gsem.at[s])

    def gather_loop(ids, s):
        def body(r, carry):
            gather_row(ids, s, r).start()
            return carry
        lax.fori_loop(0, rows, body, 0)

    def gather_wait(s):
        pltpu.make_async_copy(x_hbm.at[pl.ds(0, rows), :], xbuf.at[s], gsem.at[s]).wait()

    def scatter_rows(s, r, row, n):
        return pltpu.make_async_copy(ybuf.at[s, pl.ds(r, n), :], o_hbm.at[pl.ds(row, n), :], ssem.at[s])

    def scatter_wait(s, n):
        def body_many(r, carry):
            scatter_rows(s, 0, 0, WAIT_ROWS).wait()
            return carry
        lax.fori_loop(0, n // WAIT_ROWS, body_many, 0)

        def body_one(r, carry):
            scatter_rows(s, 0, 0, 1).wait()
            return carry
        lax.fori_loop(0, lax.rem(n, WAIT_ROWS), body_one, 0)

    @pl.when(t == 0)
    def _():
        gather_loop(idx_ref, 0)

    @pl.when(t >= 2)
    def _():
        scatter_wait(slot, step_rows(jnp.maximum(t - 2, 0))[1])

    @pl.when((t == 0) | (step_rows(jnp.maximum(t - 1, 0))[1] > 0))
    def _():
        gather_wait(slot)

    @pl.when(nv > 0)
    def _():
        tiles = []
        for k in range(tps):
            x = xbuf[slot, k * tm:(k + 1) * tm, :]
            xr = x[:, 0:d]
            tiles.append((xr, x[:, d:d + LANES], _rms(xr, gffn_ref[...]).astype(BF16)))
        for r in range(rows):
            gather_row(idxn_ref, nslot, r).start()
        pre = [[(_dot(h, wga[0]), _dot(h, wua[0])), (_dot(h, wgb[0]), _dot(h, wub[0]))]
               for (_, _, h), (wga, wua, _, wgb, wub, _) in zip(tiles, w_refs)]
        for k in range(tps):
            xr, rec, _ = tiles[k]
            lane = lax.broadcasted_iota(jnp.int32, rec.shape, 1)
            experts = (ea_ref[tps * t + k], eb_ref[tps * t + k])
            acc = None
            for (a, b), e, wd in zip(pre[k], experts, (w_refs[k][2], w_refs[k][5])):
                cw = jnp.sum(jnp.where(lane == e + R_OFF, rec, 0.0), axis=1, keepdims=True)
                hid = a * jax.nn.sigmoid(a) * b * cw
                out = _dot(hid.astype(BF16), wd[0])
                acc = out if acc is None else acc + out
            ybuf[slot, k * tm:(k + 1) * tm, :] = _rms(xr + acc, gfin_ref[...])

        for k in range(tps):
            def body_many(i, carry, k=k):
                for u in range(WAIT_ROWS):
                    r = k * tm + i * WAIT_ROWS + u
                    scatter_rows(slot, r, idx_ref[0, 0, r], 1).start()
                return carry
            lax.fori_loop(0, nvs[k] // WAIT_ROWS, body_many, 0)

            def body_one(i, carry, k=k):
                r = k * tm + i
                scatter_rows(slot, r, idx_ref[0, 0, r], 1).start()
                return carry
            lax.fori_loop((nvs[k] // WAIT_ROWS) * WAIT_ROWS, nvs[k], body_one, 0)

    @pl.when(t == nt - 1)
    def _():
        @pl.when(nv > 0)
        def _():
            gather_wait(nslot)

        @pl.when(t >= 1)
        def _():
            scatter_wait(nslot, step_rows(jnp.maximum(t - 1, 0))[1])
        scatter_wait(slot, nv)


def _moe_tile_rows(n):
    n_pairs = N_GROUPS * (EXPERTS_PER_GROUP * (EXPERTS_PER_GROUP - 1) // 2)
    mean = n / n_pairs
    rows = (mean + 3.0 * math.sqrt(mean)) / 2
    return int(min(256, max(SUB_BF16, -(-rows // SUB_BF16) * SUB_BF16)))


def _moe_sparse(x, g_ffn, w_gate, w_up, w_down, g_final, *, tm):
    n, de = x.shape
    d = de - LANES
    n_pairs = N_GROUPS * (EXPERTS_PER_GROUP * (EXPERTS_PER_GROUP - 1) // 2)
    tps = MOE_TILES_PER_STEP
    nt = -(-n // tm) + n_pairs
    nt = -(-nt // tps) * tps
    n_steps = nt // tps
    n_keys = N_EXPERTS * N_EXPERTS

    ids = x[:, d:d + 2].astype(jnp.int32)
    key = ids[:, 0] * N_EXPERTS + ids[:, 1]
    order = jnp.argsort(key).astype(jnp.int32)
    skey = key[order]
    count_below = lambda sorted_vals, q: jnp.sum((sorted_vals[None, :] < q[:, None]).astype(jnp.int32), axis=1)
    edges = count_below(skey, jnp.arange(n_keys + 1, dtype=jnp.int32))
    starts, ends = edges[:-1], edges[1:]
    tiles = (ends - starts + tm - 1) // tm
    cum = jnp.cumsum(tiles)
    total = cum[-1]
    tt = jnp.arange(nt, dtype=jnp.int32)
    tc = jnp.minimum(tt, total - 1)
    cls = count_below(cum, tc + 1)
    first = cum[cls] - tiles[cls]
    tstart = starts[cls] + (tc - first) * tm
    nv = jnp.where(tt < total, jnp.clip(ends[cls] - tstart, 0, tm), 0).astype(jnp.int32)
    ea = (cls // N_EXPERTS).astype(jnp.int32)
    eb = (cls % N_EXPERTS).astype(jnp.int32)
    rows = jnp.clip(tstart[:, None] + jnp.arange(tm, dtype=jnp.int32)[None, :], 0, n - 1)
    idx = order[rows].reshape(n_steps, 1, tps * tm)

    def wspec(shape, experts, k):
        return pl.BlockSpec(shape, lambda t, ea, eb, nv: ((ea, eb)[experts][tps * t + k], 0, 0))

    cspec = lambda shape: pl.BlockSpec(shape, lambda t, ea, eb, nv: (0,) * len(shape),
                                       pipeline_mode=pl.Buffered(1))
    w_specs, w_args = [], []
    for k in range(tps):
        for experts in (0, 1):
            w_specs += [wspec((1, d, D_EXPERT), experts, k), wspec((1, d, D_EXPERT), experts, k),
                        wspec((1, D_EXPERT, d), experts, k)]
            w_args += [w_gate, w_up, w_down]
    grid_spec = pltpu.PrefetchScalarGridSpec(
        num_scalar_prefetch=3,
        grid=(n_steps,),
        in_specs=[pl.BlockSpec((1, 1, tps * tm), lambda t, ea, eb, nv: (t, 0, 0), memory_space=pltpu.SMEM),
                  pl.BlockSpec((1, 1, tps * tm), lambda t, ea, eb, nv: (jnp.minimum(t + 1, n_steps - 1), 0, 0),
                               memory_space=pltpu.SMEM),
                  pl.BlockSpec(memory_space=pl.ANY),
                  cspec((1, d)), cspec((1, d))] + w_specs,
        out_specs=pl.BlockSpec(memory_space=pl.ANY),
        scratch_shapes=[pltpu.VMEM((2, tps * tm, de), F32),
                        pltpu.VMEM((2, tps * tm, d), F32),
                        pltpu.SemaphoreType.DMA((2,)),
                        pltpu.SemaphoreType.DMA((2,))])
    return pl.pallas_call(
        functools.partial(_moe_sparse_kernel, tm=tm, tps=tps),
        grid_spec=grid_spec,
        out_shape=jax.ShapeDtypeStruct((n, d), F32),
        compiler_params=pltpu.CompilerParams(dimension_semantics=("arbitrary",),
                                             vmem_limit_bytes=VMEM_LIMIT),
        name="moe_sparse",
    )(ea, eb, nv, idx, idx, x, g_ffn.reshape(1, d), g_final.reshape(1, d), *w_args)


C_Q = 0
C_QI = C_Q + D_ATTN
C_K = C_QI + N_IDX_HEADS * IDX_DIM
C_V = C_K + D_KV
C_KIWI = C_V + D_KV
C_KIE = C_KIWI + LANES
C_KIO = C_KIE + LANES
C_U = C_KIO + LANES
C_END = C_U + D_POOL


def _layout_w_in(w_in):
    d = w_in.shape[0]
    offs = np.cumsum((0, D_ATTN, D_KV, D_KV, N_IDX_HEADS * IDX_DIM, IDX_DIM, N_IDX_HEADS, D_POOL))
    wq, wk, wv, wqi, wki, wwi, wu = (w_in[:, offs[n]:offs[n + 1]] for n in range(7))
    z = lambda n: jnp.zeros((d, n), w_in.dtype)
    wq = wq * (HEAD_DIM_A ** -0.5 * LOG2E)
    cat = jnp.concatenate([wq, wqi, wk, wv,
                           wki, wwi, z(LANES - IDX_DIM - N_IDX_HEADS),
                           wki, z(LANES - IDX_DIM),
                           z(LANES - IDX_DIM), wki,
                           wu], axis=1)
    assert cat.shape[1] == C_END
    return cat.astype(BF16)


IN_GROUPS = ((C_Q, C_QI, 1),
             (C_QI, C_K, 1),
             (C_K, C_V, 2),
             (C_V, C_KIWI, 2),
             (C_KIWI, C_KIE, 1),
             (C_KIE, C_KIO, 1),
             (C_KIO, C_U, 1),
             (C_U, C_END, 1))
IN_DTYPES = (BF16, BF16, F32, BF16, F32, BF16, F32, BF16, BF16, F32)


def _pad_rows(a, n):
    return jnp.pad(a, ((0, 0), (0, n - a.shape[1]), (0, 0)))


def _layer(x, pos0, caches, pool_hist, mk, mv, rel_bias, W, *, tm_in, tq, tm_mix, tm_moe, sparse_moe):
    B, T, D = x.shape
    n = B * T
    q, qi, k, kb, v, vb, kiwi, kie, kio, u = _norm_matmul(
        x.reshape(n, D), W["g_mix"], W["w_in"], IN_GROUPS, IN_DTYPES, tm_in)
    r3 = lambda a: a.reshape(B, T, a.shape[-1])
    kblk = KEY_BLOCK
    if caches is None:
        n_keys = T
        kb3, vb3, kie3, kio3 = r3(kb), r3(vb), r3(kie), r3(kio)
    else:
        k_hist, v_hist, ki_hist = caches
        past = k_hist.shape[1]
        n_keys = past + T
        lp = -(-n_keys // kblk) * kblk
        ki = r3(kiwi)[:, :, :IDX_DIM]
        ki_all = jnp.concatenate([ki_hist, ki], axis=1)
        zeros = jnp.zeros_like(ki_all)
        kb3 = _pad_rows(jnp.concatenate([k_hist.reshape(B, past, D_KV), r3(k)], axis=1), lp).astype(BF16)
        vb3 = _pad_rows(jnp.concatenate([v_hist.reshape(B, past, D_KV), r3(v)], axis=1), lp).astype(BF16)
        kie3 = _pad_rows(jnp.concatenate([ki_all, zeros], axis=2), lp).astype(BF16)
        kio3 = _pad_rows(jnp.concatenate([zeros, ki_all], axis=2), lp).astype(BF16)
    a_out = _dsa(r3(q), r3(qi), r3(kiwi), kb3, vb3, kie3, kio3, rel_bias,
                 n_keys=n_keys, pos0=pos0, tq=tq, kblk=kblk)
    x2 = _mix(x, a_out, r3(u), pool_hist, mk.astype(BF16), mv.astype(BF16), W["w_pool"], W["pool_scale"],
              W["w_out"], W["g_x"], W["w_xq"], W["w_xo"], W["g_ffn"], W["w_r"], W["b_r"], pos0=pos0, tm=tm_mix)
    moe = _moe_sparse if sparse_moe else _moe
    y = moe(x2.reshape(n, D + LANES), W["g_ffn"], W["w_gate"], W["w_up"], W["w_down"], W["g_final"], tm=tm_moe)
    new_pool = jnp.concatenate([pool_hist, r3(u)], axis=1)[:, -POOL_HIST:]
    return (y.reshape(B, T, D), k.reshape(B, T, N_KV_A, HEAD_DIM_A), v.reshape(B, T, N_KV_A, HEAD_DIM_A),
            r3(kiwi)[:, :, :IDX_DIM], new_pool)


def kernel(x_prompt, x_sample, mem_prompt, cache_k, cache_v, cache_k_idx, cache_pool, cache_mem_k, cache_mem_v, rel_bias, g_mix, w_in, w_pool, pool_scale, w_out, g_mem, w_mk, w_mv, g_x, w_xq, w_xo, g_ffn, w_rg, b_rg, w_re, b_re, w_gate, w_up, w_down, g_final):
    depth = g_mix.shape[0]
    assert depth == 1
    l = 0
    B, T, D = x_prompt.shape
    Bs, Ts, _ = x_sample.shape
    n_mem = mem_prompt.shape[1]
    past = cache_k.shape[2]

    w_r = jnp.concatenate([w_rg[l], w_re[l], jnp.zeros((D, LANES - N_GROUPS - N_EXPERTS), F32)], axis=1)
    W = dict(
        g_mix=g_mix[l], w_in=_layout_w_in(w_in[l]), w_pool=w_pool[l].astype(BF16), pool_scale=pool_scale[l],
        w_out=w_out[l].astype(BF16), g_x=g_x[l], w_xq=w_xq[l].astype(BF16), w_xo=w_xo[l].astype(BF16),
        g_ffn=g_ffn[l], w_r=w_r.astype(BF16),
        b_r=jnp.concatenate([b_rg[l], b_re[l], jnp.zeros((LANES - N_GROUPS - N_EXPERTS,), F32)]).reshape(1, LANES),
        w_gate=w_gate[l].astype(BF16), w_up=w_up[l].astype(BF16), w_down=w_down[l].astype(BF16),
        g_final=g_final)

    w_m = jnp.concatenate([w_mk[l], w_mv[l]], axis=1).astype(BF16)
    mk, mv = _norm_matmul(mem_prompt.reshape(B * n_mem, D), g_mem[l], w_m,
                          ((0, D_X, 1), (D_X, 2 * D_X, 1)), (F32, F32), TOKEN_TILE)
    mk = mk.reshape(B, n_mem, D_X)
    mv = mv.reshape(B, n_mem, D_X)

    yp, kp, vp, kip, pp = _layer(x_prompt, 0, None, jnp.zeros((B, POOL_HIST, D_POOL), F32), mk, mv, rel_bias, W,
                                 tm_in=TOKEN_TILE, tq=QUERY_TILE, tm_mix=TOKEN_TILE, tm_moe=_moe_tile_rows(B * T),
                                 sparse_moe=True)
    ys, ks, vs, kis, ps = _layer(x_sample, past,
                                 (cache_k[l], cache_v[l], cache_k_idx[l]), cache_pool[l],
                                 cache_mem_k[l].reshape(Bs, n_mem, D_X), cache_mem_v[l].reshape(Bs, n_mem, D_X),
                                 rel_bias, W, tm_in=Bs * Ts, tq=Ts, tm_mix=Ts, tm_moe=Bs * Ts, sparse_moe=False)
    st = lambda a: a[None]
    return (yp, ys, st(kp), st(vp), st(kip), st(pp),
            st(mk.reshape(B, n_mem, N_XHEADS, XHEAD_DIM)), st(mv.reshape(B, n_mem, N_XHEADS, XHEAD_DIM)),
            st(ks), st(vs), st(kis), st(ps))
```

```python
import functools
import math

import numpy as np
import jax
import jax.numpy as jnp
from jax import lax
from jax.experimental import pallas as pl
from jax.experimental.pallas import tpu as pltpu

F32 = jnp.float32
BF16 = jnp.bfloat16

CHUNK = 64
N_HEADS_A = 8
HEAD_DIM_A = 128
N_KV_A = 2
D_ATTN = N_HEADS_A * HEAD_DIM_A
D_KV = N_KV_A * HEAD_DIM_A
N_IDX_HEADS = 8
IDX_DIM = 64
TOPK_MAX = 256
POOL_WINDOWS = (2, 4, 8, 16)
POOL_GROUP_DIM = 256
D_POOL = len(POOL_WINDOWS) * POOL_GROUP_DIM
POOL_HIST = max(POOL_WINDOWS) - 1
N_BUCKETS = 32
MAX_DISTANCE = 128
N_XHEADS = 4
XHEAD_DIM = 128
D_X = N_XHEADS * XHEAD_DIM
N_GROUPS = 4
EXPERTS_PER_GROUP = 8
N_EXPERTS = N_GROUPS * EXPERTS_PER_GROUP
D_EXPERT = 256
EPS = 1e-6

LANES = 128
INT_MIN = -(2 ** 31)
NEG = -1e30
LOG2E = math.log2(math.e)
VMEM_LIMIT = 48 * 1024 * 1024
TOKEN_TILE = 512
KEY_BLOCK = 256
QUERY_TILE = KEY_BLOCK


def _rms(x, g):
    ms = jnp.mean(x * x, axis=-1, keepdims=True)
    return x * lax.rsqrt(ms + EPS) * g


def _dot(a, b):
    return jnp.dot(a, b, preferred_element_type=F32)


def _dot_nt(a, b):
    return lax.dot_general(a, b, (((1,), (1,)), ((), ())), preferred_element_type=F32)


def _const_spec(shape):
    nd = len(shape)
    return pl.BlockSpec(shape, lambda *_: (0,) * nd, pipeline_mode=pl.Buffered(1))


def _norm_matmul_kernel(x_ref, g_ref, w_ref, *out_refs, groups):
    h = _rms(x_ref[...], g_ref[...]).astype(BF16)
    k = 0
    for c0, c1, n_out in groups:
        acc = _dot(h, w_ref[:, c0:c1])
        for _ in range(n_out):
            out_refs[k][...] = acc.astype(out_refs[k].dtype)
            k += 1


def _norm_matmul(x, g, w, groups, out_dtypes, tm):
    n, d = x.shape
    assert n % tm == 0
    widths = []
    for c0, c1, n_out in groups:
        widths += [c1 - c0] * n_out
    return pl.pallas_call(
        functools.partial(_norm_matmul_kernel, groups=tuple(groups)),
        grid=(n // tm,),
        in_specs=[pl.BlockSpec((tm, d), lambda i: (i, 0)),
                  _const_spec((1, d)),
                  _const_spec(w.shape)],
        out_specs=[pl.BlockSpec((tm, wd), lambda i: (i, 0)) for wd in widths],
        out_shape=[jax.ShapeDtypeStruct((n, wd), dt) for wd, dt in zip(widths, out_dtypes)],
        compiler_params=pltpu.CompilerParams(dimension_semantics=("arbitrary",),
                                             vmem_limit_bytes=VMEM_LIMIT),
        name="norm_matmul",
    )(x, g.reshape(1, d), w)


def _bias_table_kernel(relb_ref, bucket_ref, o_ref, *, shift_bucket):
    bucket = bucket_ref[...]
    for h in range(N_HEADS_A):
        acc = jnp.zeros(bucket.shape, F32)
        for b in range(N_BUCKETS):
            acc = jnp.where(bucket == b, relb_ref[b, h], acc)
        if shift_bucket is None:
            o_ref[h] = acc * LOG2E
        else:
            o_ref[h] = (acc - relb_ref[shift_bucket, h]) * LOG2E


def _bias_table(rel_bias, bucket, shift_bucket=None):
    tq, w = bucket.shape
    return pl.pallas_call(
        functools.partial(_bias_table_kernel, shift_bucket=shift_bucket),
        in_specs=[pl.BlockSpec(memory_space=pltpu.SMEM),
                  pl.BlockSpec((tq, w), lambda: (0, 0))],
        out_specs=pl.BlockSpec((N_HEADS_A, tq, w), lambda: (0, 0, 0)),
        out_shape=jax.ShapeDtypeStruct((N_HEADS_A, tq, w), F32),
        name="bias_table",
    )(rel_bias, bucket)


def _t5_bucket(rel):
    nb = N_BUCKETS // 2
    max_exact = nb // 2
    bucket = (rel > 0).astype(jnp.int32) * nb
    n = jnp.abs(rel)
    nf = jnp.maximum(n, 1).astype(F32)
    large = max_exact + (jnp.log(nf / max_exact) / math.log(MAX_DISTANCE / max_exact)
                         * (nb - max_exact)).astype(jnp.int32)
    large = jnp.minimum(large, nb - 1)
    return bucket + jnp.where(n < max_exact, n, large)


def _far_bucket(min_dist, max_dist):
    nb = N_BUCKETS // 2
    max_exact = nb // 2
    n = np.arange(min_dist, max_dist + 1, dtype=np.float64)
    large = max_exact + np.floor(np.log(n / max_exact) / math.log(MAX_DISTANCE / max_exact)
                                 * (nb - max_exact) * (1 - 1e-6)).astype(np.int64)
    assert large.min() >= nb - 1, "far key blocks must sit in the saturated distance bucket"
    return nb - 1


def _dsa_kernel(relb_ref, q_ref, qi_ref, kiwi_ref, k_ref, v_ref, kie_ref, kio_ref, bnear_ref, vis_ref,
                o_ref, keys, madd, wb, qs, xcut, mrun, lrun, acc_s,
                *, TQ, KB, RB, kb0, kstep, topk, far_bucket, tie_bits):
    i = pl.program_id(1)
    nkb = kb0 + kstep * i
    n_far = jnp.maximum(nkb - 2, 0)
    n_lc = KB // LANES

    kiwi = kiwi_ref[0]
    for h in range(N_IDX_HEADS):
        w = (kiwi[:, IDX_DIM + h:IDX_DIM + h + 1] * (N_IDX_HEADS ** -0.5)).astype(BF16).astype(F32)
        wb[h] = jnp.broadcast_to(w * (IDX_DIM ** -0.5), (TQ, LANES))

    def idx_block(j, near_col):
        c0 = pl.multiple_of(j * KB, KB)
        ke = kie_ref[0, pl.ds(c0, KB), :]
        ko = kio_ref[0, pl.ds(c0, KB), :]
        acc = [jnp.zeros((TQ, LANES), F32) for _ in range(n_lc)]
        for hp in range(N_IDX_HEADS // 2):
            qh = qi_ref[0, :, hp * LANES:(hp + 1) * LANES]
            se = _dot_nt(qh, ke)
            so = _dot_nt(qh, ko)
            we = wb[2 * hp]
            wo = wb[2 * hp + 1]
            for c in range(n_lc):
                cs = slice(c * LANES, (c + 1) * LANES)
                re = jnp.maximum(se[:, cs], 0.0).astype(BF16).astype(F32)
                ro = jnp.maximum(so[:, cs], 0.0).astype(BF16).astype(F32)
                acc[c] = acc[c] + we * re + wo * ro
        for c in range(n_lc):
            bits = lax.bitcast_convert_type(acc[c], jnp.int32)
            key = bits ^ ((bits >> 31) & jnp.int32(0x7FFFFFFF))
            if near_col is not None:
                vis = vis_ref[:, near_col + c * LANES:near_col + (c + 1) * LANES]
                key = jnp.where(vis > 0.0, key, jnp.int32(INT_MIN))
            keys[j, :, c * LANES:(c + 1) * LANES] = key

    def far_idx(j, carry):
        idx_block(j, None)
        return carry

    lax.fori_loop(0, n_far, far_idx, 0)

    @pl.when(nkb >= 2)
    def _():
        idx_block(nkb - 2, 0)

    idx_block(nkb - 1, KB)

    kf = float(topk)
    lane_f = lax.broadcasted_iota(jnp.int32, (RB, LANES), 1).astype(F32)
    NRB = TQ // RB
    rbs = [slice(rb * RB, (rb + 1) * RB) for rb in range(NRB)]

    def count(pred):
        def body(jj, accs):
            out = []
            for rb in range(NRB):
                blk = keys[jj, rbs[rb], :]
                acc = accs[rb]
                for c in range(n_lc):
                    acc = acc + jnp.where(pred(rb, blk[:, c * LANES:(c + 1) * LANES], jj, c), 1.0, 0.0)
                out.append(acc)
            return tuple(out)
        accs = lax.fori_loop(0, nkb, body, tuple(jnp.zeros((RB, LANES), F32) for _ in range(NRB)))
        return [jnp.sum(acc, axis=1, keepdims=True) for acc in accs]

    def count_ge(cands):
        cb = [jnp.broadcast_to(cand, (RB, LANES)) for cand in cands]
        return count(lambda rb, kblk, jj, c: kblk >= cb[rb])

    def bis_body(p, thrs):
        cands = [thr + lax.shift_left(jnp.int32(1), 31 - p) for thr in thrs]
        cnts = count_ge(cands)
        return tuple(jnp.where(cnt >= kf, cand, thr) for cnt, cand, thr in zip(cnts, cands, thrs))

    thrs = lax.fori_loop(0, 32, bis_body, tuple(jnp.full((RB, 1), INT_MIN, jnp.int32) for _ in range(NRB)))
    thrs = [jnp.maximum(thr, jnp.int32(INT_MIN + 1)) for thr in thrs]
    cnt_ge = count_ge(thrs)
    cnt_gt = count_ge([thr + 1 for thr in thrs])
    needs = [kf - c for c in cnt_gt]
    thrb = [jnp.broadcast_to(thr, (RB, LANES)) for thr in thrs]

    xcut[...] = jnp.full(xcut.shape, 2.0 ** 24, F32)
    excess = cnt_ge[0] - kf
    for c in cnt_ge[1:]:
        excess = jnp.maximum(excess, c - kf)

    @pl.when(jnp.max(excess) > 0.0)
    def _():
        def count_eq_lt(xs):
            xb = [jnp.broadcast_to(x, (RB, LANES)) for x in xs]

            def pred(rb, kblk, jj, c):
                col = lane_f + (jj * KB + c * LANES).astype(F32)
                return jnp.where(kblk == thrb[rb], col, 2.0 ** 25) < xb[rb]
            return count(pred)

        def tie_body(p, xs):
            cands = [x + lax.shift_left(jnp.int32(1), tie_bits - 1 - p).astype(F32) for x in xs]
            cnts = count_eq_lt(cands)
            return tuple(jnp.where(cnt < need, cand, x) for cnt, need, cand, x in zip(cnts, needs, cands, xs))

        xs = lax.fori_loop(0, tie_bits, tie_body, tuple(jnp.zeros((RB, 1), F32) for _ in range(NRB)))
        for rb in range(NRB):
            xcut[rbs[rb], :] = xs[rb]

    xb = [jnp.broadcast_to(xcut[rbs[rb], :], (RB, LANES)) for rb in range(NRB)]

    def mask_body(jj, carry):
        for rb in range(NRB):
            blk = keys[jj, rbs[rb], :]
            for c in range(n_lc):
                kblk = blk[:, c * LANES:(c + 1) * LANES]
                col = lane_f + (jj * KB + c * LANES).astype(F32)
                keep = (kblk > thrb[rb]) | ((kblk == thrb[rb]) & (col <= xb[rb]))
                madd[jj, rbs[rb], c * LANES:(c + 1) * LANES] = jnp.where(keep, 0.0, NEG)
        return carry

    lax.fori_loop(0, nkb, mask_body, 0)

    _attend(relb_ref, q_ref, k_ref, v_ref, bnear_ref, o_ref, madd, qs, mrun, lrun, acc_s,
            TQ=TQ, KB=KB, nkb=nkb, n_far=n_far, far_bucket=far_bucket)


def _attend(relb_ref, q_ref, k_ref, v_ref, bnear_ref, o_ref, madd, qs, mrun, lrun, acc_s,
            *, TQ, KB, nkb, n_far, far_bucket):
    n_lc = KB // LANES
    HPG = N_HEADS_A // N_KV_A
    for g in range(N_KV_A):
        for r in range(HPG):
            h = g * HPG + r
            qs[g, r * TQ:(r + 1) * TQ, :] = q_ref[0, :, h * HEAD_DIM_A:(h + 1) * HEAD_DIM_A]

    def logits(j, near_col, g):
        c0 = pl.multiple_of(j * KB, KB)
        kg = k_ref[0, pl.ds(c0, KB), g * HEAD_DIM_A:(g + 1) * HEAD_DIM_A]
        s = _dot_nt(qs[g], kg)
        mk = madd[j]
        out = []
        for r in range(HPG):
            h = g * HPG + r
            if near_col is None:
                bias = relb_ref[far_bucket, h] * LOG2E
            else:
                bias = bnear_ref[h, :, near_col:near_col + KB]
            out.append(s[r * TQ:(r + 1) * TQ] + bias + mk)
        return out

    def over_blocks(fn):
        def far(j, carry):
            fn(j, None)
            return carry

        lax.fori_loop(0, n_far, far, 0)

        @pl.when(nkb >= 2)
        def _():
            fn(nkb - 2, 0)

        fn(nkb - 1, KB)

    mrun[...] = jnp.full(mrun.shape, NEG, F32)

    def max_block(j, near_col):
        for g in range(N_KV_A):
            for r, sh in enumerate(logits(j, near_col, g)):
                rs = slice(r * TQ, (r + 1) * TQ)
                m = mrun[g, rs, :]
                for c in range(n_lc):
                    m = jnp.maximum(m, sh[:, c * LANES:(c + 1) * LANES])
                mrun[g, rs, :] = m

    over_blocks(max_block)

    for g in range(N_KV_A):
        mrun[g] = jnp.broadcast_to(jnp.max(mrun[g], axis=1, keepdims=True), mrun.shape[1:])
    lrun[...] = jnp.zeros(lrun.shape, F32)
    acc_s[...] = jnp.zeros(acc_s.shape, F32)

    def pv_block(j, near_col):
        c0 = pl.multiple_of(j * KB, KB)
        for g in range(N_KV_A):
            vg = v_ref[0, pl.ds(c0, KB), g * HEAD_DIM_A:(g + 1) * HEAD_DIM_A]
            parts = []
            for r, sh in enumerate(logits(j, near_col, g)):
                rs = slice(r * TQ, (r + 1) * TQ)
                mb = mrun[g, rs, :]
                l = lrun[g, rs, :]
                pcs = []
                for c in range(n_lc):
                    p = jnp.exp2(sh[:, c * LANES:(c + 1) * LANES] - mb)
                    l = l + p
                    pcs.append(p.astype(BF16))
                lrun[g, rs, :] = l
                parts.append(jnp.concatenate(pcs, axis=1))
            acc_s[g] += _dot(jnp.concatenate(parts, axis=0), vg)

    over_blocks(pv_block)

    for g in range(N_KV_A):
        out = acc_s[g] / jnp.sum(lrun[g], axis=1, keepdims=True)
        for r in range(HPG):
            h = g * HPG + r
            o_ref[0, :, h * HEAD_DIM_A:(h + 1) * HEAD_DIM_A] = out[r * TQ:(r + 1) * TQ].astype(o_ref.dtype)


DIGIT_BITS = 8
N_DIGITS = 32 // DIGIT_BITS
SUB_BF16 = 16
COUNT_BLOCKS = 4
FAR_BLOCKS = 3


def _dsa_kernel_t(q_ref, qi_ref, kiwi_ref, k_ref, vt_ref, kie_ref, kio_ref, bnear_ref, vist_ref,
                  o_ref, keys, dig, madd, qs, xcut, acc_s, tbuf,
                  *, TQ, KB, kb0, kstep, topk, tie_bits):
    i = pl.program_id(1)
    nkb = kb0 + kstep * i
    n_far = jnp.maximum(nkb - 2, 0)
    n_slab = KB // SUB_BF16

    w_t = kiwi_ref[0].T
    wrow = [(w_t[IDX_DIM + h:IDX_DIM + h + 1, :] * (N_IDX_HEADS ** -0.5)).astype(BF16).astype(F32)
            * (IDX_DIM ** -0.5) for h in range(N_IDX_HEADS)]

    def idx_dots(j):
        c0 = pl.multiple_of(j * KB, KB)
        ke = kie_ref[0, pl.ds(c0, KB), :]
        ko = kio_ref[0, pl.ds(c0, KB), :]
        raw = []
        for hp in range(N_IDX_HEADS // 2):
            qh = qi_ref[0, :, hp * LANES:(hp + 1) * LANES]
            raw.append((_dot_nt(ke, qh), _dot_nt(ko, qh)))
        return raw

    def idx_finish(j, raw, near_row):
        acc = jnp.zeros((KB, TQ), F32)
        for hp, (se, so) in enumerate(raw):
            re = jnp.maximum(se.astype(BF16), 0.0).astype(F32)
            ro = jnp.maximum(so.astype(BF16), 0.0).astype(F32)
            acc = acc + wrow[2 * hp] * re + wrow[2 * hp + 1] * ro
        bits = lax.bitcast_convert_type(acc, jnp.int32)
        key = bits ^ ((bits >> 31) & jnp.int32(0x7FFFFFFF))
        if near_row is not None:
            key = jnp.where(vist_ref[near_row:near_row + KB, :] > 0.0, key, jnp.int32(INT_MIN))
        keys[j] = key
        for dk in range(N_DIGITS):
            sh = 32 - DIGIT_BITS * (dk + 1)
            d = (key >> sh) if dk == 0 else ((key >> sh) & jnp.int32(2 ** DIGIT_BITS - 1))
            dig[dk, j] = d.astype(F32).astype(BF16)

    def idx_blocks(js, near_rows):
        raws = [idx_dots(j) for j in js]
        for j, raw, near_row in zip(js, raws, near_rows):
            idx_finish(j, raw, near_row)

    n_trips = n_far // FAR_BLOCKS
    n_left = n_far - n_trips * FAR_BLOCKS

    def far_idx_trip(jt, carry):
        idx_blocks([FAR_BLOCKS * jt + u for u in range(FAR_BLOCKS)], [None] * FAR_BLOCKS)
        return carry

    lax.fori_loop(0, n_trips, far_idx_trip, 0)

    for left in range(1, FAR_BLOCKS):
        @pl.when(n_left == left)
        def _(left=left):
            idx_blocks([n_far - left + u for u in range(left)], [None] * left)

    @pl.when(nkb >= 2)
    def _():
        idx_blocks([nkb - 2, nkb - 1], [0, KB])

    @pl.when(nkb < 2)
    def _():
        idx_blocks([nkb - 1], [KB])

    one = jnp.ones((SUB_BF16, TQ), BF16)
    zero = jnp.zeros((SUB_BF16, TQ), BF16)

    def count_ge(dk, cand):
        cb = jnp.broadcast_to(cand.astype(BF16), (SUB_BF16, TQ))

        def block_count(jj):
            blk = dig[dk, jj]
            parts = [jnp.where(blk[s * SUB_BF16:(s + 1) * SUB_BF16, :] >= cb, one, zero) for s in range(n_slab)]
            while len(parts) > 1:
                parts = [parts[n] + parts[n + 1] for n in range(0, len(parts), 2)]
            return parts[0]

        def body_many(jq, tot):
            cnts = [block_count(COUNT_BLOCKS * jq + u) for u in range(COUNT_BLOCKS)]
            while len(cnts) > 1:
                cnts = [cnts[n] + cnts[n + 1] for n in range(0, len(cnts), 2)]
            return tot + cnts[0].astype(F32)

        def body_one(jj, tot):
            return tot + block_count(jj).astype(F32)

        tot = lax.fori_loop(0, nkb // COUNT_BLOCKS, body_many, jnp.zeros((SUB_BF16, TQ), F32))
        tot = lax.fori_loop((nkb // COUNT_BLOCKS) * COUNT_BLOCKS, nkb, body_one, tot)
        return jnp.sum(tot, axis=0, keepdims=True)

    kneed = jnp.full((1, TQ), float(topk), F32)
    thr = jnp.zeros((1, TQ), jnp.int32)
    cnt_eq = None
    for dk in range(N_DIGITS):
        lo = -float(2 ** (DIGIT_BITS - 1)) if dk == 0 else 0.0

        def bis_body(p, carry, dk=dk, kneed=kneed):
            t, c_rej = carry
            cand = t + lax.shift_left(jnp.int32(1), DIGIT_BITS - 1 - p).astype(F32)
            cnt = count_ge(dk, cand)
            ok = cnt >= kneed
            return jnp.where(ok, cand, t), jnp.where(ok, c_rej, cnt)

        t, n_gt = lax.fori_loop(0, DIGIT_BITS, bis_body,
                                (jnp.full((1, TQ), lo, F32), jnp.zeros((1, TQ), F32)))
        if dk == N_DIGITS - 1:
            cnt_eq = count_ge(dk, t) - n_gt
        kneed = kneed - n_gt
        thr = thr + lax.shift_left(t.astype(jnp.int32), 32 - DIGIT_BITS * (dk + 1))
        if dk + 1 < N_DIGITS:
            tb = t.astype(BF16)

            def prep(jj, carry, dk=dk, tb=tb):
                dig[dk + 1, jj] = jnp.where(dig[dk, jj] == tb, dig[dk + 1, jj], jnp.asarray(-1.0, BF16))
                return carry

            lax.fori_loop(0, nkb, prep, 0)

    thr = jnp.maximum(thr, jnp.int32(INT_MIN + 1))
    need = kneed
    sub_f = lax.broadcasted_iota(jnp.int32, (KB, TQ), 0).astype(F32)

    xcut[...] = jnp.full(xcut.shape, 2.0 ** 24, F32)

    @pl.when(jnp.max(cnt_eq - need) > 0.0)
    def _():
        def count_eq_lt(x):
            def body(jj, tot):
                col = sub_f + (jj * KB).astype(F32)
                hit = jnp.where(jnp.where(keys[jj] == thr, col, 2.0 ** 25) < x, 1.0, 0.0)
                for s in range(KB // 8):
                    tot = tot + hit[s * 8:(s + 1) * 8, :]
                return tot
            tot = lax.fori_loop(0, nkb, body, jnp.zeros((8, TQ), F32))
            return jnp.sum(tot, axis=0, keepdims=True)

        def tie_body(p, x):
            cand = x + lax.shift_left(jnp.int32(1), tie_bits - 1 - p).astype(F32)
            return jnp.where(count_eq_lt(cand) < need, cand, x)

        x = lax.fori_loop(0, tie_bits, tie_body, jnp.zeros((1, TQ), F32))
        xcut[...] = jnp.broadcast_to(x, xcut.shape)

    xc = xcut[0:1, :]

    def mask_body(jj, carry):
        kblk = keys[jj]
        col = sub_f + (jj * KB).astype(F32)
        at_thr = jnp.where(col <= xc, 0.0, NEG)
        madd[jj] = jnp.where(kblk > thr, 0.0, jnp.where(kblk == thr, at_thr, NEG))
        return carry

    lax.fori_loop(0, nkb, mask_body, 0)

    HPG = N_HEADS_A // N_KV_A
    for g in range(N_KV_A):
        for r in range(HPG):
            h = g * HPG + r
            qs[g, r * TQ:(r + 1) * TQ, :] = q_ref[0, :, h * HEAD_DIM_A:(h + 1) * HEAD_DIM_A]
    acc_s[...] = jnp.zeros(acc_s.shape, F32)

    def attn_blocks(js, near_rows, carry):
        ms, ls = carry
        s_all = []
        for j in js:
            c0 = pl.multiple_of(j * KB, KB)
            s_all.append([_dot_nt(k_ref[0, pl.ds(c0, KB), g * HEAD_DIM_A:(g + 1) * HEAD_DIM_A], qs[g])
                          for g in range(N_KV_A)])
        for b, (j, near_row) in enumerate(zip(js, near_rows)):
            mk = madd[j]
            m_news = []
            for g in range(N_KV_A):
                for r in range(HPG):
                    h = g * HPG + r
                    t = s_all[b][g][:, r * TQ:(r + 1) * TQ] + mk
                    if near_row is not None:
                        t = t + bnear_ref[h, near_row:near_row + KB, :]
                    tbuf[b * N_HEADS_A + h] = t
                    m_news.append(jnp.maximum(ms[h], jnp.max(t, axis=0, keepdims=True)))
            ms_out, ls_out = [], []
            for g in range(N_KV_A):
                vt = vt_ref[0, j, g * HEAD_DIM_A:(g + 1) * HEAD_DIM_A, :]
                for r in range(HPG):
                    h = g * HPG + r
                    cs = slice(r * TQ, (r + 1) * TQ)
                    m_new = m_news[h]
                    alpha = jnp.exp2(ms[h] - m_new)
                    p = jnp.exp2(tbuf[b * N_HEADS_A + h] - m_new)
                    ls_out.append(alpha * ls[h] + jnp.sum(p, axis=0, keepdims=True))
                    ms_out.append(m_new)
                    acc_s[g, :, cs] = alpha * acc_s[g, :, cs] + _dot(vt, p.astype(BF16))
            ms, ls = tuple(ms_out), tuple(ls_out)
        return ms, ls

    carry = (tuple(jnp.full((1, TQ), NEG, F32) for _ in range(N_HEADS_A)),
             tuple(jnp.zeros((1, TQ), F32) for _ in range(N_HEADS_A)))
    carry = lax.fori_loop(
        0, n_trips,
        lambda jt, c: attn_blocks([FAR_BLOCKS * jt + u for u in range(FAR_BLOCKS)], [None] * FAR_BLOCKS, c), carry)
    for left in range(1, FAR_BLOCKS):
        carry = lax.cond(n_left == left,
                         lambda c, left=left: attn_blocks([n_far - left + u for u in range(left)], [None] * left, c),
                         lambda c: c, carry)
    ms, ls = lax.cond(nkb >= 2, lambda c: attn_blocks([nkb - 2, nkb - 1], [0, KB], c),
                      lambda c: attn_blocks([nkb - 1], [KB], c), carry)

    for g in range(N_KV_A):
        for r in range(HPG):
            h = g * HPG + r
            out_t = acc_s[g, :, r * TQ:(r + 1) * TQ] / ls[h]
            o_ref[0, :, h * HEAD_DIM_A:(h + 1) * HEAD_DIM_A] = out_t.T.astype(o_ref.dtype)


def _dsa(q, qi, kiwi, kb, vb, kie, kio, rel_bias, *, n_keys, pos0, tq, kblk):
    B, T, _ = q.shape
    Lp = kb.shape[1]
    assert T % tq == 0 and Lp % kblk == 0 and tq % CHUNK == 0 or T == tq
    n_qt = T // tq
    if n_qt == 1:
        kb0, kstep = Lp // kblk, 0
        assert ((pos0 + tq - 1) // CHUNK + 1) * CHUNK >= n_keys
    else:
        assert pos0 == 0 and tq == kblk and n_keys == Lp == T
        kb0, kstep = 1, 1
    topk = min(TOPK_MAX, n_keys // 4)
    rb = min(128, tq)

    t = jnp.arange(tq, dtype=jnp.int32)[:, None]
    c = jnp.arange(2 * kblk, dtype=jnp.int32)[None, :]
    if n_qt == 1:
        q_abs = pos0 + t
        s_abs = (kb0 - 2) * kblk + c
    else:
        q_abs = kblk + t
        s_abs = c
    vis = ((s_abs // CHUNK) <= (q_abs // CHUNK)) & (s_abs < (n_keys if n_qt == 1 else 2 * kblk))
    far_bucket = _far_bucket(kblk + 1, max(Lp, kblk + 2))

    n_kb = Lp // kblk
    hpg = N_HEADS_A // N_KV_A
    common = dict(TQ=tq, KB=kblk, kb0=kb0, kstep=kstep, topk=topk, far_bucket=far_bucket,
                  tie_bits=int(math.ceil(math.log2(Lp))) + 1)
    attend_scratch = [pltpu.VMEM((N_KV_A, hpg * tq, HEAD_DIM_A), BF16),
                      pltpu.VMEM((N_KV_A, hpg * tq, LANES), F32),
                      pltpu.VMEM((N_KV_A, hpg * tq, LANES), F32),
                      pltpu.VMEM((N_KV_A, hpg * tq, HEAD_DIM_A), F32)]
    if tq % LANES == 0:
        tab_t = _bias_table(rel_bias, _t5_bucket(s_abs - q_abs).T, shift_bucket=far_bucket)
        vt = jnp.swapaxes(vb.reshape(B, n_kb, kblk, D_KV), 2, 3)
        common.pop("far_bucket")
        return pl.pallas_call(
            functools.partial(_dsa_kernel_t, **common),
            grid=(B, n_qt),
            in_specs=[pl.BlockSpec((1, tq, D_ATTN), lambda b, i: (b, i, 0)),
                      pl.BlockSpec((1, tq, N_IDX_HEADS * IDX_DIM), lambda b, i: (b, i, 0)),
                      pl.BlockSpec((1, tq, LANES), lambda b, i: (b, i, 0)),
                      pl.BlockSpec((1, Lp, D_KV), lambda b, i: (b, 0, 0)),
                      pl.BlockSpec((1, n_kb, D_KV, kblk), lambda b, i: (b, 0, 0, 0)),
                      pl.BlockSpec((1, Lp, LANES), lambda b, i: (b, 0, 0)),
                      pl.BlockSpec((1, Lp, LANES), lambda b, i: (b, 0, 0)),
                      _const_spec((N_HEADS_A, 2 * kblk, tq)),
                      _const_spec((2 * kblk, tq))],
            out_specs=pl.BlockSpec((1, tq, D_ATTN), lambda b, i: (b, i, 0)),
            out_shape=jax.ShapeDtypeStruct((B, T, D_ATTN), BF16),
            scratch_shapes=[pltpu.VMEM((n_kb, kblk, tq), jnp.int32),
                            pltpu.VMEM((N_DIGITS, n_kb, kblk, tq), BF16),
                            pltpu.VMEM((n_kb, kblk, tq), F32),
                            pltpu.VMEM((N_KV_A, hpg * tq, HEAD_DIM_A), BF16),
                            pltpu.VMEM((8, tq), F32),
                            pltpu.VMEM((N_KV_A, HEAD_DIM_A, hpg * tq), F32),
                            pltpu.VMEM((FAR_BLOCKS * N_HEADS_A, kblk, tq), F32)],
            compiler_params=pltpu.CompilerParams(dimension_semantics=("arbitrary", "arbitrary"),
                                                 vmem_limit_bytes=VMEM_LIMIT),
            name="dsa_t",
        )(q, qi, kiwi, kb, vt, kie, kio, tab_t, vis.astype(F32).T)

    bias_near = _bias_table(rel_bias, _t5_bucket(s_abs - q_abs))
    kern = functools.partial(_dsa_kernel, RB=rb, **common)
    return pl.pallas_call(
        kern,
        grid=(B, n_qt),
        in_specs=[pl.BlockSpec(memory_space=pltpu.SMEM),
                  pl.BlockSpec((1, tq, D_ATTN), lambda b, i: (b, i, 0)),
                  pl.BlockSpec((1, tq, N_IDX_HEADS * IDX_DIM), lambda b, i: (b, i, 0)),
                  pl.BlockSpec((1, tq, LANES), lambda b, i: (b, i, 0)),
                  pl.BlockSpec((1, Lp, D_KV), lambda b, i: (b, 0, 0)),
                  pl.BlockSpec((1, Lp, D_KV), lambda b, i: (b, 0, 0)),
                  pl.BlockSpec((1, Lp, LANES), lambda b, i: (b, 0, 0)),
                  pl.BlockSpec((1, Lp, LANES), lambda b, i: (b, 0, 0)),
                  _const_spec((N_HEADS_A, tq, 2 * kblk)),
                  _const_spec((tq, 2 * kblk))],
        out_specs=pl.BlockSpec((1, tq, D_ATTN), lambda b, i: (b, i, 0)),
        out_shape=jax.ShapeDtypeStruct((B, T, D_ATTN), BF16),
        scratch_shapes=[pltpu.VMEM((n_kb, tq, kblk), jnp.int32),
                        pltpu.VMEM((n_kb, tq, kblk), F32),
                        pltpu.VMEM((N_IDX_HEADS, tq, LANES), F32),
                        pltpu.VMEM((N_KV_A, hpg * tq, HEAD_DIM_A), BF16),
                        pltpu.VMEM((tq, 1), F32),
                        pltpu.VMEM((N_KV_A, hpg * tq, LANES), F32),
                        pltpu.VMEM((N_KV_A, hpg * tq, LANES), F32),
                        pltpu.VMEM((N_KV_A, hpg * tq, HEAD_DIM_A), F32)],
        compiler_params=pltpu.CompilerParams(dimension_semantics=("arbitrary", "arbitrary"),
                                             vmem_limit_bytes=VMEM_LIMIT),
        name="dsa",
    )(rel_bias, q, qi, kiwi, kb, vb, kie, kio, bias_near, vis.astype(F32))


def _mix_kernel(x_ref, a_ref, u_ref, hist_ref, mk_ref, mv_ref, wpool_ref, pscale_ref, wout_ref, gx_ref,
                wxq_ref, wxo_ref, gffn_ref, wr_ref, br_ref, o_ref, ext, *, tm, pos0):
    t = pl.program_id(1)
    H = POOL_HIST + 1

    @pl.when(t == 0)
    def _():
        ext[0:H, :] = hist_ref[0]

    @pl.when(t > 0)
    def _():
        ext[0:H, :] = ext[tm:tm + H, :]

    ext[H:H + tm, :] = u_ref[0]

    da = _dot(a_ref[0], wout_ref[0:D_ATTN, :])

    pos = pos0 + t * tm + lax.broadcasted_iota(jnp.int32, (tm, 1), 0)
    pouts = []
    for gi, w in enumerate(POOL_WINDOWS):
        c0, c1 = gi * POOL_GROUP_DIM, (gi + 1) * POOL_GROUP_DIM
        wsum = ext[H:H + tm, c0:c1]
        for s in range(1, w):
            wsum = wsum + ext[H - s:H - s + tm, c0:c1]
        cnt = jnp.minimum(pos + 1, w).astype(F32)
        d = wsum / cnt - u_ref[0, :, c0:c1]
        y = _dot(d.astype(BF16), wpool_ref[gi]) * pscale_ref[:, c0:c1]
        pouts.append(y.astype(BF16))
    pcat = jnp.concatenate(pouts, axis=1)

    D = x_ref.shape[2]
    x1 = x_ref[0] + da + _dot(pcat, wout_ref[D_ATTN:D_ATTN + D_POOL, :])
    h = _rms(x1, gx_ref[...]).astype(BF16)
    qx = _dot(h, wxq_ref[...])
    heads = [slice(hh * XHEAD_DIM, (hh + 1) * XHEAD_DIM) for hh in range(N_XHEADS)]
    all_logits = [_dot_nt(qx[:, hs].astype(BF16), mk_ref[0, :, hs]) for hs in heads]
    outs = []
    for hs, logits in zip(heads, all_logits):
        logits = logits * (XHEAD_DIM ** -0.5)
        m = jnp.max(logits, axis=1, keepdims=True)
        p = jnp.exp(logits - m)
        l = jnp.sum(p, axis=1, keepdims=True)
        outs.append((_dot(p.astype(BF16), mv_ref[0, :, hs]) / l).astype(BF16))
    o = jnp.concatenate(outs, axis=1)
    x2 = x1 + _dot(o, wxo_ref[...])
    o_ref[0, :, 0:D] = x2
    hf = _rms(x2, gffn_ref[...]).astype(BF16)
    o_ref[0, :, D:D + LANES] = _route(_dot(hf, wr_ref[...]) + br_ref[...])


def _mix(x, a, u, hist, mk, mv, w_pool, pool_scale, w_out, g_x, w_xq, w_xo, g_ffn, w_r, b_r, *, pos0, tm):
    B, T, D = x.shape
    n_mem = mk.shape[1]
    H = POOL_HIST + 1
    assert T % tm == 0 and tm >= H
    hist16 = jnp.concatenate([jnp.zeros((B, 1, D_POOL), F32), hist], axis=1)
    return pl.pallas_call(
        functools.partial(_mix_kernel, tm=tm, pos0=pos0),
        grid=(B, T // tm),
        in_specs=[pl.BlockSpec((1, tm, D), lambda b, t: (b, t, 0)),
                  pl.BlockSpec((1, tm, D_ATTN), lambda b, t: (b, t, 0)),
                  pl.BlockSpec((1, tm, D_POOL), lambda b, t: (b, t, 0)),
                  pl.BlockSpec((1, H, D_POOL), lambda b, t: (b, 0, 0)),
                  pl.BlockSpec((1, n_mem, D_X), lambda b, t: (b, 0, 0)),
                  pl.BlockSpec((1, n_mem, D_X), lambda b, t: (b, 0, 0)),
                  _const_spec(w_pool.shape),
                  _const_spec((1, D_POOL)),
                  _const_spec(w_out.shape),
                  _const_spec((1, D)),
                  _const_spec(w_xq.shape),
                  _const_spec(w_xo.shape),
                  _const_spec((1, D)),
                  _const_spec(w_r.shape),
                  _const_spec((1, LANES))],
        out_specs=pl.BlockSpec((1, tm, D + LANES), lambda b, t: (b, t, 0)),
        out_shape=jax.ShapeDtypeStruct((B, T, D + LANES), F32),
        scratch_shapes=[pltpu.VMEM((H + tm, D_POOL), F32)],
        compiler_params=pltpu.CompilerParams(dimension_semantics=("arbitrary", "arbitrary"),
                                             vmem_limit_bytes=VMEM_LIMIT),
        name="mix",
    )(x, a, u, hist16, mk, mv, w_pool, pool_scale.reshape(1, D_POOL), w_out, g_x.reshape(1, D), w_xq, w_xo,
      g_ffn.reshape(1, D), w_r, b_r)


R_OFF = N_GROUPS


def _route(logits):
    tm = logits.shape[0]
    lane = lax.broadcasted_iota(jnp.int32, (tm, LANES), 1).astype(F32)
    ninf = -jnp.inf
    big = float(LANES)
    gl = jnp.where(lane < N_GROUPS, logits, ninf)
    gmax = jnp.max(gl, axis=1, keepdims=True)
    g_sel = jnp.min(jnp.where(gl == gmax, lane, big), axis=1, keepdims=True)
    g_prob = 1.0 / jnp.sum(jnp.exp(gl - gmax), axis=1, keepdims=True)
    lo = R_OFF + g_sel * EXPERTS_PER_GROUP
    el = jnp.where((lane >= lo) & (lane < lo + EXPERTS_PER_GROUP), logits, ninf)
    tv0 = jnp.max(el, axis=1, keepdims=True)
    ti0 = jnp.min(jnp.where(el == tv0, lane, big), axis=1, keepdims=True)
    el2 = jnp.where(lane == ti0, ninf, el)
    tv1 = jnp.max(el2, axis=1, keepdims=True)
    ti1 = jnp.min(jnp.where(el2 == tv1, lane, big), axis=1, keepdims=True)
    e1 = jnp.exp(tv1 - tv0)
    den = 1.0 + e1
    w0 = g_prob / den
    w1 = g_prob * e1 / den
    ids = jnp.where(lane == 0.0, jnp.minimum(ti0, ti1) - R_OFF,
                    jnp.where(lane == 1.0, jnp.maximum(ti0, ti1) - R_OFF, 0.0))
    return jnp.where(lane == ti0, w0, 0.0) + jnp.where(lane == ti1, w1, 0.0) + ids


def _expert(h, rec, e, wg, wu, wd):
    a = _dot(h, wg)
    b = _dot(h, wu)
    lane = lax.broadcasted_iota(jnp.int32, rec.shape, 1)
    cw = jnp.sum(jnp.where(lane == e + R_OFF, rec, 0.0), axis=1, keepdims=True)
    hid = a * jax.nn.sigmoid(a) * b * cw
    return _dot(hid.astype(BF16), wd)


def _moe_kernel(x_ref, gffn_ref, wg_ref, wu_ref, wd_ref, gfin_ref, o_ref, h_s, acc_s):
    e = pl.program_id(1)
    d = o_ref.shape[1]

    @pl.when(e == 0)
    def _():
        h_s[...] = _rms(x_ref[:, 0:d], gffn_ref[...]).astype(BF16)
        acc_s[...] = jnp.zeros(acc_s.shape, F32)

    acc_s[...] += _expert(h_s[...], x_ref[:, d:d + LANES], e, wg_ref[0], wu_ref[0], wd_ref[0])

    @pl.when(e == pl.num_programs(1) - 1)
    def _():
        o_ref[...] = _rms(x_ref[:, 0:d] + acc_s[...], gfin_ref[...])


def _moe(x, g_ffn, w_gate, w_up, w_down, g_final, *, tm):
    n, de = x.shape
    d = de - LANES
    assert n % tm == 0
    return pl.pallas_call(
        _moe_kernel,
        grid=(n // tm, N_EXPERTS),
        in_specs=[pl.BlockSpec((tm, de), lambda i, e: (i, 0)),
                  _const_spec((1, d)),
                  pl.BlockSpec((1, d, D_EXPERT), lambda i, e: (e, 0, 0)),
                  pl.BlockSpec((1, d, D_EXPERT), lambda i, e: (e, 0, 0)),
                  pl.BlockSpec((1, D_EXPERT, d), lambda i, e: (e, 0, 0)),
                  _const_spec((1, d))],
        out_specs=pl.BlockSpec((tm, d), lambda i, e: (i, 0)),
        out_shape=jax.ShapeDtypeStruct((n, d), F32),
        scratch_shapes=[pltpu.VMEM((tm, d), BF16),
                        pltpu.VMEM((tm, d), F32)],
        compiler_params=pltpu.CompilerParams(dimension_semantics=("arbitrary", "arbitrary"),
                                             vmem_limit_bytes=VMEM_LIMIT),
        name="moe",
    )(x, g_ffn.reshape(1, d), w_gate, w_up, w_down, g_final.reshape(1, d))


MOE_TILES_PER_STEP = 2


def _moe_sparse_kernel(ea_ref, eb_ref, nv_ref, idx_ref, idxn_ref, x_hbm, gffn_ref, gfin_ref, *rest, tm, tps):
    w_refs = [rest[6 * k:6 * k + 6] for k in range(tps)]
    o_hbm, xbuf, ybuf, gsem, ssem = rest[6 * tps:]
    t = pl.program_id(0)
    nt = pl.num_programs(0)
    d = o_hbm.shape[1]
    slot = lax.rem(t, 2)
    nslot = 1 - slot
    rows = tps * tm
    WAIT_ROWS = 8

    def step_rows(step):
        nvs = [nv_ref[tps * step + k] for k in range(tps)]
        return nvs, functools.reduce(lambda a, b: a + b, nvs)

    nvs, nv = step_rows(t)

    def gather_row(ids, s, r):
        return pltpu.make_async_copy(x_hbm.at[pl.ds(ids[0, 0, r], 1), :], xbuf.at[s, pl.ds(r, 1), :], gsem.at[s])

    def gather_loop(ids, s):
        def body(r, carry):
            gather_row(ids, s, r).start()
            return carry
        lax.fori_loop(0, rows, body, 0)

    def gather_wait(s):
        pltpu.make_async_copy(x_hbm.at[pl.ds(0, rows), :], xbuf.at[s], gsem.at[s]).wait()

    def scatter_rows(s, r, row, n):
        return pltpu.make_async_copy(ybuf.at[s, pl.ds(r, n), :], o_hbm.at[pl.ds(row, n), :], ssem.at[s])

    def scatter_wait(s, n):
        def body_many(r, carry):
            scatter_rows(s, 0, 0, WAIT_ROWS).wait()
            return carry
        lax.fori_loop(0, n // WAIT_ROWS, body_many, 0)

        def body_one(r, carry):
            scatter_rows(s, 0, 0, 1).wait()
            return carry
        lax.fori_loop(0, lax.rem(n, WAIT_ROWS), body_one, 0)

    @pl.when(t == 0)
    def _():
        gather_loop(idx_ref, 0)

    @pl.when(t >= 2)
    def _():
        scatter_wait(slot, step_rows(jnp.maximum(t - 2, 0))[1])

    @pl.when((t == 0) | (step_rows(jnp.maximum(t - 1, 0))[1] > 0))
    def _():
        gather_wait(slot)

    @pl.when(nv > 0)
    def _():
        tiles = []
        for k in range(tps):
            x = xbuf[slot, k * tm:(k + 1) * tm, :]
            xr = x[:, 0:d]
            tiles.append((xr, x[:, d:d + LANES], _rms(xr, gffn_ref[...]).astype(BF16)))
        for r in range(rows):
            gather_row(idxn_ref, nslot, r).start()
        pre = [[(_dot(h, wga[0]), _dot(h, wua[0])), (_dot(h, wgb[0]), _dot(h, wub[0]))]
               for (_, _, h), (wga, wua, _, wgb, wub, _) in zip(tiles, w_refs)]
        for k in range(tps):
            xr, rec, _ = tiles[k]
            lane = lax.broadcasted_iota(jnp.int32, rec.shape, 1)
            experts = (ea_ref[tps * t + k], eb_ref[tps * t + k])
            acc = None
            for (a, b), e, wd in zip(pre[k], experts, (w_refs[k][2], w_refs[k][5])):
                cw = jnp.sum(jnp.where(lane == e + R_OFF, rec, 0.0), axis=1, keepdims=True)
                hid = a * jax.nn.sigmoid(a) * b * cw
                out = _dot(hid.astype(BF16), wd[0])
                acc = out if acc is None else acc + out
            ybuf[slot, k * tm:(k + 1) * tm, :] = _rms(xr + acc, gfin_ref[...])

        for k in range(tps):
            def body_many(i, carry, k=k):
                for u in range(WAIT_ROWS):
                    r = k * tm + i * WAIT_ROWS + u
                    scatter_rows(slot, r, idx_ref[0, 0, r], 1).start(priority=u % 2)
                return carry
            lax.fori_loop(0, nvs[k] // WAIT_ROWS, body_many, 0)

            def body_one(i, carry, k=k):
                r = k * tm + i
                scatter_rows(slot, r, idx_ref[0, 0, r], 1).start()
                return carry
            lax.fori_loop((nvs[k] // WAIT_ROWS) * WAIT_ROWS, nvs[k], body_one, 0)

    @pl.when(t == nt - 1)
    def _():
        @pl.when(nv > 0)
        def _():
            gather_wait(nslot)

        @pl.when(t >= 1)
        def _():
            scatter_wait(nslot, step_rows(jnp.maximum(t - 1, 0))[1])
        scatter_wait(slot, nv)


def _moe_tile_rows(n):
    n_pairs = N_GROUPS * (EXPERTS_PER_GROUP * (EXPERTS_PER_GROUP - 1) // 2)
    mean = n / n_pairs
    rows = (mean + 3.0 * math.sqrt(mean)) / 2
    return int(min(256, max(SUB_BF16, -(-rows // SUB_BF16) * SUB_BF16)))


def _moe_sparse(x, g_ffn, w_gate, w_up, w_down, g_final, *, tm):
    n, de = x.shape
    d = de - LANES
    n_pairs = N_GROUPS * (EXPERTS_PER_GROUP * (EXPERTS_PER_GROUP - 1) // 2)
    tps = MOE_TILES_PER_STEP
    nt = -(-n // tm) + n_pairs
    nt = -(-nt // tps) * tps
    n_steps = nt // tps
    n_keys = N_EXPERTS * N_EXPERTS

    ids = x[:, d:d + 2].astype(jnp.int32)
    key = ids[:, 0] * N_EXPERTS + ids[:, 1]
    order = jnp.argsort(key).astype(jnp.int32)
    skey = key[order]
    count_below = lambda sorted_vals, q: jnp.sum((sorted_vals[None, :] < q[:, None]).astype(jnp.int32), axis=1)
    edges = count_below(skey, jnp.arange(n_keys + 1, dtype=jnp.int32))
    starts, ends = edges[:-1], edges[1:]
    tiles = (ends - starts + tm - 1) // tm
    cum = jnp.cumsum(tiles)
    total = cum[-1]
    tt = jnp.arange(nt, dtype=jnp.int32)
    tc = jnp.minimum(tt, total - 1)
    cls = count_below(cum, tc + 1)
    first = cum[cls] - tiles[cls]
    tstart = starts[cls] + (tc - first) * tm
    nv = jnp.where(tt < total, jnp.clip(ends[cls] - tstart, 0, tm), 0).astype(jnp.int32)
    ea = (cls // N_EXPERTS).astype(jnp.int32)
    eb = (cls % N_EXPERTS).astype(jnp.int32)
    rows = jnp.clip(tstart[:, None] + jnp.arange(tm, dtype=jnp.int32)[None, :], 0, n - 1)
    idx = order[rows].reshape(n_steps, 1, tps * tm)

    def wspec(shape, experts, k):
        return pl.BlockSpec(shape, lambda t, ea, eb, nv: ((ea, eb)[experts][tps * t + k], 0, 0))

    cspec = lambda shape: pl.BlockSpec(shape, lambda t, ea, eb, nv: (0,) * len(shape),
                                       pipeline_mode=pl.Buffered(1))
    w_specs, w_args = [], []
    for k in range(tps):
        for experts in (0, 1):
            w_specs += [wspec((1, d, D_EXPERT), experts, k), wspec((1, d, D_EXPERT), experts, k),
                        wspec((1, D_EXPERT, d), experts, k)]
            w_args += [w_gate, w_up, w_down]
    grid_spec = pltpu.PrefetchScalarGridSpec(
        num_scalar_prefetch=3,
        grid=(n_steps,),
        in_specs=[pl.BlockSpec((1, 1, tps * tm), lambda t, ea, eb, nv: (t, 0, 0), memory_space=pltpu.SMEM),
                  pl.BlockSpec((1, 1, tps * tm), lambda t, ea, eb, nv: (jnp.minimum(t + 1, n_steps - 1), 0, 0),
                               memory_space=pltpu.SMEM),
                  pl.BlockSpec(memory_space=pl.ANY),
                  cspec((1, d)), cspec((1, d))] + w_specs,
        out_specs=pl.BlockSpec(memory_space=pl.ANY),
        scratch_shapes=[pltpu.VMEM((2, tps * tm, de), F32),
                        pltpu.VMEM((2, tps * tm, d), F32),
                        pltpu.SemaphoreType.DMA((2,)),
                        pltpu.SemaphoreType.DMA((2,))])
    return pl.pallas_call(
        functools.partial(_moe_sparse_kernel, tm=tm, tps=tps),
        grid_spec=grid_spec,
        out_shape=jax.ShapeDtypeStruct((n, d), F32),
        compiler_params=pltpu.CompilerParams(dimension_semantics=("arbitrary",),
                                             vmem_limit_bytes=VMEM_LIMIT),
        name="moe_sparse",
    )(ea, eb, nv, idx, idx, x, g_ffn.reshape(1, d), g_final.reshape(1, d), *w_args)


C_Q = 0
C_QI = C_Q + D_ATTN
C_K = C_QI + N_IDX_HEADS * IDX_DIM
C_V = C_K + D_KV
C_KIWI = C_V + D_KV
C_KIE = C_KIWI + LANES
C_KIO = C_KIE + LANES
C_U = C_KIO + LANES
C_END = C_U + D_POOL


def _layout_w_in(w_in):
    d = w_in.shape[0]
    offs = np.cumsum((0, D_ATTN, D_KV, D_KV, N_IDX_HEADS * IDX_DIM, IDX_DIM, N_IDX_HEADS, D_POOL))
    wq, wk, wv, wqi, wki, wwi, wu = (w_in[:, offs[n]:offs[n + 1]] for n in range(7))
    z = lambda n: jnp.zeros((d, n), w_in.dtype)
    wq = wq * (HEAD_DIM_A ** -0.5 * LOG2E)
    cat = jnp.concatenate([wq, wqi, wk, wv,
                           wki, wwi, z(LANES - IDX_DIM - N_IDX_HEADS),
                           wki, z(LANES - IDX_DIM),
                           z(LANES - IDX_DIM), wki,
                           wu], axis=1)
    assert cat.shape[1] == C_END
    return cat.astype(BF16)


IN_GROUPS = ((C_Q, C_QI, 1),
             (C_QI, C_K, 1),
             (C_K, C_V, 2),
             (C_V, C_KIWI, 2),
             (C_KIWI, C_KIE, 1),
             (C_KIE, C_KIO, 1),
             (C_KIO, C_U, 1),
             (C_U, C_END, 1))
IN_DTYPES = (BF16, BF16, F32, BF16, F32, BF16, F32, BF16, BF16, F32)


def _pad_rows(a, n):
    return jnp.pad(a, ((0, 0), (0, n - a.shape[1]), (0, 0)))


def _layer(x, pos0, caches, pool_hist, mk, mv, rel_bias, W, *, tm_in, tq, tm_mix, tm_moe, sparse_moe):
    B, T, D = x.shape
    n = B * T
    q, qi, k, kb, v, vb, kiwi, kie, kio, u = _norm_matmul(
        x.reshape(n, D), W["g_mix"], W["w_in"], IN_GROUPS, IN_DTYPES, tm_in)
    r3 = lambda a: a.reshape(B, T, a.shape[-1])
    kblk = KEY_BLOCK
    if caches is None:
        n_keys = T
        kb3, vb3, kie3, kio3 = r3(kb), r3(vb), r3(kie), r3(kio)
    else:
        k_hist, v_hist, ki_hist = caches
        past = k_hist.shape[1]
        n_keys = past + T
        lp = -(-n_keys // kblk) * kblk
        ki = r3(kiwi)[:, :, :IDX_DIM]
        ki_all = jnp.concatenate([ki_hist, ki], axis=1)
        zeros = jnp.zeros_like(ki_all)
        kb3 = _pad_rows(jnp.concatenate([k_hist.reshape(B, past, D_KV), r3(k)], axis=1), lp).astype(BF16)
        vb3 = _pad_rows(jnp.concatenate([v_hist.reshape(B, past, D_KV), r3(v)], axis=1), lp).astype(BF16)
        kie3 = _pad_rows(jnp.concatenate([ki_all, zeros], axis=2), lp).astype(BF16)
        kio3 = _pad_rows(jnp.concatenate([zeros, ki_all], axis=2), lp).astype(BF16)
    a_out = _dsa(r3(q), r3(qi), r3(kiwi), kb3, vb3, kie3, kio3, rel_bias,
                 n_keys=n_keys, pos0=pos0, tq=tq, kblk=kblk)
    x2 = _mix(x, a_out, r3(u), pool_hist, mk.astype(BF16), mv.astype(BF16), W["w_pool"], W["pool_scale"],
              W["w_out"], W["g_x"], W["w_xq"], W["w_xo"], W["g_ffn"], W["w_r"], W["b_r"], pos0=pos0, tm=tm_mix)
    moe = _moe_sparse if sparse_moe else _moe
    y = moe(x2.reshape(n, D + LANES), W["g_ffn"], W["w_gate"], W["w_up"], W["w_down"], W["g_final"], tm=tm_moe)
    new_pool = jnp.concatenate([pool_hist, r3(u)], axis=1)[:, -POOL_HIST:]
    return (y.reshape(B, T, D), k.reshape(B, T, N_KV_A, HEAD_DIM_A), v.reshape(B, T, N_KV_A, HEAD_DIM_A),
            r3(kiwi)[:, :, :IDX_DIM], new_pool)


def kernel(x_prompt, x_sample, mem_prompt, cache_k, cache_v, cache_k_idx, cache_pool, cache_mem_k, cache_mem_v, rel_bias, g_mix, w_in, w_pool, pool_scale, w_out, g_mem, w_mk, w_mv, g_x, w_xq, w_xo, g_ffn, w_rg, b_rg, w_re, b_re, w_gate, w_up, w_down, g_final):
    depth = g_mix.shape[0]
    assert depth == 1
    l = 0
    B, T, D = x_prompt.shape
    Bs, Ts, _ = x_sample.shape
    n_mem = mem_prompt.shape[1]
    past = cache_k.shape[2]

    w_r = jnp.concatenate([w_rg[l], w_re[l], jnp.zeros((D, LANES - N_GROUPS - N_EXPERTS), F32)], axis=1)
    W = dict(
        g_mix=g_mix[l], w_in=_layout_w_in(w_in[l]), w_pool=w_pool[l].astype(BF16), pool_scale=pool_scale[l],
        w_out=w_out[l].astype(BF16), g_x=g_x[l], w_xq=w_xq[l].astype(BF16), w_xo=w_xo[l].astype(BF16),
        g_ffn=g_ffn[l], w_r=w_r.astype(BF16),
        b_r=jnp.concatenate([b_rg[l], b_re[l], jnp.zeros((LANES - N_GROUPS - N_EXPERTS,), F32)]).reshape(1, LANES),
        w_gate=w_gate[l].astype(BF16), w_up=w_up[l].astype(BF16), w_down=w_down[l].astype(BF16),
        g_final=g_final)

    w_m = jnp.concatenate([w_mk[l], w_mv[l]], axis=1).astype(BF16)
    mk, mv = _norm_matmul(mem_prompt.reshape(B * n_mem, D), g_mem[l], w_m,
                          ((0, D_X, 1), (D_X, 2 * D_X, 1)), (F32, F32), TOKEN_TILE)
    mk = mk.reshape(B, n_mem, D_X)
    mv = mv.reshape(B, n_mem, D_X)

    yp, kp, vp, kip, pp = _layer(x_prompt, 0, None, jnp.zeros((B, POOL_HIST, D_POOL), F32), mk, mv, rel_bias, W,
                                 tm_in=TOKEN_TILE, tq=QUERY_TILE, tm_mix=TOKEN_TILE, tm_moe=_moe_tile_rows(B * T),
                                 sparse_moe=True)
    ys, ks, vs, kis, ps = _layer(x_sample, past,
                                 (cache_k[l], cache_v[l], cache_k_idx[l]), cache_pool[l],
                                 cache_mem_k[l].reshape(Bs, n_mem, D_X), cache_mem_v[l].reshape(Bs, n_mem, D_X),
                                 rel_bias, W, tm_in=Bs * Ts, tq=Ts, tm_mix=Ts, tm_moe=Bs * Ts, sparse_moe=False)
    st = lambda a: a[None]
    return (yp, ys, st(kp), st(vp), st(kip), st(pp),
            st(mk.reshape(B, n_mem, N_XHEADS, XHEAD_DIM)), st(mv.reshape(B, n_mem, N_XHEADS, XHEAD_DIM)),
            st(ks), st(vs), st(kis), st(ps))
```
